```python
import math
import jax, jax.numpy as jnp
from jax import lax
import numpy as np

D_MODEL = 2048
BATCH = 16
SEQ = 256
DEPTH = 2
DEC_BATCH = 8
DEC_SEQ = 4096
PAST_LEN = 256

GRID_W = 64
N_MOD = 9
FFN_DIM = ((8 * D_MODEL // 3 + 127) // 128) * 128
HY_W = D_MODEL // 2
HY_SHORT = 3
HY_BANDS = 8
HY_EMB = 1 + 2 * HY_BANDS
HY_FILTER_HIDDEN = 64
HY_FILTER_GAIN = 0.1
HY_DECAY_SLOW = abs(math.log(1e-2)) / 1.5
HY_DECAY_FAST = abs(math.log(1e-2)) / 0.3
GLA_HEADS = 4
GLA_DK = D_MODEL // 16
GLA_DV = D_MODEL // 8
GLA_W = GLA_HEADS * GLA_DV
GLA_RANK = 16
GLA_TAU = 16.0
GLA_CHUNK = 64
DIFF_HEADS = 8
DIFF_DH = D_MODEL // 32
DIFF_DV = 2 * DIFF_DH
DIFF_W = DIFF_HEADS * DIFF_DV
ROPE_THETA = 10000.0
Q_BLOCK = 128
N_BRANCH = 3
LN_EPS = 1e-5
RMS_EPS = 1e-6
ALPHA = (2 * DEPTH) ** 0.25
BETA = (8 * DEPTH) ** -0.25
IN_SPLITS = (3 * HY_W, GLA_HEADS * GLA_DK, GLA_HEADS * GLA_DK, GLA_W, GLA_W, 2 * GLA_RANK,
             DIFF_HEADS * 2 * DIFF_DH, DIFF_HEADS * 2 * DIFF_DH, DIFF_W, N_BRANCH * D_MODEL)
IN_COLS = sum(IN_SPLITS)

kernel_name = 'hybrid_diffusion_prefix_backbone_step'


def _layer_norm(x, g, b):
    xf = x.astype(jnp.float32)
    mu = xf.mean(-1, keepdims=True)
    var = jnp.square(xf - mu).mean(-1, keepdims=True)
    return ((xf - mu) * lax.rsqrt(var + LN_EPS) * g + b).astype(x.dtype)


def _rms_norm(x, g):
    xf = x.astype(jnp.float32)
    return (xf * lax.rsqrt(jnp.mean(xf * xf, -1, keepdims=True) + RMS_EPS) * g).astype(x.dtype)


def _swiglu(h, w1, w3, w2):
    return (jax.nn.silu(h @ w1) * (h @ w3)) @ w2


def _short_conv(u, w, b):
    up = jnp.pad(u, ((0, 0), (1, 1), (0, 0)))
    return up[:, :-2] * w[0] + up[:, 1:-1] * w[1] + up[:, 2:] * w[2] + b


def _hyena_filters(L, P):
    t = jnp.linspace(0.0, 1.0, L, dtype=jnp.float32)
    w = (2.0 * math.pi / L) * jnp.arange(L, dtype=jnp.float32)
    f = jnp.linspace(1e-4, HY_BANDS - 1, HY_BANDS, dtype=jnp.float32)
    feats = jnp.concatenate([t[:, None], jnp.cos(w[:, None] * f), -jnp.sin(w[:, None] * f)], -1)
    hdn = jnp.sin(P['hy_freq'][0] * (feats @ P['hy_w1'] + P['hy_b1']))
    hdn = jnp.sin(P['hy_freq'][1] * (hdn @ P['hy_w2'] + P['hy_b2']))
    h = (hdn @ P['hy_w3']) * jnp.exp(-t[:, None] * jnp.abs(P['hy_decay']))
    h = h.astype(jnp.float32).reshape(L, 2, 2, HY_W)
    fwd, bwd = h[:, :, 0], h[:, :, 1]
    two_sided = jnp.concatenate([fwd, jnp.zeros((1, 2, HY_W), jnp.float32), bwd[:0:-1]], axis=0)
    return jnp.fft.rfft(two_sided, axis=0)


def _long_conv(z, kf, bias):
    L = z.shape[1]
    zf32 = z.astype(jnp.float32)
    y = jnp.fft.irfft(jnp.fft.rfft(zf32, n=2 * L, axis=1) * kf, n=2 * L, axis=1)[:, :L]
    return (y + zf32 * bias).astype(z.dtype)


def _hyena(u, P):
    L = u.shape[1]
    u = _short_conv(u, P['hy_conv_w'], P['hy_conv_b'])
    v, x1, x2 = jnp.split(u, 3, axis=-1)
    kf = _hyena_filters(L, P)
    z = x1 * _long_conv(v, kf[:, 0], P['hy_bias'][0])
    return x2 * _long_conv(z, kf[:, 1], P['hy_bias'][1])


def _gla_chunked(q, k, v, log_a, s0):
    B_, L, H, _ = q.shape
    dv = v.shape[-1]
    n = L // GLA_CHUNK

    def chunks(a):
        return a.astype(jnp.float32).reshape(B_, n, GLA_CHUNK, H, a.shape[-1])

    q, k, v, log_a = chunks(q), chunks(k), chunks(v), chunks(log_a)
    b = jnp.cumsum(log_a, axis=2)
    ref = b[:, :, GLA_CHUNK // 2 - 1:GLA_CHUNK // 2]
    att = jnp.einsum('bnthk,bnshk->bnhts', q * jnp.exp(b - ref), k * jnp.exp(ref - b))
    lower = jnp.tril(jnp.ones((GLA_CHUNK, GLA_CHUNK), dtype=bool))
    att = jnp.where(lower, att, 0.0)
    o_intra = jnp.einsum('bnhts,bnshv->bnthv', att, v)
    b_last = b[:, :, -1:]
    q_in = q * jnp.exp(b)
    k_st = k * jnp.exp(b_last - b)
    decay = jnp.exp(b_last[:, :, 0])

    def step(S, xs):
        qn, kn, vn, dn = xs
        o = jnp.einsum('bthk,bhkv->bthv', qn, S)
        S = dn[..., None] * S + jnp.einsum('bshk,bshv->bhkv', kn, vn)
        return S, o

    xs = (jnp.moveaxis(q_in, 1, 0), jnp.moveaxis(k_st, 1, 0), jnp.moveaxis(v, 1, 0), jnp.moveaxis(decay, 1, 0))
    S, o_inter = lax.scan(step, s0.astype(jnp.float32), xs)
    o = o_intra + jnp.moveaxis(o_inter, 0, 1)
    return o.reshape(B_, L, H, dv), S


def _gla_branch(gq, gk, gv, gr, glr, P, s0f, s0b):
    B_, L, _ = gq.shape
    dt = gq.dtype
    q = gq.reshape(B_, L, GLA_HEADS, GLA_DK) * GLA_DK ** -0.5
    k = gk.reshape(B_, L, GLA_HEADS, GLA_DK)
    v = gv.reshape(B_, L, GLA_HEADS, GLA_DV)
    lr = glr.astype(jnp.float32)

    def log_gate(d):
        logits = lr[..., d * GLA_RANK:(d + 1) * GLA_RANK] @ P['gla_wa'][d].astype(jnp.float32) + P['gla_ba'][d].astype(jnp.float32)
        return (jax.nn.log_sigmoid(logits) / GLA_TAU).reshape(B_, L, GLA_HEADS, GLA_DK)

    def flip(a):
        return jnp.flip(a, axis=1)

    o_f, s_f = _gla_chunked(q, k, v, log_gate(0), s0f)
    o_b, s_b = _gla_chunked(flip(q), flip(k), flip(v), flip(log_gate(1)), s0b)
    o = o_f + flip(o_b)
    o = _rms_norm(o, P['gla_norm_g']) * jax.nn.silu(gr.astype(jnp.float32)).reshape(B_, L, GLA_HEADS, GLA_DV)
    return o.reshape(B_, L, GLA_W).astype(dt), jnp.stack([s_f, s_b], axis=1).astype(dt)


def _axial_rope(L):
    rows = L // GRID_W
    r = jnp.repeat(jnp.arange(rows, dtype=jnp.float32), GRID_W)
    col = jnp.tile(jnp.arange(GRID_W, dtype=jnp.float32), rows)
    nf = DIFF_DH // 4
    inv = ROPE_THETA ** (-jnp.arange(nf, dtype=jnp.float32) / nf)
    ang = jnp.stack([r[:, None] * inv, col[:, None] * inv], axis=1)
    return jnp.cos(ang)[None, :, None, None], jnp.sin(ang)[None, :, None, None]


def _apply_rope(x, cos, sin):
    xs = x.reshape(x.shape[:-1] + (2, 2, DIFF_DH // 4))
    x1, x2 = xs[..., 0, :], xs[..., 1, :]
    cos, sin = cos.astype(x.dtype), sin.astype(x.dtype)
    out = jnp.stack([x1 * cos - x2 * sin, x1 * sin + x2 * cos], axis=-2)
    return out.reshape(x.shape)


def _diff_attend(q, k, v, lam):
    B_, Lq, H, _, dh = q.shape
    nb = Lq // Q_BLOCK
    qb = jnp.swapaxes(q.reshape(B_, nb, Q_BLOCK, H, 2, dh), 0, 1)
    scale = dh ** -0.5

    def block(qi):
        s = jnp.einsum('bqhjd,bkhjd->bhjqk', qi, k).astype(jnp.float32) * scale
        p = jax.nn.softmax(s, axis=-1)
        a = p[:, :, 0] - lam * p[:, :, 1]
        return jnp.einsum('bhqk,bkhv->bqhv', a.astype(v.dtype), v)

    o = lax.map(block, qb)
    return jnp.swapaxes(o, 0, 1).reshape(B_, Lq, H, v.shape[-1])


def _diff_branch(dq, dk, dv, P, lam_init, ctx_k, ctx_v):
    B_, L, _ = dq.shape
    q = dq.reshape(B_, L, DIFF_HEADS, 2, DIFF_DH)
    k = dk.reshape(B_, L, DIFF_HEADS, 2, DIFF_DH)
    v = dv.reshape(B_, L, DIFF_HEADS, DIFF_DV)
    lp = P['diff_lam'].astype(jnp.float32)
    lam = jnp.exp(jnp.sum(lp[0] * lp[1])) - jnp.exp(jnp.sum(lp[2] * lp[3])) + lam_init
    if ctx_k is None:
        o = _diff_attend(q, k, v, lam)
    else:
        cos, sin = _axial_rope(L)
        ka = jnp.concatenate([ctx_k.astype(k.dtype), _apply_rope(k, cos, sin)], axis=1)
        va = jnp.concatenate([ctx_v.astype(v.dtype), v], axis=1)
        o = _diff_attend(_apply_rope(q, cos, sin), ka, va, lam)
    o = _rms_norm(o, P['diff_norm_g']) * (1.0 - lam_init)
    return o.reshape(B_, L, DIFF_W), k, v


def _mixer(h, P, lam_init, ctx_k, ctx_v, ctx_state):
    B_, L, _ = h.shape
    z = h @ P['w_in']
    zh, gq, gk, gv, gr, glr, dq, dk, dv, zg = jnp.split(z, np.cumsum(IN_SPLITS)[:-1].tolist(), axis=-1)
    ya = _hyena(zh, P)
    if ctx_state is None:
        s0 = jnp.zeros((B_, GLA_HEADS, GLA_DK, GLA_DV), jnp.float32)
        s0f, s0b = s0, s0
    else:
        s0f, s0b = ctx_state[:, 0], ctx_state[:, 1]
    yb, st = _gla_branch(gq, gk, gv, gr, glr, P, s0f, s0b)
    yc, kc, vc = _diff_branch(dq, dk, dv, P, lam_init, ctx_k, ctx_v)
    gates = jax.nn.sigmoid(zg.astype(jnp.float32)).astype(h.dtype).reshape(B_, L, N_BRANCH, D_MODEL)
    y = (gates[:, :, 0] * (ya @ P['w_branch_a'])
         + gates[:, :, 1] * (yb @ P['w_branch_b'])
         + gates[:, :, 2] * (yc @ P['w_branch_c']))
    return y @ P['w_out'], (kc, vc, st)


def _layer(x, cond, P, lam_init, ctx_k=None, ctx_v=None, ctx_state=None):
    mod = (jax.nn.silu(cond) @ P['w_mod'] + P['b_mod']).reshape(cond.shape[0], N_MOD, D_MODEL)[:, :, None, :]
    h = x * (1.0 + mod[:, 1]) + mod[:, 0]
    x = _layer_norm(ALPHA * x + 0.5 * mod[:, 2] * _swiglu(h, P['ffn_w1'][0], P['ffn_w3'][0], P['ffn_w2'][0]),
                    P['ln_g'][0], P['ln_b'][0])
    h = x * (1.0 + mod[:, 4]) + mod[:, 3]
    y, ctx_new = _mixer(h, P, lam_init, ctx_k, ctx_v, ctx_state)
    x = _layer_norm(ALPHA * x + mod[:, 5] * y, P['ln_g'][1], P['ln_b'][1])
    h = x * (1.0 + mod[:, 7]) + mod[:, 6]
    x = _layer_norm(ALPHA * x + 0.5 * mod[:, 8] * _swiglu(h, P['ffn_w1'][1], P['ffn_w3'][1], P['ffn_w2'][1]),
                    P['ln_g'][2], P['ln_b'][2])
    return x, ctx_new


def setup_inputs(seed: int = 0) -> dict:
    key = jax.random.key(seed)
    keys = iter(jax.random.split(key, 40))

    def nrm(shape, scale=1.0):
        return scale * jax.random.normal(next(keys), shape, jnp.float32)

    D = D_MODEL
    FH = HY_FILTER_HIDDEN
    return {
        'x_prompt': nrm((BATCH, SEQ, D)),
        'x_sample': nrm((DEC_BATCH, DEC_SEQ, D)),
        'cache_k': nrm((DEC_BATCH, DEPTH, PAST_LEN, DIFF_HEADS, 2, DIFF_DH)),
        'cache_v': nrm((DEC_BATCH, DEPTH, PAST_LEN, DIFF_HEADS, DIFF_DV)),
        'state_gla': nrm((DEC_BATCH, DEPTH, 2, GLA_HEADS, GLA_DK, GLA_DV)),
        'c': nrm((DEC_BATCH, D)),
        'c_ctx': nrm((D,)),
        'w_mod': nrm((DEPTH, D, N_MOD * D), D ** -0.5),
        'b_mod': nrm((DEPTH, N_MOD * D), 0.02),
        'ln_g': 1.0 + nrm((DEPTH, 3, D), 0.02),
        'ln_b': nrm((DEPTH, 3, D), 0.02),
        'ffn_w1': nrm((DEPTH, 2, D, FFN_DIM), D ** -0.5),
        'ffn_w3': nrm((DEPTH, 2, D, FFN_DIM), D ** -0.5),
        'ffn_w2': nrm((DEPTH, 2, FFN_DIM, D), BETA * FFN_DIM ** -0.5),
        'w_in': nrm((DEPTH, D, IN_COLS), D ** -0.5),
        'hy_conv_w': nrm((DEPTH, HY_SHORT, 3 * HY_W), HY_SHORT ** -0.5),
        'hy_conv_b': nrm((DEPTH, 3 * HY_W), 0.02),
        'hy_w1': nrm((DEPTH, HY_EMB, FH), HY_EMB ** -0.5),
        'hy_b1': nrm((DEPTH, FH), 0.1),
        'hy_freq': 1.0 + nrm((DEPTH, 2, FH), 0.02),
        'hy_w2': nrm((DEPTH, FH, FH), FH ** -0.5),
        'hy_b2': nrm((DEPTH, FH), 0.1),
        'hy_w3': nrm((DEPTH, FH, 4 * HY_W), HY_FILTER_GAIN * FH ** -0.5),
        'hy_decay': jnp.linspace(HY_DECAY_SLOW, HY_DECAY_FAST, 4 * HY_W, dtype=jnp.float32)[None] + nrm((DEPTH, 4 * HY_W), 0.1),
        'hy_bias': nrm((DEPTH, 2, HY_W)),
        'gla_wa': nrm((DEPTH, 2, GLA_RANK, GLA_HEADS * GLA_DK), GLA_RANK ** -0.5),
        'gla_ba': nrm((DEPTH, 2, GLA_HEADS * GLA_DK), 0.1),
        'gla_norm_g': 1.0 + nrm((DEPTH, GLA_DV), 0.02),
        'diff_lam': nrm((DEPTH, 4, DIFF_DH), 0.1),
        'diff_norm_g': 1.0 + nrm((DEPTH, DIFF_DV), 0.02),
        'w_branch_a': nrm((DEPTH, HY_W, D), HY_W ** -0.5),
        'w_branch_b': nrm((DEPTH, GLA_W, D), GLA_W ** -0.5),
        'w_branch_c': nrm((DEPTH, DIFF_W, D), DIFF_W ** -0.5),
        'w_out': nrm((DEPTH, D, D), BETA * D ** -0.5),
    }


def reference(x_prompt, x_sample, cache_k, cache_v, state_gla, c, c_ctx, w_mod, b_mod, ln_g, ln_b,
              ffn_w1, ffn_w3, ffn_w2, w_in, hy_conv_w, hy_conv_b, hy_w1, hy_b1, hy_freq, hy_w2, hy_b2,
              hy_w3, hy_decay, hy_bias, gla_wa, gla_ba, gla_norm_g, diff_lam, diff_norm_g,
              w_branch_a, w_branch_b, w_branch_c, w_out):
    params = [dict(w_mod=w_mod[l], b_mod=b_mod[l], ln_g=ln_g[l], ln_b=ln_b[l], ffn_w1=ffn_w1[l],
                   ffn_w3=ffn_w3[l], ffn_w2=ffn_w2[l], w_in=w_in[l], hy_conv_w=hy_conv_w[l],
                   hy_conv_b=hy_conv_b[l], hy_w1=hy_w1[l], hy_b1=hy_b1[l], hy_freq=hy_freq[l],
                   hy_w2=hy_w2[l], hy_b2=hy_b2[l], hy_w3=hy_w3[l], hy_decay=hy_decay[l],
                   hy_bias=hy_bias[l], gla_wa=gla_wa[l], gla_ba=gla_ba[l], gla_norm_g=gla_norm_g[l],
                   diff_lam=diff_lam[l], diff_norm_g=diff_norm_g[l], w_branch_a=w_branch_a[l],
                   w_branch_b=w_branch_b[l], w_branch_c=w_branch_c[l], w_out=w_out[l])
              for l in range(DEPTH)]
    lam_inits = [0.8 - 0.6 * math.exp(-0.3 * l) for l in range(DEPTH)]

    xp = x_prompt
    ks, vs, ss = [], [], []
    for l in range(DEPTH):
        xp, (kc, vc, sc) = _layer(xp, c_ctx[None], params[l], lam_inits[l])
        ks.append(kc)
        vs.append(vc)
        ss.append(sc)
    new_cache_k = jnp.stack(ks, axis=1)
    new_cache_v = jnp.stack(vs, axis=1)
    new_state_gla = jnp.stack(ss, axis=1)

    xs = x_sample
    for l in range(DEPTH):
        xs, _ = _layer(xs, c, params[l], lam_inits[l], cache_k[:, l], cache_v[:, l], state_gla[:, l])

    return (xp, xs, new_cache_k, new_cache_v, new_state_gla)
```

```python
import functools
import math

import numpy as np
import jax
import jax.numpy as jnp
from jax import lax
from jax.experimental import pallas as pl
from jax.experimental.pallas import tpu as pltpu

F32 = jnp.float32
BF16 = jnp.bfloat16

GRID_W = 64
N_MOD = 9
HY_BANDS = 8
GLA_TAU = 16.0
GLA_CHUNK = 64
ROPE_THETA = 10000.0
LN_EPS = 1e-5
RMS_EPS = 1e-6

LANES = 128
MXU_DIM = 256
VMEM_BYTES_V7X = 64 * 1024 * 1024
VMEM_LIMIT = VMEM_BYTES_V7X - 8 * 1024 * 1024

GROUP = 4096
FFT_NB = 16
FFT_K1G = 8


def _cp(*sem):
    return pltpu.CompilerParams(dimension_semantics=sem, vmem_limit_bytes=VMEM_LIMIT)


def _dot(a, b):
    return jnp.dot(a, b, preferred_element_type=F32)


def _dot_nt(a, b):
    return lax.dot_general(a, b, (((1,), (1,)), ((), ())), preferred_element_type=F32)


def _dot_tn(a, b):
    return lax.dot_general(a, b, (((0,), (0,)), ((), ())), preferred_element_type=F32)


def _dot_hi(a, b):
    return jnp.dot(a, b, preferred_element_type=F32, precision=lax.Precision.HIGHEST)


def _silu(x):
    return x * jax.nn.sigmoid(x)


def _mod_kernel(c_ref, w_ref, b_ref, o_ref):
    c = c_ref[...]
    o_ref[...] = _dot(_silu(c).astype(BF16), w_ref[...].astype(BF16)) + b_ref[...]


def _modulation(cond, w_mod, b_mod):
    depth, d, n = w_mod.shape
    r = cond.shape[0]
    tn = 1024
    return pl.pallas_call(
        _mod_kernel,
        grid=(depth, n // tn),
        in_specs=[pl.BlockSpec((r, d), lambda l, j: (0, 0)),
                  pl.BlockSpec((None, d, tn), lambda l, j: (l, 0, j)),
                  pl.BlockSpec((None, 1, tn), lambda l, j: (l, 0, j))],
        out_specs=pl.BlockSpec((None, r, tn), lambda l, j: (l, 0, j)),
        out_shape=jax.ShapeDtypeStruct((depth, r, n), F32),
        compiler_params=_cp("parallel", "parallel"),
        name="mod",
    )(cond, w_mod, b_mod.reshape(depth, 1, n))


def _premod_kernel(x_ref, mod_ref, o_ref):
    o_ref[...] = (x_ref[...] * (1.0 + mod_ref[1:2, :]) + mod_ref[0:1, :]).astype(o_ref.dtype)


def _premod(x, mod, layer, tm=512):
    m, d = x.shape
    return pl.pallas_call(
        _premod_kernel,
        grid=(m // tm,),
        in_specs=[pl.BlockSpec((tm, d), lambda i: (i, 0)),
                  pl.BlockSpec((None, None, N_MOD, d), lambda i: (layer, (i * tm) // GROUP, 0, 0))],
        out_specs=pl.BlockSpec((tm, d), lambda i: (i, 0)),
        out_shape=jax.ShapeDtypeStruct((m, d), BF16),
        compiler_params=_cp("parallel"),
        name="premod",
    )(x, mod)


def _ffn1_kernel(h_ref, w1_ref, w3_ref, o_ref):
    h = h_ref[...]
    a = _dot(h, w1_ref[...])
    b = _dot(h, w3_ref[...])
    o_ref[...] = (_silu(a) * b).astype(o_ref.dtype)


def _ffn1(h, w1, w3, tm=1024, tf=512):
    m, d = h.shape
    fp = w1.shape[1]
    return pl.pallas_call(
        _ffn1_kernel,
        grid=(m // tm, fp // tf),
        in_specs=[pl.BlockSpec((tm, d), lambda i, j: (i, 0)),
                  pl.BlockSpec((d, tf), lambda i, j: (0, j)),
                  pl.BlockSpec((d, tf), lambda i, j: (0, j))],
        out_specs=pl.BlockSpec((tm, tf), lambda i, j: (i, j)),
        out_shape=jax.ShapeDtypeStruct((m, fp), BF16),
        compiler_params=_cp("parallel", "arbitrary"),
        name="ffn1",
    )(h, w1, w3)


def _mm_ln_kernel(a_ref, w_ref, x_ref, mod_ref, nmod_ref, g_ref, b_ref, xo_ref, ho_ref, acc_ref, *,
                  nk, gate_row, coef, alpha, nshift_row):
    k = pl.program_id(1)

    @pl.when(k == 0)
    def _():
        acc_ref[...] = jnp.zeros_like(acc_ref)

    acc_ref[...] += _dot(a_ref[...], w_ref[...])

    @pl.when(k == nk - 1)
    def _():
        gate = coef * mod_ref[gate_row:gate_row + 1, :]
        xr = alpha * x_ref[...] + gate * acc_ref[...]
        mu = jnp.mean(xr, axis=-1, keepdims=True)
        xc = xr - mu
        var = jnp.mean(xc * xc, axis=-1, keepdims=True)
        xn = xc * lax.rsqrt(var + LN_EPS) * g_ref[...] + b_ref[...]
        xo_ref[...] = xn
        ho_ref[...] = (xn * (1.0 + nmod_ref[nshift_row + 1:nshift_row + 2, :])
                       + nmod_ref[nshift_row:nshift_row + 1, :]).astype(ho_ref.dtype)


def _mm_ln(a, w, x, mod, layer, gate_row, coef, alpha, nlayer, nshift_row, ln_g, ln_b, tm=512, tk=512):
    m, kdim = a.shape
    d = w.shape[1]
    nk = kdim // tk
    kern = functools.partial(_mm_ln_kernel, nk=nk, gate_row=gate_row, coef=coef, alpha=alpha,
                             nshift_row=nshift_row)
    return pl.pallas_call(
        kern,
        grid=(m // tm, nk),
        in_specs=[pl.BlockSpec((tm, tk), lambda i, k: (i, k)),
                  pl.BlockSpec((tk, d), lambda i, k: (k, 0)),
                  pl.BlockSpec((tm, d), lambda i, k: (i, 0)),
                  pl.BlockSpec((None, None, N_MOD, d), lambda i, k: (layer, (i * tm) // GROUP, 0, 0)),
                  pl.BlockSpec((None, None, N_MOD, d), lambda i, k: (nlayer, (i * tm) // GROUP, 0, 0)),
                  pl.BlockSpec((1, d), lambda i, k: (0, 0)),
                  pl.BlockSpec((1, d), lambda i, k: (0, 0))],
        out_specs=[pl.BlockSpec((tm, d), lambda i, k: (i, 0)),
                   pl.BlockSpec((tm, d), lambda i, k: (i, 0))],
        out_shape=[jax.ShapeDtypeStruct((m, d), F32), jax.ShapeDtypeStruct((m, d), BF16)],
        scratch_shapes=[pltpu.VMEM((tm, d), F32)],
        compiler_params=_cp("parallel", "arbitrary"),
        name="mm_ln",
    )(a, w, x, mod, mod, ln_g.reshape(1, d), ln_b.reshape(1, d))


def _proj_kernel(h_ref, w_ref, o_ref, *, act):
    r = _dot(h_ref[...], w_ref[...])
    if act == "sigmoid":
        r = jax.nn.sigmoid(r)
    o_ref[...] = r.astype(o_ref.dtype)


def _proj(h, w, out_dtype, act=None, tm=1024, tn=1024):
    m, d = h.shape
    n = w.shape[1]
    tn = min(tn, n)
    return pl.pallas_call(
        functools.partial(_proj_kernel, act=act),
        grid=(m // tm, n // tn),
        in_specs=[pl.BlockSpec((tm, d), lambda i, j: (i, 0)),
                  pl.BlockSpec((d, tn), lambda i, j: (0, j))],
        out_specs=pl.BlockSpec((tm, tn), lambda i, j: (i, j)),
        out_shape=jax.ShapeDtypeStruct((m, n), out_dtype),
        compiler_params=_cp("parallel", "arbitrary"),
        name="proj",
    )(h, w)


def _sconv_kernel(z_ref, zp_ref, zn_ref, w_ref, b_ref, o_ref, *, tm, halo, lp, ls):
    i = pl.program_id(0)
    u = z_ref[...].astype(F32)
    prev = zp_ref[...].astype(F32)[halo - 1:halo, :]
    nxt = zn_ref[...].astype(F32)[0:1, :]
    row = lax.broadcasted_iota(jnp.int32, u.shape, 0)
    lseq = jnp.where((i * tm) // GROUP == 0, lp, ls)
    pos = (row + i * tm) & (lseq - 1)
    up = jnp.where(row == 0, prev, pltpu.roll(u, 1, 0))
    up = jnp.where(pos == 0, 0.0, up)
    un = jnp.where(row == tm - 1, nxt, pltpu.roll(u, tm - 1, 0))
    un = jnp.where(pos == lseq - 1, 0.0, un)
    w = w_ref[...]
    o_ref[...] = (up * w[0:1, :] + u * w[1:2, :] + un * w[2:3, :] + b_ref[...]).astype(o_ref.dtype)


def _short_conv(z, conv_w, conv_b, lp, ls, width, tm=1024, ct=512, halo=16):
    m = z.shape[0]
    nct = width // ct
    nrb = m // halo
    kern = functools.partial(_sconv_kernel, tm=tm, halo=halo, lp=lp, ls=ls)
    return pl.pallas_call(
        kern,
        grid=(m // tm, 3 * nct),
        in_specs=[pl.BlockSpec((tm, ct), lambda i, j: (i, j)),
                  pl.BlockSpec((halo, ct), lambda i, j: (jnp.maximum(i * (tm // halo) - 1, 0), j)),
                  pl.BlockSpec((halo, ct), lambda i, j: (jnp.minimum((i + 1) * (tm // halo), nrb - 1), j)),
                  pl.BlockSpec((3, ct), lambda i, j: (0, j)),
                  pl.BlockSpec((1, ct), lambda i, j: (0, j))],
        out_specs=pl.BlockSpec((None, tm, ct), lambda i, j: (j // nct, i, j % nct)),
        out_shape=jax.ShapeDtypeStruct((3, m, width), BF16),
        compiler_params=_cp("parallel", "parallel"),
        name="sconv",
    )(z, z, z, conv_w, conv_b.reshape(1, -1))


def _filter_features(L):
    t = np.linspace(0.0, 1.0, L, dtype=np.float32)
    w = (np.float32(2.0 * math.pi / L) * np.arange(L, dtype=np.float32)).astype(np.float32)
    f = np.linspace(1e-4, HY_BANDS - 1, HY_BANDS, dtype=np.float32)
    wf = (w[:, None] * f).astype(np.float32)
    feats = np.concatenate([t[:, None], np.cos(wf), -np.sin(wf)], -1).astype(np.float32)
    idx = np.concatenate([np.arange(L), [0], np.arange(L - 1, 0, -1)])
    tab = np.zeros((2 * L, 32), np.float32)
    tab[:, :feats.shape[1]] = feats[idx]
    tab[:, 24] = t[idx]
    tab[:L, 25] = 1.0
    tab[L + 1:, 26] = -1.0
    return tab, feats.shape[1]


def _filter_kernel(tab_ref, w1_ref, b1_ref, fr_ref, w2_ref, b2_ref, w3_ref, dec_ref, o_ref, *, width):
    tab = tab_ref[...]
    fr = fr_ref[...]
    hdn = jnp.sin(fr[0:1, :] * (_dot_hi(tab, w1_ref[...]) + b1_ref[...]))
    hdn = jnp.sin(fr[1:2, :] * (_dot_hi(hdn, w2_ref[...]) + b2_ref[...]))
    t = tab[:, 24:25]
    mf = tab[:, 25:26]
    mb = tab[:, 26:27]
    h = _dot_hi(hdn, w3_ref[...]) * jnp.exp(-t * jnp.abs(dec_ref[...]))
    for o in range(2):
        base = 2 * width * o
        o_ref[o] = mf * h[:, base:base + width] + mb * h[:, base + width:base + 2 * width]


def _hyena_filter_taps(L, hy_w1, hy_b1, hy_freq, hy_w2, hy_b2, hy_w3, hy_decay, width, rb=256):
    tab_np, nfeat = _filter_features(L)
    fh = hy_w1.shape[1]
    w1p = jnp.zeros((32, fh), F32).at[:nfeat].set(hy_w1)
    n = 2 * L
    return pl.pallas_call(
        functools.partial(_filter_kernel, width=width),
        grid=(n // rb,),
        in_specs=[pl.BlockSpec((rb, 32), lambda i: (i, 0)),
                  pl.BlockSpec((32, fh), lambda i: (0, 0)),
                  pl.BlockSpec((1, fh), lambda i: (0, 0)),
                  pl.BlockSpec((2, fh), lambda i: (0, 0)),
                  pl.BlockSpec((fh, fh), lambda i: (0, 0)),
                  pl.BlockSpec((1, fh), lambda i: (0, 0)),
                  pl.BlockSpec((fh, 4 * width), lambda i: (0, 0)),
                  pl.BlockSpec((1, 4 * width), lambda i: (0, 0))],
        out_specs=pl.BlockSpec((2, rb, width), lambda i: (0, i, 0)),
        out_shape=jax.ShapeDtypeStruct((2, n, width), F32),
        compiler_params=_cp("parallel"),
        name="hyfilter",
    )(jnp.asarray(tab_np), w1p, hy_b1.reshape(1, fh), hy_freq, hy_w2, hy_b2.reshape(1, fh), hy_w3,
      hy_decay.reshape(1, -1))


def _dft_a(na, ka):
    k1 = np.arange(na // 2)[:, None].astype(np.float64)
    a = np.arange(ka)[None, :].astype(np.float64)
    th = 2.0 * np.pi * a * (k1 + 0.5) / na
    return np.concatenate([np.cos(th), -np.sin(th)], 0)


def _idft_a(na, n, rows):
    k1 = np.arange(na // 2)[None, :].astype(np.float64)
    a = np.arange(rows)[:, None].astype(np.float64)
    th = 2.0 * np.pi * a * (k1 + 0.5) / na
    return (2.0 / n) * np.concatenate([np.cos(th), -np.sin(th)], 1)


def _dft_b(na, nb, g):
    n = na * nb
    half = na // 2
    k1 = np.arange(half).astype(np.float64)
    b = np.arange(nb).astype(np.float64)
    k2 = np.arange(nb).astype(np.float64)
    phi = 2.0 * np.pi * (b[None, None, :] * k2[None, :, None] / nb
                         + b[None, None, :] * (k1[:, None, None] + 0.5) / n)
    c, s = np.cos(phi), np.sin(phi)
    ng = half // g
    r = g * nb
    fwd = np.zeros((ng, 2 * r, 2 * r))
    for q in range(g):
        sl = slice(q * nb, (q + 1) * nb)
        sl_i = slice(r + q * nb, r + (q + 1) * nb)
        cq, sq = c[q::g], s[q::g]
        fwd[:, sl, sl] = cq
        fwd[:, sl, sl_i] = sq
        fwd[:, sl_i, sl] = -sq
        fwd[:, sl_i, sl_i] = cq
    inv = np.transpose(fwd, (0, 2, 1))
    return fwd, inv


def _lmm_kernel(f_ref, x_ref, o_ref):
    o_ref[...] = _dot(f_ref[...], x_ref[...].astype(BF16)).astype(o_ref.dtype)


def _lmm(f, x, x_index, nbatch, out_dtype, tn=2048):
    mo, k = f.shape
    n = x.shape[-1]
    lead = len(x.shape) - 2
    return pl.pallas_call(
        _lmm_kernel,
        grid=(nbatch, n // tn),
        in_specs=[pl.BlockSpec((mo, k), lambda b, j: (0, 0)),
                  pl.BlockSpec((None,) * lead + (k, tn), lambda b, j: x_index(b) + (0, j))],
        out_specs=pl.BlockSpec((None, mo, tn), lambda b, j: (b, 0, j)),
        out_shape=jax.ShapeDtypeStruct((nbatch, mo, n), out_dtype),
        compiler_params=_cp("parallel", "parallel"),
        name="dft_a",
    )(f, x)


def _stage_b_kernel(a_ref, tb_ref, tbi_ref, kf_ref, o_ref, *, r):
    ct = a_ref.shape[-1]
    a = a_ref[...].reshape(2 * r, ct)
    x = _dot(tb_ref[...], a)
    kf = kf_ref[...].reshape(2 * r, ct)
    xr, xi = x[:r], x[r:]
    kr, ki = kf[:r], kf[r:]
    y = jnp.concatenate([xr * kr - xi * ki, xr * ki + xi * kr], axis=0).astype(BF16)
    o_ref[...] = _dot(tbi_ref[...], y).astype(o_ref.dtype).reshape(2, r, ct)


def _stage_b(a, tb, tbi, kf, order):
    nbatch, _, rows, c = a.shape
    ng, r2, _ = tb.shape
    r = r2 // 2
    return pl.pallas_call(
        functools.partial(_stage_b_kernel, r=r),
        grid=(ng, nbatch),
        in_specs=[pl.BlockSpec((None, 2, r, c), lambda g, b: (b, 0, g, 0)),
                  pl.BlockSpec((None, r2, r2), lambda g, b: (g, 0, 0)),
                  pl.BlockSpec((None, r2, r2), lambda g, b: (g, 0, 0)),
                  pl.BlockSpec((None, 2, r, c), lambda g, b: (order, 0, g, 0))],
        out_specs=pl.BlockSpec((None, 2, r, c), lambda g, b: (b, 0, g, 0)),
        out_shape=jax.ShapeDtypeStruct(a.shape, BF16),
        compiler_params=_cp("parallel", "arbitrary"),
        name="dft_b",
    )(a, tb, tbi, kf)


def _stage_b_fwd_kernel(a_ref, tb_ref, o_ref, *, r):
    ct = a_ref.shape[-1]
    o_ref[...] = _dot(tb_ref[...], a_ref[...].reshape(2 * r, ct)).reshape(2, r, ct)


def _stage_b_fwd(a, tb):
    nbatch, _, rows, c = a.shape
    ng, r2, _ = tb.shape
    r = r2 // 2
    return pl.pallas_call(
        functools.partial(_stage_b_fwd_kernel, r=r),
        grid=(ng, nbatch),
        in_specs=[pl.BlockSpec((None, 2, r, c), lambda g, b: (b, 0, g, 0)),
                  pl.BlockSpec((None, r2, r2), lambda g, b: (g, 0, 0))],
        out_specs=pl.BlockSpec((None, 2, r, c), lambda g, b: (b, 0, g, 0)),
        out_shape=jax.ShapeDtypeStruct(a.shape, F32),
        compiler_params=_cp("parallel", "arbitrary"),
        name="dft_b_filter",
    )(a, tb)


def _idft_gate_kernel(g_ref, b_ref, v_ref, xg_ref, bias_ref, o_ref):
    y = _dot(g_ref[...], b_ref[...])
    v = v_ref[...].astype(F32)
    o_ref[...] = (xg_ref[...].astype(F32) * (y + v * bias_ref[...])).astype(o_ref.dtype)


def _idft_gate(g, bh, vsrc, v_index, xsrc, x_index, bias_t, nbatch, tn=2048):
    mo, k = g.shape
    n = bh.shape[-1]
    lv = len(vsrc.shape) - 2
    lx = len(xsrc.shape) - 2
    return pl.pallas_call(
        _idft_gate_kernel,
        grid=(nbatch, n // tn),
        in_specs=[pl.BlockSpec((mo, k), lambda b, j: (0, 0)),
                  pl.BlockSpec((None, k, tn), lambda b, j: (b, 0, j)),
                  pl.BlockSpec((None,) * lv + (mo, tn), lambda b, j: v_index(b) + (0, j)),
                  pl.BlockSpec((None,) * lx + (mo, tn), lambda b, j: x_index(b) + (0, j)),
                  pl.BlockSpec((1, tn), lambda b, j: (0, j))],
        out_specs=pl.BlockSpec((None, mo, tn), lambda b, j: (b, 0, j)),
        out_shape=jax.ShapeDtypeStruct((nbatch, mo, n), BF16),
        compiler_params=_cp("parallel", "parallel"),
        name="idft_a",
    )(g, bh, vsrc, xsrc, bias_t)


def _pconv_kernel(fd_ref, gd_ref, v_ref, kf_ref, xg_ref, bias_ref, o_ref, *, half):
    v = v_ref[...]
    x = _dot(fd_ref[...], v)
    kf = kf_ref[...]
    xr, xi = x[:half], x[half:]
    kr, ki = kf[:half], kf[half:]
    y = jnp.concatenate([xr * kr - xi * ki, xr * ki + xi * kr], axis=0).astype(BF16)
    yt = _dot(gd_ref[...], y)
    o_ref[...] = (xg_ref[...].astype(F32) * (yt + v.astype(F32) * bias_ref[...])).astype(o_ref.dtype)


def _pconv(fd, gd, vsrc, v_lead, kf, order, xsrc, x_lead, bias, nseq, L, ct=512):
    c = kf.shape[-1]
    n2 = fd.shape[0]
    return pl.pallas_call(
        functools.partial(_pconv_kernel, half=n2 // 2),
        grid=(nseq, c // ct),
        in_specs=[pl.BlockSpec((n2, L), lambda s, j: (0, 0)),
                  pl.BlockSpec((L, n2), lambda s, j: (0, 0)),
                  pl.BlockSpec((None,) * len(v_lead) + (L, ct), lambda s, j: v_lead + (s, j)),
                  pl.BlockSpec((None, n2, ct), lambda s, j: (order, 0, j)),
                  pl.BlockSpec((None,) * len(x_lead) + (L, ct), lambda s, j: x_lead + (s, j)),
                  pl.BlockSpec((1, ct), lambda s, j: (0, j))],
        out_specs=pl.BlockSpec((L, ct), lambda s, j: (s, j)),
        out_shape=jax.ShapeDtypeStruct((nseq * L, c), BF16),
        compiler_params=_cp("parallel", "parallel"),
        name="pconv",
    )(fd, gd, vsrc, kf, xsrc, bias.reshape(1, c))


def _gate_kernel(lr_ref, wa_ref, ba_ref, o_ref):
    logits = _dot_hi(lr_ref[...], wa_ref[...]) + ba_ref[...]
    o_ref[...] = jax.nn.log_sigmoid(logits) * (1.0 / GLA_TAU)


def _gla_gates(lr, wa_cat, ba_cat, tm=1024):
    m, k = lr.shape
    n = wa_cat.shape[1]
    return pl.pallas_call(
        _gate_kernel,
        grid=(m // tm,),
        in_specs=[pl.BlockSpec((tm, k), lambda i: (i, 0)),
                  pl.BlockSpec((k, n), lambda i: (0, 0)),
                  pl.BlockSpec((1, n), lambda i: (0, 0))],
        out_specs=pl.BlockSpec((tm, n), lambda i: (i, 0)),
        out_shape=jax.ShapeDtypeStruct((m, n), F32),
        compiler_params=_cp("parallel"),
        name="gla_gates",
    )(lr, wa_cat, ba_cat)


def _split3(x):
    hi = x.astype(BF16)
    r1 = x - hi.astype(F32)
    mid = r1.astype(BF16)
    lo = (r1 - mid.astype(F32)).astype(BF16)
    return hi, mid, lo


def _gla_dir(d, r0, q_ref, k_ref, v_ref, la_ref, o_ref, st_ref, tri, causal, ref_row, last_row,
             heads, dk, dv, scale):
    ch = GLA_CHUNK
    rows = pl.ds(pl.multiple_of(r0, ch), ch)
    la = la_ref[rows, :]
    hi, mid, lo = _split3(la)
    b = _dot(tri, hi) + _dot(tri, mid) + _dot(tri, lo)
    bref = b[ref_row:ref_row + 1, :]
    blast = b[last_row:last_row + 1, :]
    q = q_ref[rows, :].astype(F32) * scale
    k = k_ref[rows, :].astype(F32)
    qt = (q * jnp.exp(b - bref)).astype(BF16)
    kt = (k * jnp.exp(bref - b)).astype(BF16)
    qin = (q * jnp.exp(b)).astype(BF16)
    kst = (k * jnp.exp(blast - b)).astype(BF16)
    dec = jnp.exp(blast)
    for h in range(heads):
        ks = slice(h * dk, (h + 1) * dk)
        vs = slice(h * dv, (h + 1) * dv)
        att = _dot_nt(qt[:, ks], kt[:, ks])
        att = jnp.where(causal, att, 0.0).astype(BF16)
        vh = v_ref[rows, vs]
        st = st_ref[d, h]
        o_ref[rows, vs] = _dot(att, vh) + _dot_nt(qin[:, ks], st.astype(BF16))
        st_ref[d, h] = st * dec[:, ks] + _dot_tn(vh, kst[:, ks])


def _gla_kernel(*refs, nch, heads, dk, dv, has_s0, scale):
    if has_s0:
        (qf, kf, vf, laf, qb, kb, vb, lab, s0, of, ob, st) = refs
    else:
        (qf, kf, vf, laf, qb, kb, vb, lab, of, ob, st) = refs
        s0 = None
    ch = GLA_CHUNK

    @pl.when(pl.program_id(1) == 0)
    def _():
        if has_s0:
            st[...] = s0[...]
        else:
            st[...] = jnp.zeros_like(st)

    r_i = lax.broadcasted_iota(jnp.int32, (ch, ch), 0)
    c_i = lax.broadcasted_iota(jnp.int32, (ch, ch), 1)
    lower = r_i >= c_i
    upper = r_i <= c_i
    tri_l = jnp.where(lower, 1.0, 0.0).astype(BF16)
    tri_u = jnp.where(upper, 1.0, 0.0).astype(BF16)

    def body(c, carry):
        _gla_dir(0, c * ch, qf, kf, vf, laf, of, st, tri_l, lower, ch // 2 - 1, ch - 1,
                 heads, dk, dv, scale)
        _gla_dir(1, (nch - 1 - c) * ch, qb, kb, vb, lab, ob, st, tri_u, upper, ch // 2, 0,
                 heads, dk, dv, scale)
        return carry

    lax.fori_loop(0, nch, body, 0)


def _gla(zmain, qcol, kcol, vcol, la, row0, nseq, L, tb, heads, dk, dv, s0t=None):
    nblk = L // tb
    rb0 = row0 // tb
    hk, hv = heads * dk, heads * dv

    def fwd(s, j):
        return rb0 + s * nblk + j

    def bwd(s, j):
        return rb0 + s * nblk + (nblk - 1 - j)

    in_specs = []
    for rowf, lcol in ((fwd, 0), (bwd, 1)):
        in_specs += [pl.BlockSpec((tb, hk), lambda s, j, rowf=rowf: (rowf(s, j), qcol)),
                     pl.BlockSpec((tb, hk), lambda s, j, rowf=rowf: (rowf(s, j), kcol)),
                     pl.BlockSpec((tb, hv), lambda s, j, rowf=rowf: (rowf(s, j), vcol)),
                     pl.BlockSpec((tb, hk), lambda s, j, rowf=rowf, lcol=lcol: (rowf(s, j), lcol))]
    args = [zmain, zmain, zmain, la, zmain, zmain, zmain, la]
    if s0t is not None:
        in_specs.append(pl.BlockSpec((None, 2, heads, dv, dk), lambda s, j: (s, 0, 0, 0, 0)))
        args.append(s0t)
    kern = functools.partial(_gla_kernel, nch=tb // GLA_CHUNK, heads=heads, dk=dk, dv=dv,
                             has_s0=s0t is not None, scale=dk ** -0.5)
    return pl.pallas_call(
        kern,
        grid=(nseq, nblk),
        in_specs=in_specs,
        out_specs=[pl.BlockSpec((tb, hv), lambda s, j: (s * nblk + j, 0)),
                   pl.BlockSpec((tb, hv), lambda s, j: (s * nblk + (nblk - 1 - j), 0)),
                   pl.BlockSpec((None, 2, heads, dv, dk), lambda s, j: (s, 0, 0, 0, 0))],
        out_shape=[jax.ShapeDtypeStruct((nseq * L, hv), F32),
                   jax.ShapeDtypeStruct((nseq * L, hv), F32),
                   jax.ShapeDtypeStruct((nseq, 2, heads, dv, dk), F32)],
        compiler_params=_cp("parallel", "arbitrary"),
        name="gla",
    )(*args)


def _gla_post_kernel(of_ref, ob_ref, gr_ref, g_ref, o_ref, *, heads, dv):
    o = of_ref[...] + ob_ref[...]
    gate = _silu(gr_ref[...].astype(F32))
    g = g_ref[...]
    for h in range(heads):
        sl = slice(h * dv, (h + 1) * dv)
        oh = o[:, sl]
        ms = jnp.mean(oh * oh, axis=-1, keepdims=True)
        o_ref[:, sl] = (oh * lax.rsqrt(ms + RMS_EPS) * g * gate[:, sl]).astype(o_ref.dtype)


def _gla_post(o_f, o_b, zmain, grcol, norm_g, heads, dv, tm=512):
    m, hv = o_f.shape
    return pl.pallas_call(
        functools.partial(_gla_post_kernel, heads=heads, dv=dv),
        grid=(m // tm,),
        in_specs=[pl.BlockSpec((tm, hv), lambda i: (i, 0)),
                  pl.BlockSpec((tm, hv), lambda i: (i, 0)),
                  pl.BlockSpec((tm, hv), lambda i: (i, grcol)),
                  pl.BlockSpec((1, dv), lambda i: (0, 0))],
        out_specs=pl.BlockSpec((tm, hv), lambda i: (i, 0)),
        out_shape=jax.ShapeDtypeStruct((m, hv), BF16),
        compiler_params=_cp("parallel"),
        name="gla_post",
    )(o_f, o_b, zmain, norm_g.reshape(1, dv))


def _rope_tables(L, dh):
    rows = L // GRID_W
    r = np.repeat(np.arange(rows, dtype=np.float32), GRID_W)
    col = np.tile(np.arange(GRID_W, dtype=np.float32), rows)
    nf = dh // 4
    inv = (np.float32(ROPE_THETA) ** (-np.arange(nf, dtype=np.float32) / nf)).astype(np.float32)
    ang_r = (r[:, None] * inv).astype(np.float32)
    ang_c = (col[:, None] * inv).astype(np.float32)
    cos = np.concatenate([np.cos(ang_r), np.cos(ang_r), np.cos(ang_c), np.cos(ang_c)], -1)
    sin = np.concatenate([-np.sin(ang_r), np.sin(ang_r), -np.sin(ang_c), np.sin(ang_c)], -1)
    cos = np.concatenate([cos, cos], -1).astype(np.float32)
    sin = np.concatenate([sin, sin], -1).astype(np.float32)
    cos_t = np.stack([np.ones_like(cos), cos])
    sin_t = np.stack([np.zeros_like(sin), sin])
    return cos_t, sin_t


def _qkv_prep_kernel(q_ref, k_ref, v_ref, cos_ref, sin_ref, qo_ref, ko_ref, vo_ref, *, heads, hw, nf, scale):
    cos = cos_ref[...]
    sin = sin_ref[...]
    lane = lax.broadcasted_iota(jnp.int32, cos.shape, 1)
    first = (lane & (2 * nf - 1)) < nf

    def rope(x):
        sw = jnp.where(first, pltpu.roll(x, hw - nf, 1), pltpu.roll(x, nf, 1))
        return x * cos + sw * sin

    for h in range(heads):
        sl = slice(h * hw, (h + 1) * hw)
        qo_ref[:, sl] = (rope(q_ref[:, sl]) * scale).astype(qo_ref.dtype)
        ko_ref[:, sl] = rope(k_ref[:, sl]).astype(ko_ref.dtype)
    vo_ref[...] = v_ref[...].astype(vo_ref.dtype)


def _qkv_prep(zd, cos_t, sin_t, heads, dh, tm=512):
    m = zd.shape[0]
    w = zd.shape[1] // 3
    hw = w // heads
    gb = GROUP // tm
    kern = functools.partial(_qkv_prep_kernel, heads=heads, hw=hw, nf=dh // 4, scale=dh ** -0.5)
    tab = pl.BlockSpec((None, tm, hw), lambda i: (jnp.minimum(i // gb, 1), i % gb, 0))
    out = jax.ShapeDtypeStruct((m, w), BF16)
    return pl.pallas_call(
        kern,
        grid=(m // tm,),
        in_specs=[pl.BlockSpec((tm, w), lambda i: (i, 0)),
                  pl.BlockSpec((tm, w), lambda i: (i, 1)),
                  pl.BlockSpec((tm, w), lambda i: (i, 2)),
                  tab, tab],
        out_specs=[pl.BlockSpec((tm, w), lambda i: (i, 0))] * 3,
        out_shape=[out, out, out],
        compiler_params=_cp("parallel"),
        name="qkv_prep",
    )(zd, zd, zd, cos_t, sin_t)


def _attn_kernel(*refs, has_ctx, lam_init, dh):
    if has_ctx:
        q_ref, k_ref, v_ref, kc_ref, vc_ref, lam_ref, g_ref, o_ref = refs
    else:
        q_ref, k_ref, v_ref, lam_ref, g_ref, o_ref = refs
    q = q_ref[...]
    lane = lax.broadcasted_iota(jnp.int32, q.shape, 1)
    first = lane < dh
    zero = jnp.zeros_like(q)
    k = k_ref[...]
    v = v_ref[...]
    if has_ctx:
        kc = kc_ref[...].astype(BF16)
        vc = vc_ref[...].astype(BF16)

    def softmax_pv(qj):
        s = _dot_nt(qj, k)
        m = jnp.max(s, axis=-1, keepdims=True)
        if has_ctx:
            sc = _dot_nt(qj, kc)
            m = jnp.maximum(m, jnp.max(sc, axis=-1, keepdims=True))
        p = jnp.exp(s - m)
        l = jnp.sum(p, axis=-1, keepdims=True)
        acc = _dot(p.astype(BF16), v)
        if has_ctx:
            pc = jnp.exp(sc - m)
            l = l + jnp.sum(pc, axis=-1, keepdims=True)
            acc = acc + _dot(pc.astype(BF16), vc)
        return acc / l

    lp = lam_ref[...]
    lam = (jnp.exp(jnp.sum(lp[0:1] * lp[1:2], axis=-1, keepdims=True))
           - jnp.exp(jnp.sum(lp[2:3] * lp[3:4], axis=-1, keepdims=True)) + lam_init)
    o = softmax_pv(jnp.where(first, q, zero)) - lam * softmax_pv(jnp.where(first, zero, q))
    ms = jnp.mean(o * o, axis=-1, keepdims=True)
    o_ref[...] = (o * lax.rsqrt(ms + RMS_EPS) * g_ref[...] * (1.0 - lam_init)).astype(o_ref.dtype)


def _diff_attention(qs, kr, vb, row0, nseq, L, tq, heads, dh, lam_p, norm_g, lam_init, ctx=None):
    hw = 2 * dh
    nqb = L // tq
    qb0 = row0 // tq
    kb0 = row0 // L
    in_specs = [pl.BlockSpec((tq, hw), lambda s, h, i: (qb0 + s * nqb + i, h)),
                pl.BlockSpec((L, hw), lambda s, h, i: (kb0 + s, h)),
                pl.BlockSpec((L, hw), lambda s, h, i: (kb0 + s, h))]
    args = [qs, kr, vb]
    if ctx is not None:
        ck, cv, layer = ctx
        past = ck.shape[2]
        in_specs += [pl.BlockSpec((None, None, past, hw), lambda s, h, i: (s, layer, 0, h)),
                     pl.BlockSpec((None, None, past, hw), lambda s, h, i: (s, layer, 0, h))]
        args += [ck, cv]
    in_specs += [pl.BlockSpec(lam_p.shape, lambda s, h, i: (0, 0)),
                 pl.BlockSpec((1, hw), lambda s, h, i: (0, 0))]
    args += [lam_p, norm_g.reshape(1, hw)]
    kern = functools.partial(_attn_kernel, has_ctx=ctx is not None, lam_init=lam_init, dh=dh)
    return pl.pallas_call(
        kern,
        grid=(nseq, heads, nqb),
        in_specs=in_specs,
        out_specs=pl.BlockSpec((tq, hw), lambda s, h, i: (s * nqb + i, h)),
        out_shape=jax.ShapeDtypeStruct((nseq * L, heads * hw), BF16),
        compiler_params=_cp("parallel", "parallel", "arbitrary"),
        name="diff_attn",
    )(*args)


def _mix_kernel(ya_ref, yb_ref, yc_ref, g_ref, w_ref, o_ref, acc_ref):
    k = pl.program_id(1)

    def contrib(y_ref):
        return g_ref[...].astype(F32) * _dot(y_ref[...], w_ref[...])

    @pl.when(k == 0)
    def _():
        acc_ref[...] = contrib(ya_ref)

    @pl.when(k == 1)
    def _():
        acc_ref[...] += contrib(yb_ref)

    @pl.when(k == 2)
    def _():
        o_ref[...] = (acc_ref[...] + contrib(yc_ref)).astype(o_ref.dtype)


def _mix(ya, yb, yc, gates, wbr, tm=512):
    m, w = ya.shape
    d = wbr.shape[2]
    return pl.pallas_call(
        _mix_kernel,
        grid=(m // tm, 3),
        in_specs=[pl.BlockSpec((tm, w), lambda i, k: (i, 0)),
                  pl.BlockSpec((tm, w), lambda i, k: (i, 0)),
                  pl.BlockSpec((tm, w), lambda i, k: (i, 0)),
                  pl.BlockSpec((tm, d), lambda i, k: (i, k)),
                  pl.BlockSpec((None, w, d), lambda i, k: (k, 0, 0))],
        out_specs=pl.BlockSpec((tm, d), lambda i, k: (i, 0)),
        out_shape=jax.ShapeDtypeStruct((m, d), BF16),
        scratch_shapes=[pltpu.VMEM((tm, d), F32)],
        compiler_params=_cp("parallel", "arbitrary"),
        name="mix",
    )(ya, yb, yc, gates, wbr)


def _pad_cols(w, n):
    return jnp.pad(w, ((0, 0), (0, n - w.shape[1])))


def kernel(x_prompt, x_sample, cache_k, cache_v, state_gla, c, c_ctx, w_mod, b_mod, ln_g, ln_b, ffn_w1, ffn_w3, ffn_w2, w_in, hy_conv_w, hy_conv_b, hy_w1, hy_b1, hy_freq, hy_w2, hy_b2, hy_w3, hy_decay, hy_bias, gla_wa, gla_ba, gla_norm_g, diff_lam, diff_norm_g, w_branch_a, w_branch_b, w_branch_c, w_out):
    batch, seq, d = x_prompt.shape
    dec_batch, dec_seq, _ = x_sample.shape
    depth = w_mod.shape[0]
    ffn_dim = ffn_w1.shape[3]
    hy_w = hy_bias.shape[2]
    heads_g, dk_g = 4, gla_wa.shape[3] // 4
    dv_g = gla_norm_g.shape[1]
    rank = gla_wa.shape[2]
    dh = diff_lam.shape[2]
    heads_d = cache_k.shape[3]
    dw = heads_d * 2 * dh
    gw = heads_g * dv_g
    gk = heads_g * dk_g
    assert batch * seq == GROUP and dec_seq == GROUP
    mp = batch * seq
    m = mp + dec_batch * dec_seq
    ngroups = 1 + dec_batch
    alpha = (2 * depth) ** 0.25
    lam_inits = [0.8 - 0.6 * math.exp(-0.3 * l) for l in range(depth)]

    c_main = 3 * hy_w + 2 * gk + 2 * gw
    c_lr = c_main
    c_d = c_lr + 2 * rank
    c_g = c_d + 3 * dw

    cond = jnp.concatenate([c_ctx[None], c, jnp.zeros((16 - ngroups, d), F32)], axis=0)
    mod = _modulation(cond, w_mod, b_mod)[:, :ngroups].reshape(depth, ngroups, N_MOD, d)

    na = 2 * dec_seq // FFT_NB
    n_s = 2 * dec_seq
    f1_half = jnp.asarray(_dft_a(na, na // 2), BF16)
    f1_full = jnp.asarray(_dft_a(na, na), BF16)
    g_s = jnp.asarray(_idft_a(na, n_s, na // 2), BF16)
    tb_np, tbi_np = _dft_b(na, FFT_NB, FFT_K1G)
    tb, tbi = jnp.asarray(tb_np, BF16), jnp.asarray(tbi_np, BF16)
    fp_half = jnp.asarray(_dft_a(2 * seq, seq), BF16)
    fp_full = jnp.asarray(_dft_a(2 * seq, 2 * seq), BF16)
    g_p = jnp.asarray(_idft_a(2 * seq, 2 * seq, seq), BF16)
    cos_t, sin_t = _rope_tables(dec_seq, dh)
    cos_t, sin_t = jnp.asarray(cos_t), jnp.asarray(sin_t)
    half_rows = (na // 2) * FFT_NB
    lanes_s = FFT_NB * hy_w

    x = jnp.concatenate([x_prompt.reshape(mp, d), x_sample.reshape(dec_batch * dec_seq, d)], axis=0)
    h = _premod(x, mod, 0)

    fp = ((ffn_dim + 511) // 512) * 512
    ck = cache_k.reshape(dec_batch, depth, cache_k.shape[2], dw)
    cv = cache_v.reshape(dec_batch, depth, cache_v.shape[2], dw)
    new_k, new_v, new_s = [], [], []

    for l in range(depth):
        w1 = _pad_cols(ffn_w1[l, 0].astype(BF16), fp)
        w3 = _pad_cols(ffn_w3[l, 0].astype(BF16), fp)
        w2 = jnp.pad(ffn_w2[l, 0].astype(BF16), ((0, fp - ffn_dim), (0, 0)))
        hid = _ffn1(h, w1, w3)
        x, h = _mm_ln(hid, w2, x, mod, l, 2, 0.5, alpha, l, 3, ln_g[l, 0], ln_b[l, 0])

        wi = w_in[l]
        zmain = _proj(h, wi[:, :c_main].astype(BF16), BF16)
        zlr = _proj(h, _pad_cols(wi[:, c_lr:c_d].astype(BF16), LANES), F32)
        zd = _proj(h, wi[:, c_d:c_g].astype(BF16), F32)
        gates = _proj(h, wi[:, c_g:].astype(BF16), BF16, act="sigmoid")

        u3 = _short_conv(zmain, hy_conv_w[l], hy_conv_b[l], seq, dec_seq, hy_w)
        u3v = u3.reshape(3, ngroups, na // 2, lanes_s)
        fargs = (hy_w1[l], hy_b1[l], hy_freq[l], hy_w2[l], hy_b2[l], hy_w3[l], hy_decay[l], hy_w)
        taps_s = _hyena_filter_taps(dec_seq, *fargs)
        taps_p = _hyena_filter_taps(seq, *fargs)
        kf_a = _lmm(f1_full, taps_s.reshape(2, na, lanes_s), lambda b: (b,), 2, BF16)
        kf_s = _stage_b_fwd(kf_a.reshape(2, 2, half_rows, hy_w), tb)
        kf_p = _lmm(fp_full, taps_p, lambda b: (b,), 2, F32, tn=hy_w)

        src, src_index = u3v, (lambda b: (0, b + 1))
        for o in range(2):
            a_s = _lmm(f1_half, src, src_index, dec_batch, BF16)
            bh = _stage_b(a_s.reshape(dec_batch, 2, half_rows, hy_w), tb, tbi, kf_s, o)
            bias_t = jnp.tile(hy_bias[l, o], FFT_NB).reshape(1, lanes_s)
            zc = _idft_gate(g_s, bh.reshape(dec_batch, na, lanes_s), src, src_index, u3v,
                            (lambda b, o=o: (1 + o, b + 1)), bias_t, dec_batch)
            src, src_index = zc, (lambda b: (b,))
        ya_s = zc.reshape(dec_batch * dec_seq, hy_w)
        z1 = _pconv(fp_half, g_p, u3, (0,), kf_p, 0, u3, (1,), hy_bias[l, 0], batch, seq)
        ya_p = _pconv(fp_half, g_p, z1, (), kf_p, 1, u3, (2,), hy_bias[l, 1], batch, seq)
        ya = jnp.concatenate([ya_p, ya_s], axis=0)

        wa_cat = jnp.zeros((LANES, 2 * gk), F32)
        wa_cat = wa_cat.at[:rank, :gk].set(gla_wa[l, 0]).at[rank:2 * rank, gk:].set(gla_wa[l, 1])
        ba_cat = jnp.concatenate([gla_ba[l, 0], gla_ba[l, 1]]).reshape(1, 2 * gk)
        la = _gla_gates(zlr, wa_cat, ba_cat)
        qcol, kcol, vcol, grcol = (3 * hy_w) // gk, (3 * hy_w) // gk + 1, (3 * hy_w + 2 * gk) // gw, \
            (3 * hy_w + 2 * gk) // gw + 1
        s0t = jnp.swapaxes(state_gla[:, l], -1, -2)
        of_p, ob_p, st_p = _gla(zmain, qcol, kcol, vcol, la, 0, batch, seq, seq, heads_g, dk_g, dv_g)
        of_s, ob_s, _ = _gla(zmain, qcol, kcol, vcol, la, mp, dec_batch, dec_seq, 512, heads_g, dk_g, dv_g,
                             s0t=s0t)
        o_f = jnp.concatenate([of_p, of_s], axis=0)
        o_b = jnp.concatenate([ob_p, ob_s], axis=0)
        yb = _gla_post(o_f, o_b, zmain, grcol, gla_norm_g[l], heads_g, dv_g)
        new_s.append(jnp.swapaxes(st_p, -1, -2))

        qs, kr, vb = _qkv_prep(zd, cos_t, sin_t, heads_d, dh)
        yc_p = _diff_attention(qs, kr, vb, 0, batch, seq, seq, heads_d, dh, diff_lam[l], diff_norm_g[l],
                               lam_inits[l])
        yc_s = _diff_attention(qs, kr, vb, mp, dec_batch, dec_seq, 256, heads_d, dh, diff_lam[l],
                               diff_norm_g[l], lam_inits[l], ctx=(ck, cv, l))
        yc = jnp.concatenate([yc_p, yc_s], axis=0)
        new_k.append(zd[:mp, dw:2 * dw].reshape(batch, seq, heads_d, 2, dh))
        new_v.append(zd[:mp, 2 * dw:].reshape(batch, seq, heads_d, 2 * dh))

        wbr = jnp.stack([w_branch_a[l], w_branch_b[l], w_branch_c[l]]).astype(BF16)
        y = _mix(ya, yb, yc, gates, wbr)
        x, h = _mm_ln(y, w_out[l].astype(BF16), x, mod, l, 5, 1.0, alpha, l, 6, ln_g[l, 1], ln_b[l, 1],
                      tk=d)

        w1 = _pad_cols(ffn_w1[l, 1].astype(BF16), fp)
        w3 = _pad_cols(ffn_w3[l, 1].astype(BF16), fp)
        w2 = jnp.pad(ffn_w2[l, 1].astype(BF16), ((0, fp - ffn_dim), (0, 0)))
        hid = _ffn1(h, w1, w3)
        nl = min(l + 1, depth - 1)
        x, h = _mm_ln(hid, w2, x, mod, l, 8, 0.5, alpha, nl, 0, ln_g[l, 2], ln_b[l, 2])

    y_prompt = x[:mp].reshape(batch, seq, d)
    y_sample = x[mp:].reshape(dec_batch, dec_seq, d)
    return (y_prompt, y_sample, jnp.stack(new_k, axis=1), jnp.stack(new_v, axis=1),
            jnp.stack(new_s, axis=1))
```

```python
import functools
import math

import numpy as np
import jax
import jax.numpy as jnp
from jax import lax
from jax.experimental import pallas as pl
from jax.experimental.pallas import tpu as pltpu

F32 = jnp.float32
BF16 = jnp.bfloat16

GRID_W = 64
N_MOD = 9
HY_BANDS = 8
GLA_TAU = 16.0
GLA_CHUNK = 64
ROPE_THETA = 10000.0
LN_EPS = 1e-5
RMS_EPS = 1e-6

LANES = 128
MXU_DIM = 256
VMEM_BYTES_V7X = 64 * 1024 * 1024
VMEM_LIMIT = VMEM_BYTES_V7X - 8 * 1024 * 1024

GROUP = 4096
FFT_NB = 16
FFT_K1G = 8


def _cp(*sem):
    return pltpu.CompilerParams(dimension_semantics=sem, vmem_limit_bytes=VMEM_LIMIT)


def _dot(a, b):
    return jnp.dot(a, b, preferred_element_type=F32)


def _dot_nt(a, b):
    return lax.dot_general(a, b, (((1,), (1,)), ((), ())), preferred_element_type=F32)


def _dot_tn(a, b):
    return lax.dot_general(a, b, (((0,), (0,)), ((), ())), preferred_element_type=F32)


def _dot_hi(a, b):
    return jnp.dot(a, b, preferred_element_type=F32, precision=lax.Precision.HIGHEST)


def _silu(x):
    return x * jax.nn.sigmoid(x)


def _mod_kernel(c_ref, w_ref, b_ref, o_ref):
    c = c_ref[...]
    o_ref[...] = _dot(_silu(c).astype(BF16), w_ref[...].astype(BF16)) + b_ref[...]


def _modulation(cond, w_mod, b_mod):
    depth, d, n = w_mod.shape
    r = cond.shape[0]
    tn = 1024
    return pl.pallas_call(
        _mod_kernel,
        grid=(depth, n // tn),
        in_specs=[pl.BlockSpec((r, d), lambda l, j: (0, 0)),
                  pl.BlockSpec((None, d, tn), lambda l, j: (l, 0, j)),
                  pl.BlockSpec((None, 1, tn), lambda l, j: (l, 0, j))],
        out_specs=pl.BlockSpec((None, r, tn), lambda l, j: (l, 0, j)),
        out_shape=jax.ShapeDtypeStruct((depth, r, n), F32),
        compiler_params=_cp("parallel", "parallel"),
        name="mod",
    )(cond, w_mod, b_mod.reshape(depth, 1, n))


def _premod_kernel(x_ref, mod_ref, o_ref):
    o_ref[...] = (x_ref[...] * (1.0 + mod_ref[1:2, :]) + mod_ref[0:1, :]).astype(o_ref.dtype)


def _premod(x, mod, layer, tm=512):
    m, d = x.shape
    return pl.pallas_call(
        _premod_kernel,
        grid=(m // tm,),
        in_specs=[pl.BlockSpec((tm, d), lambda i: (i, 0)),
                  pl.BlockSpec((None, None, N_MOD, d), lambda i: (layer, (i * tm) // GROUP, 0, 0))],
        out_specs=pl.BlockSpec((tm, d), lambda i: (i, 0)),
        out_shape=jax.ShapeDtypeStruct((m, d), BF16),
        compiler_params=_cp("parallel"),
        name="premod",
    )(x, mod)


def _ffn1_kernel(h_ref, w1_ref, w3_ref, o_ref):
    h = h_ref[...]
    a = _dot(h, w1_ref[...])
    b = _dot(h, w3_ref[...])
    o_ref[...] = (_silu(a) * b).astype(o_ref.dtype)


def _ffn1(h, w1, w3, tm=1024, tf=512):
    m, d = h.shape
    fp = w1.shape[1]
    return pl.pallas_call(
        _ffn1_kernel,
        grid=(m // tm, fp // tf),
        in_specs=[pl.BlockSpec((tm, d), lambda i, j: (i, 0)),
                  pl.BlockSpec((d, tf), lambda i, j: (0, j)),
                  pl.BlockSpec((d, tf), lambda i, j: (0, j))],
        out_specs=pl.BlockSpec((tm, tf), lambda i, j: (i, j)),
        out_shape=jax.ShapeDtypeStruct((m, fp), BF16),
        compiler_params=_cp("parallel", "arbitrary"),
        name="ffn1",
    )(h, w1, w3)


def _mm_ln_kernel(a_ref, w_ref, x_ref, mod_ref, nmod_ref, g_ref, b_ref, xo_ref, ho_ref, *,
                  gate_row, coef, alpha, nshift_row):
    gate = coef * mod_ref[gate_row:gate_row + 1, :]
    xr = alpha * x_ref[...] + gate * _dot(a_ref[...], w_ref[...])
    mu = jnp.mean(xr, axis=-1, keepdims=True)
    xc = xr - mu
    var = jnp.mean(xc * xc, axis=-1, keepdims=True)
    xn = xc * lax.rsqrt(var + LN_EPS) * g_ref[...] + b_ref[...]
    xo_ref[...] = xn
    ho_ref[...] = (xn * (1.0 + nmod_ref[nshift_row + 1:nshift_row + 2, :])
                   + nmod_ref[nshift_row:nshift_row + 1, :]).astype(ho_ref.dtype)


def _mm_ln(a, w, x, mod, layer, gate_row, coef, alpha, nlayer, nshift_row, ln_g, ln_b, tm=256):
    m, kdim = a.shape
    d = w.shape[1]
    kern = functools.partial(_mm_ln_kernel, gate_row=gate_row, coef=coef, alpha=alpha, nshift_row=nshift_row)
    return pl.pallas_call(
        kern,
        grid=(m // tm,),
        in_specs=[pl.BlockSpec((tm, kdim), lambda i: (i, 0)),
                  pl.BlockSpec((kdim, d), lambda i: (0, 0), pipeline_mode=pl.Buffered(1)),
                  pl.BlockSpec((tm, d), lambda i: (i, 0)),
                  pl.BlockSpec((None, None, N_MOD, d), lambda i: (layer, (i * tm) // GROUP, 0, 0)),
                  pl.BlockSpec((None, None, N_MOD, d), lambda i: (nlayer, (i * tm) // GROUP, 0, 0)),
                  pl.BlockSpec((1, d), lambda i: (0, 0)),
                  pl.BlockSpec((1, d), lambda i: (0, 0))],
        out_specs=[pl.BlockSpec((tm, d), lambda i: (i, 0)),
                   pl.BlockSpec((tm, d), lambda i: (i, 0))],
        out_shape=[jax.ShapeDtypeStruct((m, d), F32), jax.ShapeDtypeStruct((m, d), BF16)],
        compiler_params=_cp("parallel"),
        name="mm_ln",
    )(a, w, x, mod, mod, ln_g.reshape(1, d), ln_b.reshape(1, d))


def _proj_kernel(h_ref, w_ref, o_ref, *, act):
    r = _dot(h_ref[...], w_ref[...])
    if act == "sigmoid":
        r = jax.nn.sigmoid(r)
    o_ref[...] = r.astype(o_ref.dtype)


def _proj(h, w, out_dtype, act=None, tm=1024, tn=1024):
    m, d = h.shape
    n = w.shape[1]
    tn = min(tn, n)
    return pl.pallas_call(
        functools.partial(_proj_kernel, act=act),
        grid=(m // tm, n // tn),
        in_specs=[pl.BlockSpec((tm, d), lambda i, j: (i, 0)),
                  pl.BlockSpec((d, tn), lambda i, j: (0, j))],
        out_specs=pl.BlockSpec((tm, tn), lambda i, j: (i, j)),
        out_shape=jax.ShapeDtypeStruct((m, n), out_dtype),
        compiler_params=_cp("parallel", "arbitrary"),
        name="proj",
    )(h, w)


def _sconv_kernel(z_ref, zp_ref, zn_ref, w_ref, b_ref, o_ref, *, tm, halo, lp, ls):
    i = pl.program_id(0)
    u = z_ref[...].astype(F32)
    prev = zp_ref[...].astype(F32)[halo - 1:halo, :]
    nxt = zn_ref[...].astype(F32)[0:1, :]
    row = lax.broadcasted_iota(jnp.int32, u.shape, 0)
    lseq = jnp.where((i * tm) // GROUP == 0, lp, ls)
    pos = (row + i * tm) & (lseq - 1)
    up = jnp.where(row == 0, prev, pltpu.roll(u, 1, 0))
    up = jnp.where(pos == 0, 0.0, up)
    un = jnp.where(row == tm - 1, nxt, pltpu.roll(u, tm - 1, 0))
    un = jnp.where(pos == lseq - 1, 0.0, un)
    w = w_ref[...]
    o_ref[...] = (up * w[0:1, :] + u * w[1:2, :] + un * w[2:3, :] + b_ref[...]).astype(o_ref.dtype)


def _short_conv(z, conv_w, conv_b, lp, ls, width, tm=1024, ct=512, halo=16):
    m = z.shape[0]
    nct = width // ct
    nrb = m // halo
    kern = functools.partial(_sconv_kernel, tm=tm, halo=halo, lp=lp, ls=ls)
    return pl.pallas_call(
        kern,
        grid=(m // tm, 3 * nct),
        in_specs=[pl.BlockSpec((tm, ct), lambda i, j: (i, j)),
                  pl.BlockSpec((halo, ct), lambda i, j: (jnp.maximum(i * (tm // halo) - 1, 0), j)),
                  pl.BlockSpec((halo, ct), lambda i, j: (jnp.minimum((i + 1) * (tm // halo), nrb - 1), j)),
                  pl.BlockSpec((3, ct), lambda i, j: (0, j)),
                  pl.BlockSpec((1, ct), lambda i, j: (0, j))],
        out_specs=pl.BlockSpec((None, tm, ct), lambda i, j: (j // nct, i, j % nct)),
        out_shape=jax.ShapeDtypeStruct((3, m, width), F32),
        compiler_params=_cp("parallel", "parallel"),
        name="sconv",
    )(z, z, z, conv_w, conv_b.reshape(1, -1))


def _filter_features(L):
    t = np.linspace(0.0, 1.0, L, dtype=np.float32)
    w = (np.float32(2.0 * math.pi / L) * np.arange(L, dtype=np.float32)).astype(np.float32)
    f = np.linspace(1e-4, HY_BANDS - 1, HY_BANDS, dtype=np.float32)
    wf = (w[:, None] * f).astype(np.float32)
    feats = np.concatenate([t[:, None], np.cos(wf), -np.sin(wf)], -1).astype(np.float32)
    idx = np.concatenate([np.arange(L), [0], np.arange(L - 1, 0, -1)])
    tab = np.zeros((2 * L, 32), np.float32)
    tab[:, :feats.shape[1]] = feats[idx]
    tab[:, 24] = t[idx]
    tab[:L, 25] = 1.0
    tab[L + 1:, 26] = -1.0
    return tab, feats.shape[1]


def _filter_kernel(tab_ref, w1_ref, b1_ref, fr_ref, w2_ref, b2_ref, w3_ref, dec_ref, o_ref, *, width):
    tab = tab_ref[...]
    fr = fr_ref[...]
    hdn = jnp.sin(fr[0:1, :] * (_dot_hi(tab, w1_ref[...]) + b1_ref[...]))
    hdn = jnp.sin(fr[1:2, :] * (_dot_hi(hdn, w2_ref[...]) + b2_ref[...]))
    t = tab[:, 24:25]
    mf = tab[:, 25:26]
    mb = tab[:, 26:27]
    h = _dot_hi(hdn, w3_ref[...]) * jnp.exp(-t * jnp.abs(dec_ref[...]))
    for o in range(2):
        base = 2 * width * o
        o_ref[o] = mf * h[:, base:base + width] + mb * h[:, base + width:base + 2 * width]


def _hyena_filter_taps(L, hy_w1, hy_b1, hy_freq, hy_w2, hy_b2, hy_w3, hy_decay, width, rb=256):
    tab_np, nfeat = _filter_features(L)
    fh = hy_w1.shape[1]
    w1p = jnp.zeros((32, fh), F32).at[:nfeat].set(hy_w1)
    n = 2 * L
    return pl.pallas_call(
        functools.partial(_filter_kernel, width=width),
        grid=(n // rb,),
        in_specs=[pl.BlockSpec((rb, 32), lambda i: (i, 0)),
                  pl.BlockSpec((32, fh), lambda i: (0, 0)),
                  pl.BlockSpec((1, fh), lambda i: (0, 0)),
                  pl.BlockSpec((2, fh), lambda i: (0, 0)),
                  pl.BlockSpec((fh, fh), lambda i: (0, 0)),
                  pl.BlockSpec((1, fh), lambda i: (0, 0)),
                  pl.BlockSpec((fh, 4 * width), lambda i: (0, 0)),
                  pl.BlockSpec((1, 4 * width), lambda i: (0, 0))],
        out_specs=pl.BlockSpec((2, rb, width), lambda i: (0, i, 0)),
        out_shape=jax.ShapeDtypeStruct((2, n, width), F32),
        compiler_params=_cp("parallel"),
        name="hyfilter",
    )(jnp.asarray(tab_np), w1p, hy_b1.reshape(1, fh), hy_freq, hy_w2, hy_b2.reshape(1, fh), hy_w3,
      hy_decay.reshape(1, -1))


def _dft_a(na, ka):
    k1 = np.arange(na // 2)[:, None].astype(np.float64)
    a = np.arange(ka)[None, :].astype(np.float64)
    th = 2.0 * np.pi * a * (k1 + 0.5) / na
    return np.concatenate([np.cos(th), -np.sin(th)], 0)


def _idft_a(na, n, rows):
    k1 = np.arange(na // 2)[None, :].astype(np.float64)
    a = np.arange(rows)[:, None].astype(np.float64)
    th = 2.0 * np.pi * a * (k1 + 0.5) / na
    return (2.0 / n) * np.concatenate([np.cos(th), -np.sin(th)], 1)


def _dft_b(na, nb, g):
    n = na * nb
    half = na // 2
    k1 = np.arange(half).astype(np.float64)
    b = np.arange(nb).astype(np.float64)
    k2 = np.arange(nb).astype(np.float64)
    phi = 2.0 * np.pi * (b[None, None, :] * k2[None, :, None] / nb
                         + b[None, None, :] * (k1[:, None, None] + 0.5) / n)
    c, s = np.cos(phi), np.sin(phi)
    ng = half // g
    r = g * nb
    fwd = np.zeros((ng, 2 * r, 2 * r))
    for q in range(g):
        rows = slice(q * nb, (q + 1) * nb)
        rows_i = slice(r + q * nb, r + (q + 1) * nb)
        cols = slice(q, r, g)
        cols_i = slice(r + q, 2 * r, g)
        cq, sq = c[q::g], s[q::g]
        fwd[:, rows, cols] = cq
        fwd[:, rows, cols_i] = sq
        fwd[:, rows_i, cols] = -sq
        fwd[:, rows_i, cols_i] = cq
    inv = np.transpose(fwd, (0, 2, 1))
    return fwd, inv


def _lmm_kernel(f_ref, x_ref, o_ref):
    o_ref[...] = _dot(f_ref[...], x_ref[...].astype(BF16)).astype(o_ref.dtype)


def _lmm(f, x, x_index, nbatch, out_dtype, tn=2048):
    mo, k = f.shape
    n = x.shape[-1]
    lead = len(x.shape) - 2
    return pl.pallas_call(
        _lmm_kernel,
        grid=(nbatch, n // tn),
        in_specs=[pl.BlockSpec((mo, k), lambda b, j: (0, 0)),
                  pl.BlockSpec((None,) * lead + (k, tn), lambda b, j: x_index(b) + (0, j))],
        out_specs=pl.BlockSpec((None, mo, tn), lambda b, j: (b, 0, j)),
        out_shape=jax.ShapeDtypeStruct((nbatch, mo, n), out_dtype),
        compiler_params=_cp("parallel", "parallel"),
        name="dft_a",
    )(f, x)


def _stage_a_fwd(x_ref, f1_ref, r_ref, nb, ct):
    for b in range(nb):
        r_ref[:, b * ct:(b + 1) * ct] = _dot(f1_ref[...], x_ref[:, b, :].astype(BF16))


def _stage_b_rows(r_ref, r0, half, nb, kg, ct):
    return [(pl.ds(ri * half + r0, kg), slice(b * ct, (b + 1) * ct)) for ri in range(2) for b in range(nb)]


def _lconv_kernel(x_ref, xg_ref, f1_ref, g_ref, tb_ref, tbi_ref, kf_ref, bias_ref, o_ref, r_ref, *,
                  nb, half, kg, ct):
    r = nb * kg
    _stage_a_fwd(x_ref, f1_ref, r_ref, nb, ct)

    def group(g, carry):
        tiles = _stage_b_rows(r_ref, pl.multiple_of(g * kg, kg), half, nb, kg, ct)
        a = jnp.concatenate([r_ref[rs, cs] for rs, cs in tiles], axis=0).astype(BF16)
        x = _dot(tb_ref[g], a)
        kf = kf_ref[pl.ds(pl.multiple_of(g * 2 * r, 2 * r), 2 * r), :]
        xr, xi = x[:r], x[r:]
        kr, ki = kf[:r], kf[r:]
        y = jnp.concatenate([xr * kr - xi * ki, xr * ki + xi * kr], axis=0).astype(BF16)
        bh = _dot(tbi_ref[g], y)
        for t, (rs, cs) in enumerate(tiles):
            r_ref[rs, cs] = bh[t * kg:(t + 1) * kg, :]
        return carry

    lax.fori_loop(0, half // kg, group, 0)
    bias = bias_ref[...]
    for b in range(nb):
        y = _dot(g_ref[...], r_ref[:, b * ct:(b + 1) * ct].astype(BF16))
        o_ref[:, b, :] = xg_ref[:, b, :] * (y + x_ref[:, b, :] * bias)


def _lconv(x, x_index, xg, xg_index, f1, g_inv, tb, tbi, kf, order, bias, nbatch, ct=256):
    ka, nb, c = x.shape[-3:]
    n2, _ = f1.shape
    ng, r2, _ = tb.shape
    kg = r2 // (2 * nb)
    lx, lg = len(x.shape) - 3, len(xg.shape) - 3
    once = dict(pipeline_mode=pl.Buffered(1))
    kern = functools.partial(_lconv_kernel, nb=nb, half=n2 // 2, kg=kg, ct=ct)
    return pl.pallas_call(
        kern,
        grid=(c // ct, nbatch),
        in_specs=[pl.BlockSpec((None,) * lx + (ka, nb, ct), lambda j, b: x_index(b) + (0, 0, j)),
                  pl.BlockSpec((None,) * lg + (ka, nb, ct), lambda j, b: xg_index(b) + (0, 0, j)),
                  pl.BlockSpec(f1.shape, lambda j, b: (0, 0), **once),
                  pl.BlockSpec(g_inv.shape, lambda j, b: (0, 0), **once),
                  pl.BlockSpec(tb.shape, lambda j, b: (0, 0, 0), **once),
                  pl.BlockSpec(tbi.shape, lambda j, b: (0, 0, 0), **once),
                  pl.BlockSpec((None, ng * r2, ct), lambda j, b: (order, 0, j), **once),
                  pl.BlockSpec((1, ct), lambda j, b: (0, j))],
        out_specs=pl.BlockSpec((None, ka, nb, ct), lambda j, b: (b, 0, 0, j)),
        out_shape=jax.ShapeDtypeStruct((nbatch, ka, nb, c), F32),
        scratch_shapes=[pltpu.VMEM((n2, nb * ct), F32)],
        compiler_params=_cp("parallel", "arbitrary"),
        name="lconv",
    )(x, xg, f1, g_inv, tb, tbi, kf, bias.reshape(1, c))


def _lconv_filter_kernel(x_ref, f1_ref, tb_ref, o_ref, r_ref, *, nb, half, kg, ct):
    r2 = 2 * nb * kg
    _stage_a_fwd(x_ref, f1_ref, r_ref, nb, ct)

    def group(g, carry):
        tiles = _stage_b_rows(r_ref, pl.multiple_of(g * kg, kg), half, nb, kg, ct)
        a = jnp.concatenate([r_ref[rs, cs] for rs, cs in tiles], axis=0).astype(BF16)
        o_ref[pl.ds(pl.multiple_of(g * r2, r2), r2), :] = _dot(tb_ref[g], a)
        return carry

    lax.fori_loop(0, half // kg, group, 0)


def _lconv_filter(taps, f1, tb, ct=256):
    norder, na, nb, c = taps.shape
    n2, _ = f1.shape
    ng, r2, _ = tb.shape
    kg = r2 // (2 * nb)
    kern = functools.partial(_lconv_filter_kernel, nb=nb, half=n2 // 2, kg=kg, ct=ct)
    return pl.pallas_call(
        kern,
        grid=(norder, c // ct),
        in_specs=[pl.BlockSpec((None, na, nb, ct), lambda o, j: (o, 0, 0, j)),
                  pl.BlockSpec(f1.shape, lambda o, j: (0, 0)),
                  pl.BlockSpec(tb.shape, lambda o, j: (0, 0, 0))],
        out_specs=pl.BlockSpec((None, ng * r2, ct), lambda o, j: (o, 0, j)),
        out_shape=jax.ShapeDtypeStruct((norder, ng * r2, c), F32),
        scratch_shapes=[pltpu.VMEM((n2, nb * ct), F32)],
        compiler_params=_cp("parallel", "parallel"),
        name="lconv_filter",
    )(taps, f1, tb)


def _pconv_kernel(fd_ref, gd_ref, v_ref, kf_ref, xg_ref, bias_ref, o_ref, *, half):
    v = v_ref[...]
    x = _dot(fd_ref[...], v.astype(BF16))
    kf = kf_ref[...]
    xr, xi = x[:half], x[half:]
    kr, ki = kf[:half], kf[half:]
    y = jnp.concatenate([xr * kr - xi * ki, xr * ki + xi * kr], axis=0).astype(BF16)
    yt = _dot(gd_ref[...], y)
    o_ref[...] = (xg_ref[...].astype(F32) * (yt + v.astype(F32) * bias_ref[...])).astype(o_ref.dtype)


def _pconv(fd, gd, vsrc, v_lead, kf, order, xsrc, x_lead, bias, nseq, L, ct=512):
    c = kf.shape[-1]
    n2 = fd.shape[0]
    return pl.pallas_call(
        functools.partial(_pconv_kernel, half=n2 // 2),
        grid=(nseq, c // ct),
        in_specs=[pl.BlockSpec((n2, L), lambda s, j: (0, 0)),
                  pl.BlockSpec((L, n2), lambda s, j: (0, 0)),
                  pl.BlockSpec((None,) * len(v_lead) + (L, ct), lambda s, j: v_lead + (s, j)),
                  pl.BlockSpec((None, n2, ct), lambda s, j: (order, 0, j)),
                  pl.BlockSpec((None,) * len(x_lead) + (L, ct), lambda s, j: x_lead + (s, j)),
                  pl.BlockSpec((1, ct), lambda s, j: (0, j))],
        out_specs=pl.BlockSpec((L, ct), lambda s, j: (s, j)),
        out_shape=jax.ShapeDtypeStruct((nseq * L, c), F32),
        compiler_params=_cp("parallel", "parallel"),
        name="pconv",
    )(fd, gd, vsrc, kf, xsrc, bias.reshape(1, c))


def _gate_kernel(lr_ref, wa_ref, ba_ref, o_ref):
    logits = _dot_hi(lr_ref[...], wa_ref[...]) + ba_ref[...]
    o_ref[...] = jax.nn.log_sigmoid(logits) * (1.0 / GLA_TAU)


def _gla_gates(lr, wa_cat, ba_cat, tm=1024):
    m, k = lr.shape
    n = wa_cat.shape[1]
    return pl.pallas_call(
        _gate_kernel,
        grid=(m // tm,),
        in_specs=[pl.BlockSpec((tm, k), lambda i: (i, 0)),
                  pl.BlockSpec((k, n), lambda i: (0, 0)),
                  pl.BlockSpec((1, n), lambda i: (0, 0))],
        out_specs=pl.BlockSpec((tm, n), lambda i: (i, 0)),
        out_shape=jax.ShapeDtypeStruct((m, n), F32),
        compiler_params=_cp("parallel"),
        name="gla_gates",
    )(lr, wa_cat, ba_cat)


def _split3(x):
    hi = x.astype(BF16)
    r1 = x - hi.astype(F32)
    mid = r1.astype(BF16)
    lo = (r1 - mid.astype(F32)).astype(BF16)
    return hi, mid, lo


def _gla_dir(d, r0, q_ref, k_ref, v_ref, la_ref, o_ref, st_ref, tri, causal, ref_row, last_row,
             heads, dk, dv, scale):
    ch = GLA_CHUNK
    rows = pl.ds(pl.multiple_of(r0, ch), ch)
    la = la_ref[rows, :]
    hi, mid, lo = _split3(la)
    b = _dot(tri, hi) + _dot(tri, mid) + _dot(tri, lo)
    bref = b[ref_row:ref_row + 1, :]
    blast = b[last_row:last_row + 1, :]
    q = q_ref[rows, :].astype(F32) * scale
    k = k_ref[rows, :].astype(F32)
    qt = (q * jnp.exp(b - bref)).astype(BF16)
    kt = (k * jnp.exp(bref - b)).astype(BF16)
    qin = (q * jnp.exp(b)).astype(BF16)
    kst = (k * jnp.exp(blast - b)).astype(BF16)
    dec = jnp.exp(blast)
    for h in range(heads):
        ks = slice(h * dk, (h + 1) * dk)
        vs = slice(h * dv, (h + 1) * dv)
        att = _dot_nt(qt[:, ks], kt[:, ks])
        att = jnp.where(causal, att, 0.0).astype(BF16)
        vh = v_ref[rows, vs]
        st = st_ref[d, h]
        o_ref[rows, vs] = _dot(att, vh) + _dot_nt(qin[:, ks], st.astype(BF16))
        st_ref[d, h] = st * dec[:, ks] + _dot_tn(vh, kst[:, ks])


def _gla_kernel(*refs, nch, heads, dk, dv, has_s0, scale):
    if has_s0:
        (qf, kf, vf, laf, qb, kb, vb, lab, s0, of, ob, st) = refs
    else:
        (qf, kf, vf, laf, qb, kb, vb, lab, of, ob, st) = refs
        s0 = None
    ch = GLA_CHUNK

    @pl.when(pl.program_id(1) == 0)
    def _():
        if has_s0:
            st[...] = s0[...]
        else:
            st[...] = jnp.zeros_like(st)

    r_i = lax.broadcasted_iota(jnp.int32, (ch, ch), 0)
    c_i = lax.broadcasted_iota(jnp.int32, (ch, ch), 1)
    lower = r_i >= c_i
    upper = r_i <= c_i
    tri_l = jnp.where(lower, 1.0, 0.0).astype(BF16)
    tri_u = jnp.where(upper, 1.0, 0.0).astype(BF16)

    def body(c, carry):
        _gla_dir(0, c * ch, qf, kf, vf, laf, of, st, tri_l, lower, ch // 2 - 1, ch - 1,
                 heads, dk, dv, scale)
        _gla_dir(1, (nch - 1 - c) * ch, qb, kb, vb, lab, ob, st, tri_u, upper, ch // 2, 0,
                 heads, dk, dv, scale)
        return carry

    lax.fori_loop(0, nch, body, 0)


def _gla(zmain, qcol, kcol, vcol, la, row0, nseq, L, tb, heads, dk, dv, s0t=None):
    nblk = L // tb
    rb0 = row0 // tb
    hk, hv = heads * dk, heads * dv

    def fwd(s, j):
        return rb0 + s * nblk + j

    def bwd(s, j):
        return rb0 + s * nblk + (nblk - 1 - j)

    in_specs = []
    for rowf, lcol in ((fwd, 0), (bwd, 1)):
        in_specs += [pl.BlockSpec((tb, hk), lambda s, j, rowf=rowf: (rowf(s, j), qcol)),
                     pl.BlockSpec((tb, hk), lambda s, j, rowf=rowf: (rowf(s, j), kcol)),
                     pl.BlockSpec((tb, hv), lambda s, j, rowf=rowf: (rowf(s, j), vcol)),
                     pl.BlockSpec((tb, hk), lambda s, j, rowf=rowf, lcol=lcol: (rowf(s, j), lcol))]
    args = [zmain, zmain, zmain, la, zmain, zmain, zmain, la]
    if s0t is not None:
        in_specs.append(pl.BlockSpec((None, 2, heads, dv, dk), lambda s, j: (s, 0, 0, 0, 0)))
        args.append(s0t)
    kern = functools.partial(_gla_kernel, nch=tb // GLA_CHUNK, heads=heads, dk=dk, dv=dv,
                             has_s0=s0t is not None, scale=dk ** -0.5)
    return pl.pallas_call(
        kern,
        grid=(nseq, nblk),
        in_specs=in_specs,
        out_specs=[pl.BlockSpec((tb, hv), lambda s, j: (s * nblk + j, 0)),
                   pl.BlockSpec((tb, hv), lambda s, j: (s * nblk + (nblk - 1 - j), 0)),
                   pl.BlockSpec((None, 2, heads, dv, dk), lambda s, j: (s, 0, 0, 0, 0))],
        out_shape=[jax.ShapeDtypeStruct((nseq * L, hv), F32),
                   jax.ShapeDtypeStruct((nseq * L, hv), F32),
                   jax.ShapeDtypeStruct((nseq, 2, heads, dv, dk), F32)],
        compiler_params=_cp("parallel", "arbitrary"),
        name="gla",
    )(*args)


def _gla_post_kernel(of_ref, ob_ref, gr_ref, g_ref, o_ref, *, heads, dv):
    o = of_ref[...] + ob_ref[...]
    gate = _silu(gr_ref[...].astype(F32))
    g = g_ref[...]
    for h in range(heads):
        sl = slice(h * dv, (h + 1) * dv)
        oh = o[:, sl]
        ms = jnp.mean(oh * oh, axis=-1, keepdims=True)
        o_ref[:, sl] = (oh * lax.rsqrt(ms + RMS_EPS) * g * gate[:, sl]).astype(o_ref.dtype)


def _gla_post(o_f, o_b, zmain, grcol, norm_g, heads, dv, tm=512):
    m, hv = o_f.shape
    return pl.pallas_call(
        functools.partial(_gla_post_kernel, heads=heads, dv=dv),
        grid=(m // tm,),
        in_specs=[pl.BlockSpec((tm, hv), lambda i: (i, 0)),
                  pl.BlockSpec((tm, hv), lambda i: (i, 0)),
                  pl.BlockSpec((tm, hv), lambda i: (i, grcol)),
                  pl.BlockSpec((1, dv), lambda i: (0, 0))],
        out_specs=pl.BlockSpec((tm, hv), lambda i: (i, 0)),
        out_shape=jax.ShapeDtypeStruct((m, hv), BF16),
        compiler_params=_cp("parallel"),
        name="gla_post",
    )(o_f, o_b, zmain, norm_g.reshape(1, dv))


def _rope_tables(L, dh):
    rows = L // GRID_W
    r = np.repeat(np.arange(rows, dtype=np.float32), GRID_W)
    col = np.tile(np.arange(GRID_W, dtype=np.float32), rows)
    nf = dh // 4
    inv = (np.float32(ROPE_THETA) ** (-np.arange(nf, dtype=np.float32) / nf)).astype(np.float32)
    ang_r = (r[:, None] * inv).astype(np.float32)
    ang_c = (col[:, None] * inv).astype(np.float32)
    cos = np.concatenate([np.cos(ang_r), np.cos(ang_r), np.cos(ang_c), np.cos(ang_c)], -1)
    sin = np.concatenate([-np.sin(ang_r), np.sin(ang_r), -np.sin(ang_c), np.sin(ang_c)], -1)
    cos = np.concatenate([cos, cos], -1).astype(np.float32)
    sin = np.concatenate([sin, sin], -1).astype(np.float32)
    cos_t = np.stack([np.ones_like(cos), cos])
    sin_t = np.stack([np.zeros_like(sin), sin])
    return cos_t, sin_t


def _qkv_prep_kernel(q_ref, k_ref, v_ref, cos_ref, sin_ref, qo_ref, ko_ref, vo_ref, *, heads, hw, nf, scale):
    cos = cos_ref[...]
    sin = sin_ref[...]
    lane = lax.broadcasted_iota(jnp.int32, cos.shape, 1)
    first = (lane & (2 * nf - 1)) < nf

    def rope(x):
        sw = jnp.where(first, pltpu.roll(x, hw - nf, 1), pltpu.roll(x, nf, 1))
        return x * cos + sw * sin

    ones = jnp.ones((q_ref.shape[0], hw), vo_ref.dtype)
    for h in range(heads):
        sl = slice(h * hw, (h + 1) * hw)
        qo_ref[:, sl] = (rope(q_ref[:, sl]) * scale).astype(qo_ref.dtype)
        ko_ref[sl, :] = rope(k_ref[:, sl]).T.astype(ko_ref.dtype)
        vo_ref[:, 2 * h * hw:(2 * h + 1) * hw] = v_ref[:, sl].astype(vo_ref.dtype)
        vo_ref[:, (2 * h + 1) * hw:(2 * h + 2) * hw] = ones


def _qkv_prep(zd, cos_t, sin_t, heads, dh, tm=512):
    m = zd.shape[0]
    w = zd.shape[1] // 3
    hw = w // heads
    gb = GROUP // tm
    kern = functools.partial(_qkv_prep_kernel, heads=heads, hw=hw, nf=dh // 4,
                             scale=dh ** -0.5 * math.log2(math.e))
    tab = pl.BlockSpec((None, tm, hw), lambda i: (jnp.minimum(i // gb, 1), i % gb, 0))
    out = jax.ShapeDtypeStruct((m, w), BF16)
    return pl.pallas_call(
        kern,
        grid=(m // tm,),
        in_specs=[pl.BlockSpec((tm, w), lambda i: (i, 0)),
                  pl.BlockSpec((tm, w), lambda i: (i, 1)),
                  pl.BlockSpec((tm, w), lambda i: (i, 2)),
                  tab, tab],
        out_specs=[pl.BlockSpec((tm, w), lambda i: (i, 0)),
                   pl.BlockSpec((w, tm), lambda i: (0, i)),
                   pl.BlockSpec((tm, 2 * w), lambda i: (i, 0))],
        out_shape=[out, jax.ShapeDtypeStruct((w, m), BF16), jax.ShapeDtypeStruct((m, 2 * w), BF16)],
        compiler_params=_cp("parallel"),
        name="qkv_prep",
    )(zd, zd, zd, cos_t, sin_t)


def _attn_kernel(*refs, has_ctx, lam_init, dh, bf16_exp, key_chunk):
    if has_ctx:
        q_ref, k_ref, v_ref, kc_ref, vc_ref, lam_ref, g_ref, o_ref = refs
    else:
        q_ref, k_ref, v_ref, lam_ref, g_ref, o_ref = refs
    q = q_ref[...]
    lane = lax.broadcasted_iota(jnp.int32, q.shape, 1)
    first = lane < dh
    zero = jnp.zeros_like(q)
    hw = 2 * dh
    lk = k_ref.shape[1]
    kchunk = min(lk, key_chunk)
    chunks = [(k_ref[:, c * kchunk:(c + 1) * kchunk], v_ref[c * kchunk:(c + 1) * kchunk, :])
              for c in range(lk // kchunk)]
    if has_ctx:
        vcv = vc_ref[...].astype(BF16)
        chunks.append((kc_ref[...].T.astype(BF16), jnp.concatenate([vcv, jnp.ones_like(vcv)], axis=1)))

    def prob(t):
        if bf16_exp:
            return jnp.exp2(t.astype(BF16))
        return jnp.exp2(t).astype(BF16)

    def softmax_pv(qj):
        scores = [_dot(qj, kt) for kt, _ in chunks]
        m = functools.reduce(jnp.maximum, [jnp.max(s, axis=-1, keepdims=True) for s in scores])
        acc = sum(_dot(prob(s - m), vc) for s, (_, vc) in zip(scores, chunks))
        return acc[:, :hw] / acc[:, hw:]

    lp = lam_ref[...]
    lam = (jnp.exp(jnp.sum(lp[0:1] * lp[1:2], axis=-1, keepdims=True))
           - jnp.exp(jnp.sum(lp[2:3] * lp[3:4], axis=-1, keepdims=True)) + lam_init)
    o = softmax_pv(jnp.where(first, q, zero)) - lam * softmax_pv(jnp.where(first, zero, q))
    ms = jnp.mean(o * o, axis=-1, keepdims=True)
    o_ref[...] = (o * lax.rsqrt(ms + RMS_EPS) * g_ref[...] * (1.0 - lam_init)).astype(o_ref.dtype)


def _diff_attention(qs, kr, vb, row0, nseq, L, tq, heads, dh, lam_p, norm_g, lam_init, ctx=None):
    hw = 2 * dh
    nqb = L // tq
    qb0 = row0 // tq
    kb0 = row0 // L
    in_specs = [pl.BlockSpec((tq, hw), lambda s, h, i: (qb0 + s * nqb + i, h)),
                pl.BlockSpec((hw, L), lambda s, h, i: (h, kb0 + s)),
                pl.BlockSpec((L, 2 * hw), lambda s, h, i: (kb0 + s, h))]
    args = [qs, kr, vb]
    if ctx is not None:
        ck, cv, layer = ctx
        past = ck.shape[2]
        in_specs += [pl.BlockSpec((None, None, past, hw), lambda s, h, i: (s, layer, 0, h)),
                     pl.BlockSpec((None, None, past, hw), lambda s, h, i: (s, layer, 0, h))]
        args += [ck, cv]
    in_specs += [pl.BlockSpec(lam_p.shape, lambda s, h, i: (0, 0)),
                 pl.BlockSpec((1, hw), lambda s, h, i: (0, 0))]
    args += [lam_p, norm_g.reshape(1, hw)]
    kern = functools.partial(_attn_kernel, has_ctx=ctx is not None, lam_init=lam_init, dh=dh,
                             bf16_exp=L > 1024, key_chunk=2048)
    return pl.pallas_call(
        kern,
        grid=(nseq, heads, nqb),
        in_specs=in_specs,
        out_specs=pl.BlockSpec((tq, hw), lambda s, h, i: (s * nqb + i, h)),
        out_shape=jax.ShapeDtypeStruct((nseq * L, heads * hw), BF16),
        compiler_params=_cp("parallel", "parallel", "arbitrary"),
        name="diff_attn",
    )(*args)


def _mix_kernel(ya_ref, yb_ref, yc_ref, g_ref, w_ref, o_ref, acc_ref):
    k = pl.program_id(1)

    def contrib(y_ref):
        return g_ref[...].astype(F32) * _dot(y_ref[...].astype(BF16), w_ref[...])

    @pl.when(k == 0)
    def _():
        acc_ref[...] = contrib(ya_ref)

    @pl.when(k == 1)
    def _():
        acc_ref[...] += contrib(yb_ref)

    @pl.when(k == 2)
    def _():
        o_ref[...] = (acc_ref[...] + contrib(yc_ref)).astype(o_ref.dtype)


def _mix(ya, yb, yc, gates, wbr, tm=512):
    m, w = ya.shape
    d = wbr.shape[2]
    return pl.pallas_call(
        _mix_kernel,
        grid=(m // tm, 3),
        in_specs=[pl.BlockSpec((tm, w), lambda i, k: (i, 0)),
                  pl.BlockSpec((tm, w), lambda i, k: (i, 0)),
                  pl.BlockSpec((tm, w), lambda i, k: (i, 0)),
                  pl.BlockSpec((tm, d), lambda i, k: (i, k)),
                  pl.BlockSpec((None, w, d), lambda i, k: (k, 0, 0))],
        out_specs=pl.BlockSpec((tm, d), lambda i, k: (i, 0)),
        out_shape=jax.ShapeDtypeStruct((m, d), BF16),
        scratch_shapes=[pltpu.VMEM((tm, d), F32)],
        compiler_params=_cp("parallel", "arbitrary"),
        name="mix",
    )(ya, yb, yc, gates, wbr)


def _pad_cols(w, n):
    return jnp.pad(w, ((0, 0), (0, n - w.shape[1])))


def kernel(x_prompt, x_sample, cache_k, cache_v, state_gla, c, c_ctx, w_mod, b_mod, ln_g, ln_b, ffn_w1, ffn_w3, ffn_w2, w_in, hy_conv_w, hy_conv_b, hy_w1, hy_b1, hy_freq, hy_w2, hy_b2, hy_w3, hy_decay, hy_bias, gla_wa, gla_ba, gla_norm_g, diff_lam, diff_norm_g, w_branch_a, w_branch_b, w_branch_c, w_out):
    batch, seq, d = x_prompt.shape
    dec_batch, dec_seq, _ = x_sample.shape
    depth = w_mod.shape[0]
    ffn_dim = ffn_w1.shape[3]
    hy_w = hy_bias.shape[2]
    heads_g, dk_g = 4, gla_wa.shape[3] // 4
    dv_g = gla_norm_g.shape[1]
    rank = gla_wa.shape[2]
    dh = diff_lam.shape[2]
    heads_d = cache_k.shape[3]
    dw = heads_d * 2 * dh
    gw = heads_g * dv_g
    gk = heads_g * dk_g
    assert batch * seq == GROUP and dec_seq == GROUP
    mp = batch * seq
    m = mp + dec_batch * dec_seq
    ngroups = 1 + dec_batch
    alpha = (2 * depth) ** 0.25
    lam_inits = [0.8 - 0.6 * math.exp(-0.3 * l) for l in range(depth)]

    c_main = 3 * hy_w + 2 * gk + 2 * gw
    c_lr = c_main
    c_d = c_lr + 2 * rank
    c_g = c_d + 3 * dw

    cond = jnp.concatenate([c_ctx[None], c, jnp.zeros((16 - ngroups, d), F32)], axis=0)
    mod = _modulation(cond, w_mod, b_mod)[:, :ngroups].reshape(depth, ngroups, N_MOD, d)

    na = 2 * dec_seq // FFT_NB
    n_s = 2 * dec_seq
    f1_half = jnp.asarray(_dft_a(na, na // 2), BF16)
    f1_full = jnp.asarray(_dft_a(na, na), BF16)
    g_s = jnp.asarray(_idft_a(na, n_s, na // 2), BF16)
    tb_np, tbi_np = _dft_b(na, FFT_NB, FFT_K1G)
    tb, tbi = jnp.asarray(tb_np, BF16), jnp.asarray(tbi_np, BF16)
    fp_half = jnp.asarray(_dft_a(2 * seq, seq), BF16)
    fp_full = jnp.asarray(_dft_a(2 * seq, 2 * seq), BF16)
    g_p = jnp.asarray(_idft_a(2 * seq, 2 * seq, seq), BF16)
    cos_t, sin_t = _rope_tables(dec_seq, dh)
    cos_t, sin_t = jnp.asarray(cos_t), jnp.asarray(sin_t)
    half_rows = (na // 2) * FFT_NB
    lanes_s = FFT_NB * hy_w

    x = jnp.concatenate([x_prompt.reshape(mp, d), x_sample.reshape(dec_batch * dec_seq, d)], axis=0)
    h = _premod(x, mod, 0)

    fp = ((ffn_dim + 511) // 512) * 512
    ck = cache_k.reshape(dec_batch, depth, cache_k.shape[2], dw)
    cv = cache_v.reshape(dec_batch, depth, cache_v.shape[2], dw)
    new_k, new_v, new_s = [], [], []

    for l in range(depth):
        w1 = _pad_cols(ffn_w1[l, 0].astype(BF16), fp)
        w3 = _pad_cols(ffn_w3[l, 0].astype(BF16), fp)
        w2 = jnp.pad(ffn_w2[l, 0].astype(BF16), ((0, fp - ffn_dim), (0, 0)))
        hid = _ffn1(h, w1, w3)
        x, h = _mm_ln(hid, w2, x, mod, l, 2, 0.5, alpha, l, 3, ln_g[l, 0], ln_b[l, 0])

        wi = w_in[l]
        zmain = _proj(h, wi[:, :c_main].astype(BF16), BF16)
        zlr = _proj(h, _pad_cols(wi[:, c_lr:c_d].astype(BF16), LANES), F32)
        zd = _proj(h, wi[:, c_d:c_g].astype(BF16), F32)
        gates = _proj(h, wi[:, c_g:].astype(BF16), BF16, act="sigmoid")

        u3 = _short_conv(zmain, hy_conv_w[l], hy_conv_b[l], seq, dec_seq, hy_w)
        u5 = u3.reshape(3, ngroups, na // 2, FFT_NB, hy_w)
        fargs = (hy_w1[l], hy_b1[l], hy_freq[l], hy_w2[l], hy_b2[l], hy_w3[l], hy_decay[l], hy_w)
        taps_s = _hyena_filter_taps(dec_seq, *fargs)
        taps_p = _hyena_filter_taps(seq, *fargs)
        kf_s = _lconv_filter(taps_s.reshape(2, na, FFT_NB, hy_w), f1_full, tb)
        kf_p = _lmm(fp_full, taps_p, lambda b: (b,), 2, F32, tn=hy_w)

        z1s = _lconv(u5, lambda b: (0, b + 1), u5, lambda b: (1, b + 1), f1_half, g_s, tb, tbi, kf_s, 0,
                     hy_bias[l, 0], dec_batch)
        z2s = _lconv(z1s, lambda b: (b,), u5, lambda b: (2, b + 1), f1_half, g_s, tb, tbi, kf_s, 1,
                     hy_bias[l, 1], dec_batch)
        ya_s = z2s.reshape(dec_batch * dec_seq, hy_w)
        z1 = _pconv(fp_half, g_p, u3, (0,), kf_p, 0, u3, (1,), hy_bias[l, 0], batch, seq)
        ya_p = _pconv(fp_half, g_p, z1, (), kf_p, 1, u3, (2,), hy_bias[l, 1], batch, seq)
        ya = jnp.concatenate([ya_p, ya_s], axis=0)

        wa_cat = jnp.zeros((LANES, 2 * gk), F32)
        wa_cat = wa_cat.at[:rank, :gk].set(gla_wa[l, 0]).at[rank:2 * rank, gk:].set(gla_wa[l, 1])
        ba_cat = jnp.concatenate([gla_ba[l, 0], gla_ba[l, 1]]).reshape(1, 2 * gk)
        la = _gla_gates(zlr, wa_cat, ba_cat)
        qcol, kcol, vcol, grcol = (3 * hy_w) // gk, (3 * hy_w) // gk + 1, (3 * hy_w + 2 * gk) // gw, \
            (3 * hy_w + 2 * gk) // gw + 1
        s0t = jnp.swapaxes(state_gla[:, l], -1, -2)
        of_p, ob_p, st_p = _gla(zmain, qcol, kcol, vcol, la, 0, batch, seq, seq, heads_g, dk_g, dv_g)
        of_s, ob_s, _ = _gla(zmain, qcol, kcol, vcol, la, mp, dec_batch, dec_seq, 512, heads_g, dk_g, dv_g,
                             s0t=s0t)
        o_f = jnp.concatenate([of_p, of_s], axis=0)
        o_b = jnp.concatenate([ob_p, ob_s], axis=0)
        yb = _gla_post(o_f, o_b, zmain, grcol, gla_norm_g[l], heads_g, dv_g)
        new_s.append(jnp.swapaxes(st_p, -1, -2))

        qs, kr, vb = _qkv_prep(zd, cos_t, sin_t, heads_d, dh)
        yc_p = _diff_attention(qs, kr, vb, 0, batch, seq, seq, heads_d, dh, diff_lam[l], diff_norm_g[l],
                               lam_inits[l])
        yc_s = _diff_attention(qs, kr, vb, mp, dec_batch, dec_seq, 256, heads_d, dh, diff_lam[l],
                               diff_norm_g[l], lam_inits[l], ctx=(ck, cv, l))
        yc = jnp.concatenate([yc_p, yc_s], axis=0)
        new_k.append(zd[:mp, dw:2 * dw].reshape(batch, seq, heads_d, 2, dh))
        new_v.append(zd[:mp, 2 * dw:].reshape(batch, seq, heads_d, 2 * dh))

        wbr = jnp.stack([w_branch_a[l], w_branch_b[l], w_branch_c[l]]).astype(BF16)
        y = _mix(ya, yb, yc, gates, wbr)
        x, h = _mm_ln(y, w_out[l].astype(BF16), x, mod, l, 5, 1.0, alpha, l, 6, ln_g[l, 1], ln_b[l, 1])

        w1 = _pad_cols(ffn_w1[l, 1].astype(BF16), fp)
        w3 = _pad_cols(ffn_w3[l, 1].astype(BF16), fp)
        w2 = jnp.pad(ffn_w2[l, 1].astype(BF16), ((0, fp - ffn_dim), (0, 0)))
        hid = _ffn1(h, w1, w3)
        nl = min(l + 1, depth - 1)
        x, h = _mm_ln(hid, w2, x, mod, l, 8, 0.5, alpha, nl, 0, ln_g[l, 2], ln_b[l, 2])

    y_prompt = x[:mp].reshape(batch, seq, d)
    y_sample = x[mp:].reshape(dec_batch, dec_seq, d)
    return (y_prompt, y_sample, jnp.stack(new_k, axis=1), jnp.stack(new_v, axis=1),
            jnp.stack(new_s, axis=1))
```

```python
import functools
import math

import numpy as np
import jax
import jax.numpy as jnp
from jax import lax
from jax.experimental import pallas as pl
from jax.experimental.pallas import tpu as pltpu

F32 = jnp.float32
BF16 = jnp.bfloat16

GRID_W = 64
N_MOD = 9
HY_BANDS = 8
GLA_TAU = 16.0
GLA_CHUNK = 64
ROPE_THETA = 10000.0
LN_EPS = 1e-5
RMS_EPS = 1e-6

LANES = 128
MXU_DIM = 256
VMEM_BYTES_V7X = 64 * 1024 * 1024
VMEM_LIMIT = VMEM_BYTES_V7X - 8 * 1024 * 1024

GROUP = 4096
FFT_NB = 16
FFT_K1G = 8


def _cp(*sem):
    return pltpu.CompilerParams(dimension_semantics=sem, vmem_limit_bytes=VMEM_LIMIT)


def _dot(a, b):
    return jnp.dot(a, b, preferred_element_type=F32)


def _dot_nt(a, b):
    return lax.dot_general(a, b, (((1,), (1,)), ((), ())), preferred_element_type=F32)


def _dot_tn(a, b):
    return lax.dot_general(a, b, (((0,), (0,)), ((), ())), preferred_element_type=F32)


def _dot_hi(a, b):
    return jnp.dot(a, b, preferred_element_type=F32, precision=lax.Precision.HIGHEST)


def _silu(x):
    return x * jax.nn.sigmoid(x)


def _mod_kernel(c_ref, w_ref, b_ref, o_ref):
    c = c_ref[...]
    o_ref[...] = _dot(_silu(c).astype(BF16), w_ref[...].astype(BF16)) + b_ref[...]


def _modulation(cond, w_mod, b_mod):
    depth, d, n = w_mod.shape
    r = cond.shape[0]
    tn = 1024
    return pl.pallas_call(
        _mod_kernel,
        grid=(depth, n // tn),
        in_specs=[pl.BlockSpec((r, d), lambda l, j: (0, 0)),
                  pl.BlockSpec((None, d, tn), lambda l, j: (l, 0, j)),
                  pl.BlockSpec((None, 1, tn), lambda l, j: (l, 0, j))],
        out_specs=pl.BlockSpec((None, r, tn), lambda l, j: (l, 0, j)),
        out_shape=jax.ShapeDtypeStruct((depth, r, n), F32),
        compiler_params=_cp("parallel", "parallel"),
        name="mod",
    )(cond, w_mod, b_mod.reshape(depth, 1, n))


def _premod_kernel(x_ref, mod_ref, o_ref):
    o_ref[...] = (x_ref[...] * (1.0 + mod_ref[1:2, :]) + mod_ref[0:1, :]).astype(o_ref.dtype)


def _premod(x, mod, layer, tm=512):
    m, d = x.shape
    return pl.pallas_call(
        _premod_kernel,
        grid=(m // tm,),
        in_specs=[pl.BlockSpec((tm, d), lambda i: (i, 0)),
                  pl.BlockSpec((None, None, N_MOD, d), lambda i: (layer, (i * tm) // GROUP, 0, 0))],
        out_specs=pl.BlockSpec((tm, d), lambda i: (i, 0)),
        out_shape=jax.ShapeDtypeStruct((m, d), BF16),
        compiler_params=_cp("parallel"),
        name="premod",
    )(x, mod)


def _ffn1_kernel(h_ref, w1_ref, w3_ref, o_ref):
    h = h_ref[...]
    a = _dot(h, w1_ref[...])
    b = _dot(h, w3_ref[...])
    o_ref[...] = (_silu(a) * b).astype(o_ref.dtype)


def _ffn1(h, w1, w3, tm=1024, tf=512):
    m, d = h.shape
    fp = w1.shape[1]
    return pl.pallas_call(
        _ffn1_kernel,
        grid=(m // tm, fp // tf),
        in_specs=[pl.BlockSpec((tm, d), lambda i, j: (i, 0)),
                  pl.BlockSpec((d, tf), lambda i, j: (0, j)),
                  pl.BlockSpec((d, tf), lambda i, j: (0, j))],
        out_specs=pl.BlockSpec((tm, tf), lambda i, j: (i, j)),
        out_shape=jax.ShapeDtypeStruct((m, fp), BF16),
        compiler_params=_cp("parallel", "arbitrary"),
        name="ffn1",
    )(h, w1, w3)


def _mm_ln_kernel(a_ref, w_ref, x_ref, mod_ref, nmod_ref, g_ref, b_ref, xo_ref, ho_ref, *,
                  gate_row, coef, alpha, nshift_row):
    gate = coef * mod_ref[gate_row:gate_row + 1, :]
    xr = alpha * x_ref[...] + gate * _dot(a_ref[...], w_ref[...])
    mu = jnp.mean(xr, axis=-1, keepdims=True)
    xc = xr - mu
    var = jnp.mean(xc * xc, axis=-1, keepdims=True)
    xn = xc * lax.rsqrt(var + LN_EPS) * g_ref[...] + b_ref[...]
    xo_ref[...] = xn
    ho_ref[...] = (xn * (1.0 + nmod_ref[nshift_row + 1:nshift_row + 2, :])
                   + nmod_ref[nshift_row:nshift_row + 1, :]).astype(ho_ref.dtype)


def _mm_ln(a, w, x, mod, layer, gate_row, coef, alpha, nlayer, nshift_row, ln_g, ln_b, tm=256):
    m, kdim = a.shape
    d = w.shape[1]
    kern = functools.partial(_mm_ln_kernel, gate_row=gate_row, coef=coef, alpha=alpha, nshift_row=nshift_row)
    return pl.pallas_call(
        kern,
        grid=(m // tm,),
        in_specs=[pl.BlockSpec((tm, kdim), lambda i: (i, 0)),
                  pl.BlockSpec((kdim, d), lambda i: (0, 0), pipeline_mode=pl.Buffered(1)),
                  pl.BlockSpec((tm, d), lambda i: (i, 0)),
                  pl.BlockSpec((None, None, N_MOD, d), lambda i: (layer, (i * tm) // GROUP, 0, 0)),
                  pl.BlockSpec((None, None, N_MOD, d), lambda i: (nlayer, (i * tm) // GROUP, 0, 0)),
                  pl.BlockSpec((1, d), lambda i: (0, 0)),
                  pl.BlockSpec((1, d), lambda i: (0, 0))],
        out_specs=[pl.BlockSpec((tm, d), lambda i: (i, 0)),
                   pl.BlockSpec((tm, d), lambda i: (i, 0))],
        out_shape=[jax.ShapeDtypeStruct((m, d), F32), jax.ShapeDtypeStruct((m, d), BF16)],
        compiler_params=_cp("parallel"),
        name="mm_ln",
    )(a, w, x, mod, mod, ln_g.reshape(1, d), ln_b.reshape(1, d))


def _proj_kernel(h_ref, w_ref, o_ref, *, act):
    r = _dot(h_ref[...], w_ref[...])
    if act == "sigmoid":
        r = jax.nn.sigmoid(r)
    o_ref[...] = r.astype(o_ref.dtype)


def _proj(h, w, out_dtype, act=None, tm=1024, tn=1024):
    m, d = h.shape
    n = w.shape[1]
    tn = min(tn, n)
    return pl.pallas_call(
        functools.partial(_proj_kernel, act=act),
        grid=(m // tm, n // tn),
        in_specs=[pl.BlockSpec((tm, d), lambda i, j: (i, 0)),
                  pl.BlockSpec((d, tn), lambda i, j: (0, j))],
        out_specs=pl.BlockSpec((tm, tn), lambda i, j: (i, j)),
        out_shape=jax.ShapeDtypeStruct((m, n), out_dtype),
        compiler_params=_cp("parallel", "arbitrary"),
        name="proj",
    )(h, w)


def _sconv_kernel(z_ref, zp_ref, zn_ref, w_ref, b_ref, o_ref, *, tm, halo, lp, ls):
    i = pl.program_id(0)
    u = z_ref[...].astype(F32)
    prev = zp_ref[...].astype(F32)[halo - 1:halo, :]
    nxt = zn_ref[...].astype(F32)[0:1, :]
    row = lax.broadcasted_iota(jnp.int32, u.shape, 0)
    lseq = jnp.where((i * tm) // GROUP == 0, lp, ls)
    pos = (row + i * tm) & (lseq - 1)
    up = jnp.where(row == 0, prev, pltpu.roll(u, 1, 0))
    up = jnp.where(pos == 0, 0.0, up)
    un = jnp.where(row == tm - 1, nxt, pltpu.roll(u, tm - 1, 0))
    un = jnp.where(pos == lseq - 1, 0.0, un)
    w = w_ref[...]
    o_ref[...] = (up * w[0:1, :] + u * w[1:2, :] + un * w[2:3, :] + b_ref[...]).astype(o_ref.dtype)


def _short_conv(z, conv_w, conv_b, lp, ls, width, tm=1024, ct=512, halo=16):
    m = z.shape[0]
    nct = width // ct
    nrb = m // halo
    kern = functools.partial(_sconv_kernel, tm=tm, halo=halo, lp=lp, ls=ls)
    return pl.pallas_call(
        kern,
        grid=(m // tm, 3 * nct),
        in_specs=[pl.BlockSpec((tm, ct), lambda i, j: (i, j)),
                  pl.BlockSpec((halo, ct), lambda i, j: (jnp.maximum(i * (tm // halo) - 1, 0), j)),
                  pl.BlockSpec((halo, ct), lambda i, j: (jnp.minimum((i + 1) * (tm // halo), nrb - 1), j)),
                  pl.BlockSpec((3, ct), lambda i, j: (0, j)),
                  pl.BlockSpec((1, ct), lambda i, j: (0, j))],
        out_specs=pl.BlockSpec((None, tm, ct), lambda i, j: (j // nct, i, j % nct)),
        out_shape=jax.ShapeDtypeStruct((3, m, width), F32),
        compiler_params=_cp("parallel", "parallel"),
        name="sconv",
    )(z, z, z, conv_w, conv_b.reshape(1, -1))


def _filter_features(L):
    t = np.linspace(0.0, 1.0, L, dtype=np.float32)
    w = (np.float32(2.0 * math.pi / L) * np.arange(L, dtype=np.float32)).astype(np.float32)
    f = np.linspace(1e-4, HY_BANDS - 1, HY_BANDS, dtype=np.float32)
    wf = (w[:, None] * f).astype(np.float32)
    feats = np.concatenate([t[:, None], np.cos(wf), -np.sin(wf)], -1).astype(np.float32)
    idx = np.concatenate([np.arange(L), [0], np.arange(L - 1, 0, -1)])
    tab = np.zeros((2 * L, 32), np.float32)
    tab[:, :feats.shape[1]] = feats[idx]
    tab[:, 24] = t[idx]
    tab[:L, 25] = 1.0
    tab[L + 1:, 26] = -1.0
    return tab, feats.shape[1]


def _filter_kernel(tab_ref, w1_ref, b1_ref, fr_ref, w2_ref, b2_ref, w3_ref, dec_ref, o_ref, *, width):
    tab = tab_ref[...]
    fr = fr_ref[...]
    hdn = jnp.sin(fr[0:1, :] * (_dot_hi(tab, w1_ref[...]) + b1_ref[...]))
    hdn = jnp.sin(fr[1:2, :] * (_dot_hi(hdn, w2_ref[...]) + b2_ref[...]))
    t = tab[:, 24:25]
    mf = tab[:, 25:26]
    mb = tab[:, 26:27]
    h = _dot_hi(hdn, w3_ref[...]) * jnp.exp(-t * jnp.abs(dec_ref[...]))
    for o in range(2):
        base = 2 * width * o
        o_ref[o] = mf * h[:, base:base + width] + mb * h[:, base + width:base + 2 * width]


def _hyena_filter_taps(L, hy_w1, hy_b1, hy_freq, hy_w2, hy_b2, hy_w3, hy_decay, width, rb=256):
    tab_np, nfeat = _filter_features(L)
    fh = hy_w1.shape[1]
    w1p = jnp.zeros((32, fh), F32).at[:nfeat].set(hy_w1)
    n = 2 * L
    return pl.pallas_call(
        functools.partial(_filter_kernel, width=width),
        grid=(n // rb,),
        in_specs=[pl.BlockSpec((rb, 32), lambda i: (i, 0)),
                  pl.BlockSpec((32, fh), lambda i: (0, 0)),
                  pl.BlockSpec((1, fh), lambda i: (0, 0)),
                  pl.BlockSpec((2, fh), lambda i: (0, 0)),
                  pl.BlockSpec((fh, fh), lambda i: (0, 0)),
                  pl.BlockSpec((1, fh), lambda i: (0, 0)),
                  pl.BlockSpec((fh, 4 * width), lambda i: (0, 0)),
                  pl.BlockSpec((1, 4 * width), lambda i: (0, 0))],
        out_specs=pl.BlockSpec((2, rb, width), lambda i: (0, i, 0)),
        out_shape=jax.ShapeDtypeStruct((2, n, width), F32),
        compiler_params=_cp("parallel"),
        name="hyfilter",
    )(jnp.asarray(tab_np), w1p, hy_b1.reshape(1, fh), hy_freq, hy_w2, hy_b2.reshape(1, fh), hy_w3,
      hy_decay.reshape(1, -1))


def _dft_a(na, ka):
    k1 = np.arange(na // 2)[:, None].astype(np.float64)
    a = np.arange(ka)[None, :].astype(np.float64)
    th = 2.0 * np.pi * a * (k1 + 0.5) / na
    return np.concatenate([np.cos(th), -np.sin(th)], 0)


def _idft_a(na, n, rows):
    k1 = np.arange(na // 2)[None, :].astype(np.float64)
    a = np.arange(rows)[:, None].astype(np.float64)
    th = 2.0 * np.pi * a * (k1 + 0.5) / na
    return (2.0 / n) * np.concatenate([np.cos(th), -np.sin(th)], 1)


def _dft_b(na, nb, g):
    n = na * nb
    half = na // 2
    k1 = np.arange(half).astype(np.float64)
    b = np.arange(nb).astype(np.float64)
    k2 = np.arange(nb).astype(np.float64)
    phi = 2.0 * np.pi * (b[None, None, :] * k2[None, :, None] / nb
                         + b[None, None, :] * (k1[:, None, None] + 0.5) / n)
    c, s = np.cos(phi), np.sin(phi)
    ng = half // g
    r = g * nb
    fwd = np.zeros((ng, 2 * r, 2 * r))
    for q in range(g):
        rows = slice(q * nb, (q + 1) * nb)
        rows_i = slice(r + q * nb, r + (q + 1) * nb)
        cols = slice(q, r, g)
        cols_i = slice(r + q, 2 * r, g)
        cq, sq = c[q::g], s[q::g]
        fwd[:, rows, cols] = cq
        fwd[:, rows, cols_i] = sq
        fwd[:, rows_i, cols] = -sq
        fwd[:, rows_i, cols_i] = cq
    inv = np.transpose(fwd, (0, 2, 1))
    return fwd, inv


def _lmm_kernel(f_ref, x_ref, o_ref):
    o_ref[...] = _dot(f_ref[...], x_ref[...].astype(BF16)).astype(o_ref.dtype)


def _lmm(f, x, x_index, nbatch, out_dtype, tn=2048):
    mo, k = f.shape
    n = x.shape[-1]
    lead = len(x.shape) - 2
    return pl.pallas_call(
        _lmm_kernel,
        grid=(nbatch, n // tn),
        in_specs=[pl.BlockSpec((mo, k), lambda b, j: (0, 0)),
                  pl.BlockSpec((None,) * lead + (k, tn), lambda b, j: x_index(b) + (0, j))],
        out_specs=pl.BlockSpec((None, mo, tn), lambda b, j: (b, 0, j)),
        out_shape=jax.ShapeDtypeStruct((nbatch, mo, n), out_dtype),
        compiler_params=_cp("parallel", "parallel"),
        name="dft_a",
    )(f, x)


def _stage_a_fwd(x_ref, f1_ref, r_ref, nb, ct):
    for b in range(nb):
        r_ref[:, b * ct:(b + 1) * ct] = _dot(f1_ref[...], x_ref[:, b, :].astype(BF16))


def _stage_b_rows(r_ref, r0, half, nb, kg, ct):
    return [(pl.ds(ri * half + r0, kg), slice(b * ct, (b + 1) * ct)) for ri in range(2) for b in range(nb)]


def _lconv_kernel(x_ref, xg_ref, f1_ref, g_ref, tb_ref, tbi_ref, kf_ref, bias_ref, o_ref, r_ref, *,
                  nb, half, kg, ct):
    r = nb * kg
    _stage_a_fwd(x_ref, f1_ref, r_ref, nb, ct)

    for g in range(half // kg):
        tiles = _stage_b_rows(r_ref, g * kg, half, nb, kg, ct)
        a = jnp.concatenate([r_ref[rs, cs] for rs, cs in tiles], axis=0).astype(BF16)
        x = _dot(tb_ref[g], a)
        kf = kf_ref[g * 2 * r:(g + 1) * 2 * r, :]
        xr, xi = x[:r], x[r:]
        kr, ki = kf[:r], kf[r:]
        y = jnp.concatenate([xr * kr - xi * ki, xr * ki + xi * kr], axis=0).astype(BF16)
        bh = _dot(tbi_ref[g], y)
        for t, (rs, cs) in enumerate(tiles):
            r_ref[rs, cs] = bh[t * kg:(t + 1) * kg, :]
    bias = bias_ref[...]
    for b in range(nb):
        y = _dot(g_ref[...], r_ref[:, b * ct:(b + 1) * ct].astype(BF16))
        o_ref[:, b, :] = xg_ref[:, b, :] * (y + x_ref[:, b, :] * bias)


def _lconv(x, x_index, xg, xg_index, f1, g_inv, tb, tbi, kf, order, bias, nbatch, ct=256):
    ka, nb, c = x.shape[-3:]
    n2, _ = f1.shape
    ng, r2, _ = tb.shape
    kg = r2 // (2 * nb)
    lx, lg = len(x.shape) - 3, len(xg.shape) - 3
    once = dict(pipeline_mode=pl.Buffered(1))
    kern = functools.partial(_lconv_kernel, nb=nb, half=n2 // 2, kg=kg, ct=ct)
    return pl.pallas_call(
        kern,
        grid=(c // ct, nbatch),
        in_specs=[pl.BlockSpec((None,) * lx + (ka, nb, ct), lambda j, b: x_index(b) + (0, 0, j)),
                  pl.BlockSpec((None,) * lg + (ka, nb, ct), lambda j, b: xg_index(b) + (0, 0, j)),
                  pl.BlockSpec(f1.shape, lambda j, b: (0, 0), **once),
                  pl.BlockSpec(g_inv.shape, lambda j, b: (0, 0), **once),
                  pl.BlockSpec(tb.shape, lambda j, b: (0, 0, 0), **once),
                  pl.BlockSpec(tbi.shape, lambda j, b: (0, 0, 0), **once),
                  pl.BlockSpec((None, ng * r2, ct), lambda j, b: (order, 0, j), **once),
                  pl.BlockSpec((1, ct), lambda j, b: (0, j))],
        out_specs=pl.BlockSpec((None, ka, nb, ct), lambda j, b: (b, 0, 0, j)),
        out_shape=jax.ShapeDtypeStruct((nbatch, ka, nb, c), F32),
        scratch_shapes=[pltpu.VMEM((n2, nb * ct), F32)],
        compiler_params=_cp("parallel", "arbitrary"),
        name="lconv",
    )(x, xg, f1, g_inv, tb, tbi, kf, bias.reshape(1, c))


def _lconv_filter_kernel(x_ref, f1_ref, tb_ref, o_ref, r_ref, *, nb, half, kg, ct):
    r2 = 2 * nb * kg
    _stage_a_fwd(x_ref, f1_ref, r_ref, nb, ct)

    for g in range(half // kg):
        tiles = _stage_b_rows(r_ref, g * kg, half, nb, kg, ct)
        a = jnp.concatenate([r_ref[rs, cs] for rs, cs in tiles], axis=0).astype(BF16)
        o_ref[g * r2:(g + 1) * r2, :] = _dot(tb_ref[g], a)


def _lconv_filter(taps, f1, tb, ct=256):
    norder, na, nb, c = taps.shape
    n2, _ = f1.shape
    ng, r2, _ = tb.shape
    kg = r2 // (2 * nb)
    kern = functools.partial(_lconv_filter_kernel, nb=nb, half=n2 // 2, kg=kg, ct=ct)
    return pl.pallas_call(
        kern,
        grid=(norder, c // ct),
        in_specs=[pl.BlockSpec((None, na, nb, ct), lambda o, j: (o, 0, 0, j)),
                  pl.BlockSpec(f1.shape, lambda o, j: (0, 0)),
                  pl.BlockSpec(tb.shape, lambda o, j: (0, 0, 0))],
        out_specs=pl.BlockSpec((None, ng * r2, ct), lambda o, j: (o, 0, j)),
        out_shape=jax.ShapeDtypeStruct((norder, ng * r2, c), F32),
        scratch_shapes=[pltpu.VMEM((n2, nb * ct), F32)],
        compiler_params=_cp("parallel", "parallel"),
        name="lconv_filter",
    )(taps, f1, tb)


def _pconv_kernel(fd_ref, gd_ref, v_ref, kf_ref, xg_ref, bias_ref, o_ref, *, half):
    v = v_ref[...]
    x = _dot(fd_ref[...], v.astype(BF16))
    kf = kf_ref[...]
    xr, xi = x[:half], x[half:]
    kr, ki = kf[:half], kf[half:]
    y = jnp.concatenate([xr * kr - xi * ki, xr * ki + xi * kr], axis=0).astype(BF16)
    yt = _dot(gd_ref[...], y)
    o_ref[...] = (xg_ref[...].astype(F32) * (yt + v.astype(F32) * bias_ref[...])).astype(o_ref.dtype)


def _pconv(fd, gd, vsrc, v_lead, kf, order, xsrc, x_lead, bias, nseq, L, ct=512):
    c = kf.shape[-1]
    n2 = fd.shape[0]
    return pl.pallas_call(
        functools.partial(_pconv_kernel, half=n2 // 2),
        grid=(nseq, c // ct),
        in_specs=[pl.BlockSpec((n2, L), lambda s, j: (0, 0)),
                  pl.BlockSpec((L, n2), lambda s, j: (0, 0)),
                  pl.BlockSpec((None,) * len(v_lead) + (L, ct), lambda s, j: v_lead + (s, j)),
                  pl.BlockSpec((None, n2, ct), lambda s, j: (order, 0, j)),
                  pl.BlockSpec((None,) * len(x_lead) + (L, ct), lambda s, j: x_lead + (s, j)),
                  pl.BlockSpec((1, ct), lambda s, j: (0, j))],
        out_specs=pl.BlockSpec((L, ct), lambda s, j: (s, j)),
        out_shape=jax.ShapeDtypeStruct((nseq * L, c), F32),
        compiler_params=_cp("parallel", "parallel"),
        name="pconv",
    )(fd, gd, vsrc, kf, xsrc, bias.reshape(1, c))


def _gate_kernel(lr_ref, wa_ref, ba_ref, o_ref):
    logits = _dot_hi(lr_ref[...], wa_ref[...]) + ba_ref[...]
    o_ref[...] = jax.nn.log_sigmoid(logits) * (1.0 / GLA_TAU)


def _gla_gates(lr, wa_cat, ba_cat, tm=1024):
    m, k = lr.shape
    n = wa_cat.shape[1]
    return pl.pallas_call(
        _gate_kernel,
        grid=(m // tm,),
        in_specs=[pl.BlockSpec((tm, k), lambda i: (i, 0)),
                  pl.BlockSpec((k, n), lambda i: (0, 0)),
                  pl.BlockSpec((1, n), lambda i: (0, 0))],
        out_specs=pl.BlockSpec((tm, n), lambda i: (i, 0)),
        out_shape=jax.ShapeDtypeStruct((m, n), F32),
        compiler_params=_cp("parallel"),
        name="gla_gates",
    )(lr, wa_cat, ba_cat)


def _split3(x):
    hi = x.astype(BF16)
    r1 = x - hi.astype(F32)
    mid = r1.astype(BF16)
    lo = (r1 - mid.astype(F32)).astype(BF16)
    return hi, mid, lo


def _gla_dir(d, r0, q_ref, k_ref, v_ref, la_ref, o_ref, st_ref, tri, causal, ref_row, last_row,
             heads, dk, dv, scale):
    ch = GLA_CHUNK
    rows = pl.ds(pl.multiple_of(r0, ch), ch)
    la = la_ref[rows, :]
    hi, mid, lo = _split3(la)
    b = _dot(tri, hi) + _dot(tri, mid) + _dot(tri, lo)
    bref = b[ref_row:ref_row + 1, :]
    blast = b[last_row:last_row + 1, :]
    q = q_ref[rows, :].astype(F32) * scale
    k = k_ref[rows, :].astype(F32)
    qt = (q * jnp.exp(b - bref)).astype(BF16)
    kt = (k * jnp.exp(bref - b)).astype(BF16)
    qin = (q * jnp.exp(b)).astype(BF16)
    kst = (k * jnp.exp(blast - b)).astype(BF16)
    dec = jnp.exp(blast)
    for h in range(heads):
        ks = slice(h * dk, (h + 1) * dk)
        vs = slice(h * dv, (h + 1) * dv)
        att = _dot_nt(qt[:, ks], kt[:, ks])
        att = jnp.where(causal, att, 0.0).astype(BF16)
        vh = v_ref[rows, vs]
        st = st_ref[d, h]
        o_ref[rows, vs] = _dot(att, vh) + _dot_nt(qin[:, ks], st.astype(BF16))
        st_ref[d, h] = st * dec[:, ks] + _dot_tn(vh, kst[:, ks])


def _gla_kernel(*refs, nch, heads, dk, dv, has_s0, scale):
    if has_s0:
        (qf, kf, vf, laf, qb, kb, vb, lab, s0, of, ob, st) = refs
    else:
        (qf, kf, vf, laf, qb, kb, vb, lab, of, ob, st) = refs
        s0 = None
    ch = GLA_CHUNK

    @pl.when(pl.program_id(1) == 0)
    def _():
        if has_s0:
            st[...] = s0[...]
        else:
            st[...] = jnp.zeros_like(st)

    r_i = lax.broadcasted_iota(jnp.int32, (ch, ch), 0)
    c_i = lax.broadcasted_iota(jnp.int32, (ch, ch), 1)
    lower = r_i >= c_i
    upper = r_i <= c_i
    tri_l = jnp.where(lower, 1.0, 0.0).astype(BF16)
    tri_u = jnp.where(upper, 1.0, 0.0).astype(BF16)

    def body(c, carry):
        _gla_dir(0, c * ch, qf, kf, vf, laf, of, st, tri_l, lower, ch // 2 - 1, ch - 1,
                 heads, dk, dv, scale)
        _gla_dir(1, (nch - 1 - c) * ch, qb, kb, vb, lab, ob, st, tri_u, upper, ch // 2, 0,
                 heads, dk, dv, scale)
        return carry

    lax.fori_loop(0, nch, body, 0)


def _gla(zmain, qcol, kcol, vcol, la, row0, nseq, L, tb, heads, dk, dv, s0t=None):
    nblk = L // tb
    rb0 = row0 // tb
    hk, hv = heads * dk, heads * dv

    def fwd(s, j):
        return rb0 + s * nblk + j

    def bwd(s, j):
        return rb0 + s * nblk + (nblk - 1 - j)

    in_specs = []
    for rowf, lcol in ((fwd, 0), (bwd, 1)):
        in_specs += [pl.BlockSpec((tb, hk), lambda s, j, rowf=rowf: (rowf(s, j), qcol)),
                     pl.BlockSpec((tb, hk), lambda s, j, rowf=rowf: (rowf(s, j), kcol)),
                     pl.BlockSpec((tb, hv), lambda s, j, rowf=rowf: (rowf(s, j), vcol)),
                     pl.BlockSpec((tb, hk), lambda s, j, rowf=rowf, lcol=lcol: (rowf(s, j), lcol))]
    args = [zmain, zmain, zmain, la, zmain, zmain, zmain, la]
    if s0t is not None:
        in_specs.append(pl.BlockSpec((None, 2, heads, dv, dk), lambda s, j: (s, 0, 0, 0, 0)))
        args.append(s0t)
    kern = functools.partial(_gla_kernel, nch=tb // GLA_CHUNK, heads=heads, dk=dk, dv=dv,
                             has_s0=s0t is not None, scale=dk ** -0.5)
    return pl.pallas_call(
        kern,
        grid=(nseq, nblk),
        in_specs=in_specs,
        out_specs=[pl.BlockSpec((tb, hv), lambda s, j: (s * nblk + j, 0)),
                   pl.BlockSpec((tb, hv), lambda s, j: (s * nblk + (nblk - 1 - j), 0)),
                   pl.BlockSpec((None, 2, heads, dv, dk), lambda s, j: (s, 0, 0, 0, 0))],
        out_shape=[jax.ShapeDtypeStruct((nseq * L, hv), F32),
                   jax.ShapeDtypeStruct((nseq * L, hv), F32),
                   jax.ShapeDtypeStruct((nseq, 2, heads, dv, dk), F32)],
        compiler_params=_cp("parallel", "arbitrary"),
        name="gla",
    )(*args)


def _gla_post_kernel(of_ref, ob_ref, gr_ref, g_ref, o_ref, *, heads, dv):
    o = of_ref[...] + ob_ref[...]
    gate = _silu(gr_ref[...].astype(F32))
    g = g_ref[...]
    for h in range(heads):
        sl = slice(h * dv, (h + 1) * dv)
        oh = o[:, sl]
        ms = jnp.mean(oh * oh, axis=-1, keepdims=True)
        o_ref[:, sl] = (oh * lax.rsqrt(ms + RMS_EPS) * g * gate[:, sl]).astype(o_ref.dtype)


def _gla_post(o_f, o_b, zmain, grcol, norm_g, heads, dv, tm=512):
    m, hv = o_f.shape
    return pl.pallas_call(
        functools.partial(_gla_post_kernel, heads=heads, dv=dv),
        grid=(m // tm,),
        in_specs=[pl.BlockSpec((tm, hv), lambda i: (i, 0)),
                  pl.BlockSpec((tm, hv), lambda i: (i, 0)),
                  pl.BlockSpec((tm, hv), lambda i: (i, grcol)),
                  pl.BlockSpec((1, dv), lambda i: (0, 0))],
        out_specs=pl.BlockSpec((tm, hv), lambda i: (i, 0)),
        out_shape=jax.ShapeDtypeStruct((m, hv), BF16),
        compiler_params=_cp("parallel"),
        name="gla_post",
    )(o_f, o_b, zmain, norm_g.reshape(1, dv))


def _rope_tables(L, dh):
    rows = L // GRID_W
    r = np.repeat(np.arange(rows, dtype=np.float32), GRID_W)
    col = np.tile(np.arange(GRID_W, dtype=np.float32), rows)
    nf = dh // 4
    inv = (np.float32(ROPE_THETA) ** (-np.arange(nf, dtype=np.float32) / nf)).astype(np.float32)
    ang_r = (r[:, None] * inv).astype(np.float32)
    ang_c = (col[:, None] * inv).astype(np.float32)
    cos = np.concatenate([np.cos(ang_r), np.cos(ang_r), np.cos(ang_c), np.cos(ang_c)], -1)
    sin = np.concatenate([-np.sin(ang_r), np.sin(ang_r), -np.sin(ang_c), np.sin(ang_c)], -1)
    cos = np.concatenate([cos, cos], -1).astype(np.float32)
    sin = np.concatenate([sin, sin], -1).astype(np.float32)
    cos_t = np.stack([np.ones_like(cos), cos])
    sin_t = np.stack([np.zeros_like(sin), sin])
    return cos_t, sin_t


def _qkv_prep_kernel(q_ref, k_ref, v_ref, cos_ref, sin_ref, qo_ref, ko_ref, vo_ref, *, heads, hw, nf, scale):
    cos = cos_ref[...]
    sin = sin_ref[...]
    lane = lax.broadcasted_iota(jnp.int32, cos.shape, 1)
    first = (lane & (2 * nf - 1)) < nf

    def rope(x):
        sw = jnp.where(first, pltpu.roll(x, hw - nf, 1), pltpu.roll(x, nf, 1))
        return x * cos + sw * sin

    ones = jnp.ones((q_ref.shape[0], hw), vo_ref.dtype)
    for h in range(heads):
        sl = slice(h * hw, (h + 1) * hw)
        qo_ref[:, sl] = (rope(q_ref[:, sl]) * scale).astype(qo_ref.dtype)
        ko_ref[sl, :] = rope(k_ref[:, sl]).T.astype(ko_ref.dtype)
        vo_ref[:, 2 * h * hw:(2 * h + 1) * hw] = v_ref[:, sl].astype(vo_ref.dtype)
        vo_ref[:, (2 * h + 1) * hw:(2 * h + 2) * hw] = ones


def _qkv_prep(zd, cos_t, sin_t, heads, dh, tm=512):
    m = zd.shape[0]
    w = zd.shape[1] // 3
    hw = w // heads
    gb = GROUP // tm
    kern = functools.partial(_qkv_prep_kernel, heads=heads, hw=hw, nf=dh // 4,
                             scale=dh ** -0.5 * math.log2(math.e))
    tab = pl.BlockSpec((None, tm, hw), lambda i: (jnp.minimum(i // gb, 1), i % gb, 0))
    out = jax.ShapeDtypeStruct((m, w), BF16)
    return pl.pallas_call(
        kern,
        grid=(m // tm,),
        in_specs=[pl.BlockSpec((tm, w), lambda i: (i, 0)),
                  pl.BlockSpec((tm, w), lambda i: (i, 1)),
                  pl.BlockSpec((tm, w), lambda i: (i, 2)),
                  tab, tab],
        out_specs=[pl.BlockSpec((tm, w), lambda i: (i, 0)),
                   pl.BlockSpec((w, tm), lambda i: (0, i)),
                   pl.BlockSpec((tm, 2 * w), lambda i: (i, 0))],
        out_shape=[out, jax.ShapeDtypeStruct((w, m), BF16), jax.ShapeDtypeStruct((m, 2 * w), BF16)],
        compiler_params=_cp("parallel"),
        name="qkv_prep",
    )(zd, zd, zd, cos_t, sin_t)


def _attn_kernel(*refs, has_ctx, lam_init, dh, bf16_exp, tq):
    if has_ctx:
        q_ref, k_ref, v_ref, kc_ref, vc_ref, lam_ref, g_ref, o_ref, s0_ref, s1_ref = refs
    else:
        q_ref, k_ref, v_ref, lam_ref, g_ref, o_ref, s0_ref, s1_ref = refs
    hw = 2 * dh
    lk = k_ref.shape[1]
    nsub = q_ref.shape[0] // tq
    lane = lax.broadcasted_iota(jnp.int32, (tq, hw), 1)
    first = lane < dh
    if has_ctx:
        kct = kc_ref[...].T.astype(BF16)
        vcv = vc_ref[...].astype(BF16)
        vce = jnp.concatenate([vcv, jnp.ones_like(vcv)], axis=1)

    def scores(i, j, s_ref):
        q = q_ref[pl.ds(i * tq if isinstance(i, int) else pl.multiple_of(i * tq, tq), tq), :]
        qj = jnp.where(first, q, jnp.zeros_like(q)) if j == 0 else jnp.where(first, jnp.zeros_like(q), q)
        s_ref[:, :lk] = _dot(qj, k_ref[...])
        if has_ctx:
            s_ref[:, lk:] = _dot(qj, kct)

    def prob(t):
        if bf16_exp:
            return jnp.exp2(t.astype(BF16))
        return jnp.exp2(t).astype(BF16)

    def softmax_pv(s_ref):
        s = s_ref[...]
        p = prob(s - jnp.max(s, axis=-1, keepdims=True))
        acc = _dot(p[:, :lk], v_ref[...])
        if has_ctx:
            acc = acc + _dot(p[:, lk:], vce)
        return acc[:, :hw] / acc[:, hw:]

    lp = lam_ref[...]
    lam = (jnp.exp(jnp.sum(lp[0:1] * lp[1:2], axis=-1, keepdims=True))
           - jnp.exp(jnp.sum(lp[2:3] * lp[3:4], axis=-1, keepdims=True)) + lam_init)
    gain = g_ref[...] * (1.0 - lam_init)

    scores(0, 0, s0_ref)

    def body(i, carry):
        scores(i, 1, s1_ref)
        sm0 = softmax_pv(s0_ref)
        scores(jnp.minimum(i + 1, nsub - 1), 0, s0_ref)
        o = sm0 - lam * softmax_pv(s1_ref)
        ms = jnp.mean(o * o, axis=-1, keepdims=True)
        o_ref[pl.ds(pl.multiple_of(i * tq, tq), tq), :] = (o * lax.rsqrt(ms + RMS_EPS) * gain).astype(o_ref.dtype)
        return carry

    lax.fori_loop(0, nsub, body, 0)


def _diff_attention(qs, kr, vb, row0, nseq, L, tb, heads, dh, lam_p, norm_g, lam_init, ctx=None, tq=256):
    hw = 2 * dh
    nqb = L // tb
    qb0 = row0 // tb
    kb0 = row0 // L
    past = 0 if ctx is None else ctx[0].shape[2]
    in_specs = [pl.BlockSpec((tb, hw), lambda s, h, i: (qb0 + s * nqb + i, h)),
                pl.BlockSpec((hw, L), lambda s, h, i: (h, kb0 + s)),
                pl.BlockSpec((L, 2 * hw), lambda s, h, i: (kb0 + s, h))]
    args = [qs, kr, vb]
    if ctx is not None:
        ck, cv, layer = ctx
        in_specs += [pl.BlockSpec((None, None, past, hw), lambda s, h, i: (s, layer, 0, h)),
                     pl.BlockSpec((None, None, past, hw), lambda s, h, i: (s, layer, 0, h))]
        args += [ck, cv]
    in_specs += [pl.BlockSpec(lam_p.shape, lambda s, h, i: (0, 0)),
                 pl.BlockSpec((1, hw), lambda s, h, i: (0, 0))]
    args += [lam_p, norm_g.reshape(1, hw)]
    kern = functools.partial(_attn_kernel, has_ctx=ctx is not None, lam_init=lam_init, dh=dh,
                             bf16_exp=L > 1024, tq=tq)
    return pl.pallas_call(
        kern,
        grid=(nseq, heads, nqb),
        in_specs=in_specs,
        out_specs=pl.BlockSpec((tb, hw), lambda s, h, i: (s * nqb + i, h)),
        out_shape=jax.ShapeDtypeStruct((nseq * L, heads * hw), BF16),
        scratch_shapes=[pltpu.VMEM((tq, L + past), F32), pltpu.VMEM((tq, L + past), F32)],
        compiler_params=_cp("parallel", "parallel", "arbitrary"),
        name="diff_attn",
    )(*args)


def _mix_kernel(ya_ref, yb_ref, yc_ref, g_ref, w_ref, o_ref, acc_ref):
    k = pl.program_id(1)

    def contrib(y_ref):
        return g_ref[...].astype(F32) * _dot(y_ref[...].astype(BF16), w_ref[...])

    @pl.when(k == 0)
    def _():
        acc_ref[...] = contrib(ya_ref)

    @pl.when(k == 1)
    def _():
        acc_ref[...] += contrib(yb_ref)

    @pl.when(k == 2)
    def _():
        o_ref[...] = (acc_ref[...] + contrib(yc_ref)).astype(o_ref.dtype)


def _mix(ya, yb, yc, gates, wbr, tm=512):
    m, w = ya.shape
    d = wbr.shape[2]
    return pl.pallas_call(
        _mix_kernel,
        grid=(m // tm, 3),
        in_specs=[pl.BlockSpec((tm, w), lambda i, k: (i, 0)),
                  pl.BlockSpec((tm, w), lambda i, k: (i, 0)),
                  pl.BlockSpec((tm, w), lambda i, k: (i, 0)),
                  pl.BlockSpec((tm, d), lambda i, k: (i, k)),
                  pl.BlockSpec((None, w, d), lambda i, k: (k, 0, 0))],
        out_specs=pl.BlockSpec((tm, d), lambda i, k: (i, 0)),
        out_shape=jax.ShapeDtypeStruct((m, d), BF16),
        scratch_shapes=[pltpu.VMEM((tm, d), F32)],
        compiler_params=_cp("parallel", "arbitrary"),
        name="mix",
    )(ya, yb, yc, gates, wbr)


def _pad_cols(w, n):
    return jnp.pad(w, ((0, 0), (0, n - w.shape[1])))


def kernel(x_prompt, x_sample, cache_k, cache_v, state_gla, c, c_ctx, w_mod, b_mod, ln_g, ln_b, ffn_w1, ffn_w3, ffn_w2, w_in, hy_conv_w, hy_conv_b, hy_w1, hy_b1, hy_freq, hy_w2, hy_b2, hy_w3, hy_decay, hy_bias, gla_wa, gla_ba, gla_norm_g, diff_lam, diff_norm_g, w_branch_a, w_branch_b, w_branch_c, w_out):
    batch, seq, d = x_prompt.shape
    dec_batch, dec_seq, _ = x_sample.shape
    depth = w_mod.shape[0]
    ffn_dim = ffn_w1.shape[3]
    hy_w = hy_bias.shape[2]
    heads_g, dk_g = 4, gla_wa.shape[3] // 4
    dv_g = gla_norm_g.shape[1]
    rank = gla_wa.shape[2]
    dh = diff_lam.shape[2]
    heads_d = cache_k.shape[3]
    dw = heads_d * 2 * dh
    gw = heads_g * dv_g
    gk = heads_g * dk_g
    assert batch * seq == GROUP and dec_seq == GROUP
    mp = batch * seq
    m = mp + dec_batch * dec_seq
    ngroups = 1 + dec_batch
    alpha = (2 * depth) ** 0.25
    lam_inits = [0.8 - 0.6 * math.exp(-0.3 * l) for l in range(depth)]

    c_main = 3 * hy_w + 2 * gk + 2 * gw
    c_lr = c_main
    c_d = c_lr + 2 * rank
    c_g = c_d + 3 * dw

    cond = jnp.concatenate([c_ctx[None], c, jnp.zeros((16 - ngroups, d), F32)], axis=0)
    mod = _modulation(cond, w_mod, b_mod)[:, :ngroups].reshape(depth, ngroups, N_MOD, d)

    na = 2 * dec_seq // FFT_NB
    n_s = 2 * dec_seq
    f1_half = jnp.asarray(_dft_a(na, na // 2), BF16)
    f1_full = jnp.asarray(_dft_a(na, na), BF16)
    g_s = jnp.asarray(_idft_a(na, n_s, na // 2), BF16)
    tb_np, tbi_np = _dft_b(na, FFT_NB, FFT_K1G)
    tb, tbi = jnp.asarray(tb_np, BF16), jnp.asarray(tbi_np, BF16)
    fp_half = jnp.asarray(_dft_a(2 * seq, seq), BF16)
    fp_full = jnp.asarray(_dft_a(2 * seq, 2 * seq), BF16)
    g_p = jnp.asarray(_idft_a(2 * seq, 2 * seq, seq), BF16)
    cos_t, sin_t = _rope_tables(dec_seq, dh)
    cos_t, sin_t = jnp.asarray(cos_t), jnp.asarray(sin_t)
    half_rows = (na // 2) * FFT_NB
    lanes_s = FFT_NB * hy_w

    x = jnp.concatenate([x_prompt.reshape(mp, d), x_sample.reshape(dec_batch * dec_seq, d)], axis=0)
    h = _premod(x, mod, 0)

    fp = ((ffn_dim + 511) // 512) * 512
    ck = cache_k.reshape(dec_batch, depth, cache_k.shape[2], dw)
    cv = cache_v.reshape(dec_batch, depth, cache_v.shape[2], dw)
    new_k, new_v, new_s = [], [], []

    for l in range(depth):
        w1 = _pad_cols(ffn_w1[l, 0].astype(BF16), fp)
        w3 = _pad_cols(ffn_w3[l, 0].astype(BF16), fp)
        w2 = jnp.pad(ffn_w2[l, 0].astype(BF16), ((0, fp - ffn_dim), (0, 0)))
        hid = _ffn1(h, w1, w3)
        x, h = _mm_ln(hid, w2, x, mod, l, 2, 0.5, alpha, l, 3, ln_g[l, 0], ln_b[l, 0])

        wi = w_in[l]
        zmain = _proj(h, wi[:, :c_main].astype(BF16), BF16)
        zlr = _proj(h, _pad_cols(wi[:, c_lr:c_d].astype(BF16), LANES), F32)
        zd = _proj(h, wi[:, c_d:c_g].astype(BF16), F32)
        gates = _proj(h, wi[:, c_g:].astype(BF16), BF16, act="sigmoid")

        u3 = _short_conv(zmain, hy_conv_w[l], hy_conv_b[l], seq, dec_seq, hy_w)
        u5 = u3.reshape(3, ngroups, na // 2, FFT_NB, hy_w)
        fargs = (hy_w1[l], hy_b1[l], hy_freq[l], hy_w2[l], hy_b2[l], hy_w3[l], hy_decay[l], hy_w)
        taps_s = _hyena_filter_taps(dec_seq, *fargs)
        taps_p = _hyena_filter_taps(seq, *fargs)
        kf_s = _lconv_filter(taps_s.reshape(2, na, FFT_NB, hy_w), f1_full, tb)
        kf_p = _lmm(fp_full, taps_p, lambda b: (b,), 2, F32, tn=hy_w)

        z1s = _lconv(u5, lambda b: (0, b + 1), u5, lambda b: (1, b + 1), f1_half, g_s, tb, tbi, kf_s, 0,
                     hy_bias[l, 0], dec_batch)
        z2s = _lconv(z1s, lambda b: (b,), u5, lambda b: (2, b + 1), f1_half, g_s, tb, tbi, kf_s, 1,
                     hy_bias[l, 1], dec_batch)
        ya_s = z2s.reshape(dec_batch * dec_seq, hy_w)
        z1 = _pconv(fp_half, g_p, u3, (0,), kf_p, 0, u3, (1,), hy_bias[l, 0], batch, seq)
        ya_p = _pconv(fp_half, g_p, z1, (), kf_p, 1, u3, (2,), hy_bias[l, 1], batch, seq)
        ya = jnp.concatenate([ya_p, ya_s], axis=0)

        wa_cat = jnp.zeros((LANES, 2 * gk), F32)
        wa_cat = wa_cat.at[:rank, :gk].set(gla_wa[l, 0]).at[rank:2 * rank, gk:].set(gla_wa[l, 1])
        ba_cat = jnp.concatenate([gla_ba[l, 0], gla_ba[l, 1]]).reshape(1, 2 * gk)
        la = _gla_gates(zlr, wa_cat, ba_cat)
        qcol, kcol, vcol, grcol = (3 * hy_w) // gk, (3 * hy_w) // gk + 1, (3 * hy_w + 2 * gk) // gw, \
            (3 * hy_w + 2 * gk) // gw + 1
        s0t = jnp.swapaxes(state_gla[:, l], -1, -2)
        of_p, ob_p, st_p = _gla(zmain, qcol, kcol, vcol, la, 0, batch, seq, seq, heads_g, dk_g, dv_g)
        of_s, ob_s, _ = _gla(zmain, qcol, kcol, vcol, la, mp, dec_batch, dec_seq, 512, heads_g, dk_g, dv_g,
                             s0t=s0t)
        o_f = jnp.concatenate([of_p, of_s], axis=0)
        o_b = jnp.concatenate([ob_p, ob_s], axis=0)
        yb = _gla_post(o_f, o_b, zmain, grcol, gla_norm_g[l], heads_g, dv_g)
        new_s.append(jnp.swapaxes(st_p, -1, -2))

        qs, kr, vb = _qkv_prep(zd, cos_t, sin_t, heads_d, dh)
        yc_p = _diff_attention(qs, kr, vb, 0, batch, seq, seq, heads_d, dh, diff_lam[l], diff_norm_g[l],
                               lam_inits[l])
        yc_s = _diff_attention(qs, kr, vb, mp, dec_batch, dec_seq, 1024, heads_d, dh, diff_lam[l],
                               diff_norm_g[l], lam_inits[l], ctx=(ck, cv, l))
        yc = jnp.concatenate([yc_p, yc_s], axis=0)
        new_k.append(zd[:mp, dw:2 * dw].reshape(batch, seq, heads_d, 2, dh))
        new_v.append(zd[:mp, 2 * dw:].reshape(batch, seq, heads_d, 2 * dh))

        wbr = jnp.stack([w_branch_a[l], w_branch_b[l], w_branch_c[l]]).astype(BF16)
        y = _mix(ya, yb, yc, gates, wbr)
        x, h = _mm_ln(y, w_out[l].astype(BF16), x, mod, l, 5, 1.0, alpha, l, 6, ln_g[l, 1], ln_b[l, 1])

        w1 = _pad_cols(ffn_w1[l, 1].astype(BF16), fp)
        w3 = _pad_cols(ffn_w3[l, 1].astype(BF16), fp)
        w2 = jnp.pad(ffn_w2[l, 1].astype(BF16), ((0, fp - ffn_dim), (0, 0)))
        hid = _ffn1(h, w1, w3)
        nl = min(l + 1, depth - 1)
        x, h = _mm_ln(hid, w2, x, mod, l, 8, 0.5, alpha, nl, 0, ln_g[l, 2], ln_b[l, 2])

    y_prompt = x[:mp].reshape(batch, seq, d)
    y_sample = x[mp:].reshape(dec_batch, dec_seq, d)
    return (y_prompt, y_sample, jnp.stack(new_k, axis=1), jnp.stack(new_v, axis=1),
            jnp.stack(new_s, axis=1))
```

```python
import functools
import math

import numpy as np
import jax
import jax.numpy as jnp
from jax import lax
from jax.experimental import pallas as pl
from jax.experimental.pallas import tpu as pltpu

F32 = jnp.float32
BF16 = jnp.bfloat16

GRID_W = 64
N_MOD = 9
HY_BANDS = 8
GLA_TAU = 16.0
GLA_CHUNK = 64
ROPE_THETA = 10000.0
LN_EPS = 1e-5
RMS_EPS = 1e-6

LANES = 128
MXU_DIM = 256
VMEM_BYTES_V7X = 64 * 1024 * 1024
VMEM_LIMIT = VMEM_BYTES_V7X - 8 * 1024 * 1024

GROUP = 4096
FFT_NB = 16
FFT_K1G = 8


def _cp(*sem):
    return pltpu.CompilerParams(dimension_semantics=sem, vmem_limit_bytes=VMEM_LIMIT)


def _dot(a, b):
    return jnp.dot(a, b, preferred_element_type=F32)


def _dot_nt(a, b):
    return lax.dot_general(a, b, (((1,), (1,)), ((), ())), preferred_element_type=F32)


def _dot_tn(a, b):
    return lax.dot_general(a, b, (((0,), (0,)), ((), ())), preferred_element_type=F32)


def _dot_hi(a, b):
    return jnp.dot(a, b, preferred_element_type=F32, precision=lax.Precision.HIGHEST)


def _silu(x):
    return x * jax.nn.sigmoid(x)


def _mod_kernel(c_ref, w_ref, b_ref, o_ref):
    c = c_ref[...]
    o_ref[...] = _dot(_silu(c).astype(BF16), w_ref[...].astype(BF16)) + b_ref[...]


def _modulation(cond, w_mod, b_mod):
    depth, d, n = w_mod.shape
    r = cond.shape[0]
    tn = 1024
    return pl.pallas_call(
        _mod_kernel,
        grid=(depth, n // tn),
        in_specs=[pl.BlockSpec((r, d), lambda l, j: (0, 0)),
                  pl.BlockSpec((None, d, tn), lambda l, j: (l, 0, j)),
                  pl.BlockSpec((None, 1, tn), lambda l, j: (l, 0, j))],
        out_specs=pl.BlockSpec((None, r, tn), lambda l, j: (l, 0, j)),
        out_shape=jax.ShapeDtypeStruct((depth, r, n), F32),
        compiler_params=_cp("parallel", "parallel"),
        name="mod",
    )(cond, w_mod, b_mod.reshape(depth, 1, n))


def _premod_kernel(x_ref, mod_ref, o_ref):
    o_ref[...] = (x_ref[...] * (1.0 + mod_ref[1:2, :]) + mod_ref[0:1, :]).astype(o_ref.dtype)


def _premod(x, mod, layer, tm=512):
    m, d = x.shape
    return pl.pallas_call(
        _premod_kernel,
        grid=(m // tm,),
        in_specs=[pl.BlockSpec((tm, d), lambda i: (i, 0)),
                  pl.BlockSpec((None, None, N_MOD, d), lambda i: (layer, (i * tm) // GROUP, 0, 0))],
        out_specs=pl.BlockSpec((tm, d), lambda i: (i, 0)),
        out_shape=jax.ShapeDtypeStruct((m, d), BF16),
        compiler_params=_cp("parallel"),
        name="premod",
    )(x, mod)


def _ffn1_kernel(h_ref, w1_ref, w3_ref, o_ref, w1b_ref, w3b_ref, *, tf, f_valid):
    @pl.when(pl.program_id(1) == 0)
    def _():
        col = pl.program_id(0) * tf + lax.broadcasted_iota(jnp.int32, w1_ref.shape, 1)
        keep = col < f_valid
        w1b_ref[...] = jnp.where(keep, w1_ref[...], 0.0).astype(BF16)
        w3b_ref[...] = jnp.where(keep, w3_ref[...], 0.0).astype(BF16)

    h = h_ref[...]
    a = _dot(h, w1b_ref[...])
    b = _dot(h, w3b_ref[...])
    o_ref[...] = (_silu(a) * b).astype(o_ref.dtype)


def _ffn1(h, w1, w3, layer, sub, tm=1024, tf=512):
    m, d = h.shape
    f = w1.shape[-1]
    nf = pl.cdiv(f, tf)
    wspec = pl.BlockSpec((None, None, d, tf), lambda j, i: (layer, sub, 0, j))
    return pl.pallas_call(
        functools.partial(_ffn1_kernel, tf=tf, f_valid=f),
        grid=(nf, m // tm),
        in_specs=[pl.BlockSpec((tm, d), lambda j, i: (i, 0)), wspec, wspec],
        out_specs=pl.BlockSpec((tm, tf), lambda j, i: (i, j)),
        out_shape=jax.ShapeDtypeStruct((m, nf * tf), BF16),
        scratch_shapes=[pltpu.VMEM((d, tf), BF16), pltpu.VMEM((d, tf), BF16)],
        compiler_params=_cp("parallel", "arbitrary"),
        name="ffn1",
    )(h, w1, w3)


def _mm_ln_kernel(a_ref, w_ref, x_ref, mod_ref, nmod_ref, g_ref, b_ref, xo_ref, ho_ref, *,
                  gate_row, coef, alpha, nshift_row):
    gate = coef * mod_ref[gate_row:gate_row + 1, :]
    xr = alpha * x_ref[...] + gate * _dot(a_ref[...], w_ref[...])
    mu = jnp.mean(xr, axis=-1, keepdims=True)
    xc = xr - mu
    var = jnp.mean(xc * xc, axis=-1, keepdims=True)
    xn = xc * lax.rsqrt(var + LN_EPS) * g_ref[...] + b_ref[...]
    xo_ref[...] = xn
    ho_ref[...] = (xn * (1.0 + nmod_ref[nshift_row + 1:nshift_row + 2, :])
                   + nmod_ref[nshift_row:nshift_row + 1, :]).astype(ho_ref.dtype)


def _mm_ln(a, w, x, mod, layer, gate_row, coef, alpha, nlayer, nshift_row, ln_g, ln_b, tm=256):
    m, kdim = a.shape
    d = w.shape[1]
    kern = functools.partial(_mm_ln_kernel, gate_row=gate_row, coef=coef, alpha=alpha, nshift_row=nshift_row)
    return pl.pallas_call(
        kern,
        grid=(m // tm,),
        in_specs=[pl.BlockSpec((tm, kdim), lambda i: (i, 0)),
                  pl.BlockSpec((kdim, d), lambda i: (0, 0), pipeline_mode=pl.Buffered(1)),
                  pl.BlockSpec((tm, d), lambda i: (i, 0)),
                  pl.BlockSpec((None, None, N_MOD, d), lambda i: (layer, (i * tm) // GROUP, 0, 0)),
                  pl.BlockSpec((None, None, N_MOD, d), lambda i: (nlayer, (i * tm) // GROUP, 0, 0)),
                  pl.BlockSpec((1, d), lambda i: (0, 0)),
                  pl.BlockSpec((1, d), lambda i: (0, 0))],
        out_specs=[pl.BlockSpec((tm, d), lambda i: (i, 0)),
                   pl.BlockSpec((tm, d), lambda i: (i, 0))],
        out_shape=[jax.ShapeDtypeStruct((m, d), F32), jax.ShapeDtypeStruct((m, d), BF16)],
        compiler_params=_cp("parallel"),
        name="mm_ln",
    )(a, w, x, mod, mod, ln_g.reshape(1, d), ln_b.reshape(1, d))


def _proj_kernel(h_ref, w_ref, o_ref, *, act):
    r = _dot(h_ref[...], w_ref[...])
    if act == "sigmoid":
        r = jax.nn.sigmoid(r)
    o_ref[...] = r.astype(o_ref.dtype)


def _proj(h, w, out_dtype, act=None, tm=1024, tn=1024):
    m, d = h.shape
    n = w.shape[1]
    tn = min(tn, n)
    return pl.pallas_call(
        functools.partial(_proj_kernel, act=act),
        grid=(m // tm, n // tn),
        in_specs=[pl.BlockSpec((tm, d), lambda i, j: (i, 0)),
                  pl.BlockSpec((d, tn), lambda i, j: (0, j))],
        out_specs=pl.BlockSpec((tm, tn), lambda i, j: (i, j)),
        out_shape=jax.ShapeDtypeStruct((m, n), out_dtype),
        compiler_params=_cp("parallel", "arbitrary"),
        name="proj",
    )(h, w)


def _sconv_kernel(z_ref, zp_ref, zn_ref, w_ref, b_ref, o_ref, *, tm, halo, lp, ls):
    i = pl.program_id(0)
    u = z_ref[...].astype(F32)
    prev = zp_ref[...].astype(F32)[halo - 1:halo, :]
    nxt = zn_ref[...].astype(F32)[0:1, :]
    row = lax.broadcasted_iota(jnp.int32, u.shape, 0)
    lseq = jnp.where((i * tm) // GROUP == 0, lp, ls)
    pos = (row + i * tm) & (lseq - 1)
    up = jnp.where(row == 0, prev, pltpu.roll(u, 1, 0))
    up = jnp.where(pos == 0, 0.0, up)
    un = jnp.where(row == tm - 1, nxt, pltpu.roll(u, tm - 1, 0))
    un = jnp.where(pos == lseq - 1, 0.0, un)
    w = w_ref[...]
    o_ref[...] = (up * w[0:1, :] + u * w[1:2, :] + un * w[2:3, :] + b_ref[...]).astype(o_ref.dtype)


def _short_conv(z, rows, conv_w, conv_b, lp, ls, width, tm=1024, ct=512, halo=16):
    m = rows
    nct = width // ct
    nrb = m // halo
    kern = functools.partial(_sconv_kernel, tm=tm, halo=halo, lp=lp, ls=ls)
    return pl.pallas_call(
        kern,
        grid=(m // tm, 3 * nct),
        in_specs=[pl.BlockSpec((tm, ct), lambda i, j: (i, j)),
                  pl.BlockSpec((halo, ct), lambda i, j: (jnp.maximum(i * (tm // halo) - 1, 0), j)),
                  pl.BlockSpec((halo, ct), lambda i, j: (jnp.minimum((i + 1) * (tm // halo), nrb - 1), j)),
                  pl.BlockSpec((3, ct), lambda i, j: (0, j)),
                  pl.BlockSpec((1, ct), lambda i, j: (0, j))],
        out_specs=pl.BlockSpec((None, tm, ct), lambda i, j: (j // nct, i, j % nct)),
        out_shape=jax.ShapeDtypeStruct((3, m, width), F32),
        compiler_params=_cp("parallel", "parallel"),
        name="sconv",
    )(z, z, z, conv_w, conv_b.reshape(1, -1))


def _filter_features(L):
    t = np.linspace(0.0, 1.0, L, dtype=np.float32)
    w = (np.float32(2.0 * math.pi / L) * np.arange(L, dtype=np.float32)).astype(np.float32)
    f = np.linspace(1e-4, HY_BANDS - 1, HY_BANDS, dtype=np.float32)
    wf = (w[:, None] * f).astype(np.float32)
    feats = np.concatenate([t[:, None], np.cos(wf), -np.sin(wf)], -1).astype(np.float32)
    idx = np.concatenate([np.arange(L), [0], np.arange(L - 1, 0, -1)])
    tab = np.zeros((2 * L, 32), np.float32)
    tab[:, :feats.shape[1]] = feats[idx]
    tab[:, 24] = t[idx]
    tab[:L, 25] = 1.0
    tab[L + 1:, 26] = -1.0
    return tab, feats.shape[1]


def _filter_kernel(tab_ref, w1_ref, b1_ref, fr_ref, w2_ref, b2_ref, w3_ref, dec_ref, o_ref, *, width):
    tab = tab_ref[...]
    fr = fr_ref[...]
    hdn = jnp.sin(fr[0:1, :] * (_dot_hi(tab, w1_ref[...]) + b1_ref[...]))
    hdn = jnp.sin(fr[1:2, :] * (_dot_hi(hdn, w2_ref[...]) + b2_ref[...]))
    t = tab[:, 24:25]
    mf = tab[:, 25:26]
    mb = tab[:, 26:27]
    h = _dot_hi(hdn, w3_ref[...]) * jnp.exp(-t * jnp.abs(dec_ref[...]))
    for o in range(2):
        base = 2 * width * o
        o_ref[o] = mf * h[:, base:base + width] + mb * h[:, base + width:base + 2 * width]


def _hyena_filter_taps(L, hy_w1, hy_b1, hy_freq, hy_w2, hy_b2, hy_w3, hy_decay, width, rb=256):
    tab_np, nfeat = _filter_features(L)
    fh = hy_w1.shape[1]
    w1p = jnp.zeros((32, fh), F32).at[:nfeat].set(hy_w1)
    n = 2 * L
    return pl.pallas_call(
        functools.partial(_filter_kernel, width=width),
        grid=(n // rb,),
        in_specs=[pl.BlockSpec((rb, 32), lambda i: (i, 0)),
                  pl.BlockSpec((32, fh), lambda i: (0, 0)),
                  pl.BlockSpec((1, fh), lambda i: (0, 0)),
                  pl.BlockSpec((2, fh), lambda i: (0, 0)),
                  pl.BlockSpec((fh, fh), lambda i: (0, 0)),
                  pl.BlockSpec((1, fh), lambda i: (0, 0)),
                  pl.BlockSpec((fh, 4 * width), lambda i: (0, 0)),
                  pl.BlockSpec((1, 4 * width), lambda i: (0, 0))],
        out_specs=pl.BlockSpec((2, rb, width), lambda i: (0, i, 0)),
        out_shape=jax.ShapeDtypeStruct((2, n, width), F32),
        compiler_params=_cp("parallel"),
        name="hyfilter",
    )(jnp.asarray(tab_np), w1p, hy_b1.reshape(1, fh), hy_freq, hy_w2, hy_b2.reshape(1, fh), hy_w3,
      hy_decay.reshape(1, -1))


def _dft_a(na, ka):
    k1 = np.arange(na // 2)[:, None].astype(np.float64)
    a = np.arange(ka)[None, :].astype(np.float64)
    th = 2.0 * np.pi * a * (k1 + 0.5) / na
    return np.concatenate([np.cos(th), -np.sin(th)], 0)


def _idft_a(na, n, rows):
    k1 = np.arange(na // 2)[None, :].astype(np.float64)
    a = np.arange(rows)[:, None].astype(np.float64)
    th = 2.0 * np.pi * a * (k1 + 0.5) / na
    return (2.0 / n) * np.concatenate([np.cos(th), -np.sin(th)], 1)


def _dft_b(na, nb, g):
    n = na * nb
    half = na // 2
    k1 = np.arange(half).astype(np.float64)
    b = np.arange(nb).astype(np.float64)
    k2 = np.arange(nb).astype(np.float64)
    phi = 2.0 * np.pi * (b[None, None, :] * k2[None, :, None] / nb
                         + b[None, None, :] * (k1[:, None, None] + 0.5) / n)
    c, s = np.cos(phi), np.sin(phi)
    ng = half // g
    r = g * nb
    fwd = np.zeros((ng, 2 * r, 2 * r))
    for q in range(g):
        rows = slice(q * nb, (q + 1) * nb)
        rows_i = slice(r + q * nb, r + (q + 1) * nb)
        cols = slice(q, r, g)
        cols_i = slice(r + q, 2 * r, g)
        cq, sq = c[q::g], s[q::g]
        fwd[:, rows, cols] = cq
        fwd[:, rows, cols_i] = sq
        fwd[:, rows_i, cols] = -sq
        fwd[:, rows_i, cols_i] = cq
    inv = np.transpose(fwd, (0, 2, 1))
    return fwd, inv


def _lmm_kernel(f_ref, x_ref, o_ref):
    o_ref[...] = _dot(f_ref[...], x_ref[...].astype(BF16)).astype(o_ref.dtype)


def _lmm(f, x, x_index, nbatch, out_dtype, tn=2048):
    mo, k = f.shape
    n = x.shape[-1]
    lead = len(x.shape) - 2
    return pl.pallas_call(
        _lmm_kernel,
        grid=(nbatch, n // tn),
        in_specs=[pl.BlockSpec((mo, k), lambda b, j: (0, 0)),
                  pl.BlockSpec((None,) * lead + (k, tn), lambda b, j: x_index(b) + (0, j))],
        out_specs=pl.BlockSpec((None, mo, tn), lambda b, j: (b, 0, j)),
        out_shape=jax.ShapeDtypeStruct((nbatch, mo, n), out_dtype),
        compiler_params=_cp("parallel", "parallel"),
        name="dft_a",
    )(f, x)


def _stage_a_fwd(x_ref, f1_ref, r_ref, nb, ct):
    for b in range(nb):
        r_ref[:, b * ct:(b + 1) * ct] = _dot(f1_ref[...], x_ref[:, b, :].astype(BF16))


def _stage_b_rows(r_ref, r0, half, nb, kg, ct):
    return [(pl.ds(ri * half + r0, kg), slice(b * ct, (b + 1) * ct)) for ri in range(2) for b in range(nb)]


def _row_permutation(nb, al):
    p = np.zeros((nb * al, nb * al), np.float32)
    for b in range(nb):
        for a in range(al):
            p[b * al + a, a * nb + b] = 1.0
    return p


def _to_residue_major(src_ref, perm_ref, dst_ref, nb, al):
    blk = nb * al
    for i in range(src_ref.shape[0] // blk):
        z = _dot(perm_ref[...], src_ref[i * blk:(i + 1) * blk, :].astype(BF16))
        for b in range(nb):
            dst_ref[b, i * al:(i + 1) * al, :] = z[b * al:(b + 1) * al, :]


def _short_conv_residue_major(u_ref, w_ref, cb_ref, nb):
    rows = u_ref.shape[1]
    row = lax.broadcasted_iota(jnp.int32, u_ref.shape[1:], 0)
    w = w_ref[...]
    cb = cb_ref[...]
    first = u_ref[0]
    prev = jnp.where(row == 0, 0.0, pltpu.roll(u_ref[nb - 1], 1, 0))
    for b in range(nb):
        cur = first if b == 0 else u_ref[b]
        nxt = u_ref[b + 1] if b + 1 < nb else jnp.where(row == rows - 1, 0.0, pltpu.roll(first, rows - 1, 0))
        u_ref[b] = prev * w[0:1, :] + cur * w[1:2, :] + nxt * w[2:3, :] + cb
        prev = cur


def _lconv_kernel(zv_ref, zx_ref, cwv_ref, cbv_ref, cwx_ref, cbx_ref, perm_ref, permt_ref, f1_ref, g_ref,
                  tb_ref, tbi_ref, kf_ref, bias_ref, o_ref, v_ref, x_ref, r_ref, *, nb, half, kg, ct, conv_v):
    r = nb * kg
    al = perm_ref.shape[0] // nb
    _to_residue_major(zv_ref, perm_ref, v_ref, nb, al)
    _to_residue_major(zx_ref, perm_ref, x_ref, nb, al)
    if conv_v:
        _short_conv_residue_major(v_ref, cwv_ref, cbv_ref, nb)
    _short_conv_residue_major(x_ref, cwx_ref, cbx_ref, nb)
    for b in range(nb):
        r_ref[:, b * ct:(b + 1) * ct] = _dot(f1_ref[...], v_ref[b].astype(BF16))

    for g in range(half // kg):
        tiles = _stage_b_rows(r_ref, g * kg, half, nb, kg, ct)
        a = jnp.concatenate([r_ref[rs, cs] for rs, cs in tiles], axis=0).astype(BF16)
        x = _dot(tb_ref[g], a)
        kf = kf_ref[g * 2 * r:(g + 1) * 2 * r, :]
        xr, xi = x[:r], x[r:]
        kr, ki = kf[:r], kf[r:]
        y = jnp.concatenate([xr * kr - xi * ki, xr * ki + xi * kr], axis=0).astype(BF16)
        bh = _dot(tbi_ref[g], y)
        for t, (rs, cs) in enumerate(tiles):
            r_ref[rs, cs] = bh[t * kg:(t + 1) * kg, :]
    bias = bias_ref[...]
    for b in range(nb):
        y = _dot(g_ref[...], r_ref[:, b * ct:(b + 1) * ct].astype(BF16))
        v_ref[b] = x_ref[b] * (y + v_ref[b] * bias)
    blk = nb * al
    for i in range(o_ref.shape[0] // blk):
        t = jnp.concatenate([v_ref[b, i * al:(i + 1) * al, :] for b in range(nb)], axis=0).astype(BF16)
        o_ref[i * blk:(i + 1) * blk, :] = _dot(permt_ref[...], t).astype(o_ref.dtype)


def _lconv(zv, v_row0, v_col0, zx, x_row0, x_col0, conv_w, conv_b, conv_v, perm, permt, f1, g_inv, tb, tbi,
           kf, order, bias, nbatch, L, ct=256):
    c = kf.shape[-1]
    n2, _ = f1.shape
    ng, r2, _ = tb.shape
    nb = L // (n2 // 2)
    kg = r2 // (2 * nb)
    ncb = c // ct
    once = dict(pipeline_mode=pl.Buffered(1))
    kern = functools.partial(_lconv_kernel, nb=nb, half=n2 // 2, kg=kg, ct=ct, conv_v=conv_v)
    vb0, xb0 = v_row0 // L, x_row0 // L
    return pl.pallas_call(
        kern,
        grid=(ncb, nbatch),
        in_specs=[pl.BlockSpec((L, ct), lambda j, b: (vb0 + b, v_col0 * ncb + j)),
                  pl.BlockSpec((L, ct), lambda j, b: (xb0 + b, x_col0 * ncb + j)),
                  pl.BlockSpec((None, 3, ct), lambda j, b: (0, 0, j)),
                  pl.BlockSpec((None, 1, ct), lambda j, b: (0, 0, j)),
                  pl.BlockSpec((None, 3, ct), lambda j, b: (1, 0, j)),
                  pl.BlockSpec((None, 1, ct), lambda j, b: (1, 0, j)),
                  pl.BlockSpec(perm.shape, lambda j, b: (0, 0), **once),
                  pl.BlockSpec(permt.shape, lambda j, b: (0, 0), **once),
                  pl.BlockSpec(f1.shape, lambda j, b: (0, 0), **once),
                  pl.BlockSpec(g_inv.shape, lambda j, b: (0, 0), **once),
                  pl.BlockSpec(tb.shape, lambda j, b: (0, 0, 0), **once),
                  pl.BlockSpec(tbi.shape, lambda j, b: (0, 0, 0), **once),
                  pl.BlockSpec((None, ng * r2, ct), lambda j, b: (order, 0, j), **once),
                  pl.BlockSpec((1, ct), lambda j, b: (0, j))],
        out_specs=pl.BlockSpec((L, ct), lambda j, b: (b, j)),
        out_shape=jax.ShapeDtypeStruct((nbatch * L, c), BF16),
        scratch_shapes=[pltpu.VMEM((nb, L // nb, ct), F32), pltpu.VMEM((nb, L // nb, ct), F32),
                        pltpu.VMEM((n2, nb * ct), F32)],
        compiler_params=_cp("parallel", "arbitrary"),
        name="lconv",
    )(zv, zx, conv_w, conv_b, conv_w, conv_b, perm, permt, f1, g_inv, tb, tbi, kf, bias.reshape(1, c))


def _lconv_filter_kernel(x_ref, f1_ref, tb_ref, o_ref, r_ref, *, nb, half, kg, ct):
    r2 = 2 * nb * kg
    _stage_a_fwd(x_ref, f1_ref, r_ref, nb, ct)

    for g in range(half // kg):
        tiles = _stage_b_rows(r_ref, g * kg, half, nb, kg, ct)
        a = jnp.concatenate([r_ref[rs, cs] for rs, cs in tiles], axis=0).astype(BF16)
        o_ref[g * r2:(g + 1) * r2, :] = _dot(tb_ref[g], a)


def _lconv_filter(taps, f1, tb, ct=256):
    norder, na, nb, c = taps.shape
    n2, _ = f1.shape
    ng, r2, _ = tb.shape
    kg = r2 // (2 * nb)
    kern = functools.partial(_lconv_filter_kernel, nb=nb, half=n2 // 2, kg=kg, ct=ct)
    return pl.pallas_call(
        kern,
        grid=(norder, c // ct),
        in_specs=[pl.BlockSpec((None, na, nb, ct), lambda o, j: (o, 0, 0, j)),
                  pl.BlockSpec(f1.shape, lambda o, j: (0, 0)),
                  pl.BlockSpec(tb.shape, lambda o, j: (0, 0, 0))],
        out_specs=pl.BlockSpec((None, ng * r2, ct), lambda o, j: (o, 0, j)),
        out_shape=jax.ShapeDtypeStruct((norder, ng * r2, c), F32),
        scratch_shapes=[pltpu.VMEM((n2, nb * ct), F32)],
        compiler_params=_cp("parallel", "parallel"),
        name="lconv_filter",
    )(taps, f1, tb)


def _pconv_kernel(fd_ref, gd_ref, v_ref, kf_ref, xg_ref, bias_ref, o_ref, *, half):
    v = v_ref[...]
    x = _dot(fd_ref[...], v.astype(BF16))
    kf = kf_ref[...]
    xr, xi = x[:half], x[half:]
    kr, ki = kf[:half], kf[half:]
    y = jnp.concatenate([xr * kr - xi * ki, xr * ki + xi * kr], axis=0).astype(BF16)
    yt = _dot(gd_ref[...], y)
    o_ref[...] = (xg_ref[...].astype(F32) * (yt + v.astype(F32) * bias_ref[...])).astype(o_ref.dtype)


def _pconv(fd, gd, vsrc, v_lead, kf, order, xsrc, x_lead, bias, nseq, L, out_dtype, ct=512):
    c = kf.shape[-1]
    n2 = fd.shape[0]
    return pl.pallas_call(
        functools.partial(_pconv_kernel, half=n2 // 2),
        grid=(nseq, c // ct),
        in_specs=[pl.BlockSpec((n2, L), lambda s, j: (0, 0)),
                  pl.BlockSpec((L, n2), lambda s, j: (0, 0)),
                  pl.BlockSpec((None,) * len(v_lead) + (L, ct), lambda s, j: v_lead + (s, j)),
                  pl.BlockSpec((None, n2, ct), lambda s, j: (order, 0, j)),
                  pl.BlockSpec((None,) * len(x_lead) + (L, ct), lambda s, j: x_lead + (s, j)),
                  pl.BlockSpec((1, ct), lambda s, j: (0, j))],
        out_specs=pl.BlockSpec((L, ct), lambda s, j: (s, j)),
        out_shape=jax.ShapeDtypeStruct((nseq * L, c), out_dtype),
        compiler_params=_cp("parallel", "parallel"),
        name="pconv",
    )(fd, gd, vsrc, kf, xsrc, bias.reshape(1, c))


def _gate_kernel(lr_ref, wa_ref, ba_ref, o_ref):
    logits = _dot_hi(lr_ref[...], wa_ref[...]) + ba_ref[...]
    o_ref[...] = jax.nn.log_sigmoid(logits) * (1.0 / GLA_TAU)


def _gla_gates(lr, wa_cat, ba_cat, tm=1024):
    m, k = lr.shape
    n = wa_cat.shape[1]
    return pl.pallas_call(
        _gate_kernel,
        grid=(m // tm,),
        in_specs=[pl.BlockSpec((tm, k), lambda i: (i, 0)),
                  pl.BlockSpec((k, n), lambda i: (0, 0)),
                  pl.BlockSpec((1, n), lambda i: (0, 0))],
        out_specs=pl.BlockSpec((tm, n), lambda i: (i, 0)),
        out_shape=jax.ShapeDtypeStruct((m, n), F32),
        compiler_params=_cp("parallel"),
        name="gla_gates",
    )(lr, wa_cat, ba_cat)


def _split3(x):
    hi = x.astype(BF16)
    r1 = x - hi.astype(F32)
    mid = r1.astype(BF16)
    lo = (r1 - mid.astype(F32)).astype(BF16)
    return hi, mid, lo


def _gla_dir(d, r0, q_ref, k_ref, v_ref, la_ref, o_ref, st_ref, tri, causal, ref_row, last_row,
             heads, dk, dv, scale):
    ch = GLA_CHUNK
    rows = pl.ds(pl.multiple_of(r0, ch), ch)
    la = la_ref[rows, :]
    hi, mid, lo = _split3(la)
    b = _dot(tri, hi) + _dot(tri, mid) + _dot(tri, lo)
    bref = b[ref_row:ref_row + 1, :]
    blast = b[last_row:last_row + 1, :]
    q = q_ref[rows, :].astype(F32) * scale
    k = k_ref[rows, :].astype(F32)
    qt = (q * jnp.exp(b - bref)).astype(BF16)
    kt = (k * jnp.exp(bref - b)).astype(BF16)
    qin = (q * jnp.exp(b)).astype(BF16)
    kst = (k * jnp.exp(blast - b)).astype(BF16)
    dec = jnp.exp(blast)
    for h in range(heads):
        ks = slice(h * dk, (h + 1) * dk)
        vs = slice(h * dv, (h + 1) * dv)
        att = _dot_nt(qt[:, ks], kt[:, ks])
        att = jnp.where(causal, att, 0.0).astype(BF16)
        vh = v_ref[rows, vs]
        st = st_ref[d, h]
        o_ref[rows, vs] = _dot(att, vh) + _dot_nt(qin[:, ks], st.astype(BF16))
        st_ref[d, h] = st * dec[:, ks] + _dot_tn(vh, kst[:, ks])


def _gla_kernel(*refs, nch, heads, dk, dv, has_s0, scale):
    if has_s0:
        (qf, kf, vf, laf, qb, kb, vb, lab, s0, of, ob, st) = refs
    else:
        (qf, kf, vf, laf, qb, kb, vb, lab, of, ob, st) = refs
        s0 = None
    ch = GLA_CHUNK

    @pl.when(pl.program_id(1) == 0)
    def _():
        if has_s0:
            st[...] = s0[...]
        else:
            st[...] = jnp.zeros_like(st)

    r_i = lax.broadcasted_iota(jnp.int32, (ch, ch), 0)
    c_i = lax.broadcasted_iota(jnp.int32, (ch, ch), 1)
    lower = r_i >= c_i
    upper = r_i <= c_i
    tri_l = jnp.where(lower, 1.0, 0.0).astype(BF16)
    tri_u = jnp.where(upper, 1.0, 0.0).astype(BF16)

    def body(c, carry):
        _gla_dir(0, c * ch, qf, kf, vf, laf, of, st, tri_l, lower, ch // 2 - 1, ch - 1,
                 heads, dk, dv, scale)
        _gla_dir(1, (nch - 1 - c) * ch, qb, kb, vb, lab, ob, st, tri_u, upper, ch // 2, 0,
                 heads, dk, dv, scale)
        return carry

    lax.fori_loop(0, nch, body, 0)


def _gla(zmain, qcol, kcol, vcol, la, row0, nseq, L, tb, heads, dk, dv, s0t=None):
    nblk = L // tb
    rb0 = row0 // tb
    hk, hv = heads * dk, heads * dv

    def fwd(s, j):
        return rb0 + s * nblk + j

    def bwd(s, j):
        return rb0 + s * nblk + (nblk - 1 - j)

    in_specs = []
    for rowf, lcol in ((fwd, 0), (bwd, 1)):
        in_specs += [pl.BlockSpec((tb, hk), lambda s, j, rowf=rowf: (rowf(s, j), qcol)),
                     pl.BlockSpec((tb, hk), lambda s, j, rowf=rowf: (rowf(s, j), kcol)),
                     pl.BlockSpec((tb, hv), lambda s, j, rowf=rowf: (rowf(s, j), vcol)),
                     pl.BlockSpec((tb, hk), lambda s, j, rowf=rowf, lcol=lcol: (rowf(s, j), lcol))]
    args = [zmain, zmain, zmain, la, zmain, zmain, zmain, la]
    if s0t is not None:
        in_specs.append(pl.BlockSpec((None, 2, heads, dv, dk), lambda s, j: (s, 0, 0, 0, 0)))
        args.append(s0t)
    kern = functools.partial(_gla_kernel, nch=tb // GLA_CHUNK, heads=heads, dk=dk, dv=dv,
                             has_s0=s0t is not None, scale=dk ** -0.5)
    return pl.pallas_call(
        kern,
        grid=(nseq, nblk),
        in_specs=in_specs,
        out_specs=[pl.BlockSpec((tb, hv), lambda s, j: (s * nblk + j, 0)),
                   pl.BlockSpec((tb, hv), lambda s, j: (s * nblk + (nblk - 1 - j), 0)),
                   pl.BlockSpec((None, 2, heads, dv, dk), lambda s, j: (s, 0, 0, 0, 0))],
        out_shape=[jax.ShapeDtypeStruct((nseq * L, hv), F32),
                   jax.ShapeDtypeStruct((nseq * L, hv), F32),
                   jax.ShapeDtypeStruct((nseq, 2, heads, dv, dk), F32)],
        compiler_params=_cp("parallel", "arbitrary"),
        name="gla",
    )(*args)


def _gla_post_kernel(of_ref, ob_ref, gr_ref, g_ref, o_ref, *, heads, dv):
    o = of_ref[...] + ob_ref[...]
    gate = _silu(gr_ref[...].astype(F32))
    g = g_ref[...]
    for h in range(heads):
        sl = slice(h * dv, (h + 1) * dv)
        oh = o[:, sl]
        ms = jnp.mean(oh * oh, axis=-1, keepdims=True)
        o_ref[:, sl] = (oh * lax.rsqrt(ms + RMS_EPS) * g * gate[:, sl]).astype(o_ref.dtype)


def _gla_post(o_f, o_b, zmain, grcol, norm_g, heads, dv, tm=512):
    m, hv = o_f.shape
    return pl.pallas_call(
        functools.partial(_gla_post_kernel, heads=heads, dv=dv),
        grid=(m // tm,),
        in_specs=[pl.BlockSpec((tm, hv), lambda i: (i, 0)),
                  pl.BlockSpec((tm, hv), lambda i: (i, 0)),
                  pl.BlockSpec((tm, hv), lambda i: (i, grcol)),
                  pl.BlockSpec((1, dv), lambda i: (0, 0))],
        out_specs=pl.BlockSpec((tm, hv), lambda i: (i, 0)),
        out_shape=jax.ShapeDtypeStruct((m, hv), BF16),
        compiler_params=_cp("parallel"),
        name="gla_post",
    )(o_f, o_b, zmain, norm_g.reshape(1, dv))


def _rope_tables(L, dh):
    rows = L // GRID_W
    r = np.repeat(np.arange(rows, dtype=np.float32), GRID_W)
    col = np.tile(np.arange(GRID_W, dtype=np.float32), rows)
    nf = dh // 4
    inv = (np.float32(ROPE_THETA) ** (-np.arange(nf, dtype=np.float32) / nf)).astype(np.float32)
    ang_r = (r[:, None] * inv).astype(np.float32)
    ang_c = (col[:, None] * inv).astype(np.float32)
    cos = np.concatenate([np.cos(ang_r), np.cos(ang_r), np.cos(ang_c), np.cos(ang_c)], -1)
    sin = np.concatenate([-np.sin(ang_r), np.sin(ang_r), -np.sin(ang_c), np.sin(ang_c)], -1)
    cos = np.concatenate([cos, cos], -1).astype(np.float32)
    sin = np.concatenate([sin, sin], -1).astype(np.float32)
    cos_t = np.stack([np.ones_like(cos), cos])
    sin_t = np.stack([np.zeros_like(sin), sin])
    return cos_t, sin_t


def _qkv_prep_kernel(q_ref, k_ref, v_ref, cos_ref, sin_ref, qo_ref, ko_ref, vo_ref, *, heads, hw, nf, scale):
    cos = cos_ref[...]
    sin = sin_ref[...]
    lane = lax.broadcasted_iota(jnp.int32, cos.shape, 1)
    first = (lane & (2 * nf - 1)) < nf

    def rope(x):
        sw = jnp.where(first, pltpu.roll(x, hw - nf, 1), pltpu.roll(x, nf, 1))
        return x * cos + sw * sin

    ones = jnp.ones((q_ref.shape[0], hw), vo_ref.dtype)
    for h in range(heads):
        sl = slice(h * hw, (h + 1) * hw)
        qo_ref[:, sl] = (rope(q_ref[:, sl]) * scale).astype(qo_ref.dtype)
        ko_ref[sl, :] = rope(k_ref[:, sl]).T.astype(ko_ref.dtype)
        vo_ref[:, 2 * h * hw:(2 * h + 1) * hw] = v_ref[:, sl].astype(vo_ref.dtype)
        vo_ref[:, (2 * h + 1) * hw:(2 * h + 2) * hw] = ones


def _qkv_prep(zd, cos_t, sin_t, heads, dh, tm=512):
    m = zd.shape[0]
    w = zd.shape[1] // 3
    hw = w // heads
    gb = GROUP // tm
    kern = functools.partial(_qkv_prep_kernel, heads=heads, hw=hw, nf=dh // 4,
                             scale=dh ** -0.5 * math.log2(math.e))
    tab = pl.BlockSpec((None, tm, hw), lambda i: (jnp.minimum(i // gb, 1), i % gb, 0))
    out = jax.ShapeDtypeStruct((m, w), BF16)
    return pl.pallas_call(
        kern,
        grid=(m // tm,),
        in_specs=[pl.BlockSpec((tm, w), lambda i: (i, 0)),
                  pl.BlockSpec((tm, w), lambda i: (i, 1)),
                  pl.BlockSpec((tm, w), lambda i: (i, 2)),
                  tab, tab],
        out_specs=[pl.BlockSpec((tm, w), lambda i: (i, 0)),
                   pl.BlockSpec((w, tm), lambda i: (0, i)),
                   pl.BlockSpec((tm, 2 * w), lambda i: (i, 0))],
        out_shape=[out, jax.ShapeDtypeStruct((w, m), BF16), jax.ShapeDtypeStruct((m, 2 * w), BF16)],
        compiler_params=_cp("parallel"),
        name="qkv_prep",
    )(zd, zd, zd, cos_t, sin_t)


def _attn_kernel(*refs, has_ctx, lam_init, dh, bf16_exp, tq):
    if has_ctx:
        q_ref, k_ref, v_ref, kc_ref, vc_ref, lam_ref, g_ref, o_ref, s0_ref, s1_ref = refs
    else:
        q_ref, k_ref, v_ref, lam_ref, g_ref, o_ref, s0_ref, s1_ref = refs
    hw = 2 * dh
    lk = k_ref.shape[1]
    nsub = q_ref.shape[0] // tq
    lane = lax.broadcasted_iota(jnp.int32, (tq, hw), 1)
    first = lane < dh
    if has_ctx:
        kct = kc_ref[...].T.astype(BF16)
        vcv = vc_ref[...].astype(BF16)
        vce = jnp.concatenate([vcv, jnp.ones_like(vcv)], axis=1)

    def scores(i, j, s_ref):
        q = q_ref[pl.ds(i * tq if isinstance(i, int) else pl.multiple_of(i * tq, tq), tq), :]
        qj = jnp.where(first, q, jnp.zeros_like(q)) if j == 0 else jnp.where(first, jnp.zeros_like(q), q)
        s_ref[:, :lk] = _dot(qj, k_ref[...])
        if has_ctx:
            s_ref[:, lk:] = _dot(qj, kct)

    def prob(t):
        if bf16_exp:
            return jnp.exp2(t.astype(BF16))
        return jnp.exp2(t).astype(BF16)

    def softmax_pv(s_ref):
        s = s_ref[...]
        p = prob(s - jnp.max(s, axis=-1, keepdims=True))
        acc = _dot(p[:, :lk], v_ref[...])
        if has_ctx:
            acc = acc + _dot(p[:, lk:], vce)
        return acc[:, :hw] / acc[:, hw:]

    lp = lam_ref[...]
    lam = (jnp.exp(jnp.sum(lp[0:1] * lp[1:2], axis=-1, keepdims=True))
           - jnp.exp(jnp.sum(lp[2:3] * lp[3:4], axis=-1, keepdims=True)) + lam_init)
    gain = g_ref[...] * (1.0 - lam_init)

    scores(0, 0, s0_ref)

    def body(i, carry):
        scores(i, 1, s1_ref)
        sm0 = softmax_pv(s0_ref)
        scores(jnp.minimum(i + 1, nsub - 1), 0, s0_ref)
        o = sm0 - lam * softmax_pv(s1_ref)
        ms = jnp.mean(o * o, axis=-1, keepdims=True)
        o_ref[pl.ds(pl.multiple_of(i * tq, tq), tq), :] = (o * lax.rsqrt(ms + RMS_EPS) * gain).astype(o_ref.dtype)
        return carry

    lax.fori_loop(0, nsub, body, 0)


def _diff_attention(qs, kr, vb, row0, nseq, L, tb, heads, dh, lam_p, norm_g, lam_init, ctx=None, tq=256):
    hw = 2 * dh
    nqb = L // tb
    qb0 = row0 // tb
    kb0 = row0 // L
    past = 0 if ctx is None else ctx[0].shape[2]
    in_specs = [pl.BlockSpec((tb, hw), lambda s, h, i: (qb0 + s * nqb + i, h)),
                pl.BlockSpec((hw, L), lambda s, h, i: (h, kb0 + s)),
                pl.BlockSpec((L, 2 * hw), lambda s, h, i: (kb0 + s, h))]
    args = [qs, kr, vb]
    if ctx is not None:
        ck, cv, layer = ctx
        in_specs += [pl.BlockSpec((None, None, past, hw), lambda s, h, i: (s, layer, 0, h)),
                     pl.BlockSpec((None, None, past, hw), lambda s, h, i: (s, layer, 0, h))]
        args += [ck, cv]
    in_specs += [pl.BlockSpec(lam_p.shape, lambda s, h, i: (0, 0)),
                 pl.BlockSpec((1, hw), lambda s, h, i: (0, 0))]
    args += [lam_p, norm_g.reshape(1, hw)]
    kern = functools.partial(_attn_kernel, has_ctx=ctx is not None, lam_init=lam_init, dh=dh,
                             bf16_exp=L > 1024, tq=tq)
    return pl.pallas_call(
        kern,
        grid=(nseq, heads, nqb),
        in_specs=in_specs,
        out_specs=pl.BlockSpec((tb, hw), lambda s, h, i: (s * nqb + i, h)),
        out_shape=jax.ShapeDtypeStruct((nseq * L, heads * hw), BF16),
        scratch_shapes=[pltpu.VMEM((tq, L + past), F32), pltpu.VMEM((tq, L + past), F32)],
        compiler_params=_cp("parallel", "parallel", "arbitrary"),
        name="diff_attn",
    )(*args)


def _mix_kernel(ya_ref, yb_ref, yc_ref, g_ref, w_ref, o_ref, acc_ref):
    k = pl.program_id(1)

    def contrib(y_ref):
        return g_ref[...].astype(F32) * _dot(y_ref[...].astype(BF16), w_ref[...])

    @pl.when(k == 0)
    def _():
        acc_ref[...] = contrib(ya_ref)

    @pl.when(k == 1)
    def _():
        acc_ref[...] += contrib(yb_ref)

    @pl.when(k == 2)
    def _():
        o_ref[...] = (acc_ref[...] + contrib(yc_ref)).astype(o_ref.dtype)


def _mix(ya, yb, yc, gates, wbr, tm=512):
    m, w = ya.shape
    d = wbr.shape[2]
    return pl.pallas_call(
        _mix_kernel,
        grid=(m // tm, 3),
        in_specs=[pl.BlockSpec((tm, w), lambda i, k: (i, 0)),
                  pl.BlockSpec((tm, w), lambda i, k: (i, 0)),
                  pl.BlockSpec((tm, w), lambda i, k: (i, 0)),
                  pl.BlockSpec((tm, d), lambda i, k: (i, k)),
                  pl.BlockSpec((None, w, d), lambda i, k: (k, 0, 0))],
        out_specs=pl.BlockSpec((tm, d), lambda i, k: (i, 0)),
        out_shape=jax.ShapeDtypeStruct((m, d), BF16),
        scratch_shapes=[pltpu.VMEM((tm, d), F32)],
        compiler_params=_cp("parallel", "arbitrary"),
        name="mix",
    )(ya, yb, yc, gates, wbr)


def _pad_cols(w, n):
    return jnp.pad(w, ((0, 0), (0, n - w.shape[1])))


def kernel(x_prompt, x_sample, cache_k, cache_v, state_gla, c, c_ctx, w_mod, b_mod, ln_g, ln_b, ffn_w1, ffn_w3, ffn_w2, w_in, hy_conv_w, hy_conv_b, hy_w1, hy_b1, hy_freq, hy_w2, hy_b2, hy_w3, hy_decay, hy_bias, gla_wa, gla_ba, gla_norm_g, diff_lam, diff_norm_g, w_branch_a, w_branch_b, w_branch_c, w_out):
    batch, seq, d = x_prompt.shape
    dec_batch, dec_seq, _ = x_sample.shape
    depth = w_mod.shape[0]
    ffn_dim = ffn_w1.shape[3]
    hy_w = hy_bias.shape[2]
    heads_g, dk_g = 4, gla_wa.shape[3] // 4
    dv_g = gla_norm_g.shape[1]
    rank = gla_wa.shape[2]
    dh = diff_lam.shape[2]
    heads_d = cache_k.shape[3]
    dw = heads_d * 2 * dh
    gw = heads_g * dv_g
    gk = heads_g * dk_g
    assert batch * seq == GROUP and dec_seq == GROUP
    mp = batch * seq
    m = mp + dec_batch * dec_seq
    ngroups = 1 + dec_batch
    alpha = (2 * depth) ** 0.25
    lam_inits = [0.8 - 0.6 * math.exp(-0.3 * l) for l in range(depth)]

    c_main = 3 * hy_w + 2 * gk + 2 * gw
    c_lr = c_main
    c_d = c_lr + 2 * rank
    c_g = c_d + 3 * dw

    cond = jnp.concatenate([c_ctx[None], c, jnp.zeros((16 - ngroups, d), F32)], axis=0)
    mod = _modulation(cond, w_mod, b_mod)[:, :ngroups].reshape(depth, ngroups, N_MOD, d)

    na = 2 * dec_seq // FFT_NB
    n_s = 2 * dec_seq
    f1_half = jnp.asarray(_dft_a(na, na // 2), BF16)
    f1_full = jnp.asarray(_dft_a(na, na), BF16)
    g_s = jnp.asarray(_idft_a(na, n_s, na // 2), BF16)
    tb_np, tbi_np = _dft_b(na, FFT_NB, FFT_K1G)
    tb, tbi = jnp.asarray(tb_np, BF16), jnp.asarray(tbi_np, BF16)
    perm_np = _row_permutation(FFT_NB, FFT_K1G)
    perm, permt = jnp.asarray(perm_np, BF16), jnp.asarray(perm_np.T, BF16)
    fp_half = jnp.asarray(_dft_a(2 * seq, seq), BF16)
    fp_full = jnp.asarray(_dft_a(2 * seq, 2 * seq), BF16)
    g_p = jnp.asarray(_idft_a(2 * seq, 2 * seq, seq), BF16)
    cos_t, sin_t = _rope_tables(dec_seq, dh)
    cos_t, sin_t = jnp.asarray(cos_t), jnp.asarray(sin_t)
    half_rows = (na // 2) * FFT_NB
    lanes_s = FFT_NB * hy_w

    x = jnp.concatenate([x_prompt.reshape(mp, d), x_sample.reshape(dec_batch * dec_seq, d)], axis=0)
    h = _premod(x, mod, 0)

    fp = ((ffn_dim + 511) // 512) * 512
    ck = cache_k.reshape(dec_batch, depth, cache_k.shape[2], dw)
    cv = cache_v.reshape(dec_batch, depth, cache_v.shape[2], dw)
    new_k, new_v, new_s = [], [], []

    for l in range(depth):
        w2 = jnp.pad(ffn_w2[l, 0].astype(BF16), ((0, fp - ffn_dim), (0, 0)))
        hid = _ffn1(h, ffn_w1, ffn_w3, l, 0)
        x, h = _mm_ln(hid, w2, x, mod, l, 2, 0.5, alpha, l, 3, ln_g[l, 0], ln_b[l, 0])

        wi = w_in[l]
        zmain = _proj(h, wi[:, :c_main].astype(BF16), BF16)
        zlr = _proj(h, _pad_cols(wi[:, c_lr:c_d].astype(BF16), LANES), F32)
        zd = _proj(h, wi[:, c_d:c_g].astype(BF16), F32)
        gates = _proj(h, wi[:, c_g:].astype(BF16), BF16, act="sigmoid")

        u3 = _short_conv(zmain, mp, hy_conv_w[l], hy_conv_b[l], seq, dec_seq, hy_w)
        fargs = (hy_w1[l], hy_b1[l], hy_freq[l], hy_w2[l], hy_b2[l], hy_w3[l], hy_decay[l], hy_w)
        taps_s = _hyena_filter_taps(dec_seq, *fargs)
        taps_p = _hyena_filter_taps(seq, *fargs)
        kf_s = _lconv_filter(taps_s.reshape(2, na, FFT_NB, hy_w), f1_full, tb)
        kf_p = _lmm(fp_full, taps_p, lambda b: (b,), 2, F32, tn=hy_w)

        cw3 = jnp.swapaxes(hy_conv_w[l].reshape(3, 3, hy_w), 0, 1)
        cb3 = hy_conv_b[l].reshape(3, 1, hy_w)
        z1s = _lconv(zmain, mp, 0, zmain, mp, 1, cw3[0:2], cb3[0:2], True, perm, permt, f1_half, g_s,
                     tb, tbi, kf_s, 0, hy_bias[l, 0], dec_batch, dec_seq)
        ya_s = _lconv(z1s, 0, 0, zmain, mp, 2, cw3[1:3], cb3[1:3], False, perm, permt, f1_half, g_s,
                      tb, tbi, kf_s, 1, hy_bias[l, 1], dec_batch, dec_seq)
        z1 = _pconv(fp_half, g_p, u3, (0,), kf_p, 0, u3, (1,), hy_bias[l, 0], batch, seq, F32)
        ya_p = _pconv(fp_half, g_p, z1, (), kf_p, 1, u3, (2,), hy_bias[l, 1], batch, seq, BF16)
        ya = jnp.concatenate([ya_p, ya_s], axis=0)

        wa_cat = jnp.zeros((LANES, 2 * gk), F32)
        wa_cat = wa_cat.at[:rank, :gk].set(gla_wa[l, 0]).at[rank:2 * rank, gk:].set(gla_wa[l, 1])
        ba_cat = jnp.concatenate([gla_ba[l, 0], gla_ba[l, 1]]).reshape(1, 2 * gk)
        la = _gla_gates(zlr, wa_cat, ba_cat)
        qcol, kcol, vcol, grcol = (3 * hy_w) // gk, (3 * hy_w) // gk + 1, (3 * hy_w + 2 * gk) // gw, \
            (3 * hy_w + 2 * gk) // gw + 1
        s0t = jnp.swapaxes(state_gla[:, l], -1, -2)
        of_p, ob_p, st_p = _gla(zmain, qcol, kcol, vcol, la, 0, batch, seq, seq, heads_g, dk_g, dv_g)
        of_s, ob_s, _ = _gla(zmain, qcol, kcol, vcol, la, mp, dec_batch, dec_seq, 512, heads_g, dk_g, dv_g,
                             s0t=s0t)
        o_f = jnp.concatenate([of_p, of_s], axis=0)
        o_b = jnp.concatenate([ob_p, ob_s], axis=0)
        yb = _gla_post(o_f, o_b, zmain, grcol, gla_norm_g[l], heads_g, dv_g)
        new_s.append(jnp.swapaxes(st_p, -1, -2))

        qs, kr, vb = _qkv_prep(zd, cos_t, sin_t, heads_d, dh)
        yc_p = _diff_attention(qs, kr, vb, 0, batch, seq, seq, heads_d, dh, diff_lam[l], diff_norm_g[l],
                               lam_inits[l])
        yc_s = _diff_attention(qs, kr, vb, mp, dec_batch, dec_seq, 1024, heads_d, dh, diff_lam[l],
                               diff_norm_g[l], lam_inits[l], ctx=(ck, cv, l))
        yc = jnp.concatenate([yc_p, yc_s], axis=0)
        new_k.append(zd[:mp, dw:2 * dw].reshape(batch, seq, heads_d, 2, dh))
        new_v.append(zd[:mp, 2 * dw:].reshape(batch, seq, heads_d, 2 * dh))

        wbr = jnp.stack([w_branch_a[l], w_branch_b[l], w_branch_c[l]]).astype(BF16)
        y = _mix(ya, yb, yc, gates, wbr)
        x, h = _mm_ln(y, w_out[l].astype(BF16), x, mod, l, 5, 1.0, alpha, l, 6, ln_g[l, 1], ln_b[l, 1])

        w2 = jnp.pad(ffn_w2[l, 1].astype(BF16), ((0, fp - ffn_dim), (0, 0)))
        hid = _ffn1(h, ffn_w1, ffn_w3, l, 1)
        nl = min(l + 1, depth - 1)
        x, h = _mm_ln(hid, w2, x, mod, l, 8, 0.5, alpha, nl, 0, ln_g[l, 2], ln_b[l, 2])

    y_prompt = x[:mp].reshape(batch, seq, d)
    y_sample = x[mp:].reshape(dec_batch, dec_seq, d)
    return (y_prompt, y_sample, jnp.stack(new_k, axis=1), jnp.stack(new_v, axis=1),
            jnp.stack(new_s, axis=1))
```

```python
import functools
import math

import numpy as np
import jax
import jax.numpy as jnp
from jax import lax
from jax.experimental import pallas as pl
from jax.experimental.pallas import tpu as pltpu

F32 = jnp.float32
BF16 = jnp.bfloat16

GRID_W = 64
N_MOD = 9
HY_BANDS = 8
GLA_TAU = 16.0
GLA_CHUNK = 64
ROPE_THETA = 10000.0
LN_EPS = 1e-5
RMS_EPS = 1e-6

LANES = 128
MXU_DIM = 256
VMEM_BYTES_V7X = 64 * 1024 * 1024
VMEM_LIMIT = VMEM_BYTES_V7X - 8 * 1024 * 1024

GROUP = 4096
FFT_NB = 16
FFT_K1G = 8


def _cp(*sem):
    return pltpu.CompilerParams(dimension_semantics=sem, vmem_limit_bytes=VMEM_LIMIT)


def _dot(a, b):
    return jnp.dot(a, b, preferred_element_type=F32)


def _dot_nt(a, b):
    return lax.dot_general(a, b, (((1,), (1,)), ((), ())), preferred_element_type=F32)


def _dot_tn(a, b):
    return lax.dot_general(a, b, (((0,), (0,)), ((), ())), preferred_element_type=F32)


def _dot_hi(a, b):
    return jnp.dot(a, b, preferred_element_type=F32, precision=lax.Precision.HIGHEST)


def _silu(x):
    return x * jax.nn.sigmoid(x)


def _mod_kernel(c_ref, w_ref, b_ref, o_ref):
    c = c_ref[...]
    o_ref[...] = _dot(_silu(c).astype(BF16), w_ref[...].astype(BF16)) + b_ref[...]


def _modulation(cond, w_mod, b_mod):
    depth, d, n = w_mod.shape
    r = cond.shape[0]
    tn = 1024
    return pl.pallas_call(
        _mod_kernel,
        grid=(depth, n // tn),
        in_specs=[pl.BlockSpec((r, d), lambda l, j: (0, 0)),
                  pl.BlockSpec((None, d, tn), lambda l, j: (l, 0, j)),
                  pl.BlockSpec((None, 1, tn), lambda l, j: (l, 0, j))],
        out_specs=pl.BlockSpec((None, r, tn), lambda l, j: (l, 0, j)),
        out_shape=jax.ShapeDtypeStruct((depth, r, n), F32),
        compiler_params=_cp("parallel", "parallel"),
        name="mod",
    )(cond, w_mod, b_mod.reshape(depth, 1, n))


def _premod_kernel(x_ref, mod_ref, o_ref):
    o_ref[...] = (x_ref[...] * (1.0 + mod_ref[1:2, :]) + mod_ref[0:1, :]).astype(o_ref.dtype)


def _premod(x, mod, layer, tm=512):
    m, d = x.shape
    return pl.pallas_call(
        _premod_kernel,
        grid=(m // tm,),
        in_specs=[pl.BlockSpec((tm, d), lambda i: (i, 0)),
                  pl.BlockSpec((None, None, N_MOD, d), lambda i: (layer, (i * tm) // GROUP, 0, 0))],
        out_specs=pl.BlockSpec((tm, d), lambda i: (i, 0)),
        out_shape=jax.ShapeDtypeStruct((m, d), BF16),
        compiler_params=_cp("parallel"),
        name="premod",
    )(x, mod)


def _ffn1_kernel(h_ref, w1_ref, w3_ref, o_ref, w1b_ref, w3b_ref, *, tf, f_valid):
    @pl.when(pl.program_id(1) == 0)
    def _():
        col = pl.program_id(0) * tf + lax.broadcasted_iota(jnp.int32, w1_ref.shape, 1)
        keep = col < f_valid
        w1b_ref[...] = jnp.where(keep, w1_ref[...], 0.0).astype(BF16)
        w3b_ref[...] = jnp.where(keep, w3_ref[...], 0.0).astype(BF16)

    h = h_ref[...]
    a = _dot(h, w1b_ref[...])
    b = _dot(h, w3b_ref[...])
    o_ref[...] = (_silu(a) * b).astype(o_ref.dtype)


def _ffn1(h, w1, w3, layer, sub, tm=1024, tf=512):
    m, d = h.shape
    f = w1.shape[-1]
    nf = pl.cdiv(f, tf)
    wspec = pl.BlockSpec((None, None, d, tf), lambda j, i: (layer, sub, 0, j))
    return pl.pallas_call(
        functools.partial(_ffn1_kernel, tf=tf, f_valid=f),
        grid=(nf, m // tm),
        in_specs=[pl.BlockSpec((tm, d), lambda j, i: (i, 0)), wspec, wspec],
        out_specs=pl.BlockSpec((tm, tf), lambda j, i: (i, j)),
        out_shape=jax.ShapeDtypeStruct((m, nf * tf), BF16),
        scratch_shapes=[pltpu.VMEM((d, tf), BF16), pltpu.VMEM((d, tf), BF16)],
        compiler_params=_cp("parallel", "arbitrary"),
        name="ffn1",
    )(h, w1, w3)


def _mm_ln_kernel(a_ref, w_ref, x_ref, mod_ref, nmod_ref, g_ref, b_ref, xo_ref, ho_ref, *,
                  gate_row, coef, alpha, nshift_row):
    gate = coef * mod_ref[gate_row:gate_row + 1, :]
    xr = alpha * x_ref[...] + gate * _dot(a_ref[...], w_ref[...])
    mu = jnp.mean(xr, axis=-1, keepdims=True)
    xc = xr - mu
    var = jnp.mean(xc * xc, axis=-1, keepdims=True)
    xn = xc * lax.rsqrt(var + LN_EPS) * g_ref[...] + b_ref[...]
    xo_ref[...] = xn
    ho_ref[...] = (xn * (1.0 + nmod_ref[nshift_row + 1:nshift_row + 2, :])
                   + nmod_ref[nshift_row:nshift_row + 1, :]).astype(ho_ref.dtype)


def _mm_ln(a, w, x, mod, layer, gate_row, coef, alpha, nlayer, nshift_row, ln_g, ln_b, tm=256,
           row0=0, rows=None):
    kdim = a.shape[1]
    m = a.shape[0] if rows is None else rows
    d = w.shape[1]
    rb0 = row0 // tm
    kern = functools.partial(_mm_ln_kernel, gate_row=gate_row, coef=coef, alpha=alpha, nshift_row=nshift_row)
    return pl.pallas_call(
        kern,
        grid=(m // tm,),
        in_specs=[pl.BlockSpec((tm, kdim), lambda i: (rb0 + i, 0)),
                  pl.BlockSpec((kdim, d), lambda i: (0, 0), pipeline_mode=pl.Buffered(1)),
                  pl.BlockSpec((tm, d), lambda i: (rb0 + i, 0)),
                  pl.BlockSpec((None, None, N_MOD, d), lambda i: (layer, ((rb0 + i) * tm) // GROUP, 0, 0)),
                  pl.BlockSpec((None, None, N_MOD, d), lambda i: (nlayer, ((rb0 + i) * tm) // GROUP, 0, 0)),
                  pl.BlockSpec((1, d), lambda i: (0, 0)),
                  pl.BlockSpec((1, d), lambda i: (0, 0))],
        out_specs=[pl.BlockSpec((tm, d), lambda i: (i, 0)),
                   pl.BlockSpec((tm, d), lambda i: (i, 0))],
        out_shape=[jax.ShapeDtypeStruct((m, d), F32), jax.ShapeDtypeStruct((m, d), BF16)],
        compiler_params=_cp("parallel"),
        name="mm_ln",
    )(a, w, x, mod, mod, ln_g.reshape(1, d), ln_b.reshape(1, d))


def _proj_kernel(h_ref, w_ref, o_ref, *, act):
    r = _dot(h_ref[...], w_ref[...])
    if act == "sigmoid":
        r = jax.nn.sigmoid(r)
    o_ref[...] = r.astype(o_ref.dtype)


def _proj(h, w, out_dtype, act=None, tm=1024, tn=1024):
    m, d = h.shape
    n = w.shape[1]
    tn = min(tn, n)
    return pl.pallas_call(
        functools.partial(_proj_kernel, act=act),
        grid=(m // tm, n // tn),
        in_specs=[pl.BlockSpec((tm, d), lambda i, j: (i, 0)),
                  pl.BlockSpec((d, tn), lambda i, j: (0, j))],
        out_specs=pl.BlockSpec((tm, tn), lambda i, j: (i, j)),
        out_shape=jax.ShapeDtypeStruct((m, n), out_dtype),
        compiler_params=_cp("parallel", "arbitrary"),
        name="proj",
    )(h, w)


def _sconv_kernel(z_ref, zp_ref, zn_ref, w_ref, b_ref, o_ref, *, tm, halo, lp, ls):
    i = pl.program_id(0)
    u = z_ref[...].astype(F32)
    prev = zp_ref[...].astype(F32)[halo - 1:halo, :]
    nxt = zn_ref[...].astype(F32)[0:1, :]
    row = lax.broadcasted_iota(jnp.int32, u.shape, 0)
    lseq = jnp.where((i * tm) // GROUP == 0, lp, ls)
    pos = (row + i * tm) & (lseq - 1)
    up = jnp.where(row == 0, prev, pltpu.roll(u, 1, 0))
    up = jnp.where(pos == 0, 0.0, up)
    un = jnp.where(row == tm - 1, nxt, pltpu.roll(u, tm - 1, 0))
    un = jnp.where(pos == lseq - 1, 0.0, un)
    w = w_ref[...]
    o_ref[...] = (up * w[0:1, :] + u * w[1:2, :] + un * w[2:3, :] + b_ref[...]).astype(o_ref.dtype)


def _short_conv(z, rows, conv_w, conv_b, lp, ls, width, tm=1024, ct=512, halo=16):
    m = rows
    nct = width // ct
    nrb = m // halo
    kern = functools.partial(_sconv_kernel, tm=tm, halo=halo, lp=lp, ls=ls)
    return pl.pallas_call(
        kern,
        grid=(m // tm, 3 * nct),
        in_specs=[pl.BlockSpec((tm, ct), lambda i, j: (i, j)),
                  pl.BlockSpec((halo, ct), lambda i, j: (jnp.maximum(i * (tm // halo) - 1, 0), j)),
                  pl.BlockSpec((halo, ct), lambda i, j: (jnp.minimum((i + 1) * (tm // halo), nrb - 1), j)),
                  pl.BlockSpec((3, ct), lambda i, j: (0, j)),
                  pl.BlockSpec((1, ct), lambda i, j: (0, j))],
        out_specs=pl.BlockSpec((None, tm, ct), lambda i, j: (j // nct, i, j % nct)),
        out_shape=jax.ShapeDtypeStruct((3, m, width), F32),
        compiler_params=_cp("parallel", "parallel"),
        name="sconv",
    )(z, z, z, conv_w, conv_b.reshape(1, -1))


def _filter_features(L):
    t = np.linspace(0.0, 1.0, L, dtype=np.float32)
    w = (np.float32(2.0 * math.pi / L) * np.arange(L, dtype=np.float32)).astype(np.float32)
    f = np.linspace(1e-4, HY_BANDS - 1, HY_BANDS, dtype=np.float32)
    wf = (w[:, None] * f).astype(np.float32)
    feats = np.concatenate([t[:, None], np.cos(wf), -np.sin(wf)], -1).astype(np.float32)
    idx = np.concatenate([np.arange(L), [0], np.arange(L - 1, 0, -1)])
    tab = np.zeros((2 * L, 32), np.float32)
    tab[:, :feats.shape[1]] = feats[idx]
    tab[:, 24] = t[idx]
    tab[:L, 25] = 1.0
    tab[L + 1:, 26] = -1.0
    return tab, feats.shape[1]


def _filter_kernel(tab_ref, w1_ref, b1_ref, fr_ref, w2_ref, b2_ref, w3_ref, dec_ref, o_ref, *, width):
    tab = tab_ref[...]
    fr = fr_ref[...]
    hdn = jnp.sin(fr[0:1, :] * (_dot_hi(tab, w1_ref[...]) + b1_ref[...]))
    hdn = jnp.sin(fr[1:2, :] * (_dot_hi(hdn, w2_ref[...]) + b2_ref[...]))
    t = tab[:, 24:25]
    mf = tab[:, 25:26]
    mb = tab[:, 26:27]
    h = _dot_hi(hdn, w3_ref[...]) * jnp.exp(-t * jnp.abs(dec_ref[...]))
    for o in range(2):
        base = 2 * width * o
        o_ref[o] = mf * h[:, base:base + width] + mb * h[:, base + width:base + 2 * width]


def _hyena_filter_taps(L, hy_w1, hy_b1, hy_freq, hy_w2, hy_b2, hy_w3, hy_decay, width, rb=256):
    tab_np, nfeat = _filter_features(L)
    fh = hy_w1.shape[1]
    w1p = jnp.zeros((32, fh), F32).at[:nfeat].set(hy_w1)
    n = 2 * L
    return pl.pallas_call(
        functools.partial(_filter_kernel, width=width),
        grid=(n // rb,),
        in_specs=[pl.BlockSpec((rb, 32), lambda i: (i, 0)),
                  pl.BlockSpec((32, fh), lambda i: (0, 0)),
                  pl.BlockSpec((1, fh), lambda i: (0, 0)),
                  pl.BlockSpec((2, fh), lambda i: (0, 0)),
                  pl.BlockSpec((fh, fh), lambda i: (0, 0)),
                  pl.BlockSpec((1, fh), lambda i: (0, 0)),
                  pl.BlockSpec((fh, 4 * width), lambda i: (0, 0)),
                  pl.BlockSpec((1, 4 * width), lambda i: (0, 0))],
        out_specs=pl.BlockSpec((2, rb, width), lambda i: (0, i, 0)),
        out_shape=jax.ShapeDtypeStruct((2, n, width), F32),
        compiler_params=_cp("parallel"),
        name="hyfilter",
    )(jnp.asarray(tab_np), w1p, hy_b1.reshape(1, fh), hy_freq, hy_w2, hy_b2.reshape(1, fh), hy_w3,
      hy_decay.reshape(1, -1))


def _dft_a(na, ka):
    k1 = np.arange(na // 2)[:, None].astype(np.float64)
    a = np.arange(ka)[None, :].astype(np.float64)
    th = 2.0 * np.pi * a * (k1 + 0.5) / na
    return np.concatenate([np.cos(th), -np.sin(th)], 0)


def _idft_a(na, n, rows):
    k1 = np.arange(na // 2)[None, :].astype(np.float64)
    a = np.arange(rows)[:, None].astype(np.float64)
    th = 2.0 * np.pi * a * (k1 + 0.5) / na
    return (2.0 / n) * np.concatenate([np.cos(th), -np.sin(th)], 1)


def _dft_b(na, nb, g):
    n = na * nb
    half = na // 2
    k1 = np.arange(half).astype(np.float64)
    b = np.arange(nb).astype(np.float64)
    k2 = np.arange(nb).astype(np.float64)
    phi = 2.0 * np.pi * (b[None, None, :] * k2[None, :, None] / nb
                         + b[None, None, :] * (k1[:, None, None] + 0.5) / n)
    c, s = np.cos(phi), np.sin(phi)
    ng = half // g
    r = g * nb
    fwd = np.zeros((ng, 2 * r, 2 * r))
    for q in range(g):
        rows = slice(q * nb, (q + 1) * nb)
        rows_i = slice(r + q * nb, r + (q + 1) * nb)
        cols = slice(q, r, g)
        cols_i = slice(r + q, 2 * r, g)
        cq, sq = c[q::g], s[q::g]
        fwd[:, rows, cols] = cq
        fwd[:, rows, cols_i] = sq
        fwd[:, rows_i, cols] = -sq
        fwd[:, rows_i, cols_i] = cq
    inv = np.transpose(fwd, (0, 2, 1))
    return fwd, inv


def _lmm_kernel(f_ref, x_ref, o_ref):
    o_ref[...] = _dot(f_ref[...], x_ref[...].astype(BF16)).astype(o_ref.dtype)


def _lmm(f, x, x_index, nbatch, out_dtype, tn=2048):
    mo, k = f.shape
    n = x.shape[-1]
    lead = len(x.shape) - 2
    return pl.pallas_call(
        _lmm_kernel,
        grid=(nbatch, n // tn),
        in_specs=[pl.BlockSpec((mo, k), lambda b, j: (0, 0)),
                  pl.BlockSpec((None,) * lead + (k, tn), lambda b, j: x_index(b) + (0, j))],
        out_specs=pl.BlockSpec((None, mo, tn), lambda b, j: (b, 0, j)),
        out_shape=jax.ShapeDtypeStruct((nbatch, mo, n), out_dtype),
        compiler_params=_cp("parallel", "parallel"),
        name="dft_a",
    )(f, x)


def _stage_a_fwd(x_ref, f1_ref, r_ref, nb, ct):
    for b in range(nb):
        r_ref[:, b * ct:(b + 1) * ct] = _dot(f1_ref[...], x_ref[:, b, :].astype(BF16))


def _stage_b_rows(r_ref, r0, half, nb, kg, ct):
    return [(pl.ds(ri * half + r0, kg), slice(b * ct, (b + 1) * ct)) for ri in range(2) for b in range(nb)]


def _row_permutation(nb, al):
    p = np.zeros((nb * al, nb * al), np.float32)
    for b in range(nb):
        for a in range(al):
            p[b * al + a, a * nb + b] = 1.0
    return p


def _to_residue_major(src_ref, perm_ref, dst_ref, nb, al):
    blk = nb * al
    for i in range(src_ref.shape[0] // blk):
        z = _dot(perm_ref[...], src_ref[i * blk:(i + 1) * blk, :].astype(BF16))
        for b in range(nb):
            dst_ref[b, i * al:(i + 1) * al, :] = z[b * al:(b + 1) * al, :]


def _short_conv_residue_major(u_ref, w_ref, cb_ref, nb):
    rows = u_ref.shape[1]
    row = lax.broadcasted_iota(jnp.int32, u_ref.shape[1:], 0)
    w = w_ref[...]
    cb = cb_ref[...]
    first = u_ref[0]
    prev = jnp.where(row == 0, 0.0, pltpu.roll(u_ref[nb - 1], 1, 0))
    for b in range(nb):
        cur = first if b == 0 else u_ref[b]
        nxt = u_ref[b + 1] if b + 1 < nb else jnp.where(row == rows - 1, 0.0, pltpu.roll(first, rows - 1, 0))
        u_ref[b] = prev * w[0:1, :] + cur * w[1:2, :] + nxt * w[2:3, :] + cb
        prev = cur


def _lconv_kernel(zv_ref, zx_ref, cwv_ref, cbv_ref, cwx_ref, cbx_ref, perm_ref, permt_ref, f1_ref, g_ref,
                  tb_ref, tbi_ref, kf_ref, bias_ref, o_ref, v_ref, x_ref, r_ref, *, nb, half, kg, ct, conv_v):
    r = nb * kg
    al = perm_ref.shape[0] // nb
    _to_residue_major(zv_ref, perm_ref, v_ref, nb, al)
    _to_residue_major(zx_ref, perm_ref, x_ref, nb, al)
    if conv_v:
        _short_conv_residue_major(v_ref, cwv_ref, cbv_ref, nb)
    _short_conv_residue_major(x_ref, cwx_ref, cbx_ref, nb)
    for b in range(nb):
        r_ref[:, b * ct:(b + 1) * ct] = _dot(f1_ref[...], v_ref[b].astype(BF16))

    for g in range(half // kg):
        tiles = _stage_b_rows(r_ref, g * kg, half, nb, kg, ct)
        a = jnp.concatenate([r_ref[rs, cs] for rs, cs in tiles], axis=0).astype(BF16)
        x = _dot(tb_ref[g], a)
        kf = kf_ref[g * 2 * r:(g + 1) * 2 * r, :]
        xr, xi = x[:r], x[r:]
        kr, ki = kf[:r], kf[r:]
        y = jnp.concatenate([xr * kr - xi * ki, xr * ki + xi * kr], axis=0).astype(BF16)
        bh = _dot(tbi_ref[g], y)
        for t, (rs, cs) in enumerate(tiles):
            r_ref[rs, cs] = bh[t * kg:(t + 1) * kg, :]
    bias = bias_ref[...]
    for b in range(nb):
        y = _dot(g_ref[...], r_ref[:, b * ct:(b + 1) * ct].astype(BF16))
        v_ref[b] = x_ref[b] * (y + v_ref[b] * bias)
    blk = nb * al
    for i in range(o_ref.shape[0] // blk):
        t = jnp.concatenate([v_ref[b, i * al:(i + 1) * al, :] for b in range(nb)], axis=0).astype(BF16)
        o_ref[i * blk:(i + 1) * blk, :] = _dot(permt_ref[...], t).astype(o_ref.dtype)


def _lconv(zv, v_row0, v_col0, zx, x_row0, x_col0, conv_w, conv_b, conv_v, perm, permt, f1, g_inv, tb, tbi,
           kf, order, bias, nbatch, L, ct=256):
    c = kf.shape[-1]
    n2, _ = f1.shape
    ng, r2, _ = tb.shape
    nb = L // (n2 // 2)
    kg = r2 // (2 * nb)
    ncb = c // ct
    once = dict(pipeline_mode=pl.Buffered(1))
    kern = functools.partial(_lconv_kernel, nb=nb, half=n2 // 2, kg=kg, ct=ct, conv_v=conv_v)
    vb0, xb0 = v_row0 // L, x_row0 // L
    return pl.pallas_call(
        kern,
        grid=(ncb, nbatch),
        in_specs=[pl.BlockSpec((L, ct), lambda j, b: (vb0 + b, v_col0 * ncb + j)),
                  pl.BlockSpec((L, ct), lambda j, b: (xb0 + b, x_col0 * ncb + j)),
                  pl.BlockSpec((None, 3, ct), lambda j, b: (0, 0, j)),
                  pl.BlockSpec((None, 1, ct), lambda j, b: (0, 0, j)),
                  pl.BlockSpec((None, 3, ct), lambda j, b: (1, 0, j)),
                  pl.BlockSpec((None, 1, ct), lambda j, b: (1, 0, j)),
                  pl.BlockSpec(perm.shape, lambda j, b: (0, 0), **once),
                  pl.BlockSpec(permt.shape, lambda j, b: (0, 0), **once),
                  pl.BlockSpec(f1.shape, lambda j, b: (0, 0), **once),
                  pl.BlockSpec(g_inv.shape, lambda j, b: (0, 0), **once),
                  pl.BlockSpec(tb.shape, lambda j, b: (0, 0, 0), **once),
                  pl.BlockSpec(tbi.shape, lambda j, b: (0, 0, 0), **once),
                  pl.BlockSpec((None, ng * r2, ct), lambda j, b: (order, 0, j), **once),
                  pl.BlockSpec((1, ct), lambda j, b: (0, j))],
        out_specs=pl.BlockSpec((L, ct), lambda j, b: (b, j)),
        out_shape=jax.ShapeDtypeStruct((nbatch * L, c), BF16),
        scratch_shapes=[pltpu.VMEM((nb, L // nb, ct), F32), pltpu.VMEM((nb, L // nb, ct), F32),
                        pltpu.VMEM((n2, nb * ct), F32)],
        compiler_params=_cp("parallel", "arbitrary"),
        name="lconv",
    )(zv, zx, conv_w, conv_b, conv_w, conv_b, perm, permt, f1, g_inv, tb, tbi, kf, bias.reshape(1, c))


def _lconv_filter_kernel(x_ref, f1_ref, tb_ref, o_ref, r_ref, *, nb, half, kg, ct):
    r2 = 2 * nb * kg
    _stage_a_fwd(x_ref, f1_ref, r_ref, nb, ct)

    for g in range(half // kg):
        tiles = _stage_b_rows(r_ref, g * kg, half, nb, kg, ct)
        a = jnp.concatenate([r_ref[rs, cs] for rs, cs in tiles], axis=0).astype(BF16)
        o_ref[g * r2:(g + 1) * r2, :] = _dot(tb_ref[g], a)


def _lconv_filter(taps, f1, tb, ct=256):
    norder, na, nb, c = taps.shape
    n2, _ = f1.shape
    ng, r2, _ = tb.shape
    kg = r2 // (2 * nb)
    kern = functools.partial(_lconv_filter_kernel, nb=nb, half=n2 // 2, kg=kg, ct=ct)
    return pl.pallas_call(
        kern,
        grid=(norder, c // ct),
        in_specs=[pl.BlockSpec((None, na, nb, ct), lambda o, j: (o, 0, 0, j)),
                  pl.BlockSpec(f1.shape, lambda o, j: (0, 0)),
                  pl.BlockSpec(tb.shape, lambda o, j: (0, 0, 0))],
        out_specs=pl.BlockSpec((None, ng * r2, ct), lambda o, j: (o, 0, j)),
        out_shape=jax.ShapeDtypeStruct((norder, ng * r2, c), F32),
        scratch_shapes=[pltpu.VMEM((n2, nb * ct), F32)],
        compiler_params=_cp("parallel", "parallel"),
        name="lconv_filter",
    )(taps, f1, tb)


def _pconv_kernel(fd_ref, gd_ref, v_ref, kf_ref, xg_ref, bias_ref, o_ref, *, half):
    v = v_ref[...]
    x = _dot(fd_ref[...], v.astype(BF16))
    kf = kf_ref[...]
    xr, xi = x[:half], x[half:]
    kr, ki = kf[:half], kf[half:]
    y = jnp.concatenate([xr * kr - xi * ki, xr * ki + xi * kr], axis=0).astype(BF16)
    yt = _dot(gd_ref[...], y)
    o_ref[...] = (xg_ref[...].astype(F32) * (yt + v.astype(F32) * bias_ref[...])).astype(o_ref.dtype)


def _pconv(fd, gd, vsrc, v_lead, kf, order, xsrc, x_lead, bias, nseq, L, out_dtype, ct=512):
    c = kf.shape[-1]
    n2 = fd.shape[0]
    return pl.pallas_call(
        functools.partial(_pconv_kernel, half=n2 // 2),
        grid=(nseq, c // ct),
        in_specs=[pl.BlockSpec((n2, L), lambda s, j: (0, 0)),
                  pl.BlockSpec((L, n2), lambda s, j: (0, 0)),
                  pl.BlockSpec((None,) * len(v_lead) + (L, ct), lambda s, j: v_lead + (s, j)),
                  pl.BlockSpec((None, n2, ct), lambda s, j: (order, 0, j)),
                  pl.BlockSpec((None,) * len(x_lead) + (L, ct), lambda s, j: x_lead + (s, j)),
                  pl.BlockSpec((1, ct), lambda s, j: (0, j))],
        out_specs=pl.BlockSpec((L, ct), lambda s, j: (s, j)),
        out_shape=jax.ShapeDtypeStruct((nseq * L, c), out_dtype),
        compiler_params=_cp("parallel", "parallel"),
        name="pconv",
    )(fd, gd, vsrc, kf, xsrc, bias.reshape(1, c))


def _gate_kernel(lr_ref, wa_ref, ba_ref, o_ref):
    logits = _dot_hi(lr_ref[...], wa_ref[...]) + ba_ref[...]
    o_ref[...] = jax.nn.log_sigmoid(logits) * (1.0 / GLA_TAU)


def _gla_gates(lr, wa_cat, ba_cat, tm=1024):
    m, k = lr.shape
    n = wa_cat.shape[1]
    return pl.pallas_call(
        _gate_kernel,
        grid=(m // tm,),
        in_specs=[pl.BlockSpec((tm, k), lambda i: (i, 0)),
                  pl.BlockSpec((k, n), lambda i: (0, 0)),
                  pl.BlockSpec((1, n), lambda i: (0, 0))],
        out_specs=pl.BlockSpec((tm, n), lambda i: (i, 0)),
        out_shape=jax.ShapeDtypeStruct((m, n), F32),
        compiler_params=_cp("parallel"),
        name="gla_gates",
    )(lr, wa_cat, ba_cat)


def _split3(x):
    hi = x.astype(BF16)
    r1 = x - hi.astype(F32)
    mid = r1.astype(BF16)
    lo = (r1 - mid.astype(F32)).astype(BF16)
    return hi, mid, lo


def _gla_dir(d, r0, q_ref, k_ref, v_ref, la_ref, o_ref, st_ref, tri, causal, ref_row, last_row,
             heads, dk, dv, scale):
    ch = GLA_CHUNK
    rows = pl.ds(pl.multiple_of(r0, ch), ch)
    la = la_ref[rows, :]
    hi, mid, lo = _split3(la)
    b = _dot(tri, hi) + _dot(tri, mid) + _dot(tri, lo)
    bref = b[ref_row:ref_row + 1, :]
    blast = b[last_row:last_row + 1, :]
    q = q_ref[rows, :].astype(F32) * scale
    k = k_ref[rows, :].astype(F32)
    qt = (q * jnp.exp(b - bref)).astype(BF16)
    kt = (k * jnp.exp(bref - b)).astype(BF16)
    qin = (q * jnp.exp(b)).astype(BF16)
    kst = (k * jnp.exp(blast - b)).astype(BF16)
    dec = jnp.exp(blast)
    for h in range(heads):
        ks = slice(h * dk, (h + 1) * dk)
        vs = slice(h * dv, (h + 1) * dv)
        att = _dot_nt(qt[:, ks], kt[:, ks])
        att = jnp.where(causal, att, 0.0).astype(BF16)
        vh = v_ref[rows, vs]
        st = st_ref[d, h]
        o_ref[rows, vs] = _dot(att, vh) + _dot_nt(qin[:, ks], st.astype(BF16))
        st_ref[d, h] = st * dec[:, ks] + _dot_tn(vh, kst[:, ks])


def _gla_kernel(*refs, nch, heads, dk, dv, has_s0, scale, nsq):
    seq_in = [refs[8 * q:8 * (q + 1)] for q in range(nsq)]
    rest = refs[8 * nsq:]
    if has_s0:
        s0, of, ob, st = rest
    else:
        of, ob, st = rest
        s0 = None
    ch = GLA_CHUNK

    @pl.when(pl.program_id(1) == 0)
    def _():
        if has_s0:
            st[...] = s0[...]
        else:
            st[...] = jnp.zeros_like(st)

    r_i = lax.broadcasted_iota(jnp.int32, (ch, ch), 0)
    c_i = lax.broadcasted_iota(jnp.int32, (ch, ch), 1)
    lower = r_i >= c_i
    upper = r_i <= c_i
    tri_l = jnp.where(lower, 1.0, 0.0).astype(BF16)
    tri_u = jnp.where(upper, 1.0, 0.0).astype(BF16)

    def body(c, carry):
        for q, (qf, kf, vf, laf, qb, kb, vb, lab) in enumerate(seq_in):
            _gla_dir(0, c * ch, qf, kf, vf, laf, of.at[q], st.at[q], tri_l, lower, ch // 2 - 1, ch - 1,
                     heads, dk, dv, scale)
            _gla_dir(1, (nch - 1 - c) * ch, qb, kb, vb, lab, ob.at[q], st.at[q], tri_u, upper, ch // 2, 0,
                     heads, dk, dv, scale)
        return carry

    lax.fori_loop(0, nch, body, 0)


def _gla(zmain, qcol, kcol, vcol, la, row0, nseq, L, tb, heads, dk, dv, s0t=None, nsq=2):
    nblk = L // tb
    rb0 = row0 // tb
    hk, hv = heads * dk, heads * dv
    in_specs, args = [], []
    for q in range(nsq):
        def fwd(s, j, q=q):
            return rb0 + (s * nsq + q) * nblk + j

        def bwd(s, j, q=q):
            return rb0 + (s * nsq + q) * nblk + (nblk - 1 - j)

        for rowf, lcol in ((fwd, 0), (bwd, 1)):
            in_specs += [pl.BlockSpec((tb, hk), lambda s, j, rowf=rowf: (rowf(s, j), qcol)),
                         pl.BlockSpec((tb, hk), lambda s, j, rowf=rowf: (rowf(s, j), kcol)),
                         pl.BlockSpec((tb, hv), lambda s, j, rowf=rowf: (rowf(s, j), vcol)),
                         pl.BlockSpec((tb, hk), lambda s, j, rowf=rowf, lcol=lcol: (rowf(s, j), lcol))]
            args += [zmain, zmain, zmain, la]
    st_spec = pl.BlockSpec((nsq, 2, heads, dv, dk), lambda s, j: (s, 0, 0, 0, 0))
    if s0t is not None:
        in_specs.append(st_spec)
        args.append(s0t)
    kern = functools.partial(_gla_kernel, nch=tb // GLA_CHUNK, heads=heads, dk=dk, dv=dv,
                             has_s0=s0t is not None, scale=dk ** -0.5, nsq=nsq)
    o_shape = jax.ShapeDtypeStruct((nseq // nsq, nsq, L, hv), F32)
    o_f, o_b, st = pl.pallas_call(
        kern,
        grid=(nseq // nsq, nblk),
        in_specs=in_specs,
        out_specs=[pl.BlockSpec((None, nsq, tb, hv), lambda s, j: (s, 0, j, 0)),
                   pl.BlockSpec((None, nsq, tb, hv), lambda s, j: (s, 0, nblk - 1 - j, 0)),
                   st_spec],
        out_shape=[o_shape, o_shape, jax.ShapeDtypeStruct((nseq, 2, heads, dv, dk), F32)],
        compiler_params=_cp("parallel", "arbitrary"),
        name="gla",
    )(*args)
    return o_f.reshape(nseq * L, hv), o_b.reshape(nseq * L, hv), st


def _gla_post_kernel(of_ref, ob_ref, gr_ref, g_ref, o_ref, *, heads, dv):
    o = of_ref[...] + ob_ref[...]
    gate = _silu(gr_ref[...].astype(F32))
    g = g_ref[...]
    for h in range(heads):
        sl = slice(h * dv, (h + 1) * dv)
        oh = o[:, sl]
        ms = jnp.mean(oh * oh, axis=-1, keepdims=True)
        o_ref[:, sl] = (oh * lax.rsqrt(ms + RMS_EPS) * g * gate[:, sl]).astype(o_ref.dtype)


def _gla_post(o_f, o_b, zmain, grcol, norm_g, heads, dv, tm=512):
    m, hv = o_f.shape
    return pl.pallas_call(
        functools.partial(_gla_post_kernel, heads=heads, dv=dv),
        grid=(m // tm,),
        in_specs=[pl.BlockSpec((tm, hv), lambda i: (i, 0)),
                  pl.BlockSpec((tm, hv), lambda i: (i, 0)),
                  pl.BlockSpec((tm, hv), lambda i: (i, grcol)),
                  pl.BlockSpec((1, dv), lambda i: (0, 0))],
        out_specs=pl.BlockSpec((tm, hv), lambda i: (i, 0)),
        out_shape=jax.ShapeDtypeStruct((m, hv), BF16),
        compiler_params=_cp("parallel"),
        name="gla_post",
    )(o_f, o_b, zmain, norm_g.reshape(1, dv))


def _rope_tables(L, dh):
    rows = L // GRID_W
    r = np.repeat(np.arange(rows, dtype=np.float32), GRID_W)
    col = np.tile(np.arange(GRID_W, dtype=np.float32), rows)
    nf = dh // 4
    inv = (np.float32(ROPE_THETA) ** (-np.arange(nf, dtype=np.float32) / nf)).astype(np.float32)
    ang_r = (r[:, None] * inv).astype(np.float32)
    ang_c = (col[:, None] * inv).astype(np.float32)
    cos = np.concatenate([np.cos(ang_r), np.cos(ang_r), np.cos(ang_c), np.cos(ang_c)], -1)
    sin = np.concatenate([-np.sin(ang_r), np.sin(ang_r), -np.sin(ang_c), np.sin(ang_c)], -1)
    cos = np.concatenate([cos, cos], -1).astype(np.float32)
    sin = np.concatenate([sin, sin], -1).astype(np.float32)
    cos_t = np.stack([np.ones_like(cos), cos])
    sin_t = np.stack([np.zeros_like(sin), sin])
    return cos_t, sin_t


def _qkv_prep_kernel(q_ref, k_ref, v_ref, cos_ref, sin_ref, qo_ref, ko_ref, vo_ref, *, heads, hw, nf, scale):
    cos = cos_ref[...]
    sin = sin_ref[...]
    lane = lax.broadcasted_iota(jnp.int32, cos.shape, 1)
    first = (lane & (2 * nf - 1)) < nf

    def rope(x):
        sw = jnp.where(first, pltpu.roll(x, hw - nf, 1), pltpu.roll(x, nf, 1))
        return x * cos + sw * sin

    ones = jnp.ones((q_ref.shape[0], hw), vo_ref.dtype)
    for h in range(heads):
        sl = slice(h * hw, (h + 1) * hw)
        qo_ref[:, sl] = (rope(q_ref[:, sl]) * scale).astype(qo_ref.dtype)
        ko_ref[sl, :] = rope(k_ref[:, sl]).T.astype(ko_ref.dtype)
        vo_ref[:, 2 * h * hw:(2 * h + 1) * hw] = v_ref[:, sl].astype(vo_ref.dtype)
        vo_ref[:, (2 * h + 1) * hw:(2 * h + 2) * hw] = ones


def _qkv_prep(zd, cos_t, sin_t, heads, dh, tm=512):
    m = zd.shape[0]
    w = zd.shape[1] // 3
    hw = w // heads
    gb = GROUP // tm
    kern = functools.partial(_qkv_prep_kernel, heads=heads, hw=hw, nf=dh // 4,
                             scale=dh ** -0.5 * math.log2(math.e))
    tab = pl.BlockSpec((None, tm, hw), lambda i: (jnp.minimum(i // gb, 1), i % gb, 0))
    out = jax.ShapeDtypeStruct((m, w), BF16)
    return pl.pallas_call(
        kern,
        grid=(m // tm,),
        in_specs=[pl.BlockSpec((tm, w), lambda i: (i, 0)),
                  pl.BlockSpec((tm, w), lambda i: (i, 1)),
                  pl.BlockSpec((tm, w), lambda i: (i, 2)),
                  tab, tab],
        out_specs=[pl.BlockSpec((tm, w), lambda i: (i, 0)),
                   pl.BlockSpec((w, tm), lambda i: (0, i)),
                   pl.BlockSpec((tm, 2 * w), lambda i: (i, 0))],
        out_shape=[out, jax.ShapeDtypeStruct((w, m), BF16), jax.ShapeDtypeStruct((m, 2 * w), BF16)],
        compiler_params=_cp("parallel"),
        name="qkv_prep",
    )(zd, zd, zd, cos_t, sin_t)


def _attn_kernel(*refs, has_ctx, lam_init, dh, bf16_exp, tq):
    if has_ctx:
        q_ref, k_ref, v_ref, kc_ref, vc_ref, lam_ref, g_ref, o_ref, s0_ref, s1_ref = refs
    else:
        q_ref, k_ref, v_ref, lam_ref, g_ref, o_ref, s0_ref, s1_ref = refs
    hw = 2 * dh
    lk = k_ref.shape[1]
    nsub = q_ref.shape[0] // tq
    lane = lax.broadcasted_iota(jnp.int32, (tq, hw), 1)
    first = lane < dh
    if has_ctx:
        kct = kc_ref[...].T.astype(BF16)
        vcv = vc_ref[...].astype(BF16)
        vce = jnp.concatenate([vcv, jnp.ones_like(vcv)], axis=1)

    def scores(i, j, s_ref):
        q = q_ref[pl.ds(i * tq if isinstance(i, int) else pl.multiple_of(i * tq, tq), tq), :]
        qj = jnp.where(first, q, jnp.zeros_like(q)) if j == 0 else jnp.where(first, jnp.zeros_like(q), q)
        s_ref[:, :lk] = _dot(qj, k_ref[...])
        if has_ctx:
            s_ref[:, lk:] = _dot(qj, kct)

    def prob(t):
        if bf16_exp:
            return jnp.exp2(t.astype(BF16))
        return jnp.exp2(t).astype(BF16)

    def softmax_pv(s_ref):
        s = s_ref[...]
        p = prob(s - jnp.max(s, axis=-1, keepdims=True))
        acc = _dot(p[:, :lk], v_ref[...])
        if has_ctx:
            acc = acc + _dot(p[:, lk:], vce)
        return acc[:, :hw] / acc[:, hw:]

    lp = lam_ref[...]
    lam = (jnp.exp(jnp.sum(lp[0:1] * lp[1:2], axis=-1, keepdims=True))
           - jnp.exp(jnp.sum(lp[2:3] * lp[3:4], axis=-1, keepdims=True)) + lam_init)
    gain = g_ref[...] * (1.0 - lam_init)

    scores(0, 0, s0_ref)

    def body(i, carry):
        scores(i, 1, s1_ref)
        sm0 = softmax_pv(s0_ref)
        scores(jnp.minimum(i + 1, nsub - 1), 0, s0_ref)
        o = sm0 - lam * softmax_pv(s1_ref)
        ms = jnp.mean(o * o, axis=-1, keepdims=True)
        o_ref[pl.ds(pl.multiple_of(i * tq, tq), tq), :] = (o * lax.rsqrt(ms + RMS_EPS) * gain).astype(o_ref.dtype)
        return carry

    lax.fori_loop(0, nsub, body, 0)


def _diff_attention(qs, kr, vb, row0, nseq, L, tb, heads, dh, lam_p, norm_g, lam_init, ctx=None, tq=256):
    hw = 2 * dh
    nqb = L // tb
    qb0 = row0 // tb
    kb0 = row0 // L
    past = 0 if ctx is None else ctx[0].shape[2]
    in_specs = [pl.BlockSpec((tb, hw), lambda s, h, i: (qb0 + s * nqb + i, h)),
                pl.BlockSpec((hw, L), lambda s, h, i: (h, kb0 + s)),
                pl.BlockSpec((L, 2 * hw), lambda s, h, i: (kb0 + s, h))]
    args = [qs, kr, vb]
    if ctx is not None:
        ck, cv, layer = ctx
        in_specs += [pl.BlockSpec((None, None, past, hw), lambda s, h, i: (s, layer, 0, h)),
                     pl.BlockSpec((None, None, past, hw), lambda s, h, i: (s, layer, 0, h))]
        args += [ck, cv]
    in_specs += [pl.BlockSpec(lam_p.shape, lambda s, h, i: (0, 0)),
                 pl.BlockSpec((1, hw), lambda s, h, i: (0, 0))]
    args += [lam_p, norm_g.reshape(1, hw)]
    kern = functools.partial(_attn_kernel, has_ctx=ctx is not None, lam_init=lam_init, dh=dh,
                             bf16_exp=L > 1024, tq=tq)
    return pl.pallas_call(
        kern,
        grid=(nseq, heads, nqb),
        in_specs=in_specs,
        out_specs=pl.BlockSpec((tb, hw), lambda s, h, i: (s * nqb + i, h)),
        out_shape=jax.ShapeDtypeStruct((nseq * L, heads * hw), BF16),
        scratch_shapes=[pltpu.VMEM((tq, L + past), F32), pltpu.VMEM((tq, L + past), F32)],
        compiler_params=_cp("parallel", "parallel", "arbitrary"),
        name="diff_attn",
    )(*args)


def _mix_kernel(ya_ref, yb_ref, yc_ref, g_ref, w_ref, o_ref, acc_ref):
    k = pl.program_id(1)

    def contrib(y_ref):
        return g_ref[...].astype(F32) * _dot(y_ref[...].astype(BF16), w_ref[...])

    @pl.when(k == 0)
    def _():
        acc_ref[...] = contrib(ya_ref)

    @pl.when(k == 1)
    def _():
        acc_ref[...] += contrib(yb_ref)

    @pl.when(k == 2)
    def _():
        o_ref[...] = (acc_ref[...] + contrib(yc_ref)).astype(o_ref.dtype)


def _mix(ya, yb, yc, gates, wbr, tm=512):
    m, w = ya.shape
    d = wbr.shape[2]
    return pl.pallas_call(
        _mix_kernel,
        grid=(m // tm, 3),
        in_specs=[pl.BlockSpec((tm, w), lambda i, k: (i, 0)),
                  pl.BlockSpec((tm, w), lambda i, k: (i, 0)),
                  pl.BlockSpec((tm, w), lambda i, k: (i, 0)),
                  pl.BlockSpec((tm, d), lambda i, k: (i, k)),
                  pl.BlockSpec((None, w, d), lambda i, k: (k, 0, 0))],
        out_specs=pl.BlockSpec((tm, d), lambda i, k: (i, 0)),
        out_shape=jax.ShapeDtypeStruct((m, d), BF16),
        scratch_shapes=[pltpu.VMEM((tm, d), F32)],
        compiler_params=_cp("parallel", "arbitrary"),
        name="mix",
    )(ya, yb, yc, gates, wbr)


def _pad_cols(w, n):
    return jnp.pad(w, ((0, 0), (0, n - w.shape[1])))


def kernel(x_prompt, x_sample, cache_k, cache_v, state_gla, c, c_ctx, w_mod, b_mod, ln_g, ln_b, ffn_w1, ffn_w3, ffn_w2, w_in, hy_conv_w, hy_conv_b, hy_w1, hy_b1, hy_freq, hy_w2, hy_b2, hy_w3, hy_decay, hy_bias, gla_wa, gla_ba, gla_norm_g, diff_lam, diff_norm_g, w_branch_a, w_branch_b, w_branch_c, w_out):
    batch, seq, d = x_prompt.shape
    dec_batch, dec_seq, _ = x_sample.shape
    depth = w_mod.shape[0]
    ffn_dim = ffn_w1.shape[3]
    hy_w = hy_bias.shape[2]
    heads_g, dk_g = 4, gla_wa.shape[3] // 4
    dv_g = gla_norm_g.shape[1]
    rank = gla_wa.shape[2]
    dh = diff_lam.shape[2]
    heads_d = cache_k.shape[3]
    dw = heads_d * 2 * dh
    gw = heads_g * dv_g
    gk = heads_g * dk_g
    assert batch * seq == GROUP and dec_seq == GROUP
    mp = batch * seq
    m = mp + dec_batch * dec_seq
    ngroups = 1 + dec_batch
    alpha = (2 * depth) ** 0.25
    lam_inits = [0.8 - 0.6 * math.exp(-0.3 * l) for l in range(depth)]

    c_main = 3 * hy_w + 2 * gk + 2 * gw
    c_lr = c_main
    c_d = c_lr + 2 * rank
    c_g = c_d + 3 * dw

    cond = jnp.concatenate([c_ctx[None], c, jnp.zeros((16 - ngroups, d), F32)], axis=0)
    mod = _modulation(cond, w_mod, b_mod)[:, :ngroups].reshape(depth, ngroups, N_MOD, d)

    na = 2 * dec_seq // FFT_NB
    n_s = 2 * dec_seq
    f1_half = jnp.asarray(_dft_a(na, na // 2), BF16)
    f1_full = jnp.asarray(_dft_a(na, na), BF16)
    g_s = jnp.asarray(_idft_a(na, n_s, na // 2), BF16)
    tb_np, tbi_np = _dft_b(na, FFT_NB, FFT_K1G)
    tb, tbi = jnp.asarray(tb_np, BF16), jnp.asarray(tbi_np, BF16)
    perm_np = _row_permutation(FFT_NB, FFT_K1G)
    perm, permt = jnp.asarray(perm_np, BF16), jnp.asarray(perm_np.T, BF16)
    fp_half = jnp.asarray(_dft_a(2 * seq, seq), BF16)
    fp_full = jnp.asarray(_dft_a(2 * seq, 2 * seq), BF16)
    g_p = jnp.asarray(_idft_a(2 * seq, 2 * seq, seq), BF16)
    cos_t, sin_t = _rope_tables(dec_seq, dh)
    cos_t, sin_t = jnp.asarray(cos_t), jnp.asarray(sin_t)
    half_rows = (na // 2) * FFT_NB
    lanes_s = FFT_NB * hy_w

    x = jnp.concatenate([x_prompt.reshape(mp, d), x_sample.reshape(dec_batch * dec_seq, d)], axis=0)
    h = _premod(x, mod, 0)

    fp = ((ffn_dim + 511) // 512) * 512
    ck = cache_k.reshape(dec_batch, depth, cache_k.shape[2], dw)
    cv = cache_v.reshape(dec_batch, depth, cache_v.shape[2], dw)
    new_k, new_v, new_s = [], [], []

    for l in range(depth):
        w2 = jnp.pad(ffn_w2[l, 0].astype(BF16), ((0, fp - ffn_dim), (0, 0)))
        hid = _ffn1(h, ffn_w1, ffn_w3, l, 0)
        x, h = _mm_ln(hid, w2, x, mod, l, 2, 0.5, alpha, l, 3, ln_g[l, 0], ln_b[l, 0])

        wi = w_in[l]
        zmain = _proj(h, wi[:, :c_main].astype(BF16), BF16)
        zlr = _proj(h, _pad_cols(wi[:, c_lr:c_d].astype(BF16), LANES), F32)
        zd = _proj(h, wi[:, c_d:c_g].astype(BF16), F32)
        gates = _proj(h, wi[:, c_g:].astype(BF16), BF16, act="sigmoid")

        u3 = _short_conv(zmain, mp, hy_conv_w[l], hy_conv_b[l], seq, dec_seq, hy_w)
        fargs = (hy_w1[l], hy_b1[l], hy_freq[l], hy_w2[l], hy_b2[l], hy_w3[l], hy_decay[l], hy_w)
        taps_s = _hyena_filter_taps(dec_seq, *fargs)
        taps_p = _hyena_filter_taps(seq, *fargs)
        kf_s = _lconv_filter(taps_s.reshape(2, na, FFT_NB, hy_w), f1_full, tb)
        kf_p = _lmm(fp_full, taps_p, lambda b: (b,), 2, F32, tn=hy_w)

        cw3 = jnp.swapaxes(hy_conv_w[l].reshape(3, 3, hy_w), 0, 1)
        cb3 = hy_conv_b[l].reshape(3, 1, hy_w)
        z1s = _lconv(zmain, mp, 0, zmain, mp, 1, cw3[0:2], cb3[0:2], True, perm, permt, f1_half, g_s,
                     tb, tbi, kf_s, 0, hy_bias[l, 0], dec_batch, dec_seq)
        ya_s = _lconv(z1s, 0, 0, zmain, mp, 2, cw3[1:3], cb3[1:3], False, perm, permt, f1_half, g_s,
                      tb, tbi, kf_s, 1, hy_bias[l, 1], dec_batch, dec_seq)
        z1 = _pconv(fp_half, g_p, u3, (0,), kf_p, 0, u3, (1,), hy_bias[l, 0], batch, seq, F32)
        ya_p = _pconv(fp_half, g_p, z1, (), kf_p, 1, u3, (2,), hy_bias[l, 1], batch, seq, BF16)
        ya = jnp.concatenate([ya_p, ya_s], axis=0)

        wa_cat = jnp.zeros((LANES, 2 * gk), F32)
        wa_cat = wa_cat.at[:rank, :gk].set(gla_wa[l, 0]).at[rank:2 * rank, gk:].set(gla_wa[l, 1])
        ba_cat = jnp.concatenate([gla_ba[l, 0], gla_ba[l, 1]]).reshape(1, 2 * gk)
        la = _gla_gates(zlr, wa_cat, ba_cat)
        qcol, kcol, vcol, grcol = (3 * hy_w) // gk, (3 * hy_w) // gk + 1, (3 * hy_w + 2 * gk) // gw, \
            (3 * hy_w + 2 * gk) // gw + 1
        s0t = jnp.swapaxes(state_gla[:, l], -1, -2)
        of_p, ob_p, st_p = _gla(zmain, qcol, kcol, vcol, la, 0, batch, seq, seq, heads_g, dk_g, dv_g)
        of_s, ob_s, _ = _gla(zmain, qcol, kcol, vcol, la, mp, dec_batch, dec_seq, 512, heads_g, dk_g, dv_g,
                             s0t=s0t)
        o_f = jnp.concatenate([of_p, of_s], axis=0)
        o_b = jnp.concatenate([ob_p, ob_s], axis=0)
        yb = _gla_post(o_f, o_b, zmain, grcol, gla_norm_g[l], heads_g, dv_g)
        new_s.append(jnp.swapaxes(st_p, -1, -2))

        qs, kr, vb = _qkv_prep(zd, cos_t, sin_t, heads_d, dh)
        yc_p = _diff_attention(qs, kr, vb, 0, batch, seq, seq, heads_d, dh, diff_lam[l], diff_norm_g[l],
                               lam_inits[l])
        yc_s = _diff_attention(qs, kr, vb, mp, dec_batch, dec_seq, 2048, heads_d, dh, diff_lam[l],
                               diff_norm_g[l], lam_inits[l], ctx=(ck, cv, l))
        yc = jnp.concatenate([yc_p, yc_s], axis=0)
        new_k.append(zd[:mp, dw:2 * dw].reshape(batch, seq, heads_d, 2, dh))
        new_v.append(zd[:mp, 2 * dw:].reshape(batch, seq, heads_d, 2 * dh))

        wbr = jnp.stack([w_branch_a[l], w_branch_b[l], w_branch_c[l]]).astype(BF16)
        y = _mix(ya, yb, yc, gates, wbr)
        x, h = _mm_ln(y, w_out[l].astype(BF16), x, mod, l, 5, 1.0, alpha, l, 6, ln_g[l, 1], ln_b[l, 1],
                      tm=512)

        w2 = jnp.pad(ffn_w2[l, 1].astype(BF16), ((0, fp - ffn_dim), (0, 0)))
        hid = _ffn1(h, ffn_w1, ffn_w3, l, 1)
        if l + 1 < depth:
            x, h = _mm_ln(hid, w2, x, mod, l, 8, 0.5, alpha, l + 1, 0, ln_g[l, 2], ln_b[l, 2])
        else:
            xp, _ = _mm_ln(hid, w2, x, mod, l, 8, 0.5, alpha, l, 0, ln_g[l, 2], ln_b[l, 2], rows=mp)
            xs, _ = _mm_ln(hid, w2, x, mod, l, 8, 0.5, alpha, l, 0, ln_g[l, 2], ln_b[l, 2], row0=mp,
                           rows=m - mp)

    y_prompt = xp.reshape(batch, seq, d)
    y_sample = xs.reshape(dec_batch, dec_seq, d)
    return (y_prompt, y_sample, jnp.stack(new_k, axis=1), jnp.stack(new_v, axis=1),
            jnp.stack(new_s, axis=1))
```

```python
import functools
import math

import numpy as np
import jax
import jax.numpy as jnp
from jax import lax
from jax.experimental import pallas as pl
from jax.experimental.pallas import tpu as pltpu

F32 = jnp.float32
BF16 = jnp.bfloat16

GRID_W = 64
N_MOD = 9
HY_BANDS = 8
GLA_TAU = 16.0
GLA_CHUNK = 64
ROPE_THETA = 10000.0
LN_EPS = 1e-5
RMS_EPS = 1e-6

LANES = 128
MXU_DIM = 256
VMEM_BYTES_V7X = 64 * 1024 * 1024
VMEM_LIMIT = VMEM_BYTES_V7X - 8 * 1024 * 1024

GROUP = 4096
FFT_NB = 16
FFT_K1G = 8


def _cp(*sem):
    return pltpu.CompilerParams(dimension_semantics=sem, vmem_limit_bytes=VMEM_LIMIT)


def _dot(a, b):
    return jnp.dot(a, b, preferred_element_type=F32)


def _dot_nt(a, b):
    return lax.dot_general(a, b, (((1,), (1,)), ((), ())), preferred_element_type=F32)


def _dot_tn(a, b):
    return lax.dot_general(a, b, (((0,), (0,)), ((), ())), preferred_element_type=F32)


def _dot_hi(a, b):
    return jnp.dot(a, b, preferred_element_type=F32, precision=lax.Precision.HIGHEST)


def _silu(x):
    return x * jax.nn.sigmoid(x)


def _mod_kernel(c_ref, w_ref, b_ref, o_ref):
    c = c_ref[...]
    o_ref[...] = _dot(_silu(c).astype(BF16), w_ref[...].astype(BF16)) + b_ref[...]


def _modulation(cond, w_mod, b_mod):
    depth, d, n = w_mod.shape
    r = cond.shape[0]
    tn = 1024
    return pl.pallas_call(
        _mod_kernel,
        grid=(depth, n // tn),
        in_specs=[pl.BlockSpec((r, d), lambda l, j: (0, 0)),
                  pl.BlockSpec((None, d, tn), lambda l, j: (l, 0, j)),
                  pl.BlockSpec((None, 1, tn), lambda l, j: (l, 0, j))],
        out_specs=pl.BlockSpec((None, r, tn), lambda l, j: (l, 0, j)),
        out_shape=jax.ShapeDtypeStruct((depth, r, n), F32),
        compiler_params=_cp("parallel", "parallel"),
        name="mod",
    )(cond, w_mod, b_mod.reshape(depth, 1, n))


def _premod_kernel(x_ref, mod_ref, o_ref):
    o_ref[...] = (x_ref[...] * (1.0 + mod_ref[1:2, :]) + mod_ref[0:1, :]).astype(o_ref.dtype)


def _premod(x, mod, layer, tm=512):
    m, d = x.shape
    return pl.pallas_call(
        _premod_kernel,
        grid=(m // tm,),
        in_specs=[pl.BlockSpec((tm, d), lambda i: (i, 0)),
                  pl.BlockSpec((None, None, N_MOD, d), lambda i: (layer, (i * tm) // GROUP, 0, 0))],
        out_specs=pl.BlockSpec((tm, d), lambda i: (i, 0)),
        out_shape=jax.ShapeDtypeStruct((m, d), BF16),
        compiler_params=_cp("parallel"),
        name="premod",
    )(x, mod)


def _ffn1_kernel(h_ref, w1_ref, w3_ref, o_ref, w1b_ref, w3b_ref, *, tf, f_valid):
    @pl.when(pl.program_id(1) == 0)
    def _():
        col = pl.program_id(0) * tf + lax.broadcasted_iota(jnp.int32, w1_ref.shape, 1)
        keep = col < f_valid
        w1b_ref[...] = jnp.where(keep, w1_ref[...], 0.0).astype(BF16)
        w3b_ref[...] = jnp.where(keep, w3_ref[...], 0.0).astype(BF16)

    h = h_ref[...]
    a = _dot(h, w1b_ref[...])
    b = _dot(h, w3b_ref[...])
    o_ref[...] = (_silu(a) * b).astype(o_ref.dtype)


def _ffn1(h, w1, w3, layer, sub, tm=1024, tf=512):
    m, d = h.shape
    f = w1.shape[-1]
    nf = pl.cdiv(f, tf)
    wspec = pl.BlockSpec((None, None, d, tf), lambda j, i: (layer, sub, 0, j))
    return pl.pallas_call(
        functools.partial(_ffn1_kernel, tf=tf, f_valid=f),
        grid=(nf, m // tm),
        in_specs=[pl.BlockSpec((tm, d), lambda j, i: (i, 0)), wspec, wspec],
        out_specs=pl.BlockSpec((tm, tf), lambda j, i: (i, j)),
        out_shape=jax.ShapeDtypeStruct((m, nf * tf), BF16),
        scratch_shapes=[pltpu.VMEM((d, tf), BF16), pltpu.VMEM((d, tf), BF16)],
        compiler_params=_cp("parallel", "arbitrary"),
        name="ffn1",
    )(h, w1, w3)


def _mm_ln_kernel(a_ref, w_ref, x_ref, mod_ref, nmod_ref, g_ref, b_ref, xo_ref, ho_ref, r0_ref, r1_ref, *,
                  gate_row, coef, alpha, nshift_row):
    i = pl.program_id(0)

    @pl.when(i == 0)
    def _():
        r1_ref[...] = jnp.zeros_like(r1_ref)

    def step(r_new_ref, r_old_ref):
        r_new_ref[...] = _dot(a_ref[...], w_ref[...])
        gate = coef * mod_ref[gate_row:gate_row + 1, :]
        xr = alpha * x_ref[...] + gate * r_old_ref[...]
        mu = jnp.mean(xr, axis=-1, keepdims=True)
        xc = xr - mu
        var = jnp.mean(xc * xc, axis=-1, keepdims=True)
        xn = xc * lax.rsqrt(var + LN_EPS) * g_ref[...] + b_ref[...]
        xo_ref[...] = xn
        ho_ref[...] = (xn * (1.0 + nmod_ref[nshift_row + 1:nshift_row + 2, :])
                       + nmod_ref[nshift_row:nshift_row + 1, :]).astype(ho_ref.dtype)

    @pl.when(i % 2 == 0)
    def _():
        step(r0_ref, r1_ref)

    @pl.when(i % 2 == 1)
    def _():
        step(r1_ref, r0_ref)


def _mm_ln(a, w, x, mod, layer, gate_row, coef, alpha, nlayer, nshift_row, ln_g, ln_b, tm=256,
           row0=0, rows=None):
    kdim = a.shape[1]
    m = a.shape[0] if rows is None else rows
    d = w.shape[1]
    rb0 = row0 // tm
    nt = m // tm

    def cur(i):
        return jnp.minimum(i, nt - 1)

    def prv(i):
        return jnp.maximum(i - 1, 0)

    kern = functools.partial(_mm_ln_kernel, gate_row=gate_row, coef=coef, alpha=alpha, nshift_row=nshift_row)
    return pl.pallas_call(
        kern,
        grid=(nt + 1,),
        in_specs=[pl.BlockSpec((tm, kdim), lambda i: (rb0 + cur(i), 0)),
                  pl.BlockSpec((kdim, d), lambda i: (0, 0), pipeline_mode=pl.Buffered(1)),
                  pl.BlockSpec((tm, d), lambda i: (rb0 + prv(i), 0)),
                  pl.BlockSpec((None, None, N_MOD, d), lambda i: (layer, ((rb0 + prv(i)) * tm) // GROUP, 0, 0)),
                  pl.BlockSpec((None, None, N_MOD, d), lambda i: (nlayer, ((rb0 + prv(i)) * tm) // GROUP, 0, 0)),
                  pl.BlockSpec((1, d), lambda i: (0, 0)),
                  pl.BlockSpec((1, d), lambda i: (0, 0))],
        out_specs=[pl.BlockSpec((tm, d), lambda i: (prv(i), 0)),
                   pl.BlockSpec((tm, d), lambda i: (prv(i), 0))],
        out_shape=[jax.ShapeDtypeStruct((m, d), F32), jax.ShapeDtypeStruct((m, d), BF16)],
        scratch_shapes=[pltpu.VMEM((tm, d), F32), pltpu.VMEM((tm, d), F32)],
        compiler_params=_cp("arbitrary"),
        name="mm_ln",
    )(a, w, x, mod, mod, ln_g.reshape(1, d), ln_b.reshape(1, d))


def _proj_kernel(h_ref, w_ref, o_ref, *, act):
    r = _dot(h_ref[...], w_ref[...])
    if act == "sigmoid":
        r = jax.nn.sigmoid(r)
    o_ref[...] = r.astype(o_ref.dtype)


def _proj(h, w, out_dtype, act=None, tm=1024, tn=1024):
    m, d = h.shape
    n = w.shape[1]
    tn = min(tn, n)
    return pl.pallas_call(
        functools.partial(_proj_kernel, act=act),
        grid=(m // tm, n // tn),
        in_specs=[pl.BlockSpec((tm, d), lambda i, j: (i, 0)),
                  pl.BlockSpec((d, tn), lambda i, j: (0, j))],
        out_specs=pl.BlockSpec((tm, tn), lambda i, j: (i, j)),
        out_shape=jax.ShapeDtypeStruct((m, n), out_dtype),
        compiler_params=_cp("parallel", "arbitrary"),
        name="proj",
    )(h, w)


def _sconv_kernel(z_ref, zp_ref, zn_ref, w_ref, b_ref, o_ref, *, tm, halo, lp, ls):
    i = pl.program_id(0)
    u = z_ref[...].astype(F32)
    prev = zp_ref[...].astype(F32)[halo - 1:halo, :]
    nxt = zn_ref[...].astype(F32)[0:1, :]
    row = lax.broadcasted_iota(jnp.int32, u.shape, 0)
    lseq = jnp.where((i * tm) // GROUP == 0, lp, ls)
    pos = (row + i * tm) & (lseq - 1)
    up = jnp.where(row == 0, prev, pltpu.roll(u, 1, 0))
    up = jnp.where(pos == 0, 0.0, up)
    un = jnp.where(row == tm - 1, nxt, pltpu.roll(u, tm - 1, 0))
    un = jnp.where(pos == lseq - 1, 0.0, un)
    w = w_ref[...]
    o_ref[...] = (up * w[0:1, :] + u * w[1:2, :] + un * w[2:3, :] + b_ref[...]).astype(o_ref.dtype)


def _short_conv(z, rows, conv_w, conv_b, lp, ls, width, tm=1024, ct=512, halo=16):
    m = rows
    nct = width // ct
    nrb = m // halo
    kern = functools.partial(_sconv_kernel, tm=tm, halo=halo, lp=lp, ls=ls)
    return pl.pallas_call(
        kern,
        grid=(m // tm, 3 * nct),
        in_specs=[pl.BlockSpec((tm, ct), lambda i, j: (i, j)),
                  pl.BlockSpec((halo, ct), lambda i, j: (jnp.maximum(i * (tm // halo) - 1, 0), j)),
                  pl.BlockSpec((halo, ct), lambda i, j: (jnp.minimum((i + 1) * (tm // halo), nrb - 1), j)),
                  pl.BlockSpec((3, ct), lambda i, j: (0, j)),
                  pl.BlockSpec((1, ct), lambda i, j: (0, j))],
        out_specs=pl.BlockSpec((None, tm, ct), lambda i, j: (j // nct, i, j % nct)),
        out_shape=jax.ShapeDtypeStruct((3, m, width), F32),
        compiler_params=_cp("parallel", "parallel"),
        name="sconv",
    )(z, z, z, conv_w, conv_b.reshape(1, -1))


def _filter_features(L):
    t = np.linspace(0.0, 1.0, L, dtype=np.float32)
    w = (np.float32(2.0 * math.pi / L) * np.arange(L, dtype=np.float32)).astype(np.float32)
    f = np.linspace(1e-4, HY_BANDS - 1, HY_BANDS, dtype=np.float32)
    wf = (w[:, None] * f).astype(np.float32)
    feats = np.concatenate([t[:, None], np.cos(wf), -np.sin(wf)], -1).astype(np.float32)
    idx = np.concatenate([np.arange(L), [0], np.arange(L - 1, 0, -1)])
    tab = np.zeros((2 * L, 32), np.float32)
    tab[:, :feats.shape[1]] = feats[idx]
    tab[:, 24] = t[idx]
    tab[:L, 25] = 1.0
    tab[L + 1:, 26] = -1.0
    return tab, feats.shape[1]


def _filter_kernel(tab_ref, w1_ref, b1_ref, fr_ref, w2_ref, b2_ref, w3_ref, dec_ref, o_ref):
    tab = tab_ref[...]
    fr = fr_ref[...]
    hdn = jnp.sin(fr[0:1, :] * (_dot_hi(tab, w1_ref[...]) + b1_ref[...]))
    hdn = jnp.sin(fr[1:2, :] * (_dot_hi(hdn, w2_ref[...]) + b2_ref[...]))
    t = tab[:, 24:25]
    sign = tab[:, 25:26] + tab[:, 26:27]
    for o in range(2):
        o_ref[o] = sign * (_dot_hi(hdn, w3_ref[o]) * jnp.exp(-t * jnp.abs(dec_ref[o])))


def _hyena_filter_taps(L, hy_w1, hy_b1, hy_freq, hy_w2, hy_b2, hy_w3, hy_decay, width, rb=256):
    tab_np, nfeat = _filter_features(L)
    fh = hy_w1.shape[1]
    w1p = jnp.zeros((32, fh), F32).at[:nfeat].set(hy_w1)
    n = 2 * L
    nhalf = L // rb
    w3d = jnp.transpose(hy_w3.reshape(fh, 2, 2, width), (1, 2, 0, 3))
    decd = hy_decay.reshape(2, 2, 1, width)
    return pl.pallas_call(
        _filter_kernel,
        grid=(n // rb,),
        in_specs=[pl.BlockSpec((rb, 32), lambda i: (i, 0)),
                  pl.BlockSpec((32, fh), lambda i: (0, 0)),
                  pl.BlockSpec((1, fh), lambda i: (0, 0)),
                  pl.BlockSpec((2, fh), lambda i: (0, 0)),
                  pl.BlockSpec((fh, fh), lambda i: (0, 0)),
                  pl.BlockSpec((1, fh), lambda i: (0, 0)),
                  pl.BlockSpec((2, None, fh, width), lambda i: (0, i // nhalf, 0, 0)),
                  pl.BlockSpec((2, None, 1, width), lambda i: (0, i // nhalf, 0, 0))],
        out_specs=pl.BlockSpec((2, rb, width), lambda i: (0, i, 0)),
        out_shape=jax.ShapeDtypeStruct((2, n, width), F32),
        compiler_params=_cp("parallel"),
        name="hyfilter",
    )(jnp.asarray(tab_np), w1p, hy_b1.reshape(1, fh), hy_freq, hy_w2, hy_b2.reshape(1, fh), w3d, decd)


def _dft_a(na, ka):
    k1 = np.arange(na // 2)[:, None].astype(np.float64)
    a = np.arange(ka)[None, :].astype(np.float64)
    th = 2.0 * np.pi * a * (k1 + 0.5) / na
    return np.concatenate([np.cos(th), -np.sin(th)], 0)


def _idft_a(na, n, rows):
    k1 = np.arange(na // 2)[None, :].astype(np.float64)
    a = np.arange(rows)[:, None].astype(np.float64)
    th = 2.0 * np.pi * a * (k1 + 0.5) / na
    return (2.0 / n) * np.concatenate([np.cos(th), -np.sin(th)], 1)


def _dft_b(na, nb, g):
    n = na * nb
    half = na // 2
    k1 = np.arange(half).astype(np.float64)
    b = np.arange(nb).astype(np.float64)
    k2 = np.arange(nb).astype(np.float64)
    phi = 2.0 * np.pi * (b[None, None, :] * k2[None, :, None] / nb
                         + b[None, None, :] * (k1[:, None, None] + 0.5) / n)
    c, s = np.cos(phi), np.sin(phi)
    ng = half // g
    r = g * nb
    fwd = np.zeros((ng, 2 * r, 2 * r))
    for q in range(g):
        rows = slice(q * nb, (q + 1) * nb)
        rows_i = slice(r + q * nb, r + (q + 1) * nb)
        cols = slice(q, r, g)
        cols_i = slice(r + q, 2 * r, g)
        cq, sq = c[q::g], s[q::g]
        fwd[:, rows, cols] = cq
        fwd[:, rows, cols_i] = sq
        fwd[:, rows_i, cols] = -sq
        fwd[:, rows_i, cols_i] = cq
    inv = np.transpose(fwd, (0, 2, 1))
    return fwd, inv


def _lmm_kernel(f_ref, x_ref, o_ref):
    o_ref[...] = _dot(f_ref[...], x_ref[...].astype(BF16)).astype(o_ref.dtype)


def _lmm(f, x, x_index, nbatch, out_dtype, tn=2048):
    mo, k = f.shape
    n = x.shape[-1]
    lead = len(x.shape) - 2
    return pl.pallas_call(
        _lmm_kernel,
        grid=(nbatch, n // tn),
        in_specs=[pl.BlockSpec((mo, k), lambda b, j: (0, 0)),
                  pl.BlockSpec((None,) * lead + (k, tn), lambda b, j: x_index(b) + (0, j))],
        out_specs=pl.BlockSpec((None, mo, tn), lambda b, j: (b, 0, j)),
        out_shape=jax.ShapeDtypeStruct((nbatch, mo, n), out_dtype),
        compiler_params=_cp("parallel", "parallel"),
        name="dft_a",
    )(f, x)


def _stage_a_fwd(x_ref, f1_ref, r_ref, nb, ct):
    for b in range(nb):
        r_ref[:, b * ct:(b + 1) * ct] = _dot(f1_ref[...], x_ref[:, b, :].astype(BF16))


def _stage_b_rows(r_ref, r0, half, nb, kg, ct):
    return [(pl.ds(ri * half + r0, kg), slice(b * ct, (b + 1) * ct)) for ri in range(2) for b in range(nb)]


def _row_permutation(nb, al):
    p = np.zeros((nb * al, nb * al), np.float32)
    for b in range(nb):
        for a in range(al):
            p[b * al + a, a * nb + b] = 1.0
    return p


def _to_residue_major(src_ref, perm_ref, dst_ref, nb, al):
    blk = nb * al
    for i in range(src_ref.shape[0] // blk):
        z = _dot(perm_ref[...], src_ref[i * blk:(i + 1) * blk, :].astype(BF16))
        for b in range(nb):
            dst_ref[b, i * al:(i + 1) * al, :] = z[b * al:(b + 1) * al, :]


def _short_conv_residue_major(u_ref, w_ref, cb_ref, nb):
    rows = u_ref.shape[1]
    row = lax.broadcasted_iota(jnp.int32, u_ref.shape[1:], 0)
    w = w_ref[...]
    cb = cb_ref[...]
    first = u_ref[0]
    prev = jnp.where(row == 0, 0.0, pltpu.roll(u_ref[nb - 1], 1, 0))
    for b in range(nb):
        cur = first if b == 0 else u_ref[b]
        nxt = u_ref[b + 1] if b + 1 < nb else jnp.where(row == rows - 1, 0.0, pltpu.roll(first, rows - 1, 0))
        u_ref[b] = prev * w[0:1, :] + cur * w[1:2, :] + nxt * w[2:3, :] + cb
        prev = cur


def _lconv_kernel(zv_ref, zx_ref, cwv_ref, cbv_ref, cwx_ref, cbx_ref, perm_ref, permt_ref, f1_ref, g_ref,
                  tb_ref, tbi_ref, kf_ref, bias_ref, o_ref, v_ref, x_ref, r_ref, *, nb, half, kg, ct, conv_v):
    r = nb * kg
    al = perm_ref.shape[0] // nb
    _to_residue_major(zv_ref, perm_ref, v_ref, nb, al)
    _to_residue_major(zx_ref, perm_ref, x_ref, nb, al)
    if conv_v:
        _short_conv_residue_major(v_ref, cwv_ref, cbv_ref, nb)
    _short_conv_residue_major(x_ref, cwx_ref, cbx_ref, nb)
    for b in range(nb):
        r_ref[:, b * ct:(b + 1) * ct] = _dot(f1_ref[...], v_ref[b].astype(BF16))

    for g in range(half // kg):
        tiles = _stage_b_rows(r_ref, g * kg, half, nb, kg, ct)
        a = jnp.concatenate([r_ref[rs, cs] for rs, cs in tiles], axis=0).astype(BF16)
        x = _dot(tb_ref[g], a)
        kf = kf_ref[g * 2 * r:(g + 1) * 2 * r, :]
        xr, xi = x[:r], x[r:]
        kr, ki = kf[:r], kf[r:]
        y = jnp.concatenate([xr * kr - xi * ki, xr * ki + xi * kr], axis=0).astype(BF16)
        bh = _dot(tbi_ref[g], y)
        for t, (rs, cs) in enumerate(tiles):
            r_ref[rs, cs] = bh[t * kg:(t + 1) * kg, :]
    bias = bias_ref[...]
    for b in range(nb):
        y = _dot(g_ref[...], r_ref[:, b * ct:(b + 1) * ct].astype(BF16))
        v_ref[b] = x_ref[b] * (y + v_ref[b] * bias)
    blk = nb * al
    for i in range(o_ref.shape[0] // blk):
        t = jnp.concatenate([v_ref[b, i * al:(i + 1) * al, :] for b in range(nb)], axis=0).astype(BF16)
        o_ref[i * blk:(i + 1) * blk, :] = _dot(permt_ref[...], t).astype(o_ref.dtype)


def _lconv(zv, v_row0, v_col0, zx, x_row0, x_col0, conv_w, conv_b, conv_v, perm, permt, f1, g_inv, tb, tbi,
           kf, order, bias, nbatch, L, ct=256):
    c = kf.shape[-1]
    n2, _ = f1.shape
    ng, r2, _ = tb.shape
    nb = L // (n2 // 2)
    kg = r2 // (2 * nb)
    ncb = c // ct
    once = dict(pipeline_mode=pl.Buffered(1))
    kern = functools.partial(_lconv_kernel, nb=nb, half=n2 // 2, kg=kg, ct=ct, conv_v=conv_v)
    vb0, xb0 = v_row0 // L, x_row0 // L
    return pl.pallas_call(
        kern,
        grid=(ncb, nbatch),
        in_specs=[pl.BlockSpec((L, ct), lambda j, b: (vb0 + b, v_col0 * ncb + j)),
                  pl.BlockSpec((L, ct), lambda j, b: (xb0 + b, x_col0 * ncb + j)),
                  pl.BlockSpec((None, 3, ct), lambda j, b: (0, 0, j)),
                  pl.BlockSpec((None, 1, ct), lambda j, b: (0, 0, j)),
                  pl.BlockSpec((None, 3, ct), lambda j, b: (1, 0, j)),
                  pl.BlockSpec((None, 1, ct), lambda j, b: (1, 0, j)),
                  pl.BlockSpec(perm.shape, lambda j, b: (0, 0), **once),
                  pl.BlockSpec(permt.shape, lambda j, b: (0, 0), **once),
                  pl.BlockSpec(f1.shape, lambda j, b: (0, 0), **once),
                  pl.BlockSpec(g_inv.shape, lambda j, b: (0, 0), **once),
                  pl.BlockSpec(tb.shape, lambda j, b: (0, 0, 0), **once),
                  pl.BlockSpec(tbi.shape, lambda j, b: (0, 0, 0), **once),
                  pl.BlockSpec((None, ng * r2, ct), lambda j, b: (order, 0, j), **once),
                  pl.BlockSpec((1, ct), lambda j, b: (0, j))],
        out_specs=pl.BlockSpec((L, ct), lambda j, b: (b, j)),
        out_shape=jax.ShapeDtypeStruct((nbatch * L, c), BF16),
        scratch_shapes=[pltpu.VMEM((nb, L // nb, ct), F32), pltpu.VMEM((nb, L // nb, ct), F32),
                        pltpu.VMEM((n2, nb * ct), F32)],
        compiler_params=_cp("parallel", "arbitrary"),
        name="lconv",
    )(zv, zx, conv_w, conv_b, conv_w, conv_b, perm, permt, f1, g_inv, tb, tbi, kf, bias.reshape(1, c))


def _lconv_filter_kernel(x_ref, f1_ref, tb_ref, o_ref, r_ref, *, nb, half, kg, ct):
    r2 = 2 * nb * kg
    _stage_a_fwd(x_ref, f1_ref, r_ref, nb, ct)

    for g in range(half // kg):
        tiles = _stage_b_rows(r_ref, g * kg, half, nb, kg, ct)
        a = jnp.concatenate([r_ref[rs, cs] for rs, cs in tiles], axis=0).astype(BF16)
        o_ref[g * r2:(g + 1) * r2, :] = _dot(tb_ref[g], a)


def _lconv_filter(taps, f1, tb, ct=256):
    norder, na, nb, c = taps.shape
    n2, _ = f1.shape
    ng, r2, _ = tb.shape
    kg = r2 // (2 * nb)
    kern = functools.partial(_lconv_filter_kernel, nb=nb, half=n2 // 2, kg=kg, ct=ct)
    return pl.pallas_call(
        kern,
        grid=(norder, c // ct),
        in_specs=[pl.BlockSpec((None, na, nb, ct), lambda o, j: (o, 0, 0, j)),
                  pl.BlockSpec(f1.shape, lambda o, j: (0, 0)),
                  pl.BlockSpec(tb.shape, lambda o, j: (0, 0, 0))],
        out_specs=pl.BlockSpec((None, ng * r2, ct), lambda o, j: (o, 0, j)),
        out_shape=jax.ShapeDtypeStruct((norder, ng * r2, c), F32),
        scratch_shapes=[pltpu.VMEM((n2, nb * ct), F32)],
        compiler_params=_cp("parallel", "parallel"),
        name="lconv_filter",
    )(taps, f1, tb)


def _pconv_kernel(fd_ref, gd_ref, v_ref, kf_ref, xg_ref, bias_ref, o_ref, *, half):
    v = v_ref[...]
    x = _dot(fd_ref[...], v.astype(BF16))
    kf = kf_ref[...]
    xr, xi = x[:half], x[half:]
    kr, ki = kf[:half], kf[half:]
    y = jnp.concatenate([xr * kr - xi * ki, xr * ki + xi * kr], axis=0).astype(BF16)
    yt = _dot(gd_ref[...], y)
    o_ref[...] = (xg_ref[...].astype(F32) * (yt + v.astype(F32) * bias_ref[...])).astype(o_ref.dtype)


def _pconv(fd, gd, vsrc, v_lead, kf, order, xsrc, x_lead, bias, nseq, L, out_dtype, ct=512):
    c = kf.shape[-1]
    n2 = fd.shape[0]
    return pl.pallas_call(
        functools.partial(_pconv_kernel, half=n2 // 2),
        grid=(nseq, c // ct),
        in_specs=[pl.BlockSpec((n2, L), lambda s, j: (0, 0)),
                  pl.BlockSpec((L, n2), lambda s, j: (0, 0)),
                  pl.BlockSpec((None,) * len(v_lead) + (L, ct), lambda s, j: v_lead + (s, j)),
                  pl.BlockSpec((None, n2, ct), lambda s, j: (order, 0, j)),
                  pl.BlockSpec((None,) * len(x_lead) + (L, ct), lambda s, j: x_lead + (s, j)),
                  pl.BlockSpec((1, ct), lambda s, j: (0, j))],
        out_specs=pl.BlockSpec((L, ct), lambda s, j: (s, j)),
        out_shape=jax.ShapeDtypeStruct((nseq * L, c), out_dtype),
        compiler_params=_cp("parallel", "parallel"),
        name="pconv",
    )(fd, gd, vsrc, kf, xsrc, bias.reshape(1, c))


def _gate_kernel(lr_ref, wa_ref, ba_ref, o_ref):
    logits = _dot_hi(lr_ref[...], wa_ref[...]) + ba_ref[...]
    o_ref[...] = jax.nn.log_sigmoid(logits) * (1.0 / GLA_TAU)


def _gla_gates(lr, col, wa_cat, ba_cat, tm=1024):
    m = lr.shape[0]
    k, n = wa_cat.shape
    return pl.pallas_call(
        _gate_kernel,
        grid=(m // tm,),
        in_specs=[pl.BlockSpec((tm, k), lambda i: (i, col)),
                  pl.BlockSpec((k, n), lambda i: (0, 0)),
                  pl.BlockSpec((1, n), lambda i: (0, 0))],
        out_specs=pl.BlockSpec((tm, n), lambda i: (i, 0)),
        out_shape=jax.ShapeDtypeStruct((m, n), F32),
        compiler_params=_cp("parallel"),
        name="gla_gates",
    )(lr, wa_cat, ba_cat)


def _split3(x):
    hi = x.astype(BF16)
    r1 = x - hi.astype(F32)
    mid = r1.astype(BF16)
    lo = (r1 - mid.astype(F32)).astype(BF16)
    return hi, mid, lo


def _gla_dir(d, r0, q_ref, k_ref, v_ref, la_ref, o_ref, st_ref, tri, causal, ref_row, last_row,
             heads, dk, dv, scale):
    ch = GLA_CHUNK
    rows = pl.ds(pl.multiple_of(r0, ch), ch)
    la = la_ref[rows, :]
    hi, mid, lo = _split3(la)
    b = _dot(tri, hi) + _dot(tri, mid) + _dot(tri, lo)
    bref = b[ref_row:ref_row + 1, :]
    blast = b[last_row:last_row + 1, :]
    q = q_ref[rows, :].astype(F32) * scale
    k = k_ref[rows, :].astype(F32)
    qt = (q * jnp.exp(b - bref)).astype(BF16)
    kt = (k * jnp.exp(bref - b)).astype(BF16)
    qin = (q * jnp.exp(b)).astype(BF16)
    kst = (k * jnp.exp(blast - b)).astype(BF16)
    dec = jnp.exp(blast)
    for h in range(heads):
        ks = slice(h * dk, (h + 1) * dk)
        vs = slice(h * dv, (h + 1) * dv)
        att = _dot_nt(qt[:, ks], kt[:, ks])
        att = jnp.where(causal, att, 0.0).astype(BF16)
        vh = v_ref[rows, vs]
        st = st_ref[d, h]
        o_ref[rows, vs] = _dot(att, vh) + _dot_nt(qin[:, ks], st.astype(BF16))
        st_ref[d, h] = st * dec[:, ks] + _dot_tn(vh, kst[:, ks])


def _gla_kernel(*refs, nch, heads, dk, dv, has_s0, scale, nsq):
    seq_in = [refs[8 * q:8 * (q + 1)] for q in range(nsq)]
    rest = refs[8 * nsq:]
    if has_s0:
        s0, of, ob, st = rest
    else:
        of, ob, st = rest
        s0 = None
    ch = GLA_CHUNK

    @pl.when(pl.program_id(1) == 0)
    def _():
        if has_s0:
            st[...] = s0[...]
        else:
            st[...] = jnp.zeros_like(st)

    r_i = lax.broadcasted_iota(jnp.int32, (ch, ch), 0)
    c_i = lax.broadcasted_iota(jnp.int32, (ch, ch), 1)
    lower = r_i >= c_i
    upper = r_i <= c_i
    tri_l = jnp.where(lower, 1.0, 0.0).astype(BF16)
    tri_u = jnp.where(upper, 1.0, 0.0).astype(BF16)

    def body(c, carry):
        for q, (qf, kf, vf, laf, qb, kb, vb, lab) in enumerate(seq_in):
            _gla_dir(0, c * ch, qf, kf, vf, laf, of.at[q], st.at[q], tri_l, lower, ch // 2 - 1, ch - 1,
                     heads, dk, dv, scale)
            _gla_dir(1, (nch - 1 - c) * ch, qb, kb, vb, lab, ob.at[q], st.at[q], tri_u, upper, ch // 2, 0,
                     heads, dk, dv, scale)
        return carry

    lax.fori_loop(0, nch, body, 0)


def _gla(zmain, qcol, kcol, vcol, la, row0, nseq, L, tb, heads, dk, dv, s0t=None, nsq=2):
    nblk = L // tb
    rb0 = row0 // tb
    hk, hv = heads * dk, heads * dv
    in_specs, args = [], []
    for q in range(nsq):
        def fwd(s, j, q=q):
            return rb0 + (s * nsq + q) * nblk + j

        def bwd(s, j, q=q):
            return rb0 + (s * nsq + q) * nblk + (nblk - 1 - j)

        for rowf, lcol in ((fwd, 0), (bwd, 1)):
            in_specs += [pl.BlockSpec((tb, hk), lambda s, j, rowf=rowf: (rowf(s, j), qcol)),
                         pl.BlockSpec((tb, hk), lambda s, j, rowf=rowf: (rowf(s, j), kcol)),
                         pl.BlockSpec((tb, hv), lambda s, j, rowf=rowf: (rowf(s, j), vcol)),
                         pl.BlockSpec((tb, hk), lambda s, j, rowf=rowf, lcol=lcol: (rowf(s, j), lcol))]
            args += [zmain, zmain, zmain, la]
    st_spec = pl.BlockSpec((nsq, 2, heads, dv, dk), lambda s, j: (s, 0, 0, 0, 0))
    if s0t is not None:
        in_specs.append(st_spec)
        args.append(s0t)
    kern = functools.partial(_gla_kernel, nch=tb // GLA_CHUNK, heads=heads, dk=dk, dv=dv,
                             has_s0=s0t is not None, scale=dk ** -0.5, nsq=nsq)
    o_shape = jax.ShapeDtypeStruct((nseq // nsq, nsq, L, hv), F32)
    o_f, o_b, st = pl.pallas_call(
        kern,
        grid=(nseq // nsq, nblk),
        in_specs=in_specs,
        out_specs=[pl.BlockSpec((None, nsq, tb, hv), lambda s, j: (s, 0, j, 0)),
                   pl.BlockSpec((None, nsq, tb, hv), lambda s, j: (s, 0, nblk - 1 - j, 0)),
                   st_spec],
        out_shape=[o_shape, o_shape, jax.ShapeDtypeStruct((nseq, 2, heads, dv, dk), F32)],
        compiler_params=_cp("parallel", "arbitrary"),
        name="gla",
    )(*args)
    return o_f.reshape(nseq * L, hv), o_b.reshape(nseq * L, hv), st


def _gla_post_kernel(of_ref, ob_ref, gr_ref, g_ref, o_ref, *, heads, dv):
    o = of_ref[...] + ob_ref[...]
    gate = _silu(gr_ref[...].astype(F32))
    g = g_ref[...]
    for h in range(heads):
        sl = slice(h * dv, (h + 1) * dv)
        oh = o[:, sl]
        ms = jnp.mean(oh * oh, axis=-1, keepdims=True)
        o_ref[:, sl] = (oh * lax.rsqrt(ms + RMS_EPS) * g * gate[:, sl]).astype(o_ref.dtype)


def _gla_post(o_f, o_b, zmain, grcol, norm_g, heads, dv, tm=512):
    m, hv = o_f.shape
    return pl.pallas_call(
        functools.partial(_gla_post_kernel, heads=heads, dv=dv),
        grid=(m // tm,),
        in_specs=[pl.BlockSpec((tm, hv), lambda i: (i, 0)),
                  pl.BlockSpec((tm, hv), lambda i: (i, 0)),
                  pl.BlockSpec((tm, hv), lambda i: (i, grcol)),
                  pl.BlockSpec((1, dv), lambda i: (0, 0))],
        out_specs=pl.BlockSpec((tm, hv), lambda i: (i, 0)),
        out_shape=jax.ShapeDtypeStruct((m, hv), BF16),
        compiler_params=_cp("parallel"),
        name="gla_post",
    )(o_f, o_b, zmain, norm_g.reshape(1, dv))


def _rope_tables(L, dh):
    rows = L // GRID_W
    r = np.repeat(np.arange(rows, dtype=np.float32), GRID_W)
    col = np.tile(np.arange(GRID_W, dtype=np.float32), rows)
    nf = dh // 4
    inv = (np.float32(ROPE_THETA) ** (-np.arange(nf, dtype=np.float32) / nf)).astype(np.float32)
    ang_r = (r[:, None] * inv).astype(np.float32)
    ang_c = (col[:, None] * inv).astype(np.float32)
    cos = np.concatenate([np.cos(ang_r), np.cos(ang_r), np.cos(ang_c), np.cos(ang_c)], -1)
    sin = np.concatenate([-np.sin(ang_r), np.sin(ang_r), -np.sin(ang_c), np.sin(ang_c)], -1)
    cos = np.concatenate([cos, cos], -1).astype(np.float32)
    sin = np.concatenate([sin, sin], -1).astype(np.float32)
    cos_t = np.stack([np.ones_like(cos), cos])
    sin_t = np.stack([np.zeros_like(sin), sin])
    return cos_t, sin_t


def _qkv_prep_kernel(q_ref, k_ref, v_ref, cos_ref, sin_ref, qo_ref, ko_ref, vo_ref, *, heads, hw, nf, scale):
    cos = cos_ref[...]
    sin = sin_ref[...]
    lane = lax.broadcasted_iota(jnp.int32, cos.shape, 1)
    first = (lane & (2 * nf - 1)) < nf

    def rope(x):
        sw = jnp.where(first, pltpu.roll(x, hw - nf, 1), pltpu.roll(x, nf, 1))
        return x * cos + sw * sin

    ones = jnp.ones((q_ref.shape[0], hw), vo_ref.dtype)
    for h in range(heads):
        sl = slice(h * hw, (h + 1) * hw)
        qo_ref[:, sl] = (rope(q_ref[:, sl]) * scale).astype(qo_ref.dtype)
        ko_ref[sl, :] = rope(k_ref[:, sl]).T.astype(ko_ref.dtype)
        vo_ref[:, 2 * h * hw:(2 * h + 1) * hw] = v_ref[:, sl].astype(vo_ref.dtype)
        vo_ref[:, (2 * h + 1) * hw:(2 * h + 2) * hw] = ones


def _qkv_prep(zd, cos_t, sin_t, heads, dh, tm=512):
    m = zd.shape[0]
    hw = 2 * dh
    w = heads * hw
    gb = GROUP // tm
    kern = functools.partial(_qkv_prep_kernel, heads=heads, hw=hw, nf=dh // 4,
                             scale=dh ** -0.5 * math.log2(math.e))
    tab = pl.BlockSpec((None, tm, hw), lambda i: (jnp.minimum(i // gb, 1), i % gb, 0))
    out = jax.ShapeDtypeStruct((m, w), BF16)
    return pl.pallas_call(
        kern,
        grid=(m // tm,),
        in_specs=[pl.BlockSpec((tm, w), lambda i: (i, 0)),
                  pl.BlockSpec((tm, w), lambda i: (i, 1)),
                  pl.BlockSpec((tm, w), lambda i: (i, 2)),
                  tab, tab],
        out_specs=[pl.BlockSpec((tm, w), lambda i: (i, 0)),
                   pl.BlockSpec((w, tm), lambda i: (0, i)),
                   pl.BlockSpec((tm, 2 * w), lambda i: (i, 0))],
        out_shape=[out, jax.ShapeDtypeStruct((w, m), BF16), jax.ShapeDtypeStruct((m, 2 * w), BF16)],
        compiler_params=_cp("parallel"),
        name="qkv_prep",
    )(zd, zd, zd, cos_t, sin_t)


def _attn_kernel(*refs, has_ctx, lam_init, dh, bf16_exp, tq):
    if has_ctx:
        q_ref, k_ref, v_ref, kc_ref, vc_ref, lam_ref, g_ref, o_ref, s0_ref, s1_ref = refs
    else:
        q_ref, k_ref, v_ref, lam_ref, g_ref, o_ref, s0_ref, s1_ref = refs
    hw = 2 * dh
    lk = k_ref.shape[1]
    nsub = q_ref.shape[0] // tq
    lane = lax.broadcasted_iota(jnp.int32, (tq, hw), 1)
    first = lane < dh
    if has_ctx:
        kct = kc_ref[...].T.astype(BF16)
        vcv = vc_ref[...].astype(BF16)
        vce = jnp.concatenate([vcv, jnp.ones_like(vcv)], axis=1)

    def scores(i, j, s_ref):
        q = q_ref[pl.ds(i * tq if isinstance(i, int) else pl.multiple_of(i * tq, tq), tq), :]
        qj = jnp.where(first, q, jnp.zeros_like(q)) if j == 0 else jnp.where(first, jnp.zeros_like(q), q)
        s_ref[:, :lk] = _dot(qj, k_ref[...])
        if has_ctx:
            s_ref[:, lk:] = _dot(qj, kct)

    def prob(t):
        if bf16_exp:
            return jnp.exp2(t.astype(BF16))
        return jnp.exp2(t).astype(BF16)

    def softmax_pv(s_ref):
        s = s_ref[...]
        p = prob(s - jnp.max(s, axis=-1, keepdims=True))
        acc = _dot(p[:, :lk], v_ref[...])
        if has_ctx:
            acc = acc + _dot(p[:, lk:], vce)
        return acc[:, :hw] / acc[:, hw:]

    lp = lam_ref[...]
    lam = (jnp.exp(jnp.sum(lp[0:1] * lp[1:2], axis=-1, keepdims=True))
           - jnp.exp(jnp.sum(lp[2:3] * lp[3:4], axis=-1, keepdims=True)) + lam_init)
    gain = g_ref[...] * (1.0 - lam_init)

    scores(0, 0, s0_ref)

    def body(i, carry):
        scores(i, 1, s1_ref)
        sm0 = softmax_pv(s0_ref)
        scores(jnp.minimum(i + 1, nsub - 1), 0, s0_ref)
        o = sm0 - lam * softmax_pv(s1_ref)
        ms = jnp.mean(o * o, axis=-1, keepdims=True)
        o_ref[pl.ds(pl.multiple_of(i * tq, tq), tq), :] = (o * lax.rsqrt(ms + RMS_EPS) * gain).astype(o_ref.dtype)
        return carry

    lax.fori_loop(0, nsub, body, 0)


def _diff_attention(qs, kr, vb, row0, nseq, L, tb, heads, dh, lam_p, norm_g, lam_init, ctx=None, tq=256):
    hw = 2 * dh
    nqb = L // tb
    qb0 = row0 // tb
    kb0 = row0 // L
    past = 0 if ctx is None else ctx[0].shape[2]
    in_specs = [pl.BlockSpec((tb, hw), lambda s, h, i: (qb0 + s * nqb + i, h)),
                pl.BlockSpec((hw, L), lambda s, h, i: (h, kb0 + s)),
                pl.BlockSpec((L, 2 * hw), lambda s, h, i: (kb0 + s, h))]
    args = [qs, kr, vb]
    if ctx is not None:
        ck, cv, layer = ctx
        in_specs += [pl.BlockSpec((None, None, past, hw), lambda s, h, i: (s, layer, 0, h)),
                     pl.BlockSpec((None, None, past, hw), lambda s, h, i: (s, layer, 0, h))]
        args += [ck, cv]
    in_specs += [pl.BlockSpec(lam_p.shape, lambda s, h, i: (0, 0)),
                 pl.BlockSpec((1, hw), lambda s, h, i: (0, 0))]
    args += [lam_p, norm_g.reshape(1, hw)]
    kern = functools.partial(_attn_kernel, has_ctx=ctx is not None, lam_init=lam_init, dh=dh,
                             bf16_exp=L > 1024, tq=tq)
    return pl.pallas_call(
        kern,
        grid=(nseq, heads, nqb),
        in_specs=in_specs,
        out_specs=pl.BlockSpec((tb, hw), lambda s, h, i: (s * nqb + i, h)),
        out_shape=jax.ShapeDtypeStruct((nseq * L, heads * hw), BF16),
        scratch_shapes=[pltpu.VMEM((tq, L + past), F32), pltpu.VMEM((tq, L + past), F32)],
        compiler_params=_cp("parallel", "parallel", "arbitrary"),
        name="diff_attn",
    )(*args)


def _mix_kernel(ya_ref, yb_ref, yc_ref, g_ref, w_ref, o_ref, acc_ref):
    k = pl.program_id(1)

    def contrib(y_ref):
        return g_ref[...].astype(F32) * _dot(y_ref[...].astype(BF16), w_ref[...])

    @pl.when(k == 0)
    def _():
        acc_ref[...] = contrib(ya_ref)

    @pl.when(k == 1)
    def _():
        acc_ref[...] += contrib(yb_ref)

    @pl.when(k == 2)
    def _():
        o_ref[...] = (acc_ref[...] + contrib(yc_ref)).astype(o_ref.dtype)


def _mix(ya, yb, yc, gates, wbr, tm=512):
    m, w = ya.shape
    d = wbr.shape[2]
    return pl.pallas_call(
        _mix_kernel,
        grid=(m // tm, 3),
        in_specs=[pl.BlockSpec((tm, w), lambda i, k: (i, 0)),
                  pl.BlockSpec((tm, w), lambda i, k: (i, 0)),
                  pl.BlockSpec((tm, w), lambda i, k: (i, 0)),
                  pl.BlockSpec((tm, d), lambda i, k: (i, k)),
                  pl.BlockSpec((None, w, d), lambda i, k: (k, 0, 0))],
        out_specs=pl.BlockSpec((tm, d), lambda i, k: (i, 0)),
        out_shape=jax.ShapeDtypeStruct((m, d), BF16),
        scratch_shapes=[pltpu.VMEM((tm, d), F32)],
        compiler_params=_cp("parallel", "arbitrary"),
        name="mix",
    )(ya, yb, yc, gates, wbr)


def _pad_cols(w, n):
    return jnp.pad(w, ((0, 0), (0, n - w.shape[1])))


def kernel(x_prompt, x_sample, cache_k, cache_v, state_gla, c, c_ctx, w_mod, b_mod, ln_g, ln_b, ffn_w1, ffn_w3, ffn_w2, w_in, hy_conv_w, hy_conv_b, hy_w1, hy_b1, hy_freq, hy_w2, hy_b2, hy_w3, hy_decay, hy_bias, gla_wa, gla_ba, gla_norm_g, diff_lam, diff_norm_g, w_branch_a, w_branch_b, w_branch_c, w_out):
    batch, seq, d = x_prompt.shape
    dec_batch, dec_seq, _ = x_sample.shape
    depth = w_mod.shape[0]
    ffn_dim = ffn_w1.shape[3]
    hy_w = hy_bias.shape[2]
    heads_g, dk_g = 4, gla_wa.shape[3] // 4
    dv_g = gla_norm_g.shape[1]
    rank = gla_wa.shape[2]
    dh = diff_lam.shape[2]
    heads_d = cache_k.shape[3]
    dw = heads_d * 2 * dh
    gw = heads_g * dv_g
    gk = heads_g * dk_g
    assert batch * seq == GROUP and dec_seq == GROUP
    mp = batch * seq
    m = mp + dec_batch * dec_seq
    ngroups = 1 + dec_batch
    alpha = (2 * depth) ** 0.25
    lam_inits = [0.8 - 0.6 * math.exp(-0.3 * l) for l in range(depth)]

    c_main = 3 * hy_w + 2 * gk + 2 * gw
    c_lr = c_main
    c_d = c_lr + 2 * rank
    c_g = c_d + 3 * dw

    cond = jnp.concatenate([c_ctx[None], c, jnp.zeros((16 - ngroups, d), F32)], axis=0)
    mod = _modulation(cond, w_mod, b_mod)[:, :ngroups].reshape(depth, ngroups, N_MOD, d)

    na = 2 * dec_seq // FFT_NB
    n_s = 2 * dec_seq
    f1_half = jnp.asarray(_dft_a(na, na // 2), BF16)
    f1_full = jnp.asarray(_dft_a(na, na), BF16)
    g_s = jnp.asarray(_idft_a(na, n_s, na // 2), BF16)
    tb_np, tbi_np = _dft_b(na, FFT_NB, FFT_K1G)
    tb, tbi = jnp.asarray(tb_np, BF16), jnp.asarray(tbi_np, BF16)
    perm_np = _row_permutation(FFT_NB, FFT_K1G)
    perm, permt = jnp.asarray(perm_np, BF16), jnp.asarray(perm_np.T, BF16)
    fp_half = jnp.asarray(_dft_a(2 * seq, seq), BF16)
    fp_full = jnp.asarray(_dft_a(2 * seq, 2 * seq), BF16)
    g_p = jnp.asarray(_idft_a(2 * seq, 2 * seq, seq), BF16)
    cos_t, sin_t = _rope_tables(dec_seq, dh)
    cos_t, sin_t = jnp.asarray(cos_t), jnp.asarray(sin_t)
    half_rows = (na // 2) * FFT_NB
    lanes_s = FFT_NB * hy_w

    x = jnp.concatenate([x_prompt.reshape(mp, d), x_sample.reshape(dec_batch * dec_seq, d)], axis=0)
    h = _premod(x, mod, 0)

    fp = ((ffn_dim + 511) // 512) * 512
    ck = cache_k.reshape(dec_batch, depth, cache_k.shape[2], dw)
    cv = cache_v.reshape(dec_batch, depth, cache_v.shape[2], dw)
    new_k, new_v, new_s = [], [], []

    for l in range(depth):
        w2 = jnp.pad(ffn_w2[l, 0].astype(BF16), ((0, fp - ffn_dim), (0, 0)))
        hid = _ffn1(h, ffn_w1, ffn_w3, l, 0)
        x, h = _mm_ln(hid, w2, x, mod, l, 2, 0.5, alpha, l, 3, ln_g[l, 0], ln_b[l, 0])

        wi = w_in[l]
        zmain = _proj(h, wi[:, :c_main].astype(BF16), BF16)
        w_d = jnp.concatenate([wi[:, c_d:c_g], _pad_cols(wi[:, c_lr:c_d], LANES)], axis=1).astype(BF16)
        zd = _proj(h, w_d, F32, tn=5 * LANES)
        gates = _proj(h, wi[:, c_g:].astype(BF16), BF16, act="sigmoid")

        u3 = _short_conv(zmain, mp, hy_conv_w[l], hy_conv_b[l], seq, dec_seq, hy_w)
        fargs = (hy_w1[l], hy_b1[l], hy_freq[l], hy_w2[l], hy_b2[l], hy_w3[l], hy_decay[l], hy_w)
        taps_s = _hyena_filter_taps(dec_seq, *fargs)
        taps_p = _hyena_filter_taps(seq, *fargs)
        kf_s = _lconv_filter(taps_s.reshape(2, na, FFT_NB, hy_w), f1_full, tb)
        kf_p = _lmm(fp_full, taps_p, lambda b: (b,), 2, F32, tn=hy_w)

        cw3 = jnp.swapaxes(hy_conv_w[l].reshape(3, 3, hy_w), 0, 1)
        cb3 = hy_conv_b[l].reshape(3, 1, hy_w)
        z1s = _lconv(zmain, mp, 0, zmain, mp, 1, cw3[0:2], cb3[0:2], True, perm, permt, f1_half, g_s,
                     tb, tbi, kf_s, 0, hy_bias[l, 0], dec_batch, dec_seq)
        ya_s = _lconv(z1s, 0, 0, zmain, mp, 2, cw3[1:3], cb3[1:3], False, perm, permt, f1_half, g_s,
                      tb, tbi, kf_s, 1, hy_bias[l, 1], dec_batch, dec_seq)
        z1 = _pconv(fp_half, g_p, u3, (0,), kf_p, 0, u3, (1,), hy_bias[l, 0], batch, seq, F32)
        ya_p = _pconv(fp_half, g_p, z1, (), kf_p, 1, u3, (2,), hy_bias[l, 1], batch, seq, BF16)
        ya = jnp.concatenate([ya_p, ya_s], axis=0)

        wa_cat = jnp.zeros((LANES, 2 * gk), F32)
        wa_cat = wa_cat.at[:rank, :gk].set(gla_wa[l, 0]).at[rank:2 * rank, gk:].set(gla_wa[l, 1])
        ba_cat = jnp.concatenate([gla_ba[l, 0], gla_ba[l, 1]]).reshape(1, 2 * gk)
        la = _gla_gates(zd, 3 * dw // LANES, wa_cat, ba_cat)
        qcol, kcol, vcol, grcol = (3 * hy_w) // gk, (3 * hy_w) // gk + 1, (3 * hy_w + 2 * gk) // gw, \
            (3 * hy_w + 2 * gk) // gw + 1
        s0t = jnp.swapaxes(state_gla[:, l], -1, -2)
        of_p, ob_p, st_p = _gla(zmain, qcol, kcol, vcol, la, 0, batch, seq, seq, heads_g, dk_g, dv_g)
        of_s, ob_s, _ = _gla(zmain, qcol, kcol, vcol, la, mp, dec_batch, dec_seq, 512, heads_g, dk_g, dv_g,
                             s0t=s0t)
        o_f = jnp.concatenate([of_p, of_s], axis=0)
        o_b = jnp.concatenate([ob_p, ob_s], axis=0)
        yb = _gla_post(o_f, o_b, zmain, grcol, gla_norm_g[l], heads_g, dv_g)
        new_s.append(jnp.swapaxes(st_p, -1, -2))

        qs, kr, vb = _qkv_prep(zd, cos_t, sin_t, heads_d, dh)
        yc_p = _diff_attention(qs, kr, vb, 0, batch, seq, seq, heads_d, dh, diff_lam[l], diff_norm_g[l],
                               lam_inits[l])
        yc_s = _diff_attention(qs, kr, vb, mp, dec_batch, dec_seq, 2048, heads_d, dh, diff_lam[l],
                               diff_norm_g[l], lam_inits[l], ctx=(ck, cv, l), tq=512)
        yc = jnp.concatenate([yc_p, yc_s], axis=0)
        new_k.append(zd[:mp, dw:2 * dw].reshape(batch, seq, heads_d, 2, dh))
        new_v.append(zd[:mp, 2 * dw:3 * dw].reshape(batch, seq, heads_d, 2 * dh))

        wbr = jnp.stack([w_branch_a[l], w_branch_b[l], w_branch_c[l]]).astype(BF16)
        y = _mix(ya, yb, yc, gates, wbr)
        x, h = _mm_ln(y, w_out[l].astype(BF16), x, mod, l, 5, 1.0, alpha, l, 6, ln_g[l, 1], ln_b[l, 1],
                      tm=512)

        w2 = jnp.pad(ffn_w2[l, 1].astype(BF16), ((0, fp - ffn_dim), (0, 0)))
        hid = _ffn1(h, ffn_w1, ffn_w3, l, 1)
        if l + 1 < depth:
            x, h = _mm_ln(hid, w2, x, mod, l, 8, 0.5, alpha, l + 1, 0, ln_g[l, 2], ln_b[l, 2])
        else:
            xp, _ = _mm_ln(hid, w2, x, mod, l, 8, 0.5, alpha, l, 0, ln_g[l, 2], ln_b[l, 2], rows=mp)
            xs, _ = _mm_ln(hid, w2, x, mod, l, 8, 0.5, alpha, l, 0, ln_g[l, 2], ln_b[l, 2], row0=mp,
                           rows=m - mp)

    y_prompt = xp.reshape(batch, seq, d)
    y_sample = xs.reshape(dec_batch, dec_seq, d)
    return (y_prompt, y_sample, jnp.stack(new_k, axis=1), jnp.stack(new_v, axis=1),
            jnp.stack(new_s, axis=1))
```

```python
import functools
import math

import numpy as np
import jax
import jax.numpy as jnp
from jax import lax
from jax.experimental import pallas as pl
from jax.experimental.pallas import tpu as pltpu

F32 = jnp.float32
BF16 = jnp.bfloat16

GRID_W = 64
N_MOD = 9
HY_BANDS = 8
GLA_TAU = 16.0
GLA_CHUNK = 64
ROPE_THETA = 10000.0
LN_EPS = 1e-5
RMS_EPS = 1e-6

LANES = 128
MXU_DIM = 256
VMEM_BYTES_V7X = 64 * 1024 * 1024
VMEM_LIMIT = VMEM_BYTES_V7X - 8 * 1024 * 1024

GROUP = 4096
FFT_NB = 16
FFT_K1G = 8


def _cp(*sem):
    return pltpu.CompilerParams(dimension_semantics=sem, vmem_limit_bytes=VMEM_LIMIT)


def _dot(a, b):
    return jnp.dot(a, b, preferred_element_type=F32)


def _dot_nt(a, b):
    return lax.dot_general(a, b, (((1,), (1,)), ((), ())), preferred_element_type=F32)


def _dot_tn(a, b):
    return lax.dot_general(a, b, (((0,), (0,)), ((), ())), preferred_element_type=F32)


def _dot_hi(a, b):
    return jnp.dot(a, b, preferred_element_type=F32, precision=lax.Precision.HIGHEST)


def _silu(x):
    return x * jax.nn.sigmoid(x)


def _mod_kernel(c_ref, w_ref, b_ref, o_ref):
    c = c_ref[...]
    o_ref[...] = _dot(_silu(c).astype(BF16), w_ref[...].astype(BF16)) + b_ref[...]


def _modulation(cond, w_mod, b_mod):
    depth, d, n = w_mod.shape
    r = cond.shape[0]
    tn = 1024
    return pl.pallas_call(
        _mod_kernel,
        grid=(depth, n // tn),
        in_specs=[pl.BlockSpec((r, d), lambda l, j: (0, 0)),
                  pl.BlockSpec((None, d, tn), lambda l, j: (l, 0, j)),
                  pl.BlockSpec((None, 1, tn), lambda l, j: (l, 0, j))],
        out_specs=pl.BlockSpec((None, r, tn), lambda l, j: (l, 0, j)),
        out_shape=jax.ShapeDtypeStruct((depth, r, n), F32),
        compiler_params=_cp("parallel", "parallel"),
        name="mod",
    )(cond, w_mod, b_mod.reshape(depth, 1, n))


def _premod_kernel(x_ref, mod_ref, o_ref):
    o_ref[...] = (x_ref[...] * (1.0 + mod_ref[1:2, :]) + mod_ref[0:1, :]).astype(o_ref.dtype)


def _premod(x, mod, layer, tm=512):
    m, d = x.shape
    return pl.pallas_call(
        _premod_kernel,
        grid=(m // tm,),
        in_specs=[pl.BlockSpec((tm, d), lambda i: (i, 0)),
                  pl.BlockSpec((None, None, N_MOD, d), lambda i: (layer, (i * tm) // GROUP, 0, 0))],
        out_specs=pl.BlockSpec((tm, d), lambda i: (i, 0)),
        out_shape=jax.ShapeDtypeStruct((m, d), BF16),
        compiler_params=_cp("parallel"),
        name="premod",
    )(x, mod)


def _ffn1_kernel(h_ref, w1_ref, w3_ref, o_ref, w1b_ref, w3b_ref, *, tf, f_valid):
    @pl.when(pl.program_id(1) == 0)
    def _():
        col = pl.program_id(0) * tf + lax.broadcasted_iota(jnp.int32, w1_ref.shape, 1)
        keep = col < f_valid
        w1b_ref[...] = jnp.where(keep, w1_ref[...], 0.0).astype(BF16)
        w3b_ref[...] = jnp.where(keep, w3_ref[...], 0.0).astype(BF16)

    h = h_ref[...]
    a = _dot(h, w1b_ref[...])
    b = _dot(h, w3b_ref[...])
    o_ref[...] = (_silu(a) * b).astype(o_ref.dtype)


def _ffn1(h, w1, w3, layer, sub, tm=1024, tf=512):
    m, d = h.shape
    f = w1.shape[-1]
    nf = pl.cdiv(f, tf)
    wspec = pl.BlockSpec((None, None, d, tf), lambda j, i: (layer, sub, 0, j))
    return pl.pallas_call(
        functools.partial(_ffn1_kernel, tf=tf, f_valid=f),
        grid=(nf, m // tm),
        in_specs=[pl.BlockSpec((tm, d), lambda j, i: (i, 0)), wspec, wspec],
        out_specs=pl.BlockSpec((tm, tf), lambda j, i: (i, j)),
        out_shape=jax.ShapeDtypeStruct((m, nf * tf), BF16),
        scratch_shapes=[pltpu.VMEM((d, tf), BF16), pltpu.VMEM((d, tf), BF16)],
        compiler_params=_cp("parallel", "arbitrary"),
        name="ffn1",
    )(h, w1, w3)


def _cast_rows_kernel(w_ref, o_ref, *, rows_valid, tr):
    row = pl.program_id(0) * tr + lax.broadcasted_iota(jnp.int32, w_ref.shape, 0)
    o_ref[...] = jnp.where(row < rows_valid, w_ref[...], 0.0).astype(o_ref.dtype)


def _cast_pad_rows(w, layer, sub, rows_out, tr=512):
    f, d = w.shape[-2:]
    return pl.pallas_call(
        functools.partial(_cast_rows_kernel, rows_valid=f, tr=tr),
        grid=(rows_out // tr,),
        in_specs=[pl.BlockSpec((None, None, tr, d), lambda i: (layer, sub, i, 0))],
        out_specs=pl.BlockSpec((tr, d), lambda i: (i, 0)),
        out_shape=jax.ShapeDtypeStruct((rows_out, d), BF16),
        compiler_params=_cp("parallel"),
        name="cast_w2",
    )(w)


def _mm_ln_kernel(a_ref, w_ref, x_ref, mod_ref, nmod_ref, g_ref, b_ref, xo_ref, ho_ref, *,
                  gate_row, coef, alpha, nshift_row):
    gate = coef * mod_ref[gate_row:gate_row + 1, :]
    xr = alpha * x_ref[...] + gate * _dot(a_ref[...], w_ref[...])
    mu = jnp.mean(xr, axis=-1, keepdims=True)
    xc = xr - mu
    var = jnp.mean(xc * xc, axis=-1, keepdims=True)
    xn = xc * lax.rsqrt(var + LN_EPS) * g_ref[...] + b_ref[...]
    xo_ref[...] = xn
    ho_ref[...] = (xn * (1.0 + nmod_ref[nshift_row + 1:nshift_row + 2, :])
                   + nmod_ref[nshift_row:nshift_row + 1, :]).astype(ho_ref.dtype)


def _mm_ln(a, w, x, mod, layer, gate_row, coef, alpha, nlayer, nshift_row, ln_g, ln_b, tm=256,
           row0=0, rows=None):
    kdim = a.shape[1]
    m = a.shape[0] if rows is None else rows
    d = w.shape[1]
    rb0 = row0 // tm
    kern = functools.partial(_mm_ln_kernel, gate_row=gate_row, coef=coef, alpha=alpha, nshift_row=nshift_row)
    return pl.pallas_call(
        kern,
        grid=(m // tm,),
        in_specs=[pl.BlockSpec((tm, kdim), lambda i: (rb0 + i, 0)),
                  pl.BlockSpec((kdim, d), lambda i: (0, 0), pipeline_mode=pl.Buffered(1)),
                  pl.BlockSpec((tm, d), lambda i: (rb0 + i, 0)),
                  pl.BlockSpec((None, None, N_MOD, d), lambda i: (layer, ((rb0 + i) * tm) // GROUP, 0, 0)),
                  pl.BlockSpec((None, None, N_MOD, d), lambda i: (nlayer, ((rb0 + i) * tm) // GROUP, 0, 0)),
                  pl.BlockSpec((1, d), lambda i: (0, 0)),
                  pl.BlockSpec((1, d), lambda i: (0, 0))],
        out_specs=[pl.BlockSpec((tm, d), lambda i: (i, 0)),
                   pl.BlockSpec((tm, d), lambda i: (i, 0))],
        out_shape=[jax.ShapeDtypeStruct((m, d), F32), jax.ShapeDtypeStruct((m, d), BF16)],
        compiler_params=_cp("parallel"),
        name="mm_ln",
    )(a, w, x, mod, mod, ln_g.reshape(1, d), ln_b.reshape(1, d))


def _proj_kernel(h_ref, w_ref, o_ref, *, act):
    r = _dot(h_ref[...], w_ref[...])
    if act == "sigmoid":
        r = jax.nn.sigmoid(r)
    o_ref[...] = r.astype(o_ref.dtype)


def _proj(h, w, out_dtype, act=None, tm=1024, tn=1024):
    m, d = h.shape
    n = w.shape[1]
    tn = min(tn, n)
    return pl.pallas_call(
        functools.partial(_proj_kernel, act=act),
        grid=(m // tm, n // tn),
        in_specs=[pl.BlockSpec((tm, d), lambda i, j: (i, 0)),
                  pl.BlockSpec((d, tn), lambda i, j: (0, j))],
        out_specs=pl.BlockSpec((tm, tn), lambda i, j: (i, j)),
        out_shape=jax.ShapeDtypeStruct((m, n), out_dtype),
        compiler_params=_cp("parallel", "arbitrary"),
        name="proj",
    )(h, w)


def _sconv_kernel(z_ref, zp_ref, zn_ref, w_ref, b_ref, o_ref, *, tm, halo, lp, ls):
    i = pl.program_id(0)
    u = z_ref[...].astype(F32)
    prev = zp_ref[...].astype(F32)[halo - 1:halo, :]
    nxt = zn_ref[...].astype(F32)[0:1, :]
    row = lax.broadcasted_iota(jnp.int32, u.shape, 0)
    lseq = jnp.where((i * tm) // GROUP == 0, lp, ls)
    pos = (row + i * tm) & (lseq - 1)
    up = jnp.where(row == 0, prev, pltpu.roll(u, 1, 0))
    up = jnp.where(pos == 0, 0.0, up)
    un = jnp.where(row == tm - 1, nxt, pltpu.roll(u, tm - 1, 0))
    un = jnp.where(pos == lseq - 1, 0.0, un)
    w = w_ref[...]
    o_ref[...] = (up * w[0:1, :] + u * w[1:2, :] + un * w[2:3, :] + b_ref[...]).astype(o_ref.dtype)


def _short_conv(z, rows, conv_w, conv_b, lp, ls, width, tm=1024, ct=512, halo=16):
    m = rows
    nct = width // ct
    nrb = m // halo
    kern = functools.partial(_sconv_kernel, tm=tm, halo=halo, lp=lp, ls=ls)
    return pl.pallas_call(
        kern,
        grid=(m // tm, 3 * nct),
        in_specs=[pl.BlockSpec((tm, ct), lambda i, j: (i, j)),
                  pl.BlockSpec((halo, ct), lambda i, j: (jnp.maximum(i * (tm // halo) - 1, 0), j)),
                  pl.BlockSpec((halo, ct), lambda i, j: (jnp.minimum((i + 1) * (tm // halo), nrb - 1), j)),
                  pl.BlockSpec((3, ct), lambda i, j: (0, j)),
                  pl.BlockSpec((1, ct), lambda i, j: (0, j))],
        out_specs=pl.BlockSpec((None, tm, ct), lambda i, j: (j // nct, i, j % nct)),
        out_shape=jax.ShapeDtypeStruct((3, m, width), F32),
        compiler_params=_cp("parallel", "parallel"),
        name="sconv",
    )(z, z, z, conv_w, conv_b.reshape(1, -1))


def _filter_features(L):
    t = np.linspace(0.0, 1.0, L, dtype=np.float32)
    w = (np.float32(2.0 * math.pi / L) * np.arange(L, dtype=np.float32)).astype(np.float32)
    f = np.linspace(1e-4, HY_BANDS - 1, HY_BANDS, dtype=np.float32)
    wf = (w[:, None] * f).astype(np.float32)
    feats = np.concatenate([t[:, None], np.cos(wf), -np.sin(wf)], -1).astype(np.float32)
    idx = np.concatenate([np.arange(L), [0], np.arange(L - 1, 0, -1)])
    tab = np.zeros((2 * L, 32), np.float32)
    tab[:, :feats.shape[1]] = feats[idx]
    tab[:, 24] = t[idx]
    tab[:L, 25] = 1.0
    tab[L + 1:, 26] = -1.0
    return tab, feats.shape[1]


def _filter_kernel(tab_ref, w1_ref, b1_ref, fr_ref, w2_ref, b2_ref, w3_ref, dec_ref, o_ref):
    tab = tab_ref[...]
    fr = fr_ref[...]
    hdn = jnp.sin(fr[0:1, :] * (_dot_hi(tab, w1_ref[...]) + b1_ref[...]))
    hdn = jnp.sin(fr[1:2, :] * (_dot_hi(hdn, w2_ref[...]) + b2_ref[...]))
    t = tab[:, 24:25]
    sign = tab[:, 25:26] + tab[:, 26:27]
    for o in range(2):
        o_ref[o] = sign * (_dot_hi(hdn, w3_ref[o]) * jnp.exp(-t * jnp.abs(dec_ref[o])))


def _hyena_filter_taps(L, hy_w1, hy_b1, hy_freq, hy_w2, hy_b2, hy_w3, hy_decay, width, rb=256):
    tab_np, nfeat = _filter_features(L)
    fh = hy_w1.shape[1]
    w1p = jnp.zeros((32, fh), F32).at[:nfeat].set(hy_w1)
    n = 2 * L
    nhalf = L // rb
    w3d = jnp.transpose(hy_w3.reshape(fh, 2, 2, width), (1, 2, 0, 3))
    decd = hy_decay.reshape(2, 2, 1, width)
    return pl.pallas_call(
        _filter_kernel,
        grid=(n // rb,),
        in_specs=[pl.BlockSpec((rb, 32), lambda i: (i, 0)),
                  pl.BlockSpec((32, fh), lambda i: (0, 0)),
                  pl.BlockSpec((1, fh), lambda i: (0, 0)),
                  pl.BlockSpec((2, fh), lambda i: (0, 0)),
                  pl.BlockSpec((fh, fh), lambda i: (0, 0)),
                  pl.BlockSpec((1, fh), lambda i: (0, 0)),
                  pl.BlockSpec((2, None, fh, width), lambda i: (0, i // nhalf, 0, 0)),
                  pl.BlockSpec((2, None, 1, width), lambda i: (0, i // nhalf, 0, 0))],
        out_specs=pl.BlockSpec((2, rb, width), lambda i: (0, i, 0)),
        out_shape=jax.ShapeDtypeStruct((2, n, width), F32),
        compiler_params=_cp("parallel"),
        name="hyfilter",
    )(jnp.asarray(tab_np), w1p, hy_b1.reshape(1, fh), hy_freq, hy_w2, hy_b2.reshape(1, fh), w3d, decd)


def _dft_a(na, ka):
    k1 = np.arange(na // 2)[:, None].astype(np.float64)
    a = np.arange(ka)[None, :].astype(np.float64)
    th = 2.0 * np.pi * a * (k1 + 0.5) / na
    return np.concatenate([np.cos(th), -np.sin(th)], 0)


def _idft_a(na, n, rows):
    k1 = np.arange(na // 2)[None, :].astype(np.float64)
    a = np.arange(rows)[:, None].astype(np.float64)
    th = 2.0 * np.pi * a * (k1 + 0.5) / na
    return (2.0 / n) * np.concatenate([np.cos(th), -np.sin(th)], 1)


def _dft_b(na, nb, g):
    n = na * nb
    half = na // 2
    k1 = np.arange(half).astype(np.float64)
    b = np.arange(nb).astype(np.float64)
    k2 = np.arange(nb).astype(np.float64)
    phi = 2.0 * np.pi * (b[None, None, :] * k2[None, :, None] / nb
                         + b[None, None, :] * (k1[:, None, None] + 0.5) / n)
    c, s = np.cos(phi), np.sin(phi)
    ng = half // g
    r = g * nb
    fwd = np.zeros((ng, 2 * r, 2 * r))
    for q in range(g):
        rows = slice(q * nb, (q + 1) * nb)
        rows_i = slice(r + q * nb, r + (q + 1) * nb)
        cols = slice(q, r, g)
        cols_i = slice(r + q, 2 * r, g)
        cq, sq = c[q::g], s[q::g]
        fwd[:, rows, cols] = cq
        fwd[:, rows, cols_i] = sq
        fwd[:, rows_i, cols] = -sq
        fwd[:, rows_i, cols_i] = cq
    inv = np.transpose(fwd, (0, 2, 1))
    return fwd, inv


def _lmm_kernel(f_ref, x_ref, o_ref):
    o_ref[...] = _dot(f_ref[...], x_ref[...].astype(BF16)).astype(o_ref.dtype)


def _lmm(f, x, x_index, nbatch, out_dtype, tn=2048):
    mo, k = f.shape
    n = x.shape[-1]
    lead = len(x.shape) - 2
    return pl.pallas_call(
        _lmm_kernel,
        grid=(nbatch, n // tn),
        in_specs=[pl.BlockSpec((mo, k), lambda b, j: (0, 0)),
                  pl.BlockSpec((None,) * lead + (k, tn), lambda b, j: x_index(b) + (0, j))],
        out_specs=pl.BlockSpec((None, mo, tn), lambda b, j: (b, 0, j)),
        out_shape=jax.ShapeDtypeStruct((nbatch, mo, n), out_dtype),
        compiler_params=_cp("parallel", "parallel"),
        name="dft_a",
    )(f, x)


def _stage_a_fwd(x_ref, f1_ref, r_ref, nb, ct):
    for b in range(nb):
        r_ref[:, b * ct:(b + 1) * ct] = _dot(f1_ref[...], x_ref[:, b, :].astype(BF16))


def _stage_b_rows(r_ref, r0, half, nb, kg, ct):
    return [(pl.ds(ri * half + r0, kg), slice(b * ct, (b + 1) * ct)) for ri in range(2) for b in range(nb)]


def _row_permutation(nb, al):
    p = np.zeros((nb * al, nb * al), np.float32)
    for b in range(nb):
        for a in range(al):
            p[b * al + a, a * nb + b] = 1.0
    return p


def _to_residue_major(src_ref, perm_ref, dst_ref, nb, al):
    blk = nb * al
    for i in range(src_ref.shape[0] // blk):
        z = _dot(perm_ref[...], src_ref[i * blk:(i + 1) * blk, :].astype(BF16))
        for b in range(nb):
            dst_ref[b, i * al:(i + 1) * al, :] = z[b * al:(b + 1) * al, :]


def _short_conv_residue_major(u_ref, w_ref, cb_ref, nb):
    rows = u_ref.shape[1]
    row = lax.broadcasted_iota(jnp.int32, u_ref.shape[1:], 0)
    w = w_ref[...]
    cb = cb_ref[...]
    first = u_ref[0]
    prev = jnp.where(row == 0, 0.0, pltpu.roll(u_ref[nb - 1], 1, 0))
    for b in range(nb):
        cur = first if b == 0 else u_ref[b]
        nxt = u_ref[b + 1] if b + 1 < nb else jnp.where(row == rows - 1, 0.0, pltpu.roll(first, rows - 1, 0))
        u_ref[b] = prev * w[0:1, :] + cur * w[1:2, :] + nxt * w[2:3, :] + cb
        prev = cur


def _lconv_kernel(zv_ref, zx_ref, cwv_ref, cbv_ref, cwx_ref, cbx_ref, perm_ref, permt_ref, f1_ref, g_ref,
                  tb_ref, tbi_ref, kf_ref, bias_ref, o_ref, v_ref, x_ref, r_ref, *, nb, half, kg, ct, conv_v):
    r = nb * kg
    al = perm_ref.shape[0] // nb
    _to_residue_major(zv_ref, perm_ref, v_ref, nb, al)
    _to_residue_major(zx_ref, perm_ref, x_ref, nb, al)
    if conv_v:
        _short_conv_residue_major(v_ref, cwv_ref, cbv_ref, nb)
    _short_conv_residue_major(x_ref, cwx_ref, cbx_ref, nb)
    for b in range(nb):
        r_ref[:, b * ct:(b + 1) * ct] = _dot(f1_ref[...], v_ref[b].astype(BF16))

    for g in range(half // kg):
        tiles = _stage_b_rows(r_ref, g * kg, half, nb, kg, ct)
        a = jnp.concatenate([r_ref[rs, cs] for rs, cs in tiles], axis=0).astype(BF16)
        x = _dot(tb_ref[g], a)
        kf = kf_ref[g * 2 * r:(g + 1) * 2 * r, :]
        xr, xi = x[:r], x[r:]
        kr, ki = kf[:r], kf[r:]
        y = jnp.concatenate([xr * kr - xi * ki, xr * ki + xi * kr], axis=0).astype(BF16)
        bh = _dot(tbi_ref[g], y)
        for t, (rs, cs) in enumerate(tiles):
            r_ref[rs, cs] = bh[t * kg:(t + 1) * kg, :]
    bias = bias_ref[...]
    for b in range(nb):
        y = _dot(g_ref[...], r_ref[:, b * ct:(b + 1) * ct].astype(BF16))
        v_ref[b] = x_ref[b] * (y + v_ref[b] * bias)
    blk = nb * al
    for i in range(o_ref.shape[0] // blk):
        t = jnp.concatenate([v_ref[b, i * al:(i + 1) * al, :] for b in range(nb)], axis=0).astype(BF16)
        o_ref[i * blk:(i + 1) * blk, :] = _dot(permt_ref[...], t).astype(o_ref.dtype)


def _lconv(zv, v_row0, v_col0, zx, x_row0, x_col0, conv_w, conv_b, conv_v, perm, permt, f1, g_inv, tb, tbi,
           kf, order, bias, nbatch, L, ct=256):
    c = kf.shape[-1]
    n2, _ = f1.shape
    ng, r2, _ = tb.shape
    nb = L // (n2 // 2)
    kg = r2 // (2 * nb)
    ncb = c // ct
    once = dict(pipeline_mode=pl.Buffered(1))
    kern = functools.partial(_lconv_kernel, nb=nb, half=n2 // 2, kg=kg, ct=ct, conv_v=conv_v)
    vb0, xb0 = v_row0 // L, x_row0 // L
    return pl.pallas_call(
        kern,
        grid=(ncb, nbatch),
        in_specs=[pl.BlockSpec((L, ct), lambda j, b: (vb0 + b, v_col0 * ncb + j)),
                  pl.BlockSpec((L, ct), lambda j, b: (xb0 + b, x_col0 * ncb + j)),
                  pl.BlockSpec((None, 3, ct), lambda j, b: (0, 0, j)),
                  pl.BlockSpec((None, 1, ct), lambda j, b: (0, 0, j)),
                  pl.BlockSpec((None, 3, ct), lambda j, b: (1, 0, j)),
                  pl.BlockSpec((None, 1, ct), lambda j, b: (1, 0, j)),
                  pl.BlockSpec(perm.shape, lambda j, b: (0, 0), **once),
                  pl.BlockSpec(permt.shape, lambda j, b: (0, 0), **once),
                  pl.BlockSpec(f1.shape, lambda j, b: (0, 0), **once),
                  pl.BlockSpec(g_inv.shape, lambda j, b: (0, 0), **once),
                  pl.BlockSpec(tb.shape, lambda j, b: (0, 0, 0), **once),
                  pl.BlockSpec(tbi.shape, lambda j, b: (0, 0, 0), **once),
                  pl.BlockSpec((None, ng * r2, ct), lambda j, b: (order, 0, j), **once),
                  pl.BlockSpec((1, ct), lambda j, b: (0, j))],
        out_specs=pl.BlockSpec((L, ct), lambda j, b: (b, j)),
        out_shape=jax.ShapeDtypeStruct((nbatch * L, c), BF16),
        scratch_shapes=[pltpu.VMEM((nb, L // nb, ct), F32), pltpu.VMEM((nb, L // nb, ct), F32),
                        pltpu.VMEM((n2, nb * ct), F32)],
        compiler_params=_cp("parallel", "arbitrary"),
        name="lconv",
    )(zv, zx, conv_w, conv_b, conv_w, conv_b, perm, permt, f1, g_inv, tb, tbi, kf, bias.reshape(1, c))


def _lconv_filter_kernel(x_ref, f1_ref, tb_ref, o_ref, r_ref, *, nb, half, kg, ct):
    r2 = 2 * nb * kg
    _stage_a_fwd(x_ref, f1_ref, r_ref, nb, ct)

    for g in range(half // kg):
        tiles = _stage_b_rows(r_ref, g * kg, half, nb, kg, ct)
        a = jnp.concatenate([r_ref[rs, cs] for rs, cs in tiles], axis=0).astype(BF16)
        o_ref[g * r2:(g + 1) * r2, :] = _dot(tb_ref[g], a)


def _lconv_filter(taps, f1, tb, ct=256):
    norder, na, nb, c = taps.shape
    n2, _ = f1.shape
    ng, r2, _ = tb.shape
    kg = r2 // (2 * nb)
    kern = functools.partial(_lconv_filter_kernel, nb=nb, half=n2 // 2, kg=kg, ct=ct)
    return pl.pallas_call(
        kern,
        grid=(norder, c // ct),
        in_specs=[pl.BlockSpec((None, na, nb, ct), lambda o, j: (o, 0, 0, j)),
                  pl.BlockSpec(f1.shape, lambda o, j: (0, 0)),
                  pl.BlockSpec(tb.shape, lambda o, j: (0, 0, 0))],
        out_specs=pl.BlockSpec((None, ng * r2, ct), lambda o, j: (o, 0, j)),
        out_shape=jax.ShapeDtypeStruct((norder, ng * r2, c), F32),
        scratch_shapes=[pltpu.VMEM((n2, nb * ct), F32)],
        compiler_params=_cp("parallel", "parallel"),
        name="lconv_filter",
    )(taps, f1, tb)


def _pconv_kernel(fd_ref, gd_ref, v_ref, kf_ref, xg_ref, bias_ref, o_ref, *, half):
    v = v_ref[...]
    x = _dot(fd_ref[...], v.astype(BF16))
    kf = kf_ref[...]
    xr, xi = x[:half], x[half:]
    kr, ki = kf[:half], kf[half:]
    y = jnp.concatenate([xr * kr - xi * ki, xr * ki + xi * kr], axis=0).astype(BF16)
    yt = _dot(gd_ref[...], y)
    o_ref[...] = (xg_ref[...].astype(F32) * (yt + v.astype(F32) * bias_ref[...])).astype(o_ref.dtype)


def _pconv(fd, gd, vsrc, v_lead, kf, order, xsrc, x_lead, bias, nseq, L, out_dtype, ct=512):
    c = kf.shape[-1]
    n2 = fd.shape[0]
    return pl.pallas_call(
        functools.partial(_pconv_kernel, half=n2 // 2),
        grid=(nseq, c // ct),
        in_specs=[pl.BlockSpec((n2, L), lambda s, j: (0, 0)),
                  pl.BlockSpec((L, n2), lambda s, j: (0, 0)),
                  pl.BlockSpec((None,) * len(v_lead) + (L, ct), lambda s, j: v_lead + (s, j)),
                  pl.BlockSpec((None, n2, ct), lambda s, j: (order, 0, j)),
                  pl.BlockSpec((None,) * len(x_lead) + (L, ct), lambda s, j: x_lead + (s, j)),
                  pl.BlockSpec((1, ct), lambda s, j: (0, j))],
        out_specs=pl.BlockSpec((L, ct), lambda s, j: (s, j)),
        out_shape=jax.ShapeDtypeStruct((nseq * L, c), out_dtype),
        compiler_params=_cp("parallel", "parallel"),
        name="pconv",
    )(fd, gd, vsrc, kf, xsrc, bias.reshape(1, c))


def _gate_kernel(lr_ref, wa_ref, ba_ref, o_ref):
    logits = _dot_hi(lr_ref[...], wa_ref[...]) + ba_ref[...]
    o_ref[...] = jax.nn.log_sigmoid(logits) * (1.0 / GLA_TAU)


def _gla_gates(lr, col, wa_cat, ba_cat, tm=1024):
    m = lr.shape[0]
    k, n = wa_cat.shape
    return pl.pallas_call(
        _gate_kernel,
        grid=(m // tm,),
        in_specs=[pl.BlockSpec((tm, k), lambda i: (i, col)),
                  pl.BlockSpec((k, n), lambda i: (0, 0)),
                  pl.BlockSpec((1, n), lambda i: (0, 0))],
        out_specs=pl.BlockSpec((tm, n), lambda i: (i, 0)),
        out_shape=jax.ShapeDtypeStruct((m, n), F32),
        compiler_params=_cp("parallel"),
        name="gla_gates",
    )(lr, wa_cat, ba_cat)


def _split3(x):
    hi = x.astype(BF16)
    r1 = x - hi.astype(F32)
    mid = r1.astype(BF16)
    lo = (r1 - mid.astype(F32)).astype(BF16)
    return hi, mid, lo


def _gla_dir(d, r0, q_ref, k_ref, v_ref, la_ref, o_ref, st_ref, tri, causal, ref_row, last_row,
             heads, dk, dv, scale):
    ch = GLA_CHUNK
    rows = pl.ds(pl.multiple_of(r0, ch), ch)
    la = la_ref[rows, :]
    hi, mid, lo = _split3(la)
    b = _dot(tri, hi) + _dot(tri, mid) + _dot(tri, lo)
    bref = b[ref_row:ref_row + 1, :]
    blast = b[last_row:last_row + 1, :]
    q = q_ref[rows, :].astype(F32) * scale
    k = k_ref[rows, :].astype(F32)
    qt = (q * jnp.exp(b - bref)).astype(BF16)
    kt = (k * jnp.exp(bref - b)).astype(BF16)
    qin = (q * jnp.exp(b)).astype(BF16)
    kst = (k * jnp.exp(blast - b)).astype(BF16)
    dec = jnp.exp(blast)
    for h in range(heads):
        ks = slice(h * dk, (h + 1) * dk)
        vs = slice(h * dv, (h + 1) * dv)
        att = _dot_nt(qt[:, ks], kt[:, ks])
        att = jnp.where(causal, att, 0.0).astype(BF16)
        vh = v_ref[rows, vs]
        st = st_ref[d, h]
        o_ref[rows, vs] = _dot(att, vh) + _dot_nt(qin[:, ks], st.astype(BF16))
        st_ref[d, h] = st * dec[:, ks] + _dot_tn(vh, kst[:, ks])


def _gla_kernel(*refs, nch, heads, dk, dv, has_s0, scale, nsq):
    seq_in = [refs[8 * q:8 * (q + 1)] for q in range(nsq)]
    rest = refs[8 * nsq:]
    if has_s0:
        s0, of, ob, st = rest
    else:
        of, ob, st = rest
        s0 = None
    ch = GLA_CHUNK

    @pl.when(pl.program_id(1) == 0)
    def _():
        if has_s0:
            st[...] = s0[...]
        else:
            st[...] = jnp.zeros_like(st)

    r_i = lax.broadcasted_iota(jnp.int32, (ch, ch), 0)
    c_i = lax.broadcasted_iota(jnp.int32, (ch, ch), 1)
    lower = r_i >= c_i
    upper = r_i <= c_i
    tri_l = jnp.where(lower, 1.0, 0.0).astype(BF16)
    tri_u = jnp.where(upper, 1.0, 0.0).astype(BF16)

    def body(c, carry):
        for q, (qf, kf, vf, laf, qb, kb, vb, lab) in enumerate(seq_in):
            _gla_dir(0, c * ch, qf, kf, vf, laf, of.at[q], st.at[q], tri_l, lower, ch // 2 - 1, ch - 1,
                     heads, dk, dv, scale)
            _gla_dir(1, (nch - 1 - c) * ch, qb, kb, vb, lab, ob.at[q], st.at[q], tri_u, upper, ch // 2, 0,
                     heads, dk, dv, scale)
        return carry

    lax.fori_loop(0, nch, body, 0)


def _gla(zmain, qcol, kcol, vcol, la, row0, nseq, L, tb, heads, dk, dv, s0t=None, nsq=2):
    nblk = L // tb
    rb0 = row0 // tb
    hk, hv = heads * dk, heads * dv
    in_specs, args = [], []
    for q in range(nsq):
        def fwd(s, j, q=q):
            return rb0 + (s * nsq + q) * nblk + j

        def bwd(s, j, q=q):
            return rb0 + (s * nsq + q) * nblk + (nblk - 1 - j)

        for rowf, lcol in ((fwd, 0), (bwd, 1)):
            in_specs += [pl.BlockSpec((tb, hk), lambda s, j, rowf=rowf: (rowf(s, j), qcol)),
                         pl.BlockSpec((tb, hk), lambda s, j, rowf=rowf: (rowf(s, j), kcol)),
                         pl.BlockSpec((tb, hv), lambda s, j, rowf=rowf: (rowf(s, j), vcol)),
                         pl.BlockSpec((tb, hk), lambda s, j, rowf=rowf, lcol=lcol: (rowf(s, j), lcol))]
            args += [zmain, zmain, zmain, la]
    st_spec = pl.BlockSpec((nsq, 2, heads, dv, dk), lambda s, j: (s, 0, 0, 0, 0))
    if s0t is not None:
        in_specs.append(st_spec)
        args.append(s0t)
    kern = functools.partial(_gla_kernel, nch=tb // GLA_CHUNK, heads=heads, dk=dk, dv=dv,
                             has_s0=s0t is not None, scale=dk ** -0.5, nsq=nsq)
    o_shape = jax.ShapeDtypeStruct((nseq // nsq, nsq, L, hv), F32)
    o_f, o_b, st = pl.pallas_call(
        kern,
        grid=(nseq // nsq, nblk),
        in_specs=in_specs,
        out_specs=[pl.BlockSpec((None, nsq, tb, hv), lambda s, j: (s, 0, j, 0)),
                   pl.BlockSpec((None, nsq, tb, hv), lambda s, j: (s, 0, nblk - 1 - j, 0)),
                   st_spec],
        out_shape=[o_shape, o_shape, jax.ShapeDtypeStruct((nseq, 2, heads, dv, dk), F32)],
        compiler_params=_cp("parallel", "arbitrary"),
        name="gla",
    )(*args)
    return o_f.reshape(nseq * L, hv), o_b.reshape(nseq * L, hv), st


def _gla_post_kernel(of_ref, ob_ref, gr_ref, g_ref, o_ref, *, heads, dv):
    o = of_ref[...] + ob_ref[...]
    gate = _silu(gr_ref[...].astype(F32))
    g = g_ref[...]
    for h in range(heads):
        sl = slice(h * dv, (h + 1) * dv)
        oh = o[:, sl]
        ms = jnp.mean(oh * oh, axis=-1, keepdims=True)
        o_ref[:, sl] = (oh * lax.rsqrt(ms + RMS_EPS) * g * gate[:, sl]).astype(o_ref.dtype)


def _gla_post(o_f, o_b, zmain, grcol, norm_g, heads, dv, tm=512):
    m, hv = o_f.shape
    return pl.pallas_call(
        functools.partial(_gla_post_kernel, heads=heads, dv=dv),
        grid=(m // tm,),
        in_specs=[pl.BlockSpec((tm, hv), lambda i: (i, 0)),
                  pl.BlockSpec((tm, hv), lambda i: (i, 0)),
                  pl.BlockSpec((tm, hv), lambda i: (i, grcol)),
                  pl.BlockSpec((1, dv), lambda i: (0, 0))],
        out_specs=pl.BlockSpec((tm, hv), lambda i: (i, 0)),
        out_shape=jax.ShapeDtypeStruct((m, hv), BF16),
        compiler_params=_cp("parallel"),
        name="gla_post",
    )(o_f, o_b, zmain, norm_g.reshape(1, dv))


def _rope_tables(L, dh):
    rows = L // GRID_W
    r = np.repeat(np.arange(rows, dtype=np.float32), GRID_W)
    col = np.tile(np.arange(GRID_W, dtype=np.float32), rows)
    nf = dh // 4
    inv = (np.float32(ROPE_THETA) ** (-np.arange(nf, dtype=np.float32) / nf)).astype(np.float32)
    ang_r = (r[:, None] * inv).astype(np.float32)
    ang_c = (col[:, None] * inv).astype(np.float32)
    cos = np.concatenate([np.cos(ang_r), np.cos(ang_r), np.cos(ang_c), np.cos(ang_c)], -1)
    sin = np.concatenate([-np.sin(ang_r), np.sin(ang_r), -np.sin(ang_c), np.sin(ang_c)], -1)
    cos = np.concatenate([cos, cos], -1).astype(np.float32)
    sin = np.concatenate([sin, sin], -1).astype(np.float32)
    cos_t = np.stack([np.ones_like(cos), cos])
    sin_t = np.stack([np.zeros_like(sin), sin])
    return cos_t, sin_t


def _attn_kernel(*refs, has_ctx, lam_init, dh, bf16_exp, tq):
    if has_ctx:
        (q_ref, k_ref, v_ref, cq_ref, sq_ref, ck_ref, sk_ref, kc_ref, vc_ref, lam_ref, g_ref, o_ref,
         kt_ref, ve_ref, s0_ref, s1_ref) = refs
    else:
        (q_ref, k_ref, v_ref, cq_ref, sq_ref, ck_ref, sk_ref, lam_ref, g_ref, o_ref,
         kt_ref, ve_ref, s0_ref, s1_ref) = refs
    hw = 2 * dh
    nf = dh // 4
    lk = k_ref.shape[0]
    nsub = q_ref.shape[0] // tq
    lane = lax.broadcasted_iota(jnp.int32, (tq, hw), 1)
    first = lane < dh
    scale = dh ** -0.5 * math.log2(math.e)

    def rope(x, cos, sin):
        ln = lax.broadcasted_iota(jnp.int32, x.shape, 1)
        sw = jnp.where((ln & (2 * nf - 1)) < nf, pltpu.roll(x, hw - nf, 1), pltpu.roll(x, nf, 1))
        return x * cos + sw * sin

    @pl.when(pl.program_id(2) == 0)
    def _():
        kt_ref[...] = rope(k_ref[...], ck_ref[...], sk_ref[...]).T.astype(BF16)
        ve_ref[:, :hw] = v_ref[...].astype(BF16)
        ve_ref[:, hw:] = jnp.ones((lk, hw), BF16)

    if has_ctx:
        kct = kc_ref[...].T.astype(BF16)
        vcv = vc_ref[...].astype(BF16)
        vce = jnp.concatenate([vcv, jnp.ones_like(vcv)], axis=1)

    def scores(i, j, s_ref):
        rows = pl.ds(i * tq if isinstance(i, int) else pl.multiple_of(i * tq, tq), tq)
        q = (rope(q_ref[rows, :], cq_ref[rows, :], sq_ref[rows, :]) * scale).astype(BF16)
        qj = jnp.where(first, q, jnp.zeros_like(q)) if j == 0 else jnp.where(first, jnp.zeros_like(q), q)
        s_ref[:, :lk] = _dot(qj, kt_ref[...])
        if has_ctx:
            s_ref[:, lk:] = _dot(qj, kct)

    def prob(t):
        if bf16_exp:
            return jnp.exp2(t.astype(BF16))
        return jnp.exp2(t).astype(BF16)

    def softmax_pv(s_ref):
        s = s_ref[...]
        p = prob(s - jnp.max(s, axis=-1, keepdims=True))
        acc = _dot(p[:, :lk], ve_ref[...])
        if has_ctx:
            acc = acc + _dot(p[:, lk:], vce)
        return acc[:, :hw] / acc[:, hw:]

    lp = lam_ref[...]
    lam = (jnp.exp(jnp.sum(lp[0:1] * lp[1:2], axis=-1, keepdims=True))
           - jnp.exp(jnp.sum(lp[2:3] * lp[3:4], axis=-1, keepdims=True)) + lam_init)
    gain = g_ref[...] * (1.0 - lam_init)

    scores(0, 0, s0_ref)

    def body(i, carry):
        scores(i, 1, s1_ref)
        sm0 = softmax_pv(s0_ref)
        scores(jnp.minimum(i + 1, nsub - 1), 0, s0_ref)
        o = sm0 - lam * softmax_pv(s1_ref)
        ms = jnp.mean(o * o, axis=-1, keepdims=True)
        o_ref[pl.ds(pl.multiple_of(i * tq, tq), tq), :] = (o * lax.rsqrt(ms + RMS_EPS) * gain).astype(o_ref.dtype)
        return carry

    lax.fori_loop(0, nsub, body, 0)


def _diff_attention(zd, cos_t, sin_t, rope_kind, row0, nseq, L, tb, heads, dh, lam_p, norm_g, lam_init,
                    ctx=None, tq=256):
    hw = 2 * dh
    nqb = L // tb
    qb0 = row0 // tb
    kb0 = row0 // L
    past = 0 if ctx is None else ctx[0].shape[2]
    once = dict(pipeline_mode=pl.Buffered(1))
    qtab = pl.BlockSpec((None, tb, hw), lambda s, h, i: (rope_kind, i, 0))
    ktab = pl.BlockSpec((None, L, hw), lambda s, h, i: (rope_kind, 0, 0), **once)
    in_specs = [pl.BlockSpec((tb, hw), lambda s, h, i: (qb0 + s * nqb + i, h)),
                pl.BlockSpec((L, hw), lambda s, h, i: (kb0 + s, heads + h)),
                pl.BlockSpec((L, hw), lambda s, h, i: (kb0 + s, 2 * heads + h)),
                qtab, qtab, ktab, ktab]
    args = [zd, zd, zd, cos_t, sin_t, cos_t, sin_t]
    if ctx is not None:
        ck, cv, layer = ctx
        in_specs += [pl.BlockSpec((None, None, past, hw), lambda s, h, i: (s, layer, 0, h)),
                     pl.BlockSpec((None, None, past, hw), lambda s, h, i: (s, layer, 0, h))]
        args += [ck, cv]
    in_specs += [pl.BlockSpec(lam_p.shape, lambda s, h, i: (0, 0)),
                 pl.BlockSpec((1, hw), lambda s, h, i: (0, 0))]
    args += [lam_p, norm_g.reshape(1, hw)]
    kern = functools.partial(_attn_kernel, has_ctx=ctx is not None, lam_init=lam_init, dh=dh,
                             bf16_exp=L > 1024, tq=tq)
    return pl.pallas_call(
        kern,
        grid=(nseq, heads, nqb),
        in_specs=in_specs,
        out_specs=pl.BlockSpec((tb, hw), lambda s, h, i: (s * nqb + i, h)),
        out_shape=jax.ShapeDtypeStruct((nseq * L, heads * hw), BF16),
        scratch_shapes=[pltpu.VMEM((hw, L), BF16), pltpu.VMEM((L, 2 * hw), BF16),
                        pltpu.VMEM((tq, L + past), F32), pltpu.VMEM((tq, L + past), F32)],
        compiler_params=_cp("parallel", "parallel", "arbitrary"),
        name="diff_attn",
    )(*args)


def _mix_kernel(ya_ref, yb_ref, yc_ref, g_ref, w_ref, o_ref, acc_ref):
    k = pl.program_id(1)

    def contrib(y_ref):
        return g_ref[...].astype(F32) * _dot(y_ref[...].astype(BF16), w_ref[...])

    @pl.when(k == 0)
    def _():
        acc_ref[...] = contrib(ya_ref)

    @pl.when(k == 1)
    def _():
        acc_ref[...] += contrib(yb_ref)

    @pl.when(k == 2)
    def _():
        o_ref[...] = (acc_ref[...] + contrib(yc_ref)).astype(o_ref.dtype)


def _mix(ya, yb, yc, gates, wbr, tm=512):
    m, w = ya.shape
    d = wbr.shape[2]
    return pl.pallas_call(
        _mix_kernel,
        grid=(m // tm, 3),
        in_specs=[pl.BlockSpec((tm, w), lambda i, k: (i, 0)),
                  pl.BlockSpec((tm, w), lambda i, k: (i, 0)),
                  pl.BlockSpec((tm, w), lambda i, k: (i, 0)),
                  pl.BlockSpec((tm, d), lambda i, k: (i, k)),
                  pl.BlockSpec((None, w, d), lambda i, k: (k, 0, 0))],
        out_specs=pl.BlockSpec((tm, d), lambda i, k: (i, 0)),
        out_shape=jax.ShapeDtypeStruct((m, d), BF16),
        scratch_shapes=[pltpu.VMEM((tm, d), F32)],
        compiler_params=_cp("parallel", "arbitrary"),
        name="mix",
    )(ya, yb, yc, gates, wbr)


def _pad_cols(w, n):
    return jnp.pad(w, ((0, 0), (0, n - w.shape[1])))


def kernel(x_prompt, x_sample, cache_k, cache_v, state_gla, c, c_ctx, w_mod, b_mod, ln_g, ln_b, ffn_w1, ffn_w3, ffn_w2, w_in, hy_conv_w, hy_conv_b, hy_w1, hy_b1, hy_freq, hy_w2, hy_b2, hy_w3, hy_decay, hy_bias, gla_wa, gla_ba, gla_norm_g, diff_lam, diff_norm_g, w_branch_a, w_branch_b, w_branch_c, w_out):
    batch, seq, d = x_prompt.shape
    dec_batch, dec_seq, _ = x_sample.shape
    depth = w_mod.shape[0]
    ffn_dim = ffn_w1.shape[3]
    hy_w = hy_bias.shape[2]
    heads_g, dk_g = 4, gla_wa.shape[3] // 4
    dv_g = gla_norm_g.shape[1]
    rank = gla_wa.shape[2]
    dh = diff_lam.shape[2]
    heads_d = cache_k.shape[3]
    dw = heads_d * 2 * dh
    gw = heads_g * dv_g
    gk = heads_g * dk_g
    assert batch * seq == GROUP and dec_seq == GROUP
    mp = batch * seq
    m = mp + dec_batch * dec_seq
    ngroups = 1 + dec_batch
    alpha = (2 * depth) ** 0.25
    lam_inits = [0.8 - 0.6 * math.exp(-0.3 * l) for l in range(depth)]

    c_main = 3 * hy_w + 2 * gk + 2 * gw
    c_lr = c_main
    c_d = c_lr + 2 * rank
    c_g = c_d + 3 * dw

    cond = jnp.concatenate([c_ctx[None], c, jnp.zeros((16 - ngroups, d), F32)], axis=0)
    mod = _modulation(cond, w_mod, b_mod)[:, :ngroups].reshape(depth, ngroups, N_MOD, d)

    na = 2 * dec_seq // FFT_NB
    n_s = 2 * dec_seq
    f1_half = jnp.asarray(_dft_a(na, na // 2), BF16)
    f1_full = jnp.asarray(_dft_a(na, na), BF16)
    g_s = jnp.asarray(_idft_a(na, n_s, na // 2), BF16)
    tb_np, tbi_np = _dft_b(na, FFT_NB, FFT_K1G)
    tb, tbi = jnp.asarray(tb_np, BF16), jnp.asarray(tbi_np, BF16)
    perm_np = _row_permutation(FFT_NB, FFT_K1G)
    perm, permt = jnp.asarray(perm_np, BF16), jnp.asarray(perm_np.T, BF16)
    fp_half = jnp.asarray(_dft_a(2 * seq, seq), BF16)
    fp_full = jnp.asarray(_dft_a(2 * seq, 2 * seq), BF16)
    g_p = jnp.asarray(_idft_a(2 * seq, 2 * seq, seq), BF16)
    cos_t, sin_t = _rope_tables(dec_seq, dh)
    cos_t, sin_t = jnp.asarray(cos_t), jnp.asarray(sin_t)
    half_rows = (na // 2) * FFT_NB
    lanes_s = FFT_NB * hy_w

    x = jnp.concatenate([x_prompt.reshape(mp, d), x_sample.reshape(dec_batch * dec_seq, d)], axis=0)
    h = _premod(x, mod, 0)

    fp = ((ffn_dim + 511) // 512) * 512
    ck = cache_k.reshape(dec_batch, depth, cache_k.shape[2], dw)
    cv = cache_v.reshape(dec_batch, depth, cache_v.shape[2], dw)
    new_k, new_v, new_s = [], [], []

    for l in range(depth):
        w2 = _cast_pad_rows(ffn_w2, l, 0, fp)
        hid = _ffn1(h, ffn_w1, ffn_w3, l, 0)
        x, h = _mm_ln(hid, w2, x, mod, l, 2, 0.5, alpha, l, 3, ln_g[l, 0], ln_b[l, 0])

        wi = w_in[l]
        zmain = _proj(h, wi[:, :c_main].astype(BF16), BF16)
        zlr = _proj(h, _pad_cols(wi[:, c_lr:c_d].astype(BF16), LANES), F32)
        zd = _proj(h, wi[:, c_d:c_g].astype(BF16), F32)
        gates = _proj(h, wi[:, c_g:].astype(BF16), BF16, act="sigmoid")

        u3 = _short_conv(zmain, mp, hy_conv_w[l], hy_conv_b[l], seq, dec_seq, hy_w)
        fargs = (hy_w1[l], hy_b1[l], hy_freq[l], hy_w2[l], hy_b2[l], hy_w3[l], hy_decay[l], hy_w)
        taps_s = _hyena_filter_taps(dec_seq, *fargs)
        taps_p = _hyena_filter_taps(seq, *fargs)
        kf_s = _lconv_filter(taps_s.reshape(2, na, FFT_NB, hy_w), f1_full, tb)
        kf_p = _lmm(fp_full, taps_p, lambda b: (b,), 2, F32, tn=hy_w)

        cw3 = jnp.swapaxes(hy_conv_w[l].reshape(3, 3, hy_w), 0, 1)
        cb3 = hy_conv_b[l].reshape(3, 1, hy_w)
        z1s = _lconv(zmain, mp, 0, zmain, mp, 1, cw3[0:2], cb3[0:2], True, perm, permt, f1_half, g_s,
                     tb, tbi, kf_s, 0, hy_bias[l, 0], dec_batch, dec_seq)
        ya_s = _lconv(z1s, 0, 0, zmain, mp, 2, cw3[1:3], cb3[1:3], False, perm, permt, f1_half, g_s,
                      tb, tbi, kf_s, 1, hy_bias[l, 1], dec_batch, dec_seq)
        z1 = _pconv(fp_half, g_p, u3, (0,), kf_p, 0, u3, (1,), hy_bias[l, 0], batch, seq, F32)
        ya_p = _pconv(fp_half, g_p, z1, (), kf_p, 1, u3, (2,), hy_bias[l, 1], batch, seq, BF16)
        ya = jnp.concatenate([ya_p, ya_s], axis=0)

        wa_cat = jnp.zeros((LANES, 2 * gk), F32)
        wa_cat = wa_cat.at[:rank, :gk].set(gla_wa[l, 0]).at[rank:2 * rank, gk:].set(gla_wa[l, 1])
        ba_cat = jnp.concatenate([gla_ba[l, 0], gla_ba[l, 1]]).reshape(1, 2 * gk)
        la = _gla_gates(zlr, 0, wa_cat, ba_cat)
        qcol, kcol, vcol, grcol = (3 * hy_w) // gk, (3 * hy_w) // gk + 1, (3 * hy_w + 2 * gk) // gw, \
            (3 * hy_w + 2 * gk) // gw + 1
        s0t = jnp.swapaxes(state_gla[:, l], -1, -2)
        of_p, ob_p, st_p = _gla(zmain, qcol, kcol, vcol, la, 0, batch, seq, seq, heads_g, dk_g, dv_g)
        of_s, ob_s, _ = _gla(zmain, qcol, kcol, vcol, la, mp, dec_batch, dec_seq, 512, heads_g, dk_g, dv_g,
                             s0t=s0t)
        o_f = jnp.concatenate([of_p, of_s], axis=0)
        o_b = jnp.concatenate([ob_p, ob_s], axis=0)
        yb = _gla_post(o_f, o_b, zmain, grcol, gla_norm_g[l], heads_g, dv_g)
        new_s.append(jnp.swapaxes(st_p, -1, -2))

        yc_p = _diff_attention(zd, cos_t, sin_t, 0, 0, batch, seq, seq, heads_d, dh, diff_lam[l],
                               diff_norm_g[l], lam_inits[l])
        yc_s = _diff_attention(zd, cos_t, sin_t, 1, mp, dec_batch, dec_seq, 2048, heads_d, dh, diff_lam[l],
                               diff_norm_g[l], lam_inits[l], ctx=(ck, cv, l), tq=512)
        yc = jnp.concatenate([yc_p, yc_s], axis=0)
        new_k.append(zd[:mp, dw:2 * dw].reshape(batch, seq, heads_d, 2, dh))
        new_v.append(zd[:mp, 2 * dw:3 * dw].reshape(batch, seq, heads_d, 2 * dh))

        wbr = jnp.stack([w_branch_a[l], w_branch_b[l], w_branch_c[l]]).astype(BF16)
        y = _mix(ya, yb, yc, gates, wbr)
        x, h = _mm_ln(y, w_out[l].astype(BF16), x, mod, l, 5, 1.0, alpha, l, 6, ln_g[l, 1], ln_b[l, 1],
                      tm=512)

        w2 = _cast_pad_rows(ffn_w2, l, 1, fp)
        hid = _ffn1(h, ffn_w1, ffn_w3, l, 1)
        if l + 1 < depth:
            x, h = _mm_ln(hid, w2, x, mod, l, 8, 0.5, alpha, l + 1, 0, ln_g[l, 2], ln_b[l, 2])
        else:
            xp, _ = _mm_ln(hid, w2, x, mod, l, 8, 0.5, alpha, l, 0, ln_g[l, 2], ln_b[l, 2], rows=mp)
            xs, _ = _mm_ln(hid, w2, x, mod, l, 8, 0.5, alpha, l, 0, ln_g[l, 2], ln_b[l, 2], row0=mp,
                           rows=m - mp)

    y_prompt = xp.reshape(batch, seq, d)
    y_sample = xs.reshape(dec_batch, dec_seq, d)
    return (y_prompt, y_sample, jnp.stack(new_k, axis=1), jnp.stack(new_v, axis=1),
            jnp.stack(new_s, axis=1))
```

```python
import functools
import math

import numpy as np
import jax
import jax.numpy as jnp
from jax import lax
from jax.experimental import pallas as pl
from jax.experimental.pallas import tpu as pltpu

F32 = jnp.float32
BF16 = jnp.bfloat16

GRID_W = 64
N_MOD = 9
HY_BANDS = 8
GLA_TAU = 16.0
GLA_CHUNK = 64
ROPE_THETA = 10000.0
LN_EPS = 1e-5
RMS_EPS = 1e-6

LANES = 128
MXU_DIM = 256
VMEM_BYTES_V7X = 64 * 1024 * 1024
VMEM_LIMIT = VMEM_BYTES_V7X - 8 * 1024 * 1024

GROUP = 4096
FFT_NB = 16
FFT_K1G = 8


def _cp(*sem):
    return pltpu.CompilerParams(dimension_semantics=sem, vmem_limit_bytes=VMEM_LIMIT)


def _dot(a, b):
    return jnp.dot(a, b, preferred_element_type=F32)


def _dot_nt(a, b):
    return lax.dot_general(a, b, (((1,), (1,)), ((), ())), preferred_element_type=F32)


def _dot_tn(a, b):
    return lax.dot_general(a, b, (((0,), (0,)), ((), ())), preferred_element_type=F32)


def _dot_hi(a, b):
    return jnp.dot(a, b, preferred_element_type=F32, precision=lax.Precision.HIGHEST)


def _silu(x):
    return x * jax.nn.sigmoid(x)


def _mod_kernel(c_ref, w_ref, b_ref, o_ref):
    c = c_ref[...]
    o_ref[...] = _dot(_silu(c).astype(BF16), w_ref[...].astype(BF16)) + b_ref[...]


def _modulation(cond, w_mod, b_mod):
    depth, d, n = w_mod.shape
    r = cond.shape[0]
    tn = 1024
    return pl.pallas_call(
        _mod_kernel,
        grid=(depth, n // tn),
        in_specs=[pl.BlockSpec((r, d), lambda l, j: (0, 0)),
                  pl.BlockSpec((None, d, tn), lambda l, j: (l, 0, j)),
                  pl.BlockSpec((None, 1, tn), lambda l, j: (l, 0, j))],
        out_specs=pl.BlockSpec((None, r, tn), lambda l, j: (l, 0, j)),
        out_shape=jax.ShapeDtypeStruct((depth, r, n), F32),
        compiler_params=_cp("parallel", "parallel"),
        name="mod",
    )(cond, w_mod, b_mod.reshape(depth, 1, n))


def _premod_kernel(x_ref, mod_ref, o_ref):
    o_ref[...] = (x_ref[...] * (1.0 + mod_ref[1:2, :]) + mod_ref[0:1, :]).astype(o_ref.dtype)


def _premod(x, mod, layer, tm=512):
    m, d = x.shape
    return pl.pallas_call(
        _premod_kernel,
        grid=(m // tm,),
        in_specs=[pl.BlockSpec((tm, d), lambda i: (i, 0)),
                  pl.BlockSpec((None, None, N_MOD, d), lambda i: (layer, (i * tm) // GROUP, 0, 0))],
        out_specs=pl.BlockSpec((tm, d), lambda i: (i, 0)),
        out_shape=jax.ShapeDtypeStruct((m, d), BF16),
        compiler_params=_cp("parallel"),
        name="premod",
    )(x, mod)


def _ffn1_kernel(h_ref, w1_ref, w3_ref, o_ref, w1b_ref, w3b_ref, *, tf, f_valid):
    @pl.when(pl.program_id(1) == 0)
    def _():
        col = pl.program_id(0) * tf + lax.broadcasted_iota(jnp.int32, w1_ref.shape, 1)
        keep = col < f_valid
        w1b_ref[...] = jnp.where(keep, w1_ref[...], 0.0).astype(BF16)
        w3b_ref[...] = jnp.where(keep, w3_ref[...], 0.0).astype(BF16)

    h = h_ref[...]
    a = _dot(h, w1b_ref[...])
    b = _dot(h, w3b_ref[...])
    o_ref[...] = (_silu(a) * b).astype(o_ref.dtype)


def _ffn1(h, w1, w3, layer, sub, tm=1024, tf=512):
    m, d = h.shape
    f = w1.shape[-1]
    nf = pl.cdiv(f, tf)
    wspec = pl.BlockSpec((None, None, d, tf), lambda j, i: (layer, sub, 0, j))
    return pl.pallas_call(
        functools.partial(_ffn1_kernel, tf=tf, f_valid=f),
        grid=(nf, m // tm),
        in_specs=[pl.BlockSpec((tm, d), lambda j, i: (i, 0)), wspec, wspec],
        out_specs=pl.BlockSpec((tm, tf), lambda j, i: (i, j)),
        out_shape=jax.ShapeDtypeStruct((m, nf * tf), BF16),
        scratch_shapes=[pltpu.VMEM((d, tf), BF16), pltpu.VMEM((d, tf), BF16)],
        compiler_params=_cp("parallel", "arbitrary"),
        name="ffn1",
    )(h, w1, w3)


def _cast_rows_kernel(w_ref, o_ref, *, rows_valid, tr):
    row = pl.program_id(0) * tr + lax.broadcasted_iota(jnp.int32, w_ref.shape, 0)
    o_ref[...] = jnp.where(row < rows_valid, w_ref[...], 0.0).astype(o_ref.dtype)


def _cast_pad_rows(w, layer, sub, rows_out, tr=512):
    f, d = w.shape[-2:]
    return pl.pallas_call(
        functools.partial(_cast_rows_kernel, rows_valid=f, tr=tr),
        grid=(rows_out // tr,),
        in_specs=[pl.BlockSpec((None, None, tr, d), lambda i: (layer, sub, i, 0))],
        out_specs=pl.BlockSpec((tr, d), lambda i: (i, 0)),
        out_shape=jax.ShapeDtypeStruct((rows_out, d), BF16),
        compiler_params=_cp("parallel"),
        name="cast_w2",
    )(w)


def _mm_ln_kernel(a_ref, w_ref, x_ref, mod_ref, nmod_ref, g_ref, b_ref, xo_ref, ho_ref, *,
                  gate_row, coef, alpha, nshift_row):
    gate = coef * mod_ref[gate_row:gate_row + 1, :]
    xr = alpha * x_ref[...] + gate * _dot(a_ref[...], w_ref[...])
    mu = jnp.mean(xr, axis=-1, keepdims=True)
    xc = xr - mu
    var = jnp.mean(xc * xc, axis=-1, keepdims=True)
    xn = xc * lax.rsqrt(var + LN_EPS) * g_ref[...] + b_ref[...]
    xo_ref[...] = xn
    ho_ref[...] = (xn * (1.0 + nmod_ref[nshift_row + 1:nshift_row + 2, :])
                   + nmod_ref[nshift_row:nshift_row + 1, :]).astype(ho_ref.dtype)


def _mm_ln(a, w, x, mod, layer, gate_row, coef, alpha, nlayer, nshift_row, ln_g, ln_b, tm=256,
           row0=0, rows=None):
    kdim = a.shape[1]
    m = a.shape[0] if rows is None else rows
    d = w.shape[1]
    rb0 = row0 // tm
    kern = functools.partial(_mm_ln_kernel, gate_row=gate_row, coef=coef, alpha=alpha, nshift_row=nshift_row)
    return pl.pallas_call(
        kern,
        grid=(m // tm,),
        in_specs=[pl.BlockSpec((tm, kdim), lambda i: (rb0 + i, 0)),
                  pl.BlockSpec((kdim, d), lambda i: (0, 0), pipeline_mode=pl.Buffered(1)),
                  pl.BlockSpec((tm, d), lambda i: (rb0 + i, 0)),
                  pl.BlockSpec((None, None, N_MOD, d), lambda i: (layer, ((rb0 + i) * tm) // GROUP, 0, 0)),
                  pl.BlockSpec((None, None, N_MOD, d), lambda i: (nlayer, ((rb0 + i) * tm) // GROUP, 0, 0)),
                  pl.BlockSpec((1, d), lambda i: (0, 0)),
                  pl.BlockSpec((1, d), lambda i: (0, 0))],
        out_specs=[pl.BlockSpec((tm, d), lambda i: (i, 0)),
                   pl.BlockSpec((tm, d), lambda i: (i, 0))],
        out_shape=[jax.ShapeDtypeStruct((m, d), F32), jax.ShapeDtypeStruct((m, d), BF16)],
        compiler_params=_cp("parallel"),
        name="mm_ln",
    )(a, w, x, mod, mod, ln_g.reshape(1, d), ln_b.reshape(1, d))


def _proj_kernel(h_ref, w_ref, o_ref, *, act):
    r = _dot(h_ref[...], w_ref[...])
    if act == "sigmoid":
        r = jax.nn.sigmoid(r)
    o_ref[...] = r.astype(o_ref.dtype)


def _proj(h, w, out_dtype, act=None, tm=1024, tn=1024):
    m, d = h.shape
    n = w.shape[1]
    tn = min(tn, n)
    return pl.pallas_call(
        functools.partial(_proj_kernel, act=act),
        grid=(m // tm, n // tn),
        in_specs=[pl.BlockSpec((tm, d), lambda i, j: (i, 0)),
                  pl.BlockSpec((d, tn), lambda i, j: (0, j))],
        out_specs=pl.BlockSpec((tm, tn), lambda i, j: (i, j)),
        out_shape=jax.ShapeDtypeStruct((m, n), out_dtype),
        compiler_params=_cp("parallel", "arbitrary"),
        name="proj",
    )(h, w)


def _sconv_kernel(z_ref, zp_ref, zn_ref, w_ref, b_ref, o_ref, *, tm, halo, lp, ls):
    i = pl.program_id(0)
    u = z_ref[...].astype(F32)
    prev = zp_ref[...].astype(F32)[halo - 1:halo, :]
    nxt = zn_ref[...].astype(F32)[0:1, :]
    row = lax.broadcasted_iota(jnp.int32, u.shape, 0)
    lseq = jnp.where((i * tm) // GROUP == 0, lp, ls)
    pos = (row + i * tm) & (lseq - 1)
    up = jnp.where(row == 0, prev, pltpu.roll(u, 1, 0))
    up = jnp.where(pos == 0, 0.0, up)
    un = jnp.where(row == tm - 1, nxt, pltpu.roll(u, tm - 1, 0))
    un = jnp.where(pos == lseq - 1, 0.0, un)
    w = w_ref[...]
    o_ref[...] = (up * w[0:1, :] + u * w[1:2, :] + un * w[2:3, :] + b_ref[...]).astype(o_ref.dtype)


def _short_conv(z, rows, conv_w, conv_b, lp, ls, width, tm=1024, ct=512, halo=16):
    m = rows
    nct = width // ct
    nrb = m // halo
    kern = functools.partial(_sconv_kernel, tm=tm, halo=halo, lp=lp, ls=ls)
    return pl.pallas_call(
        kern,
        grid=(m // tm, 3 * nct),
        in_specs=[pl.BlockSpec((tm, ct), lambda i, j: (i, j)),
                  pl.BlockSpec((halo, ct), lambda i, j: (jnp.maximum(i * (tm // halo) - 1, 0), j)),
                  pl.BlockSpec((halo, ct), lambda i, j: (jnp.minimum((i + 1) * (tm // halo), nrb - 1), j)),
                  pl.BlockSpec((3, ct), lambda i, j: (0, j)),
                  pl.BlockSpec((1, ct), lambda i, j: (0, j))],
        out_specs=pl.BlockSpec((None, tm, ct), lambda i, j: (j // nct, i, j % nct)),
        out_shape=jax.ShapeDtypeStruct((3, m, width), F32),
        compiler_params=_cp("parallel", "parallel"),
        name="sconv",
    )(z, z, z, conv_w, conv_b.reshape(1, -1))


def _filter_features(L):
    t = np.linspace(0.0, 1.0, L, dtype=np.float32)
    w = (np.float32(2.0 * math.pi / L) * np.arange(L, dtype=np.float32)).astype(np.float32)
    f = np.linspace(1e-4, HY_BANDS - 1, HY_BANDS, dtype=np.float32)
    wf = (w[:, None] * f).astype(np.float32)
    feats = np.concatenate([t[:, None], np.cos(wf), -np.sin(wf)], -1).astype(np.float32)
    idx = np.concatenate([np.arange(L), [0], np.arange(L - 1, 0, -1)])
    tab = np.zeros((2 * L, 32), np.float32)
    tab[:, :feats.shape[1]] = feats[idx]
    tab[:, 24] = t[idx]
    tab[:L, 25] = 1.0
    tab[L + 1:, 26] = -1.0
    return tab, feats.shape[1]


def _filter_kernel(tab_ref, w1_ref, b1_ref, fr_ref, w2_ref, b2_ref, w3_ref, dec_ref, o_ref):
    tab = tab_ref[...]
    fr = fr_ref[...]
    hdn = jnp.sin(fr[0:1, :] * (_dot_hi(tab, w1_ref[...]) + b1_ref[...]))
    hdn = jnp.sin(fr[1:2, :] * (_dot_hi(hdn, w2_ref[...]) + b2_ref[...]))
    t = tab[:, 24:25]
    sign = tab[:, 25:26] + tab[:, 26:27]
    for o in range(2):
        o_ref[o] = sign * (_dot_hi(hdn, w3_ref[o]) * jnp.exp(-t * jnp.abs(dec_ref[o])))


def _hyena_filter_taps(L, hy_w1, hy_b1, hy_freq, hy_w2, hy_b2, hy_w3, hy_decay, width, rb=256):
    tab_np, nfeat = _filter_features(L)
    fh = hy_w1.shape[1]
    w1p = jnp.zeros((32, fh), F32).at[:nfeat].set(hy_w1)
    n = 2 * L
    nhalf = L // rb
    w3d = jnp.transpose(hy_w3.reshape(fh, 2, 2, width), (1, 2, 0, 3))
    decd = hy_decay.reshape(2, 2, 1, width)
    return pl.pallas_call(
        _filter_kernel,
        grid=(n // rb,),
        in_specs=[pl.BlockSpec((rb, 32), lambda i: (i, 0)),
                  pl.BlockSpec((32, fh), lambda i: (0, 0)),
                  pl.BlockSpec((1, fh), lambda i: (0, 0)),
                  pl.BlockSpec((2, fh), lambda i: (0, 0)),
                  pl.BlockSpec((fh, fh), lambda i: (0, 0)),
                  pl.BlockSpec((1, fh), lambda i: (0, 0)),
                  pl.BlockSpec((2, None, fh, width), lambda i: (0, i // nhalf, 0, 0)),
                  pl.BlockSpec((2, None, 1, width), lambda i: (0, i // nhalf, 0, 0))],
        out_specs=pl.BlockSpec((2, rb, width), lambda i: (0, i, 0)),
        out_shape=jax.ShapeDtypeStruct((2, n, width), F32),
        compiler_params=_cp("parallel"),
        name="hyfilter",
    )(jnp.asarray(tab_np), w1p, hy_b1.reshape(1, fh), hy_freq, hy_w2, hy_b2.reshape(1, fh), w3d, decd)


def _dft_a(na, ka):
    k1 = np.arange(na // 2)[:, None].astype(np.float64)
    a = np.arange(ka)[None, :].astype(np.float64)
    th = 2.0 * np.pi * a * (k1 + 0.5) / na
    return np.concatenate([np.cos(th), -np.sin(th)], 0)


def _idft_a(na, n, rows):
    k1 = np.arange(na // 2)[None, :].astype(np.float64)
    a = np.arange(rows)[:, None].astype(np.float64)
    th = 2.0 * np.pi * a * (k1 + 0.5) / na
    return (2.0 / n) * np.concatenate([np.cos(th), -np.sin(th)], 1)


def _dft_b(na, nb, g):
    n = na * nb
    half = na // 2
    k1 = np.arange(half).astype(np.float64)
    b = np.arange(nb).astype(np.float64)
    k2 = np.arange(nb).astype(np.float64)
    phi = 2.0 * np.pi * (b[None, None, :] * k2[None, :, None] / nb
                         + b[None, None, :] * (k1[:, None, None] + 0.5) / n)
    c, s = np.cos(phi), np.sin(phi)
    ng = half // g
    r = g * nb
    fwd = np.zeros((ng, 2 * r, 2 * r))
    for q in range(g):
        rows = slice(q * nb, (q + 1) * nb)
        rows_i = slice(r + q * nb, r + (q + 1) * nb)
        cols = slice(q, r, g)
        cols_i = slice(r + q, 2 * r, g)
        cq, sq = c[q::g], s[q::g]
        fwd[:, rows, cols] = cq
        fwd[:, rows, cols_i] = sq
        fwd[:, rows_i, cols] = -sq
        fwd[:, rows_i, cols_i] = cq
    inv = np.transpose(fwd, (0, 2, 1))
    return fwd, inv


def _lmm_kernel(f_ref, x_ref, o_ref):
    o_ref[...] = _dot(f_ref[...], x_ref[...].astype(BF16)).astype(o_ref.dtype)


def _lmm(f, x, x_index, nbatch, out_dtype, tn=2048):
    mo, k = f.shape
    n = x.shape[-1]
    lead = len(x.shape) - 2
    return pl.pallas_call(
        _lmm_kernel,
        grid=(nbatch, n // tn),
        in_specs=[pl.BlockSpec((mo, k), lambda b, j: (0, 0)),
                  pl.BlockSpec((None,) * lead + (k, tn), lambda b, j: x_index(b) + (0, j))],
        out_specs=pl.BlockSpec((None, mo, tn), lambda b, j: (b, 0, j)),
        out_shape=jax.ShapeDtypeStruct((nbatch, mo, n), out_dtype),
        compiler_params=_cp("parallel", "parallel"),
        name="dft_a",
    )(f, x)


def _stage_b_rows(r_ref, r0, half, nb, kg, ct):
    return [(pl.ds(ri * half + r0, kg), slice(b * ct, (b + 1) * ct)) for ri in range(2) for b in range(nb)]


def _row_permutation(nb, al):
    p = np.zeros((nb * al, nb * al), np.float32)
    for b in range(nb):
        for a in range(al):
            p[b * al + a, a * nb + b] = 1.0
    return p


def _to_residue_major(src_ref, perm_ref, dst_ref, nb, al):
    blk = nb * al
    for i in range(src_ref.shape[0] // blk):
        z = _dot(perm_ref[...], src_ref[i * blk:(i + 1) * blk, :].astype(BF16))
        for b in range(nb):
            dst_ref[b, i * al:(i + 1) * al, :] = z[b * al:(b + 1) * al, :]


def _short_conv_residue_major(u_ref, w_ref, cb_ref, nb):
    rows = u_ref.shape[1]
    row = lax.broadcasted_iota(jnp.int32, u_ref.shape[1:], 0)
    w = w_ref[...]
    cb = cb_ref[...]
    first = u_ref[0]
    prev = jnp.where(row == 0, 0.0, pltpu.roll(u_ref[nb - 1], 1, 0))
    for b in range(nb):
        cur = first if b == 0 else u_ref[b]
        nxt = u_ref[b + 1] if b + 1 < nb else jnp.where(row == rows - 1, 0.0, pltpu.roll(first, rows - 1, 0))
        u_ref[b] = prev * w[0:1, :] + cur * w[1:2, :] + nxt * w[2:3, :] + cb
        prev = cur


def _lconv_kernel(zv_ref, zx_ref, cwv_ref, cbv_ref, cwx_ref, cbx_ref, perm_ref, permt_ref, f1_ref, g_ref,
                  tb_ref, tbi_ref, kf_ref, bias_ref, o_ref, v_ref, x_ref, r_ref, *, nb, half, kg, ct, conv_v):
    r = nb * kg
    al = perm_ref.shape[0] // nb
    _to_residue_major(zv_ref, perm_ref, v_ref, nb, al)
    _to_residue_major(zx_ref, perm_ref, x_ref, nb, al)
    if conv_v:
        _short_conv_residue_major(v_ref, cwv_ref, cbv_ref, nb)
    _short_conv_residue_major(x_ref, cwx_ref, cbx_ref, nb)
    for b in range(nb):
        r_ref[:, b * ct:(b + 1) * ct] = _dot(f1_ref[...], v_ref[b].astype(BF16))

    for g in range(half // kg):
        tiles = _stage_b_rows(r_ref, g * kg, half, nb, kg, ct)
        a = jnp.concatenate([r_ref[rs, cs] for rs, cs in tiles], axis=0).astype(BF16)
        x = _dot(tb_ref[g], a)
        kf = kf_ref[g * 2 * r:(g + 1) * 2 * r, :]
        xr, xi = x[:r], x[r:]
        kr, ki = kf[:r], kf[r:]
        y = jnp.concatenate([xr * kr - xi * ki, xr * ki + xi * kr], axis=0).astype(BF16)
        bh = _dot(tbi_ref[g], y)
        for t, (rs, cs) in enumerate(tiles):
            r_ref[rs, cs] = bh[t * kg:(t + 1) * kg, :]
    bias = bias_ref[...]
    for b in range(nb):
        y = _dot(g_ref[...], r_ref[:, b * ct:(b + 1) * ct].astype(BF16))
        v_ref[b] = x_ref[b] * (y + v_ref[b] * bias)
    blk = nb * al
    for i in range(o_ref.shape[0] // blk):
        t = jnp.concatenate([v_ref[b, i * al:(i + 1) * al, :] for b in range(nb)], axis=0).astype(BF16)
        o_ref[i * blk:(i + 1) * blk, :] = _dot(permt_ref[...], t).astype(o_ref.dtype)


def _lconv(zv, v_row0, v_col0, zx, x_row0, x_col0, conv_w, conv_b, conv_v, perm, permt, f1, g_inv, tb, tbi,
           kf, order, bias, nbatch, L, ct=256):
    c = kf.shape[-1]
    n2, _ = f1.shape
    ng, r2, _ = tb.shape
    nb = L // (n2 // 2)
    kg = r2 // (2 * nb)
    ncb = c // ct
    once = dict(pipeline_mode=pl.Buffered(1))
    kern = functools.partial(_lconv_kernel, nb=nb, half=n2 // 2, kg=kg, ct=ct, conv_v=conv_v)
    vb0, xb0 = v_row0 // L, x_row0 // L
    return pl.pallas_call(
        kern,
        grid=(ncb, nbatch),
        in_specs=[pl.BlockSpec((L, ct), lambda j, b: (vb0 + b, v_col0 * ncb + j)),
                  pl.BlockSpec((L, ct), lambda j, b: (xb0 + b, x_col0 * ncb + j)),
                  pl.BlockSpec((None, 3, ct), lambda j, b: (0, 0, j)),
                  pl.BlockSpec((None, 1, ct), lambda j, b: (0, 0, j)),
                  pl.BlockSpec((None, 3, ct), lambda j, b: (1, 0, j)),
                  pl.BlockSpec((None, 1, ct), lambda j, b: (1, 0, j)),
                  pl.BlockSpec(perm.shape, lambda j, b: (0, 0), **once),
                  pl.BlockSpec(permt.shape, lambda j, b: (0, 0), **once),
                  pl.BlockSpec(f1.shape, lambda j, b: (0, 0), **once),
                  pl.BlockSpec(g_inv.shape, lambda j, b: (0, 0), **once),
                  pl.BlockSpec(tb.shape, lambda j, b: (0, 0, 0), **once),
                  pl.BlockSpec(tbi.shape, lambda j, b: (0, 0, 0), **once),
                  pl.BlockSpec((None, ng * r2, ct), lambda j, b: (order, 0, j), **once),
                  pl.BlockSpec((1, ct), lambda j, b: (0, j))],
        out_specs=pl.BlockSpec((L, ct), lambda j, b: (b, j)),
        out_shape=jax.ShapeDtypeStruct((nbatch * L, c), BF16),
        scratch_shapes=[pltpu.VMEM((nb, L // nb, ct), F32), pltpu.VMEM((nb, L // nb, ct), F32),
                        pltpu.VMEM((n2, nb * ct), F32)],
        compiler_params=_cp("parallel", "arbitrary"),
        name="lconv",
    )(zv, zx, conv_w, conv_b, conv_w, conv_b, perm, permt, f1, g_inv, tb, tbi, kf, bias.reshape(1, c))


def _lconv_filter_kernel(x_ref, perm_ref, f1_ref, tb_ref, o_ref, t_ref, r_ref, *, nb, half, kg, ct):
    r2 = 2 * nb * kg
    _to_residue_major(x_ref, perm_ref, t_ref, nb, perm_ref.shape[0] // nb)
    for b in range(nb):
        r_ref[:, b * ct:(b + 1) * ct] = _dot(f1_ref[...], t_ref[b].astype(BF16))

    for g in range(half // kg):
        tiles = _stage_b_rows(r_ref, g * kg, half, nb, kg, ct)
        a = jnp.concatenate([r_ref[rs, cs] for rs, cs in tiles], axis=0).astype(BF16)
        o_ref[g * r2:(g + 1) * r2, :] = _dot(tb_ref[g], a)


def _lconv_filter(taps, perm, f1, tb, ct=256):
    norder, n, c = taps.shape
    n2, na = f1.shape
    nb = n // na
    ng, r2, _ = tb.shape
    kg = r2 // (2 * nb)
    kern = functools.partial(_lconv_filter_kernel, nb=nb, half=n2 // 2, kg=kg, ct=ct)
    return pl.pallas_call(
        kern,
        grid=(norder, c // ct),
        in_specs=[pl.BlockSpec((None, n, ct), lambda o, j: (o, 0, j)),
                  pl.BlockSpec(perm.shape, lambda o, j: (0, 0)),
                  pl.BlockSpec(f1.shape, lambda o, j: (0, 0)),
                  pl.BlockSpec(tb.shape, lambda o, j: (0, 0, 0))],
        out_specs=pl.BlockSpec((None, ng * r2, ct), lambda o, j: (o, 0, j)),
        out_shape=jax.ShapeDtypeStruct((norder, ng * r2, c), F32),
        scratch_shapes=[pltpu.VMEM((nb, na, ct), F32), pltpu.VMEM((n2, nb * ct), F32)],
        compiler_params=_cp("parallel", "parallel"),
        name="lconv_filter",
    )(taps, perm, f1, tb)


def _pconv_kernel(fd_ref, gd_ref, v_ref, kf_ref, xg_ref, bias_ref, o_ref, *, half):
    v = v_ref[...]
    x = _dot(fd_ref[...], v.astype(BF16))
    kf = kf_ref[...]
    xr, xi = x[:half], x[half:]
    kr, ki = kf[:half], kf[half:]
    y = jnp.concatenate([xr * kr - xi * ki, xr * ki + xi * kr], axis=0).astype(BF16)
    yt = _dot(gd_ref[...], y)
    o_ref[...] = (xg_ref[...].astype(F32) * (yt + v.astype(F32) * bias_ref[...])).astype(o_ref.dtype)


def _pconv(fd, gd, vsrc, v_lead, kf, order, xsrc, x_lead, bias, nseq, L, out_dtype, ct=512):
    c = kf.shape[-1]
    n2 = fd.shape[0]
    return pl.pallas_call(
        functools.partial(_pconv_kernel, half=n2 // 2),
        grid=(nseq, c // ct),
        in_specs=[pl.BlockSpec((n2, L), lambda s, j: (0, 0)),
                  pl.BlockSpec((L, n2), lambda s, j: (0, 0)),
                  pl.BlockSpec((None,) * len(v_lead) + (L, ct), lambda s, j: v_lead + (s, j)),
                  pl.BlockSpec((None, n2, ct), lambda s, j: (order, 0, j)),
                  pl.BlockSpec((None,) * len(x_lead) + (L, ct), lambda s, j: x_lead + (s, j)),
                  pl.BlockSpec((1, ct), lambda s, j: (0, j))],
        out_specs=pl.BlockSpec((L, ct), lambda s, j: (s, j)),
        out_shape=jax.ShapeDtypeStruct((nseq * L, c), out_dtype),
        compiler_params=_cp("parallel", "parallel"),
        name="pconv",
    )(fd, gd, vsrc, kf, xsrc, bias.reshape(1, c))


def _gate_kernel(lr_ref, wa_ref, ba_ref, o_ref):
    logits = _dot_hi(lr_ref[...], wa_ref[...]) + ba_ref[...]
    o_ref[...] = jax.nn.log_sigmoid(logits) * (1.0 / GLA_TAU)


def _gla_gates(lr, col, wa_cat, ba_cat, tm=1024):
    m = lr.shape[0]
    k, n = wa_cat.shape
    return pl.pallas_call(
        _gate_kernel,
        grid=(m // tm,),
        in_specs=[pl.BlockSpec((tm, k), lambda i: (i, col)),
                  pl.BlockSpec((k, n), lambda i: (0, 0)),
                  pl.BlockSpec((1, n), lambda i: (0, 0))],
        out_specs=pl.BlockSpec((tm, n), lambda i: (i, 0)),
        out_shape=jax.ShapeDtypeStruct((m, n), F32),
        compiler_params=_cp("parallel"),
        name="gla_gates",
    )(lr, wa_cat, ba_cat)


def _split3(x):
    hi = x.astype(BF16)
    r1 = x - hi.astype(F32)
    mid = r1.astype(BF16)
    lo = (r1 - mid.astype(F32)).astype(BF16)
    return hi, mid, lo


def _gla_dir(d, r0, q_ref, k_ref, v_ref, la_ref, o_ref, st_ref, tri, causal, ref_row, last_row,
             heads, dk, dv, scale):
    ch = GLA_CHUNK
    rows = pl.ds(pl.multiple_of(r0, ch), ch)
    la = la_ref[rows, :]
    hi, mid, lo = _split3(la)
    b = _dot(tri, hi) + _dot(tri, mid) + _dot(tri, lo)
    bref = b[ref_row:ref_row + 1, :]
    blast = b[last_row:last_row + 1, :]
    q = q_ref[rows, :].astype(F32) * scale
    k = k_ref[rows, :].astype(F32)
    qt = (q * jnp.exp(b - bref)).astype(BF16)
    kt = (k * jnp.exp(bref - b)).astype(BF16)
    qin = (q * jnp.exp(b)).astype(BF16)
    kst = (k * jnp.exp(blast - b)).astype(BF16)
    dec = jnp.exp(blast)
    for h in range(heads):
        ks = slice(h * dk, (h + 1) * dk)
        vs = slice(h * dv, (h + 1) * dv)
        att = _dot_nt(qt[:, ks], kt[:, ks])
        att = jnp.where(causal, att, 0.0).astype(BF16)
        vh = v_ref[rows, vs]
        st = st_ref[d, h]
        o_ref[rows, vs] = (_dot(att, vh) + _dot_nt(qin[:, ks], st.astype(BF16))).astype(o_ref.dtype)
        st_ref[d, h] = st * dec[:, ks] + _dot_tn(vh, kst[:, ks])


def _gla_kernel(*refs, nch, heads, dk, dv, has_s0, scale, nsq):
    seq_in = [refs[8 * q:8 * (q + 1)] for q in range(nsq)]
    rest = refs[8 * nsq:]
    if has_s0:
        s0, of, ob, st = rest
    else:
        of, ob, st = rest
        s0 = None
    ch = GLA_CHUNK

    @pl.when(pl.program_id(1) == 0)
    def _():
        if has_s0:
            st[...] = s0[...]
        else:
            st[...] = jnp.zeros_like(st)

    r_i = lax.broadcasted_iota(jnp.int32, (ch, ch), 0)
    c_i = lax.broadcasted_iota(jnp.int32, (ch, ch), 1)
    lower = r_i >= c_i
    upper = r_i <= c_i
    tri_l = jnp.where(lower, 1.0, 0.0).astype(BF16)
    tri_u = jnp.where(upper, 1.0, 0.0).astype(BF16)

    def body(c, carry):
        for q, (qf, kf, vf, laf, qb, kb, vb, lab) in enumerate(seq_in):
            _gla_dir(0, c * ch, qf, kf, vf, laf, of.at[q], st.at[q], tri_l, lower, ch // 2 - 1, ch - 1,
                     heads, dk, dv, scale)
            _gla_dir(1, (nch - 1 - c) * ch, qb, kb, vb, lab, ob.at[q], st.at[q], tri_u, upper, ch // 2, 0,
                     heads, dk, dv, scale)
        return carry

    lax.fori_loop(0, nch, body, 0)


def _gla(zmain, qcol, kcol, vcol, la, row0, nseq, L, tb, heads, dk, dv, s0=None, s0_layer=0, nsq=2):
    nblk = L // tb
    rb0 = row0 // tb
    hk, hv = heads * dk, heads * dv
    in_specs, args = [], []
    for q in range(nsq):
        def fwd(s, j, q=q):
            return rb0 + (s * nsq + q) * nblk + j

        def bwd(s, j, q=q):
            return rb0 + (s * nsq + q) * nblk + (nblk - 1 - j)

        for rowf, lcol in ((fwd, 0), (bwd, 1)):
            in_specs += [pl.BlockSpec((tb, hk), lambda s, j, rowf=rowf: (rowf(s, j), qcol)),
                         pl.BlockSpec((tb, hk), lambda s, j, rowf=rowf: (rowf(s, j), kcol)),
                         pl.BlockSpec((tb, hv), lambda s, j, rowf=rowf: (rowf(s, j), vcol)),
                         pl.BlockSpec((tb, hk), lambda s, j, rowf=rowf, lcol=lcol: (rowf(s, j), lcol))]
            args += [zmain, zmain, zmain, la]
    st_spec = pl.BlockSpec((nsq, 2, heads, dv, dk), lambda s, j: (s, 0, 0, 0, 0))
    if s0 is not None:
        in_specs.append(pl.BlockSpec((nsq, None, 2, heads, dv, dk), lambda s, j: (s, s0_layer, 0, 0, 0, 0)))
        args.append(s0)
    kern = functools.partial(_gla_kernel, nch=tb // GLA_CHUNK, heads=heads, dk=dk, dv=dv,
                             has_s0=s0 is not None, scale=dk ** -0.5, nsq=nsq)
    o_shape = jax.ShapeDtypeStruct((nseq // nsq, nsq, L, hv), BF16)
    o_f, o_b, st = pl.pallas_call(
        kern,
        grid=(nseq // nsq, nblk),
        in_specs=in_specs,
        out_specs=[pl.BlockSpec((None, nsq, tb, hv), lambda s, j: (s, 0, j, 0)),
                   pl.BlockSpec((None, nsq, tb, hv), lambda s, j: (s, 0, nblk - 1 - j, 0)),
                   st_spec],
        out_shape=[o_shape, o_shape, jax.ShapeDtypeStruct((nseq, 2, heads, dv, dk), F32)],
        compiler_params=_cp("parallel", "arbitrary"),
        name="gla",
    )(*args)
    return o_f.reshape(nseq * L, hv), o_b.reshape(nseq * L, hv), st


def _gla_post_kernel(of_ref, ob_ref, gr_ref, g_ref, o_ref, *, heads, dv):
    o = of_ref[...].astype(F32) + ob_ref[...].astype(F32)
    gate = _silu(gr_ref[...].astype(F32))
    g = g_ref[...]
    for h in range(heads):
        sl = slice(h * dv, (h + 1) * dv)
        oh = o[:, sl]
        ms = jnp.mean(oh * oh, axis=-1, keepdims=True)
        o_ref[:, sl] = (oh * lax.rsqrt(ms + RMS_EPS) * g * gate[:, sl]).astype(o_ref.dtype)


def _gla_post(o_f, o_b, zmain, grcol, norm_g, heads, dv, tm=512):
    m, hv = o_f.shape
    return pl.pallas_call(
        functools.partial(_gla_post_kernel, heads=heads, dv=dv),
        grid=(m // tm,),
        in_specs=[pl.BlockSpec((tm, hv), lambda i: (i, 0)),
                  pl.BlockSpec((tm, hv), lambda i: (i, 0)),
                  pl.BlockSpec((tm, hv), lambda i: (i, grcol)),
                  pl.BlockSpec((1, dv), lambda i: (0, 0))],
        out_specs=pl.BlockSpec((tm, hv), lambda i: (i, 0)),
        out_shape=jax.ShapeDtypeStruct((m, hv), BF16),
        compiler_params=_cp("parallel"),
        name="gla_post",
    )(o_f, o_b, zmain, norm_g.reshape(1, dv))


def _rope_tables(L, dh):
    rows = L // GRID_W
    r = np.repeat(np.arange(rows, dtype=np.float32), GRID_W)
    col = np.tile(np.arange(GRID_W, dtype=np.float32), rows)
    nf = dh // 4
    inv = (np.float32(ROPE_THETA) ** (-np.arange(nf, dtype=np.float32) / nf)).astype(np.float32)
    ang_r = (r[:, None] * inv).astype(np.float32)
    ang_c = (col[:, None] * inv).astype(np.float32)
    cos = np.concatenate([np.cos(ang_r), np.cos(ang_r), np.cos(ang_c), np.cos(ang_c)], -1)
    sin = np.concatenate([-np.sin(ang_r), np.sin(ang_r), -np.sin(ang_c), np.sin(ang_c)], -1)
    cos = np.concatenate([cos, cos], -1).astype(np.float32)
    sin = np.concatenate([sin, sin], -1).astype(np.float32)
    cos_t = np.stack([np.ones_like(cos), cos])
    sin_t = np.stack([np.zeros_like(sin), sin])
    return cos_t, sin_t


def _attn_kernel(*refs, has_ctx, lam_init, dh, bf16_exp, tq):
    if has_ctx:
        (q_ref, k_ref, v_ref, cq_ref, sq_ref, ck_ref, sk_ref, kc_ref, vc_ref, lam_ref, g_ref, o_ref,
         kt_ref, ve_ref, s0_ref, s1_ref) = refs
    else:
        (q_ref, k_ref, v_ref, cq_ref, sq_ref, ck_ref, sk_ref, lam_ref, g_ref, o_ref,
         kt_ref, ve_ref, s0_ref, s1_ref) = refs
    hw = 2 * dh
    nf = dh // 4
    lk = k_ref.shape[0]
    nsub = q_ref.shape[0] // tq
    lane = lax.broadcasted_iota(jnp.int32, (tq, hw), 1)
    first = lane < dh
    scale = dh ** -0.5 * math.log2(math.e)

    def rope(x, cos, sin):
        ln = lax.broadcasted_iota(jnp.int32, x.shape, 1)
        sw = jnp.where((ln & (2 * nf - 1)) < nf, pltpu.roll(x, hw - nf, 1), pltpu.roll(x, nf, 1))
        return x * cos + sw * sin

    @pl.when(pl.program_id(2) == 0)
    def _():
        kt_ref[...] = rope(k_ref[...], ck_ref[...], sk_ref[...]).T.astype(BF16)
        ve_ref[:, :hw] = v_ref[...].astype(BF16)
        ve_ref[:, hw:] = jnp.ones((lk, hw), BF16)

    if has_ctx:
        kct = kc_ref[...].T.astype(BF16)
        vcv = vc_ref[...].astype(BF16)
        vce = jnp.concatenate([vcv, jnp.ones_like(vcv)], axis=1)

    def scores(i, j, s_ref):
        rows = pl.ds(i * tq if isinstance(i, int) else pl.multiple_of(i * tq, tq), tq)
        q = (rope(q_ref[rows, :], cq_ref[rows, :], sq_ref[rows, :]) * scale).astype(BF16)
        qj = jnp.where(first, q, jnp.zeros_like(q)) if j == 0 else jnp.where(first, jnp.zeros_like(q), q)
        s_ref[:, :lk] = _dot(qj, kt_ref[...])
        if has_ctx:
            s_ref[:, lk:] = _dot(qj, kct)

    def prob(t):
        if bf16_exp:
            return jnp.exp2(t.astype(BF16))
        return jnp.exp2(t).astype(BF16)

    def softmax_pv(s_ref):
        s = s_ref[...]
        p = prob(s - jnp.max(s, axis=-1, keepdims=True))
        acc = _dot(p[:, :lk], ve_ref[...])
        if has_ctx:
            acc = acc + _dot(p[:, lk:], vce)
        return acc[:, :hw] / acc[:, hw:]

    lp = lam_ref[...]
    lam = (jnp.exp(jnp.sum(lp[0:1] * lp[1:2], axis=-1, keepdims=True))
           - jnp.exp(jnp.sum(lp[2:3] * lp[3:4], axis=-1, keepdims=True)) + lam_init)
    gain = g_ref[...] * (1.0 - lam_init)

    scores(0, 0, s0_ref)

    def body(i, carry):
        scores(i, 1, s1_ref)
        sm0 = softmax_pv(s0_ref)
        scores(jnp.minimum(i + 1, nsub - 1), 0, s0_ref)
        o = sm0 - lam * softmax_pv(s1_ref)
        ms = jnp.mean(o * o, axis=-1, keepdims=True)
        o_ref[pl.ds(pl.multiple_of(i * tq, tq), tq), :] = (o * lax.rsqrt(ms + RMS_EPS) * gain).astype(o_ref.dtype)
        return carry

    lax.fori_loop(0, nsub, body, 0)


def _diff_attention(zd, cos_t, sin_t, rope_kind, row0, nseq, L, tb, heads, dh, lam_p, norm_g, lam_init,
                    ctx=None, tq=256):
    hw = 2 * dh
    nqb = L // tb
    qb0 = row0 // tb
    kb0 = row0 // L
    past = 0 if ctx is None else ctx[0].shape[2]
    once = dict(pipeline_mode=pl.Buffered(1))
    qtab = pl.BlockSpec((None, tb, hw), lambda s, h, i: (rope_kind, i, 0))
    ktab = pl.BlockSpec((None, L, hw), lambda s, h, i: (rope_kind, 0, 0), **once)
    in_specs = [pl.BlockSpec((tb, hw), lambda s, h, i: (qb0 + s * nqb + i, h)),
                pl.BlockSpec((L, hw), lambda s, h, i: (kb0 + s, heads + h)),
                pl.BlockSpec((L, hw), lambda s, h, i: (kb0 + s, 2 * heads + h)),
                qtab, qtab, ktab, ktab]
    args = [zd, zd, zd, cos_t, sin_t, cos_t, sin_t]
    if ctx is not None:
        ck, cv, layer = ctx
        in_specs += [pl.BlockSpec((None, None, past, hw), lambda s, h, i: (s, layer, 0, h)),
                     pl.BlockSpec((None, None, past, hw), lambda s, h, i: (s, layer, 0, h))]
        args += [ck, cv]
    in_specs += [pl.BlockSpec(lam_p.shape, lambda s, h, i: (0, 0)),
                 pl.BlockSpec((1, hw), lambda s, h, i: (0, 0))]
    args += [lam_p, norm_g.reshape(1, hw)]
    kern = functools.partial(_attn_kernel, has_ctx=ctx is not None, lam_init=lam_init, dh=dh,
                             bf16_exp=L > 1024, tq=tq)
    return pl.pallas_call(
        kern,
        grid=(nseq, heads, nqb),
        in_specs=in_specs,
        out_specs=pl.BlockSpec((tb, hw), lambda s, h, i: (s * nqb + i, h)),
        out_shape=jax.ShapeDtypeStruct((nseq * L, heads * hw), BF16),
        scratch_shapes=[pltpu.VMEM((hw, L), BF16), pltpu.VMEM((L, 2 * hw), BF16),
                        pltpu.VMEM((tq, L + past), F32), pltpu.VMEM((tq, L + past), F32)],
        compiler_params=_cp("parallel", "parallel", "arbitrary"),
        name="diff_attn",
    )(*args)


def _mix_kernel(ya_ref, yb_ref, yc_ref, g_ref, w_ref, o_ref, acc_ref):
    k = pl.program_id(1)

    def contrib(y_ref):
        return g_ref[...].astype(F32) * _dot(y_ref[...].astype(BF16), w_ref[...])

    @pl.when(k == 0)
    def _():
        acc_ref[...] = contrib(ya_ref)

    @pl.when(k == 1)
    def _():
        acc_ref[...] += contrib(yb_ref)

    @pl.when(k == 2)
    def _():
        o_ref[...] = (acc_ref[...] + contrib(yc_ref)).astype(o_ref.dtype)


def _mix(ya, yb, yc, gates, wbr, tm=512):
    m, w = ya.shape
    d = wbr.shape[2]
    return pl.pallas_call(
        _mix_kernel,
        grid=(m // tm, 3),
        in_specs=[pl.BlockSpec((tm, w), lambda i, k: (i, 0)),
                  pl.BlockSpec((tm, w), lambda i, k: (i, 0)),
                  pl.BlockSpec((tm, w), lambda i, k: (i, 0)),
                  pl.BlockSpec((tm, d), lambda i, k: (i, k)),
                  pl.BlockSpec((None, w, d), lambda i, k: (k, 0, 0))],
        out_specs=pl.BlockSpec((tm, d), lambda i, k: (i, 0)),
        out_shape=jax.ShapeDtypeStruct((m, d), BF16),
        scratch_shapes=[pltpu.VMEM((tm, d), F32)],
        compiler_params=_cp("parallel", "arbitrary"),
        name="mix",
    )(ya, yb, yc, gates, wbr)


def _pad_cols(w, n):
    return jnp.pad(w, ((0, 0), (0, n - w.shape[1])))


def kernel(x_prompt, x_sample, cache_k, cache_v, state_gla, c, c_ctx, w_mod, b_mod, ln_g, ln_b, ffn_w1, ffn_w3, ffn_w2, w_in, hy_conv_w, hy_conv_b, hy_w1, hy_b1, hy_freq, hy_w2, hy_b2, hy_w3, hy_decay, hy_bias, gla_wa, gla_ba, gla_norm_g, diff_lam, diff_norm_g, w_branch_a, w_branch_b, w_branch_c, w_out):
    batch, seq, d = x_prompt.shape
    dec_batch, dec_seq, _ = x_sample.shape
    depth = w_mod.shape[0]
    ffn_dim = ffn_w1.shape[3]
    hy_w = hy_bias.shape[2]
    heads_g, dk_g = 4, gla_wa.shape[3] // 4
    dv_g = gla_norm_g.shape[1]
    rank = gla_wa.shape[2]
    dh = diff_lam.shape[2]
    heads_d = cache_k.shape[3]
    dw = heads_d * 2 * dh
    gw = heads_g * dv_g
    gk = heads_g * dk_g
    assert batch * seq == GROUP and dec_seq == GROUP
    mp = batch * seq
    m = mp + dec_batch * dec_seq
    ngroups = 1 + dec_batch
    alpha = (2 * depth) ** 0.25
    lam_inits = [0.8 - 0.6 * math.exp(-0.3 * l) for l in range(depth)]

    c_main = 3 * hy_w + 2 * gk + 2 * gw
    c_lr = c_main
    c_d = c_lr + 2 * rank
    c_g = c_d + 3 * dw

    cond = jnp.concatenate([c_ctx[None], c, jnp.zeros((16 - ngroups, d), F32)], axis=0)
    mod = _modulation(cond, w_mod, b_mod)[:, :ngroups].reshape(depth, ngroups, N_MOD, d)

    na = 2 * dec_seq // FFT_NB
    n_s = 2 * dec_seq
    f1_half = jnp.asarray(_dft_a(na, na // 2), BF16)
    f1_full = jnp.asarray(_dft_a(na, na), BF16)
    g_s = jnp.asarray(_idft_a(na, n_s, na // 2), BF16)
    tb_np, tbi_np = _dft_b(na, FFT_NB, FFT_K1G)
    tb, tbi = jnp.asarray(tb_np, BF16), jnp.asarray(tbi_np, BF16)
    perm_np = _row_permutation(FFT_NB, FFT_K1G)
    perm, permt = jnp.asarray(perm_np, BF16), jnp.asarray(perm_np.T, BF16)
    fp_half = jnp.asarray(_dft_a(2 * seq, seq), BF16)
    fp_full = jnp.asarray(_dft_a(2 * seq, 2 * seq), BF16)
    g_p = jnp.asarray(_idft_a(2 * seq, 2 * seq, seq), BF16)
    cos_t, sin_t = _rope_tables(dec_seq, dh)
    cos_t, sin_t = jnp.asarray(cos_t), jnp.asarray(sin_t)
    half_rows = (na // 2) * FFT_NB
    lanes_s = FFT_NB * hy_w

    x = jnp.concatenate([x_prompt.reshape(mp, d), x_sample.reshape(dec_batch * dec_seq, d)], axis=0)
    h = _premod(x, mod, 0)

    fp = ((ffn_dim + 511) // 512) * 512
    ck = cache_k.reshape(dec_batch, depth, cache_k.shape[2], dw)
    cv = cache_v.reshape(dec_batch, depth, cache_v.shape[2], dw)
    state_t = jnp.swapaxes(state_gla, -1, -2)
    new_k, new_v, new_s = [], [], []

    for l in range(depth):
        w2 = _cast_pad_rows(ffn_w2, l, 0, fp)
        hid = _ffn1(h, ffn_w1, ffn_w3, l, 0)
        x, h = _mm_ln(hid, w2, x, mod, l, 2, 0.5, alpha, l, 3, ln_g[l, 0], ln_b[l, 0])

        wi = w_in[l]
        zmain = _proj(h, wi[:, :c_main].astype(BF16), BF16)
        zlr = _proj(h, _pad_cols(wi[:, c_lr:c_d].astype(BF16), LANES), F32)
        zd = _proj(h, wi[:, c_d:c_g].astype(BF16), F32)
        gates = _proj(h, wi[:, c_g:].astype(BF16), BF16, act="sigmoid")

        u3 = _short_conv(zmain, mp, hy_conv_w[l], hy_conv_b[l], seq, dec_seq, hy_w)
        fargs = (hy_w1[l], hy_b1[l], hy_freq[l], hy_w2[l], hy_b2[l], hy_w3[l], hy_decay[l], hy_w)
        taps_s = _hyena_filter_taps(dec_seq, *fargs)
        taps_p = _hyena_filter_taps(seq, *fargs)
        kf_s = _lconv_filter(taps_s, perm, f1_full, tb)
        kf_p = _lmm(fp_full, taps_p, lambda b: (b,), 2, F32, tn=hy_w)

        cw3 = jnp.swapaxes(hy_conv_w[l].reshape(3, 3, hy_w), 0, 1)
        cb3 = hy_conv_b[l].reshape(3, 1, hy_w)
        z1s = _lconv(zmain, mp, 0, zmain, mp, 1, cw3[0:2], cb3[0:2], True, perm, permt, f1_half, g_s,
                     tb, tbi, kf_s, 0, hy_bias[l, 0], dec_batch, dec_seq)
        ya_s = _lconv(z1s, 0, 0, zmain, mp, 2, cw3[1:3], cb3[1:3], False, perm, permt, f1_half, g_s,
                      tb, tbi, kf_s, 1, hy_bias[l, 1], dec_batch, dec_seq)
        z1 = _pconv(fp_half, g_p, u3, (0,), kf_p, 0, u3, (1,), hy_bias[l, 0], batch, seq, F32)
        ya_p = _pconv(fp_half, g_p, z1, (), kf_p, 1, u3, (2,), hy_bias[l, 1], batch, seq, BF16)
        ya = jnp.concatenate([ya_p, ya_s], axis=0)

        wa_cat = jnp.zeros((LANES, 2 * gk), F32)
        wa_cat = wa_cat.at[:rank, :gk].set(gla_wa[l, 0]).at[rank:2 * rank, gk:].set(gla_wa[l, 1])
        ba_cat = jnp.concatenate([gla_ba[l, 0], gla_ba[l, 1]]).reshape(1, 2 * gk)
        la = _gla_gates(zlr, 0, wa_cat, ba_cat)
        qcol, kcol, vcol, grcol = (3 * hy_w) // gk, (3 * hy_w) // gk + 1, (3 * hy_w + 2 * gk) // gw, \
            (3 * hy_w + 2 * gk) // gw + 1
        of_p, ob_p, st_p = _gla(zmain, qcol, kcol, vcol, la, 0, batch, seq, seq, heads_g, dk_g, dv_g)
        of_s, ob_s, _ = _gla(zmain, qcol, kcol, vcol, la, mp, dec_batch, dec_seq, 512, heads_g, dk_g, dv_g,
                             s0=state_t, s0_layer=l)
        o_f = jnp.concatenate([of_p, of_s], axis=0)
        o_b = jnp.concatenate([ob_p, ob_s], axis=0)
        yb = _gla_post(o_f, o_b, zmain, grcol, gla_norm_g[l], heads_g, dv_g)
        new_s.append(jnp.swapaxes(st_p, -1, -2))

        yc_p = _diff_attention(zd, cos_t, sin_t, 0, 0, batch, seq, seq, heads_d, dh, diff_lam[l],
                               diff_norm_g[l], lam_inits[l])
        yc_s = _diff_attention(zd, cos_t, sin_t, 1, mp, dec_batch, dec_seq, 2048, heads_d, dh, diff_lam[l],
                               diff_norm_g[l], lam_inits[l], ctx=(ck, cv, l), tq=512)
        yc = jnp.concatenate([yc_p, yc_s], axis=0)
        new_k.append(zd[:mp, dw:2 * dw].reshape(batch, seq, heads_d, 2, dh))
        new_v.append(zd[:mp, 2 * dw:3 * dw].reshape(batch, seq, heads_d, 2 * dh))

        wbr = jnp.stack([w_branch_a[l], w_branch_b[l], w_branch_c[l]]).astype(BF16)
        y = _mix(ya, yb, yc, gates, wbr)
        x, h = _mm_ln(y, w_out[l].astype(BF16), x, mod, l, 5, 1.0, alpha, l, 6, ln_g[l, 1], ln_b[l, 1],
                      tm=512)

        w2 = _cast_pad_rows(ffn_w2, l, 1, fp)
        hid = _ffn1(h, ffn_w1, ffn_w3, l, 1)
        if l + 1 < depth:
            x, h = _mm_ln(hid, w2, x, mod, l, 8, 0.5, alpha, l + 1, 0, ln_g[l, 2], ln_b[l, 2])
        else:
            xp, _ = _mm_ln(hid, w2, x, mod, l, 8, 0.5, alpha, l, 0, ln_g[l, 2], ln_b[l, 2], rows=mp)
            xs, _ = _mm_ln(hid, w2, x, mod, l, 8, 0.5, alpha, l, 0, ln_g[l, 2], ln_b[l, 2], row0=mp,
                           rows=m - mp)

    y_prompt = xp.reshape(batch, seq, d)
    y_sample = xs.reshape(dec_batch, dec_seq, d)
    return (y_prompt, y_sample, jnp.stack(new_k, axis=1), jnp.stack(new_v, axis=1),
            jnp.stack(new_s, axis=1))
```

```python
import functools
import math

import numpy as np
import jax
import jax.numpy as jnp
from jax import lax
from jax.experimental import pallas as pl
from jax.experimental.pallas import tpu as pltpu

F32 = jnp.float32
BF16 = jnp.bfloat16

GRID_W = 64
N_MOD = 9
HY_BANDS = 8
GLA_TAU = 16.0
GLA_CHUNK = 64
ROPE_THETA = 10000.0
LN_EPS = 1e-5
RMS_EPS = 1e-6

LANES = 128
MXU_DIM = 256
VMEM_BYTES_V7X = 64 * 1024 * 1024
VMEM_LIMIT = VMEM_BYTES_V7X - 8 * 1024 * 1024

GROUP = 4096
FFT_NB = 16
FFT_K1G = 8


def _cp(*sem):
    return pltpu.CompilerParams(dimension_semantics=sem, vmem_limit_bytes=VMEM_LIMIT)


def _dot(a, b):
    return jnp.dot(a, b, preferred_element_type=F32)


def _dot_nt(a, b):
    return lax.dot_general(a, b, (((1,), (1,)), ((), ())), preferred_element_type=F32)


def _dot_tn(a, b):
    return lax.dot_general(a, b, (((0,), (0,)), ((), ())), preferred_element_type=F32)


def _dot_hi(a, b):
    return jnp.dot(a, b, preferred_element_type=F32, precision=lax.Precision.HIGHEST)


def _silu(x):
    return x * jax.nn.sigmoid(x)


def _mod_kernel(c_ref, w_ref, b_ref, o_ref):
    c = c_ref[...]
    o_ref[...] = _dot(_silu(c).astype(BF16), w_ref[...].astype(BF16)) + b_ref[...]


def _modulation(cond, w_mod, b_mod):
    depth, d, n = w_mod.shape
    r = cond.shape[0]
    tn = 1024
    return pl.pallas_call(
        _mod_kernel,
        grid=(depth, n // tn),
        in_specs=[pl.BlockSpec((r, d), lambda l, j: (0, 0)),
                  pl.BlockSpec((None, d, tn), lambda l, j: (l, 0, j)),
                  pl.BlockSpec((None, 1, tn), lambda l, j: (l, 0, j))],
        out_specs=pl.BlockSpec((None, r, tn), lambda l, j: (l, 0, j)),
        out_shape=jax.ShapeDtypeStruct((depth, r, n), F32),
        compiler_params=_cp("parallel", "parallel"),
        name="mod",
    )(cond, w_mod, b_mod.reshape(depth, 1, n))


def _two_source_specs(rows_a, tm, width, col=0, ndim_grid=1):
    na = rows_a // tm
    if ndim_grid == 1:
        return (pl.BlockSpec((tm, width), lambda i: (jnp.minimum(i, na - 1), col)),
                pl.BlockSpec((tm, width), lambda i: (jnp.maximum(i - na, 0), col)))
    return (pl.BlockSpec((tm, width), lambda i, k: (jnp.minimum(i, na - 1), col)),
            pl.BlockSpec((tm, width), lambda i, k: (jnp.maximum(i - na, 0), col)))


def _pick(a_ref, b_ref, rows_a, tm):
    return jnp.where(pl.program_id(0) < rows_a // tm, a_ref[...], b_ref[...])


def _premod_kernel(xa_ref, xb_ref, mod_ref, x_ref, o_ref, *, rows_a, tm):
    x = _pick(xa_ref, xb_ref, rows_a, tm)
    x_ref[...] = x
    o_ref[...] = (x * (1.0 + mod_ref[1:2, :]) + mod_ref[0:1, :]).astype(o_ref.dtype)


def _premod(xa, xb, mod, layer, tm=512):
    ra, d = xa.shape
    m = ra + xb.shape[0]
    return pl.pallas_call(
        functools.partial(_premod_kernel, rows_a=ra, tm=tm),
        grid=(m // tm,),
        in_specs=[*_two_source_specs(ra, tm, d),
                  pl.BlockSpec((None, None, N_MOD, d), lambda i: (layer, (i * tm) // GROUP, 0, 0))],
        out_specs=[pl.BlockSpec((tm, d), lambda i: (i, 0)), pl.BlockSpec((tm, d), lambda i: (i, 0))],
        out_shape=[jax.ShapeDtypeStruct((m, d), F32), jax.ShapeDtypeStruct((m, d), BF16)],
        compiler_params=_cp("arbitrary"),
        name="premod",
    )(xa, xb, mod)


def _ffn1_kernel(h_ref, w1_ref, w3_ref, o_ref, w1b_ref, w3b_ref, *, tf, f_valid):
    @pl.when(pl.program_id(1) == 0)
    def _():
        col = pl.program_id(0) * tf + lax.broadcasted_iota(jnp.int32, w1_ref.shape, 1)
        keep = col < f_valid
        w1b_ref[...] = jnp.where(keep, w1_ref[...], 0.0).astype(BF16)
        w3b_ref[...] = jnp.where(keep, w3_ref[...], 0.0).astype(BF16)

    h = h_ref[...]
    a = _dot(h, w1b_ref[...])
    b = _dot(h, w3b_ref[...])
    o_ref[...] = (_silu(a) * b).astype(o_ref.dtype)


def _ffn1(h, w1, w3, layer, sub, tm=1024, tf=512):
    m, d = h.shape
    f = w1.shape[-1]
    nf = pl.cdiv(f, tf)
    wspec = pl.BlockSpec((None, None, d, tf), lambda j, i: (layer, sub, 0, j))
    return pl.pallas_call(
        functools.partial(_ffn1_kernel, tf=tf, f_valid=f),
        grid=(nf, m // tm),
        in_specs=[pl.BlockSpec((tm, d), lambda j, i: (i, 0)), wspec, wspec],
        out_specs=pl.BlockSpec((tm, tf), lambda j, i: (i, j)),
        out_shape=jax.ShapeDtypeStruct((m, nf * tf), BF16),
        scratch_shapes=[pltpu.VMEM((d, tf), BF16), pltpu.VMEM((d, tf), BF16)],
        compiler_params=_cp("parallel", "arbitrary"),
        name="ffn1",
    )(h, w1, w3)


def _cast_rows_kernel(w_ref, o_ref, *, rows_valid, tr):
    row = pl.program_id(0) * tr + lax.broadcasted_iota(jnp.int32, w_ref.shape, 0)
    o_ref[...] = jnp.where(row < rows_valid, w_ref[...], 0.0).astype(o_ref.dtype)


def _cast_pad_rows(w, layer, sub, rows_out, tr=512):
    f, d = w.shape[-2:]
    return pl.pallas_call(
        functools.partial(_cast_rows_kernel, rows_valid=f, tr=tr),
        grid=(rows_out // tr,),
        in_specs=[pl.BlockSpec((None, None, tr, d), lambda i: (layer, sub, i, 0))],
        out_specs=pl.BlockSpec((tr, d), lambda i: (i, 0)),
        out_shape=jax.ShapeDtypeStruct((rows_out, d), BF16),
        compiler_params=_cp("parallel"),
        name="cast_w2",
    )(w)


def _mm_ln_kernel(a_ref, w_ref, x_ref, mod_ref, nmod_ref, g_ref, b_ref, xo_ref, ho_ref, *,
                  gate_row, coef, alpha, nshift_row):
    gate = coef * mod_ref[gate_row:gate_row + 1, :]
    xr = alpha * x_ref[...] + gate * _dot(a_ref[...], w_ref[...])
    mu = jnp.mean(xr, axis=-1, keepdims=True)
    xc = xr - mu
    var = jnp.mean(xc * xc, axis=-1, keepdims=True)
    xn = xc * lax.rsqrt(var + LN_EPS) * g_ref[...] + b_ref[...]
    xo_ref[...] = xn
    ho_ref[...] = (xn * (1.0 + nmod_ref[nshift_row + 1:nshift_row + 2, :])
                   + nmod_ref[nshift_row:nshift_row + 1, :]).astype(ho_ref.dtype)


def _mm_ln(a, w, x, mod, layer, gate_row, coef, alpha, nlayer, nshift_row, ln_g, ln_b, tm=256,
           row0=0, rows=None):
    kdim = a.shape[1]
    m = a.shape[0] if rows is None else rows
    d = w.shape[1]
    rb0 = row0 // tm
    kern = functools.partial(_mm_ln_kernel, gate_row=gate_row, coef=coef, alpha=alpha, nshift_row=nshift_row)
    return pl.pallas_call(
        kern,
        grid=(m // tm,),
        in_specs=[pl.BlockSpec((tm, kdim), lambda i: (rb0 + i, 0)),
                  pl.BlockSpec((kdim, d), lambda i: (0, 0), pipeline_mode=pl.Buffered(1)),
                  pl.BlockSpec((tm, d), lambda i: (rb0 + i, 0)),
                  pl.BlockSpec((None, None, N_MOD, d), lambda i: (layer, ((rb0 + i) * tm) // GROUP, 0, 0)),
                  pl.BlockSpec((None, None, N_MOD, d), lambda i: (nlayer, ((rb0 + i) * tm) // GROUP, 0, 0)),
                  pl.BlockSpec((1, d), lambda i: (0, 0)),
                  pl.BlockSpec((1, d), lambda i: (0, 0))],
        out_specs=[pl.BlockSpec((tm, d), lambda i: (i, 0)),
                   pl.BlockSpec((tm, d), lambda i: (i, 0))],
        out_shape=[jax.ShapeDtypeStruct((m, d), F32), jax.ShapeDtypeStruct((m, d), BF16)],
        compiler_params=_cp("parallel"),
        name="mm_ln",
    )(a, w, x, mod, mod, ln_g.reshape(1, d), ln_b.reshape(1, d))


def _proj_kernel(h_ref, w_ref, o_ref, *, act):
    r = _dot(h_ref[...], w_ref[...])
    if act == "sigmoid":
        r = jax.nn.sigmoid(r)
    o_ref[...] = r.astype(o_ref.dtype)


def _proj(h, w, out_dtype, act=None, tm=1024, tn=1024):
    m, d = h.shape
    n = w.shape[1]
    tn = min(tn, n)
    return pl.pallas_call(
        functools.partial(_proj_kernel, act=act),
        grid=(m // tm, n // tn),
        in_specs=[pl.BlockSpec((tm, d), lambda i, j: (i, 0)),
                  pl.BlockSpec((d, tn), lambda i, j: (0, j))],
        out_specs=pl.BlockSpec((tm, tn), lambda i, j: (i, j)),
        out_shape=jax.ShapeDtypeStruct((m, n), out_dtype),
        compiler_params=_cp("parallel", "arbitrary"),
        name="proj",
    )(h, w)


def _sconv_kernel(z_ref, zp_ref, zn_ref, w_ref, b_ref, o_ref, *, tm, halo, lp, ls):
    i = pl.program_id(0)
    u = z_ref[...].astype(F32)
    prev = zp_ref[...].astype(F32)[halo - 1:halo, :]
    nxt = zn_ref[...].astype(F32)[0:1, :]
    row = lax.broadcasted_iota(jnp.int32, u.shape, 0)
    lseq = jnp.where((i * tm) // GROUP == 0, lp, ls)
    pos = (row + i * tm) & (lseq - 1)
    up = jnp.where(row == 0, prev, pltpu.roll(u, 1, 0))
    up = jnp.where(pos == 0, 0.0, up)
    un = jnp.where(row == tm - 1, nxt, pltpu.roll(u, tm - 1, 0))
    un = jnp.where(pos == lseq - 1, 0.0, un)
    w = w_ref[...]
    o_ref[...] = (up * w[0:1, :] + u * w[1:2, :] + un * w[2:3, :] + b_ref[...]).astype(o_ref.dtype)


def _short_conv(z, rows, conv_w, conv_b, lp, ls, width, tm=1024, ct=512, halo=16):
    m = rows
    nct = width // ct
    nrb = m // halo
    kern = functools.partial(_sconv_kernel, tm=tm, halo=halo, lp=lp, ls=ls)
    return pl.pallas_call(
        kern,
        grid=(m // tm, 3 * nct),
        in_specs=[pl.BlockSpec((tm, ct), lambda i, j: (i, j)),
                  pl.BlockSpec((halo, ct), lambda i, j: (jnp.maximum(i * (tm // halo) - 1, 0), j)),
                  pl.BlockSpec((halo, ct), lambda i, j: (jnp.minimum((i + 1) * (tm // halo), nrb - 1), j)),
                  pl.BlockSpec((3, ct), lambda i, j: (0, j)),
                  pl.BlockSpec((1, ct), lambda i, j: (0, j))],
        out_specs=pl.BlockSpec((None, tm, ct), lambda i, j: (j // nct, i, j % nct)),
        out_shape=jax.ShapeDtypeStruct((3, m, width), F32),
        compiler_params=_cp("parallel", "parallel"),
        name="sconv",
    )(z, z, z, conv_w, conv_b.reshape(1, -1))


def _filter_features(L):
    t = np.linspace(0.0, 1.0, L, dtype=np.float32)
    w = (np.float32(2.0 * math.pi / L) * np.arange(L, dtype=np.float32)).astype(np.float32)
    f = np.linspace(1e-4, HY_BANDS - 1, HY_BANDS, dtype=np.float32)
    wf = (w[:, None] * f).astype(np.float32)
    feats = np.concatenate([t[:, None], np.cos(wf), -np.sin(wf)], -1).astype(np.float32)
    idx = np.concatenate([np.arange(L), [0], np.arange(L - 1, 0, -1)])
    tab = np.zeros((2 * L, 32), np.float32)
    tab[:, :feats.shape[1]] = feats[idx]
    tab[:, 24] = t[idx]
    tab[:L, 25] = 1.0
    tab[L + 1:, 26] = -1.0
    return tab, feats.shape[1]


def _filter_kernel(tab_ref, w1_ref, b1_ref, fr_ref, w2_ref, b2_ref, w3_ref, dec_ref, o_ref):
    tab = tab_ref[...]
    fr = fr_ref[...]
    hdn = jnp.sin(fr[0:1, :] * (_dot_hi(tab, w1_ref[...]) + b1_ref[...]))
    hdn = jnp.sin(fr[1:2, :] * (_dot_hi(hdn, w2_ref[...]) + b2_ref[...]))
    t = tab[:, 24:25]
    sign = tab[:, 25:26] + tab[:, 26:27]
    for o in range(2):
        o_ref[o] = sign * (_dot_hi(hdn, w3_ref[o]) * jnp.exp(-t * jnp.abs(dec_ref[o])))


def _hyena_filter_taps(L, hy_w1, hy_b1, hy_freq, hy_w2, hy_b2, hy_w3, hy_decay, width, rb=256):
    tab_np, nfeat = _filter_features(L)
    fh = hy_w1.shape[1]
    w1p = jnp.zeros((32, fh), F32).at[:nfeat].set(hy_w1)
    n = 2 * L
    nhalf = L // rb
    w3d = jnp.transpose(hy_w3.reshape(fh, 2, 2, width), (1, 2, 0, 3))
    decd = hy_decay.reshape(2, 2, 1, width)
    return pl.pallas_call(
        _filter_kernel,
        grid=(n // rb,),
        in_specs=[pl.BlockSpec((rb, 32), lambda i: (i, 0)),
                  pl.BlockSpec((32, fh), lambda i: (0, 0)),
                  pl.BlockSpec((1, fh), lambda i: (0, 0)),
                  pl.BlockSpec((2, fh), lambda i: (0, 0)),
                  pl.BlockSpec((fh, fh), lambda i: (0, 0)),
                  pl.BlockSpec((1, fh), lambda i: (0, 0)),
                  pl.BlockSpec((2, None, fh, width), lambda i: (0, i // nhalf, 0, 0)),
                  pl.BlockSpec((2, None, 1, width), lambda i: (0, i // nhalf, 0, 0))],
        out_specs=pl.BlockSpec((2, rb, width), lambda i: (0, i, 0)),
        out_shape=jax.ShapeDtypeStruct((2, n, width), F32),
        compiler_params=_cp("parallel"),
        name="hyfilter",
    )(jnp.asarray(tab_np), w1p, hy_b1.reshape(1, fh), hy_freq, hy_w2, hy_b2.reshape(1, fh), w3d, decd)


def _dft_a(na, ka):
    k1 = np.arange(na // 2)[:, None].astype(np.float64)
    a = np.arange(ka)[None, :].astype(np.float64)
    th = 2.0 * np.pi * a * (k1 + 0.5) / na
    return np.concatenate([np.cos(th), -np.sin(th)], 0)


def _idft_a(na, n, rows):
    k1 = np.arange(na // 2)[None, :].astype(np.float64)
    a = np.arange(rows)[:, None].astype(np.float64)
    th = 2.0 * np.pi * a * (k1 + 0.5) / na
    return (2.0 / n) * np.concatenate([np.cos(th), -np.sin(th)], 1)


def _dft_b(na, nb, g):
    n = na * nb
    half = na // 2
    k1 = np.arange(half).astype(np.float64)
    b = np.arange(nb).astype(np.float64)
    k2 = np.arange(nb).astype(np.float64)
    phi = 2.0 * np.pi * (b[None, None, :] * k2[None, :, None] / nb
                         + b[None, None, :] * (k1[:, None, None] + 0.5) / n)
    c, s = np.cos(phi), np.sin(phi)
    ng = half // g
    r = g * nb
    fwd = np.zeros((ng, 2 * r, 2 * r))
    for q in range(g):
        rows = slice(q * nb, (q + 1) * nb)
        rows_i = slice(r + q * nb, r + (q + 1) * nb)
        cols = slice(q, r, g)
        cols_i = slice(r + q, 2 * r, g)
        cq, sq = c[q::g], s[q::g]
        fwd[:, rows, cols] = cq
        fwd[:, rows, cols_i] = sq
        fwd[:, rows_i, cols] = -sq
        fwd[:, rows_i, cols_i] = cq
    inv = np.transpose(fwd, (0, 2, 1))
    return fwd, inv


def _lmm_kernel(f_ref, x_ref, o_ref):
    o_ref[...] = _dot(f_ref[...], x_ref[...].astype(BF16)).astype(o_ref.dtype)


def _lmm(f, x, x_index, nbatch, out_dtype, tn=2048):
    mo, k = f.shape
    n = x.shape[-1]
    lead = len(x.shape) - 2
    return pl.pallas_call(
        _lmm_kernel,
        grid=(nbatch, n // tn),
        in_specs=[pl.BlockSpec((mo, k), lambda b, j: (0, 0)),
                  pl.BlockSpec((None,) * lead + (k, tn), lambda b, j: x_index(b) + (0, j))],
        out_specs=pl.BlockSpec((None, mo, tn), lambda b, j: (b, 0, j)),
        out_shape=jax.ShapeDtypeStruct((nbatch, mo, n), out_dtype),
        compiler_params=_cp("parallel", "parallel"),
        name="dft_a",
    )(f, x)


def _stage_b_rows(r_ref, r0, half, nb, kg, ct):
    return [(pl.ds(ri * half + r0, kg), slice(b * ct, (b + 1) * ct)) for ri in range(2) for b in range(nb)]


def _row_permutation(nb, al):
    p = np.zeros((nb * al, nb * al), np.float32)
    for b in range(nb):
        for a in range(al):
            p[b * al + a, a * nb + b] = 1.0
    return p


def _to_residue_major(src_ref, perm_ref, dst_ref, nb, al):
    blk = nb * al
    for i in range(src_ref.shape[0] // blk):
        z = _dot(perm_ref[...], src_ref[i * blk:(i + 1) * blk, :].astype(BF16))
        for b in range(nb):
            dst_ref[b, i * al:(i + 1) * al, :] = z[b * al:(b + 1) * al, :]


def _short_conv_residue_major(u_ref, w_ref, cb_ref, nb):
    rows = u_ref.shape[1]
    row = lax.broadcasted_iota(jnp.int32, u_ref.shape[1:], 0)
    w = w_ref[...]
    cb = cb_ref[...]
    first = u_ref[0]
    prev = jnp.where(row == 0, 0.0, pltpu.roll(u_ref[nb - 1], 1, 0))
    for b in range(nb):
        cur = first if b == 0 else u_ref[b]
        nxt = u_ref[b + 1] if b + 1 < nb else jnp.where(row == rows - 1, 0.0, pltpu.roll(first, rows - 1, 0))
        u_ref[b] = prev * w[0:1, :] + cur * w[1:2, :] + nxt * w[2:3, :] + cb
        prev = cur


def _lconv_kernel(zv_ref, zx_ref, cwv_ref, cbv_ref, cwx_ref, cbx_ref, perm_ref, permt_ref, f1_ref, g_ref,
                  tb_ref, tbi_ref, kf_ref, bias_ref, o_ref, v_ref, x_ref, r_ref, *, nb, half, kg, ct, conv_v):
    r = nb * kg
    al = perm_ref.shape[0] // nb
    _to_residue_major(zv_ref, perm_ref, v_ref, nb, al)
    _to_residue_major(zx_ref, perm_ref, x_ref, nb, al)
    if conv_v:
        _short_conv_residue_major(v_ref, cwv_ref, cbv_ref, nb)
    _short_conv_residue_major(x_ref, cwx_ref, cbx_ref, nb)
    for b in range(nb):
        r_ref[:, b * ct:(b + 1) * ct] = _dot(f1_ref[...], v_ref[b].astype(BF16))

    for g in range(half // kg):
        tiles = _stage_b_rows(r_ref, g * kg, half, nb, kg, ct)
        a = jnp.concatenate([r_ref[rs, cs] for rs, cs in tiles], axis=0).astype(BF16)
        x = _dot(tb_ref[g], a)
        kf = kf_ref[g * 2 * r:(g + 1) * 2 * r, :]
        xr, xi = x[:r], x[r:]
        kr, ki = kf[:r], kf[r:]
        y = jnp.concatenate([xr * kr - xi * ki, xr * ki + xi * kr], axis=0).astype(BF16)
        bh = _dot(tbi_ref[g], y)
        for t, (rs, cs) in enumerate(tiles):
            r_ref[rs, cs] = bh[t * kg:(t + 1) * kg, :]
    bias = bias_ref[...]
    for b in range(nb):
        y = _dot(g_ref[...], r_ref[:, b * ct:(b + 1) * ct].astype(BF16))
        v_ref[b] = x_ref[b] * (y + v_ref[b] * bias)
    blk = nb * al
    for i in range(o_ref.shape[0] // blk):
        t = jnp.concatenate([v_ref[b, i * al:(i + 1) * al, :] for b in range(nb)], axis=0).astype(BF16)
        o_ref[i * blk:(i + 1) * blk, :] = _dot(permt_ref[...], t).astype(o_ref.dtype)


def _lconv(zv, v_row0, v_col0, zx, x_row0, x_col0, conv_w, conv_b, conv_v, perm, permt, f1, g_inv, tb, tbi,
           kf, order, bias, nbatch, L, ct=256):
    c = kf.shape[-1]
    n2, _ = f1.shape
    ng, r2, _ = tb.shape
    nb = L // (n2 // 2)
    kg = r2 // (2 * nb)
    ncb = c // ct
    once = dict(pipeline_mode=pl.Buffered(1))
    kern = functools.partial(_lconv_kernel, nb=nb, half=n2 // 2, kg=kg, ct=ct, conv_v=conv_v)
    vb0, xb0 = v_row0 // L, x_row0 // L
    return pl.pallas_call(
        kern,
        grid=(ncb, nbatch),
        in_specs=[pl.BlockSpec((L, ct), lambda j, b: (vb0 + b, v_col0 * ncb + j)),
                  pl.BlockSpec((L, ct), lambda j, b: (xb0 + b, x_col0 * ncb + j)),
                  pl.BlockSpec((None, 3, ct), lambda j, b: (0, 0, j)),
                  pl.BlockSpec((None, 1, ct), lambda j, b: (0, 0, j)),
                  pl.BlockSpec((None, 3, ct), lambda j, b: (1, 0, j)),
                  pl.BlockSpec((None, 1, ct), lambda j, b: (1, 0, j)),
                  pl.BlockSpec(perm.shape, lambda j, b: (0, 0), **once),
                  pl.BlockSpec(permt.shape, lambda j, b: (0, 0), **once),
                  pl.BlockSpec(f1.shape, lambda j, b: (0, 0), **once),
                  pl.BlockSpec(g_inv.shape, lambda j, b: (0, 0), **once),
                  pl.BlockSpec(tb.shape, lambda j, b: (0, 0, 0), **once),
                  pl.BlockSpec(tbi.shape, lambda j, b: (0, 0, 0), **once),
                  pl.BlockSpec((None, ng * r2, ct), lambda j, b: (order, 0, j), **once),
                  pl.BlockSpec((1, ct), lambda j, b: (0, j))],
        out_specs=pl.BlockSpec((L, ct), lambda j, b: (b, j)),
        out_shape=jax.ShapeDtypeStruct((nbatch * L, c), BF16),
        scratch_shapes=[pltpu.VMEM((nb, L // nb, ct), F32), pltpu.VMEM((nb, L // nb, ct), F32),
                        pltpu.VMEM((n2, nb * ct), F32)],
        compiler_params=_cp("parallel", "arbitrary"),
        name="lconv",
    )(zv, zx, conv_w, conv_b, conv_w, conv_b, perm, permt, f1, g_inv, tb, tbi, kf, bias.reshape(1, c))


def _lconv_filter_kernel(x_ref, perm_ref, f1_ref, tb_ref, o_ref, t_ref, r_ref, *, nb, half, kg, ct):
    r2 = 2 * nb * kg
    _to_residue_major(x_ref, perm_ref, t_ref, nb, perm_ref.shape[0] // nb)
    for b in range(nb):
        r_ref[:, b * ct:(b + 1) * ct] = _dot(f1_ref[...], t_ref[b].astype(BF16))

    for g in range(half // kg):
        tiles = _stage_b_rows(r_ref, g * kg, half, nb, kg, ct)
        a = jnp.concatenate([r_ref[rs, cs] for rs, cs in tiles], axis=0).astype(BF16)
        o_ref[g * r2:(g + 1) * r2, :] = _dot(tb_ref[g], a)


def _lconv_filter(taps, perm, f1, tb, ct=256):
    norder, n, c = taps.shape
    n2, na = f1.shape
    nb = n // na
    ng, r2, _ = tb.shape
    kg = r2 // (2 * nb)
    kern = functools.partial(_lconv_filter_kernel, nb=nb, half=n2 // 2, kg=kg, ct=ct)
    return pl.pallas_call(
        kern,
        grid=(norder, c // ct),
        in_specs=[pl.BlockSpec((None, n, ct), lambda o, j: (o, 0, j)),
                  pl.BlockSpec(perm.shape, lambda o, j: (0, 0)),
                  pl.BlockSpec(f1.shape, lambda o, j: (0, 0)),
                  pl.BlockSpec(tb.shape, lambda o, j: (0, 0, 0))],
        out_specs=pl.BlockSpec((None, ng * r2, ct), lambda o, j: (o, 0, j)),
        out_shape=jax.ShapeDtypeStruct((norder, ng * r2, c), F32),
        scratch_shapes=[pltpu.VMEM((nb, na, ct), F32), pltpu.VMEM((n2, nb * ct), F32)],
        compiler_params=_cp("parallel", "parallel"),
        name="lconv_filter",
    )(taps, perm, f1, tb)


def _pconv_kernel(fd_ref, gd_ref, v_ref, kf_ref, xg_ref, bias_ref, o_ref, *, half):
    v = v_ref[...]
    x = _dot(fd_ref[...], v.astype(BF16))
    kf = kf_ref[...]
    xr, xi = x[:half], x[half:]
    kr, ki = kf[:half], kf[half:]
    y = jnp.concatenate([xr * kr - xi * ki, xr * ki + xi * kr], axis=0).astype(BF16)
    yt = _dot(gd_ref[...], y)
    o_ref[...] = (xg_ref[...].astype(F32) * (yt + v.astype(F32) * bias_ref[...])).astype(o_ref.dtype)


def _pconv(fd, gd, vsrc, v_lead, kf, order, xsrc, x_lead, bias, nseq, L, out_dtype, ct=512):
    c = kf.shape[-1]
    n2 = fd.shape[0]
    return pl.pallas_call(
        functools.partial(_pconv_kernel, half=n2 // 2),
        grid=(nseq, c // ct),
        in_specs=[pl.BlockSpec((n2, L), lambda s, j: (0, 0)),
                  pl.BlockSpec((L, n2), lambda s, j: (0, 0)),
                  pl.BlockSpec((None,) * len(v_lead) + (L, ct), lambda s, j: v_lead + (s, j)),
                  pl.BlockSpec((None, n2, ct), lambda s, j: (order, 0, j)),
                  pl.BlockSpec((None,) * len(x_lead) + (L, ct), lambda s, j: x_lead + (s, j)),
                  pl.BlockSpec((1, ct), lambda s, j: (0, j))],
        out_specs=pl.BlockSpec((L, ct), lambda s, j: (s, j)),
        out_shape=jax.ShapeDtypeStruct((nseq * L, c), out_dtype),
        compiler_params=_cp("parallel", "parallel"),
        name="pconv",
    )(fd, gd, vsrc, kf, xsrc, bias.reshape(1, c))


def _gate_kernel(lr_ref, wa_ref, ba_ref, o_ref):
    logits = _dot_hi(lr_ref[...], wa_ref[...]) + ba_ref[...]
    o_ref[...] = jax.nn.log_sigmoid(logits) * (1.0 / GLA_TAU)


def _gla_gates(lr, col, wa_cat, ba_cat, tm=1024):
    m = lr.shape[0]
    k, n = wa_cat.shape
    return pl.pallas_call(
        _gate_kernel,
        grid=(m // tm,),
        in_specs=[pl.BlockSpec((tm, k), lambda i: (i, col)),
                  pl.BlockSpec((k, n), lambda i: (0, 0)),
                  pl.BlockSpec((1, n), lambda i: (0, 0))],
        out_specs=pl.BlockSpec((tm, n), lambda i: (i, 0)),
        out_shape=jax.ShapeDtypeStruct((m, n), F32),
        compiler_params=_cp("parallel"),
        name="gla_gates",
    )(lr, wa_cat, ba_cat)


def _split3(x):
    hi = x.astype(BF16)
    r1 = x - hi.astype(F32)
    mid = r1.astype(BF16)
    lo = (r1 - mid.astype(F32)).astype(BF16)
    return hi, mid, lo


def _gla_dir(d, r0, q_ref, k_ref, v_ref, la_ref, o_ref, st_ref, tri, causal, ref_row, last_row,
             heads, dk, dv, scale):
    ch = GLA_CHUNK
    rows = pl.ds(pl.multiple_of(r0, ch), ch)
    la = la_ref[rows, :]
    hi, mid, lo = _split3(la)
    b = _dot(tri, hi) + _dot(tri, mid) + _dot(tri, lo)
    bref = b[ref_row:ref_row + 1, :]
    blast = b[last_row:last_row + 1, :]
    q = q_ref[rows, :].astype(F32) * scale
    k = k_ref[rows, :].astype(F32)
    qt = (q * jnp.exp(b - bref)).astype(BF16)
    kt = (k * jnp.exp(bref - b)).astype(BF16)
    qin = (q * jnp.exp(b)).astype(BF16)
    kst = (k * jnp.exp(blast - b)).astype(BF16)
    dec = jnp.exp(blast)
    for h in range(heads):
        ks = slice(h * dk, (h + 1) * dk)
        vs = slice(h * dv, (h + 1) * dv)
        att = _dot_nt(qt[:, ks], kt[:, ks])
        att = jnp.where(causal, att, 0.0).astype(BF16)
        vh = v_ref[rows, vs]
        st = st_ref[d, h]
        o_ref[rows, vs] = (_dot(att, vh) + _dot_nt(qin[:, ks], st.astype(BF16))).astype(o_ref.dtype)
        st_ref[d, h] = st * dec[:, ks] + _dot_tn(vh, kst[:, ks])


def _gla_kernel(*refs, nch, heads, dk, dv, has_s0, scale, nsq):
    seq_in = [refs[8 * q:8 * (q + 1)] for q in range(nsq)]
    rest = refs[8 * nsq:]
    if has_s0:
        s0, of, ob, st = rest
    else:
        of, ob, st = rest
        s0 = None
    ch = GLA_CHUNK

    @pl.when(pl.program_id(1) == 0)
    def _():
        if has_s0:
            st[...] = s0[...]
        else:
            st[...] = jnp.zeros_like(st)

    r_i = lax.broadcasted_iota(jnp.int32, (ch, ch), 0)
    c_i = lax.broadcasted_iota(jnp.int32, (ch, ch), 1)
    lower = r_i >= c_i
    upper = r_i <= c_i
    tri_l = jnp.where(lower, 1.0, 0.0).astype(BF16)
    tri_u = jnp.where(upper, 1.0, 0.0).astype(BF16)

    def body(c, carry):
        for q, (qf, kf, vf, laf, qb, kb, vb, lab) in enumerate(seq_in):
            _gla_dir(0, c * ch, qf, kf, vf, laf, of.at[q], st.at[q], tri_l, lower, ch // 2 - 1, ch - 1,
                     heads, dk, dv, scale)
            _gla_dir(1, (nch - 1 - c) * ch, qb, kb, vb, lab, ob.at[q], st.at[q], tri_u, upper, ch // 2, 0,
                     heads, dk, dv, scale)
        return carry

    lax.fori_loop(0, nch, body, 0)


def _gla(zmain, qcol, kcol, vcol, la, row0, nseq, L, tb, heads, dk, dv, s0=None, s0_layer=0, nsq=2):
    nblk = L // tb
    rb0 = row0 // tb
    hk, hv = heads * dk, heads * dv
    in_specs, args = [], []
    for q in range(nsq):
        def fwd(s, j, q=q):
            return rb0 + (s * nsq + q) * nblk + j

        def bwd(s, j, q=q):
            return rb0 + (s * nsq + q) * nblk + (nblk - 1 - j)

        for rowf, lcol in ((fwd, 0), (bwd, 1)):
            in_specs += [pl.BlockSpec((tb, hk), lambda s, j, rowf=rowf: (rowf(s, j), qcol)),
                         pl.BlockSpec((tb, hk), lambda s, j, rowf=rowf: (rowf(s, j), kcol)),
                         pl.BlockSpec((tb, hv), lambda s, j, rowf=rowf: (rowf(s, j), vcol)),
                         pl.BlockSpec((tb, hk), lambda s, j, rowf=rowf, lcol=lcol: (rowf(s, j), lcol))]
            args += [zmain, zmain, zmain, la]
    st_spec = pl.BlockSpec((nsq, 2, heads, dv, dk), lambda s, j: (s, 0, 0, 0, 0))
    if s0 is not None:
        in_specs.append(pl.BlockSpec((nsq, None, 2, heads, dv, dk), lambda s, j: (s, s0_layer, 0, 0, 0, 0)))
        args.append(s0)
    kern = functools.partial(_gla_kernel, nch=tb // GLA_CHUNK, heads=heads, dk=dk, dv=dv,
                             has_s0=s0 is not None, scale=dk ** -0.5, nsq=nsq)
    o_shape = jax.ShapeDtypeStruct((nseq // nsq, nsq, L, hv), BF16)
    o_f, o_b, st = pl.pallas_call(
        kern,
        grid=(nseq // nsq, nblk),
        in_specs=in_specs,
        out_specs=[pl.BlockSpec((None, nsq, tb, hv), lambda s, j: (s, 0, j, 0)),
                   pl.BlockSpec((None, nsq, tb, hv), lambda s, j: (s, 0, nblk - 1 - j, 0)),
                   st_spec],
        out_shape=[o_shape, o_shape, jax.ShapeDtypeStruct((nseq, 2, heads, dv, dk), F32)],
        compiler_params=_cp("parallel", "arbitrary"),
        name="gla",
    )(*args)
    return o_f.reshape(nseq * L, hv), o_b.reshape(nseq * L, hv), st


def _gla_post_kernel(ofa_ref, ofb_ref, oba_ref, obb_ref, gr_ref, g_ref, o_ref, *, heads, dv, rows_a, tm):
    o = (_pick(ofa_ref, ofb_ref, rows_a, tm).astype(F32) + _pick(oba_ref, obb_ref, rows_a, tm).astype(F32))
    gate = _silu(gr_ref[...].astype(F32))
    g = g_ref[...]
    for h in range(heads):
        sl = slice(h * dv, (h + 1) * dv)
        oh = o[:, sl]
        ms = jnp.mean(oh * oh, axis=-1, keepdims=True)
        o_ref[:, sl] = (oh * lax.rsqrt(ms + RMS_EPS) * g * gate[:, sl]).astype(o_ref.dtype)


def _gla_post(of_a, of_b, ob_a, ob_b, zmain, grcol, norm_g, heads, dv, tm=512):
    ra, hv = of_a.shape
    m = ra + of_b.shape[0]
    return pl.pallas_call(
        functools.partial(_gla_post_kernel, heads=heads, dv=dv, rows_a=ra, tm=tm),
        grid=(m // tm,),
        in_specs=[*_two_source_specs(ra, tm, hv), *_two_source_specs(ra, tm, hv),
                  pl.BlockSpec((tm, hv), lambda i: (i, grcol)),
                  pl.BlockSpec((1, dv), lambda i: (0, 0))],
        out_specs=pl.BlockSpec((tm, hv), lambda i: (i, 0)),
        out_shape=jax.ShapeDtypeStruct((m, hv), BF16),
        compiler_params=_cp("arbitrary"),
        name="gla_post",
    )(of_a, of_b, ob_a, ob_b, zmain, norm_g.reshape(1, dv))


def _rope_tables(L, dh):
    rows = L // GRID_W
    r = np.repeat(np.arange(rows, dtype=np.float32), GRID_W)
    col = np.tile(np.arange(GRID_W, dtype=np.float32), rows)
    nf = dh // 4
    inv = (np.float32(ROPE_THETA) ** (-np.arange(nf, dtype=np.float32) / nf)).astype(np.float32)
    ang_r = (r[:, None] * inv).astype(np.float32)
    ang_c = (col[:, None] * inv).astype(np.float32)
    cos = np.concatenate([np.cos(ang_r), np.cos(ang_r), np.cos(ang_c), np.cos(ang_c)], -1)
    sin = np.concatenate([-np.sin(ang_r), np.sin(ang_r), -np.sin(ang_c), np.sin(ang_c)], -1)
    cos = np.concatenate([cos, cos], -1).astype(np.float32)
    sin = np.concatenate([sin, sin], -1).astype(np.float32)
    cos_t = np.stack([np.ones_like(cos), cos])
    sin_t = np.stack([np.zeros_like(sin), sin])
    return cos_t, sin_t


def _attn_kernel(*refs, has_ctx, lam_init, dh, bf16_exp, tq):
    if has_ctx:
        (q_ref, k_ref, v_ref, cq_ref, sq_ref, ck_ref, sk_ref, kc_ref, vc_ref, lam_ref, g_ref, o_ref,
         kt_ref, ve_ref, s0_ref, s1_ref) = refs
    else:
        (q_ref, k_ref, v_ref, cq_ref, sq_ref, ck_ref, sk_ref, lam_ref, g_ref, o_ref,
         kt_ref, ve_ref, s0_ref, s1_ref) = refs
    hw = 2 * dh
    nf = dh // 4
    lk = k_ref.shape[0]
    nsub = q_ref.shape[0] // tq
    lane = lax.broadcasted_iota(jnp.int32, (tq, hw), 1)
    first = lane < dh
    scale = dh ** -0.5 * math.log2(math.e)

    def rope(x, cos, sin):
        ln = lax.broadcasted_iota(jnp.int32, x.shape, 1)
        sw = jnp.where((ln & (2 * nf - 1)) < nf, pltpu.roll(x, hw - nf, 1), pltpu.roll(x, nf, 1))
        return x * cos + sw * sin

    @pl.when(pl.program_id(2) == 0)
    def _():
        kt_ref[...] = rope(k_ref[...], ck_ref[...], sk_ref[...]).T.astype(BF16)
        ve_ref[:, :hw] = v_ref[...].astype(BF16)
        ve_ref[:, hw:] = jnp.ones((lk, hw), BF16)

    if has_ctx:
        kct = kc_ref[...].T.astype(BF16)
        vcv = vc_ref[...].astype(BF16)
        vce = jnp.concatenate([vcv, jnp.ones_like(vcv)], axis=1)

    def scores(i, j, s_ref):
        rows = pl.ds(i * tq if isinstance(i, int) else pl.multiple_of(i * tq, tq), tq)
        q = (rope(q_ref[rows, :], cq_ref[rows, :], sq_ref[rows, :]) * scale).astype(BF16)
        qj = jnp.where(first, q, jnp.zeros_like(q)) if j == 0 else jnp.where(first, jnp.zeros_like(q), q)
        s_ref[:, :lk] = _dot(qj, kt_ref[...])
        if has_ctx:
            s_ref[:, lk:] = _dot(qj, kct)

    def prob(t):
        if bf16_exp:
            return jnp.exp2(t.astype(BF16))
        return jnp.exp2(t).astype(BF16)

    def softmax_pv(s_ref):
        s = s_ref[...]
        p = prob(s - jnp.max(s, axis=-1, keepdims=True))
        acc = _dot(p[:, :lk], ve_ref[...])
        if has_ctx:
            acc = acc + _dot(p[:, lk:], vce)
        return acc[:, :hw] / acc[:, hw:]

    lp = lam_ref[...]
    lam = (jnp.exp(jnp.sum(lp[0:1] * lp[1:2], axis=-1, keepdims=True))
           - jnp.exp(jnp.sum(lp[2:3] * lp[3:4], axis=-1, keepdims=True)) + lam_init)
    gain = g_ref[...] * (1.0 - lam_init)

    scores(0, 0, s0_ref)

    def body(i, carry):
        scores(i, 1, s1_ref)
        sm0 = softmax_pv(s0_ref)
        scores(jnp.minimum(i + 1, nsub - 1), 0, s0_ref)
        o = sm0 - lam * softmax_pv(s1_ref)
        ms = jnp.mean(o * o, axis=-1, keepdims=True)
        o_ref[pl.ds(pl.multiple_of(i * tq, tq), tq), :] = (o * lax.rsqrt(ms + RMS_EPS) * gain).astype(o_ref.dtype)
        return carry

    lax.fori_loop(0, nsub, body, 0)


def _diff_attention(zd, cos_t, sin_t, rope_kind, row0, nseq, L, tb, heads, dh, lam_p, norm_g, lam_init,
                    ctx=None, tq=256):
    hw = 2 * dh
    nqb = L // tb
    qb0 = row0 // tb
    kb0 = row0 // L
    past = 0 if ctx is None else ctx[0].shape[2]
    once = dict(pipeline_mode=pl.Buffered(1))
    qtab = pl.BlockSpec((None, tb, hw), lambda s, h, i: (rope_kind, i, 0))
    ktab = pl.BlockSpec((None, L, hw), lambda s, h, i: (rope_kind, 0, 0), **once)
    in_specs = [pl.BlockSpec((tb, hw), lambda s, h, i: (qb0 + s * nqb + i, h)),
                pl.BlockSpec((L, hw), lambda s, h, i: (kb0 + s, heads + h)),
                pl.BlockSpec((L, hw), lambda s, h, i: (kb0 + s, 2 * heads + h)),
                qtab, qtab, ktab, ktab]
    args = [zd, zd, zd, cos_t, sin_t, cos_t, sin_t]
    if ctx is not None:
        ck, cv, layer = ctx
        in_specs += [pl.BlockSpec((None, None, past, hw), lambda s, h, i: (s, layer, 0, h)),
                     pl.BlockSpec((None, None, past, hw), lambda s, h, i: (s, layer, 0, h))]
        args += [ck, cv]
    in_specs += [pl.BlockSpec(lam_p.shape, lambda s, h, i: (0, 0)),
                 pl.BlockSpec((1, hw), lambda s, h, i: (0, 0))]
    args += [lam_p, norm_g.reshape(1, hw)]
    kern = functools.partial(_attn_kernel, has_ctx=ctx is not None, lam_init=lam_init, dh=dh,
                             bf16_exp=L > 1024, tq=tq)
    return pl.pallas_call(
        kern,
        grid=(nseq, heads, nqb),
        in_specs=in_specs,
        out_specs=pl.BlockSpec((tb, hw), lambda s, h, i: (s * nqb + i, h)),
        out_shape=jax.ShapeDtypeStruct((nseq * L, heads * hw), BF16),
        scratch_shapes=[pltpu.VMEM((hw, L), BF16), pltpu.VMEM((L, 2 * hw), BF16),
                        pltpu.VMEM((tq, L + past), F32), pltpu.VMEM((tq, L + past), F32)],
        compiler_params=_cp("parallel", "parallel", "arbitrary"),
        name="diff_attn",
    )(*args)


def _mix_kernel(yaa_ref, yab_ref, yb_ref, yca_ref, ycb_ref, g_ref, w_ref, o_ref, acc_ref, *, rows_a, tm):
    k = pl.program_id(1)

    def contrib(y):
        return g_ref[...].astype(F32) * _dot(y, w_ref[...])

    @pl.when(k == 0)
    def _():
        acc_ref[...] = contrib(_pick(yaa_ref, yab_ref, rows_a, tm))

    @pl.when(k == 1)
    def _():
        acc_ref[...] += contrib(yb_ref[...])

    @pl.when(k == 2)
    def _():
        o_ref[...] = (acc_ref[...] + contrib(_pick(yca_ref, ycb_ref, rows_a, tm))).astype(o_ref.dtype)


def _mix(ya_a, ya_b, yb, yc_a, yc_b, gates, wbr, tm=512):
    ra, w = ya_a.shape
    m = yb.shape[0]
    d = wbr.shape[2]
    return pl.pallas_call(
        functools.partial(_mix_kernel, rows_a=ra, tm=tm),
        grid=(m // tm, 3),
        in_specs=[*_two_source_specs(ra, tm, w, ndim_grid=2),
                  pl.BlockSpec((tm, w), lambda i, k: (i, 0)),
                  *_two_source_specs(ra, tm, w, ndim_grid=2),
                  pl.BlockSpec((tm, d), lambda i, k: (i, k)),
                  pl.BlockSpec((None, w, d), lambda i, k: (k, 0, 0))],
        out_specs=pl.BlockSpec((tm, d), lambda i, k: (i, 0)),
        out_shape=jax.ShapeDtypeStruct((m, d), BF16),
        scratch_shapes=[pltpu.VMEM((tm, d), F32)],
        compiler_params=_cp("arbitrary", "arbitrary"),
        name="mix",
    )(ya_a, ya_b, yb, yc_a, yc_b, gates, wbr)


def _pad_cols(w, n):
    return jnp.pad(w, ((0, 0), (0, n - w.shape[1])))


def kernel(x_prompt, x_sample, cache_k, cache_v, state_gla, c, c_ctx, w_mod, b_mod, ln_g, ln_b, ffn_w1, ffn_w3, ffn_w2, w_in, hy_conv_w, hy_conv_b, hy_w1, hy_b1, hy_freq, hy_w2, hy_b2, hy_w3, hy_decay, hy_bias, gla_wa, gla_ba, gla_norm_g, diff_lam, diff_norm_g, w_branch_a, w_branch_b, w_branch_c, w_out):
    batch, seq, d = x_prompt.shape
    dec_batch, dec_seq, _ = x_sample.shape
    depth = w_mod.shape[0]
    ffn_dim = ffn_w1.shape[3]
    hy_w = hy_bias.shape[2]
    heads_g, dk_g = 4, gla_wa.shape[3] // 4
    dv_g = gla_norm_g.shape[1]
    rank = gla_wa.shape[2]
    dh = diff_lam.shape[2]
    heads_d = cache_k.shape[3]
    dw = heads_d * 2 * dh
    gw = heads_g * dv_g
    gk = heads_g * dk_g
    assert batch * seq == GROUP and dec_seq == GROUP
    mp = batch * seq
    m = mp + dec_batch * dec_seq
    ngroups = 1 + dec_batch
    alpha = (2 * depth) ** 0.25
    lam_inits = [0.8 - 0.6 * math.exp(-0.3 * l) for l in range(depth)]

    c_main = 3 * hy_w + 2 * gk + 2 * gw
    c_lr = c_main
    c_d = c_lr + 2 * rank
    c_g = c_d + 3 * dw

    cond = jnp.concatenate([c_ctx[None], c, jnp.zeros((16 - ngroups, d), F32)], axis=0)
    mod = _modulation(cond, w_mod, b_mod)[:, :ngroups].reshape(depth, ngroups, N_MOD, d)

    na = 2 * dec_seq // FFT_NB
    n_s = 2 * dec_seq
    f1_half = jnp.asarray(_dft_a(na, na // 2), BF16)
    f1_full = jnp.asarray(_dft_a(na, na), BF16)
    g_s = jnp.asarray(_idft_a(na, n_s, na // 2), BF16)
    tb_np, tbi_np = _dft_b(na, FFT_NB, FFT_K1G)
    tb, tbi = jnp.asarray(tb_np, BF16), jnp.asarray(tbi_np, BF16)
    perm_np = _row_permutation(FFT_NB, FFT_K1G)
    perm, permt = jnp.asarray(perm_np, BF16), jnp.asarray(perm_np.T, BF16)
    fp_half = jnp.asarray(_dft_a(2 * seq, seq), BF16)
    fp_full = jnp.asarray(_dft_a(2 * seq, 2 * seq), BF16)
    g_p = jnp.asarray(_idft_a(2 * seq, 2 * seq, seq), BF16)
    cos_t, sin_t = _rope_tables(dec_seq, dh)
    cos_t, sin_t = jnp.asarray(cos_t), jnp.asarray(sin_t)
    half_rows = (na // 2) * FFT_NB
    lanes_s = FFT_NB * hy_w

    x, h = _premod(x_prompt.reshape(mp, d), x_sample.reshape(dec_batch * dec_seq, d), mod, 0)

    fp = ((ffn_dim + 511) // 512) * 512
    ck = cache_k.reshape(dec_batch, depth, cache_k.shape[2], dw)
    cv = cache_v.reshape(dec_batch, depth, cache_v.shape[2], dw)
    state_t = jnp.swapaxes(state_gla, -1, -2)
    new_k, new_v, new_s = [], [], []

    for l in range(depth):
        w2 = _cast_pad_rows(ffn_w2, l, 0, fp)
        hid = _ffn1(h, ffn_w1, ffn_w3, l, 0)
        x, h = _mm_ln(hid, w2, x, mod, l, 2, 0.5, alpha, l, 3, ln_g[l, 0], ln_b[l, 0])

        wi = w_in[l]
        zmain = _proj(h, wi[:, :c_main].astype(BF16), BF16)
        zlr = _proj(h, _pad_cols(wi[:, c_lr:c_d].astype(BF16), LANES), F32)
        zd = _proj(h, wi[:, c_d:c_g].astype(BF16), F32)
        gates = _proj(h, wi[:, c_g:].astype(BF16), BF16, act="sigmoid")

        u3 = _short_conv(zmain, mp, hy_conv_w[l], hy_conv_b[l], seq, dec_seq, hy_w)
        fargs = (hy_w1[l], hy_b1[l], hy_freq[l], hy_w2[l], hy_b2[l], hy_w3[l], hy_decay[l], hy_w)
        taps_s = _hyena_filter_taps(dec_seq, *fargs)
        taps_p = _hyena_filter_taps(seq, *fargs)
        kf_s = _lconv_filter(taps_s, perm, f1_full, tb)
        kf_p = _lmm(fp_full, taps_p, lambda b: (b,), 2, F32, tn=hy_w)

        cw3 = jnp.swapaxes(hy_conv_w[l].reshape(3, 3, hy_w), 0, 1)
        cb3 = hy_conv_b[l].reshape(3, 1, hy_w)
        z1s = _lconv(zmain, mp, 0, zmain, mp, 1, cw3[0:2], cb3[0:2], True, perm, permt, f1_half, g_s,
                     tb, tbi, kf_s, 0, hy_bias[l, 0], dec_batch, dec_seq)
        ya_s = _lconv(z1s, 0, 0, zmain, mp, 2, cw3[1:3], cb3[1:3], False, perm, permt, f1_half, g_s,
                      tb, tbi, kf_s, 1, hy_bias[l, 1], dec_batch, dec_seq)
        z1 = _pconv(fp_half, g_p, u3, (0,), kf_p, 0, u3, (1,), hy_bias[l, 0], batch, seq, F32)
        ya_p = _pconv(fp_half, g_p, z1, (), kf_p, 1, u3, (2,), hy_bias[l, 1], batch, seq, BF16)

        wa_cat = jnp.zeros((LANES, 2 * gk), F32)
        wa_cat = wa_cat.at[:rank, :gk].set(gla_wa[l, 0]).at[rank:2 * rank, gk:].set(gla_wa[l, 1])
        ba_cat = jnp.concatenate([gla_ba[l, 0], gla_ba[l, 1]]).reshape(1, 2 * gk)
        la = _gla_gates(zlr, 0, wa_cat, ba_cat)
        qcol, kcol, vcol, grcol = (3 * hy_w) // gk, (3 * hy_w) // gk + 1, (3 * hy_w + 2 * gk) // gw, \
            (3 * hy_w + 2 * gk) // gw + 1
        of_p, ob_p, st_p = _gla(zmain, qcol, kcol, vcol, la, 0, batch, seq, seq, heads_g, dk_g, dv_g)
        of_s, ob_s, _ = _gla(zmain, qcol, kcol, vcol, la, mp, dec_batch, dec_seq, 512, heads_g, dk_g, dv_g,
                             s0=state_t, s0_layer=l)
        yb = _gla_post(of_p, of_s, ob_p, ob_s, zmain, grcol, gla_norm_g[l], heads_g, dv_g)
        new_s.append(jnp.swapaxes(st_p, -1, -2))

        yc_p = _diff_attention(zd, cos_t, sin_t, 0, 0, batch, seq, seq, heads_d, dh, diff_lam[l],
                               diff_norm_g[l], lam_inits[l])
        yc_s = _diff_attention(zd, cos_t, sin_t, 1, mp, dec_batch, dec_seq, 2048, heads_d, dh, diff_lam[l],
                               diff_norm_g[l], lam_inits[l], ctx=(ck, cv, l), tq=512)
        new_k.append(zd[:mp, dw:2 * dw].reshape(batch, seq, heads_d, 2, dh))
        new_v.append(zd[:mp, 2 * dw:3 * dw].reshape(batch, seq, heads_d, 2 * dh))

        wbr = jnp.stack([w_branch_a[l], w_branch_b[l], w_branch_c[l]]).astype(BF16)
        y = _mix(ya_p, ya_s, yb, yc_p, yc_s, gates, wbr)
        x, h = _mm_ln(y, w_out[l].astype(BF16), x, mod, l, 5, 1.0, alpha, l, 6, ln_g[l, 1], ln_b[l, 1],
                      tm=512)

        w2 = _cast_pad_rows(ffn_w2, l, 1, fp)
        hid = _ffn1(h, ffn_w1, ffn_w3, l, 1)
        if l + 1 < depth:
            x, h = _mm_ln(hid, w2, x, mod, l, 8, 0.5, alpha, l + 1, 0, ln_g[l, 2], ln_b[l, 2])
        else:
            xp, _ = _mm_ln(hid, w2, x, mod, l, 8, 0.5, alpha, l, 0, ln_g[l, 2], ln_b[l, 2], rows=mp)
            xs, _ = _mm_ln(hid, w2, x, mod, l, 8, 0.5, alpha, l, 0, ln_g[l, 2], ln_b[l, 2], row0=mp,
                           rows=m - mp)

    y_prompt = xp.reshape(batch, seq, d)
    y_sample = xs.reshape(dec_batch, dec_seq, d)
    return (y_prompt, y_sample, jnp.stack(new_k, axis=1), jnp.stack(new_v, axis=1),
            jnp.stack(new_s, axis=1))
```

```python
import functools
import math

import numpy as np
import jax
import jax.numpy as jnp
from jax import lax
from jax.experimental import pallas as pl
from jax.experimental.pallas import tpu as pltpu

F32 = jnp.float32
BF16 = jnp.bfloat16

GRID_W = 64
N_MOD = 9
HY_BANDS = 8
GLA_TAU = 16.0
GLA_CHUNK = 64
ROPE_THETA = 10000.0
LN_EPS = 1e-5
RMS_EPS = 1e-6

LANES = 128
MXU_DIM = 256
VMEM_BYTES_V7X = 64 * 1024 * 1024
VMEM_LIMIT = VMEM_BYTES_V7X - 8 * 1024 * 1024

GROUP = 4096
FFT_NB = 16
FFT_K1G = 8


def _cp(*sem):
    return pltpu.CompilerParams(dimension_semantics=sem, vmem_limit_bytes=VMEM_LIMIT)


def _dot(a, b):
    return jnp.dot(a, b, preferred_element_type=F32)


def _dot_nt(a, b):
    return lax.dot_general(a, b, (((1,), (1,)), ((), ())), preferred_element_type=F32)


def _dot_tn(a, b):
    return lax.dot_general(a, b, (((0,), (0,)), ((), ())), preferred_element_type=F32)


def _dot_hi(a, b):
    return jnp.dot(a, b, preferred_element_type=F32, precision=lax.Precision.HIGHEST)


def _silu(x):
    return x * jax.nn.sigmoid(x)


def _mod_kernel(c_ref, w_ref, b_ref, o_ref):
    c = c_ref[...]
    o_ref[...] = _dot(_silu(c).astype(BF16), w_ref[...].astype(BF16)) + b_ref[...]


def _modulation(cond, w_mod, b_mod):
    depth, d, n = w_mod.shape
    r = cond.shape[0]
    tn = 1024
    return pl.pallas_call(
        _mod_kernel,
        grid=(depth, n // tn),
        in_specs=[pl.BlockSpec((r, d), lambda l, j: (0, 0)),
                  pl.BlockSpec((None, d, tn), lambda l, j: (l, 0, j)),
                  pl.BlockSpec((None, 1, tn), lambda l, j: (l, 0, j))],
        out_specs=pl.BlockSpec((None, r, tn), lambda l, j: (l, 0, j)),
        out_shape=jax.ShapeDtypeStruct((depth, r, n), F32),
        compiler_params=_cp("parallel", "parallel"),
        name="mod",
    )(cond, w_mod, b_mod.reshape(depth, 1, n))


def _two_source_specs(rows_a, tm, width, col=0, ndim_grid=1):
    na = rows_a // tm
    if ndim_grid == 1:
        return (pl.BlockSpec((tm, width), lambda i: (jnp.minimum(i, na - 1), col)),
                pl.BlockSpec((tm, width), lambda i: (jnp.maximum(i - na, 0), col)))
    return (pl.BlockSpec((tm, width), lambda i, k: (jnp.minimum(i, na - 1), col)),
            pl.BlockSpec((tm, width), lambda i, k: (jnp.maximum(i - na, 0), col)))


def _pick(a_ref, b_ref, rows_a, tm):
    return jnp.where(pl.program_id(0) < rows_a // tm, a_ref[...], b_ref[...])


def _premod_kernel(xa_ref, xb_ref, mod_ref, x_ref, o_ref, *, rows_a, tm):
    x = _pick(xa_ref, xb_ref, rows_a, tm)
    x_ref[...] = x
    o_ref[...] = (x * (1.0 + mod_ref[1:2, :]) + mod_ref[0:1, :]).astype(o_ref.dtype)


def _premod(xa, xb, mod, layer, tm=512):
    ra, d = xa.shape
    m = ra + xb.shape[0]
    return pl.pallas_call(
        functools.partial(_premod_kernel, rows_a=ra, tm=tm),
        grid=(m // tm,),
        in_specs=[*_two_source_specs(ra, tm, d),
                  pl.BlockSpec((None, None, N_MOD, d), lambda i: (layer, (i * tm) // GROUP, 0, 0))],
        out_specs=[pl.BlockSpec((tm, d), lambda i: (i, 0)), pl.BlockSpec((tm, d), lambda i: (i, 0))],
        out_shape=[jax.ShapeDtypeStruct((m, d), F32), jax.ShapeDtypeStruct((m, d), BF16)],
        compiler_params=_cp("arbitrary"),
        name="premod",
    )(xa, xb, mod)


def _ffn1_kernel(h_ref, w1_ref, w3_ref, o_ref, w1b_ref, w3b_ref, *, tf, f_valid):
    @pl.when(pl.program_id(1) == 0)
    def _():
        col = pl.program_id(0) * tf + lax.broadcasted_iota(jnp.int32, w1_ref.shape, 1)
        keep = col < f_valid
        w1b_ref[...] = jnp.where(keep, w1_ref[...], 0.0).astype(BF16)
        w3b_ref[...] = jnp.where(keep, w3_ref[...], 0.0).astype(BF16)

    h = h_ref[...]
    a = _dot(h, w1b_ref[...])
    b = _dot(h, w3b_ref[...])
    o_ref[...] = (_silu(a) * b).astype(o_ref.dtype)


def _ffn1(h, w1, w3, layer, sub, tm=2048, tf=512):
    m, d = h.shape
    f = w1.shape[-1]
    nf = pl.cdiv(f, tf)
    wspec = pl.BlockSpec((None, None, d, tf), lambda j, i: (layer, sub, 0, j))
    return pl.pallas_call(
        functools.partial(_ffn1_kernel, tf=tf, f_valid=f),
        grid=(nf, m // tm),
        in_specs=[pl.BlockSpec((tm, d), lambda j, i: (i, 0)), wspec, wspec],
        out_specs=pl.BlockSpec((tm, tf), lambda j, i: (i, j)),
        out_shape=jax.ShapeDtypeStruct((m, nf * tf), BF16),
        scratch_shapes=[pltpu.VMEM((d, tf), BF16), pltpu.VMEM((d, tf), BF16)],
        compiler_params=_cp("parallel", "arbitrary"),
        name="ffn1",
    )(h, w1, w3)


def _cast_rows_kernel(w_ref, o_ref, *, rows_valid, tr):
    row = pl.program_id(0) * tr + lax.broadcasted_iota(jnp.int32, w_ref.shape, 0)
    o_ref[...] = jnp.where(row < rows_valid, w_ref[...], 0.0).astype(o_ref.dtype)


def _cast_pad_rows(w, layer, sub, rows_out, tr=512):
    f, d = w.shape[-2:]
    return pl.pallas_call(
        functools.partial(_cast_rows_kernel, rows_valid=f, tr=tr),
        grid=(rows_out // tr,),
        in_specs=[pl.BlockSpec((None, None, tr, d), lambda i: (layer, sub, i, 0))],
        out_specs=pl.BlockSpec((tr, d), lambda i: (i, 0)),
        out_shape=jax.ShapeDtypeStruct((rows_out, d), BF16),
        compiler_params=_cp("parallel"),
        name="cast_w2",
    )(w)


def _mm_ln_kernel(a_ref, w_ref, x_ref, mod_ref, nmod_ref, g_ref, b_ref, xo_ref, ho_ref, *,
                  gate_row, coef, alpha, nshift_row):
    gate = coef * mod_ref[gate_row:gate_row + 1, :]
    xr = alpha * x_ref[...] + gate * _dot(a_ref[...], w_ref[...])
    mu = jnp.mean(xr, axis=-1, keepdims=True)
    xc = xr - mu
    var = jnp.mean(xc * xc, axis=-1, keepdims=True)
    xn = xc * lax.rsqrt(var + LN_EPS) * g_ref[...] + b_ref[...]
    xo_ref[...] = xn
    ho_ref[...] = (xn * (1.0 + nmod_ref[nshift_row + 1:nshift_row + 2, :])
                   + nmod_ref[nshift_row:nshift_row + 1, :]).astype(ho_ref.dtype)


def _mm_ln(a, w, x, mod, layer, gate_row, coef, alpha, nlayer, nshift_row, ln_g, ln_b, tm=256,
           row0=0, rows=None):
    kdim = a.shape[1]
    m = a.shape[0] if rows is None else rows
    d = w.shape[1]
    rb0 = row0 // tm
    kern = functools.partial(_mm_ln_kernel, gate_row=gate_row, coef=coef, alpha=alpha, nshift_row=nshift_row)
    return pl.pallas_call(
        kern,
        grid=(m // tm,),
        in_specs=[pl.BlockSpec((tm, kdim), lambda i: (rb0 + i, 0)),
                  pl.BlockSpec((kdim, d), lambda i: (0, 0), pipeline_mode=pl.Buffered(1)),
                  pl.BlockSpec((tm, d), lambda i: (rb0 + i, 0)),
                  pl.BlockSpec((None, None, N_MOD, d), lambda i: (layer, ((rb0 + i) * tm) // GROUP, 0, 0)),
                  pl.BlockSpec((None, None, N_MOD, d), lambda i: (nlayer, ((rb0 + i) * tm) // GROUP, 0, 0)),
                  pl.BlockSpec((1, d), lambda i: (0, 0)),
                  pl.BlockSpec((1, d), lambda i: (0, 0))],
        out_specs=[pl.BlockSpec((tm, d), lambda i: (i, 0)),
                   pl.BlockSpec((tm, d), lambda i: (i, 0))],
        out_shape=[jax.ShapeDtypeStruct((m, d), F32), jax.ShapeDtypeStruct((m, d), BF16)],
        compiler_params=_cp("parallel"),
        name="mm_ln",
    )(a, w, x, mod, mod, ln_g.reshape(1, d), ln_b.reshape(1, d))


def _proj_kernel(h_ref, w_ref, o_ref, *, act):
    r = _dot(h_ref[...], w_ref[...])
    if act == "sigmoid":
        r = jax.nn.sigmoid(r)
    o_ref[...] = r.astype(o_ref.dtype)


def _proj(h, w, out_dtype, act=None, tm=1024, tn=1024):
    m, d = h.shape
    n = w.shape[1]
    tn = min(tn, n)
    return pl.pallas_call(
        functools.partial(_proj_kernel, act=act),
        grid=(m // tm, n // tn),
        in_specs=[pl.BlockSpec((tm, d), lambda i, j: (i, 0)),
                  pl.BlockSpec((d, tn), lambda i, j: (0, j))],
        out_specs=pl.BlockSpec((tm, tn), lambda i, j: (i, j)),
        out_shape=jax.ShapeDtypeStruct((m, n), out_dtype),
        compiler_params=_cp("parallel", "arbitrary"),
        name="proj",
    )(h, w)


def _sconv_kernel(z_ref, zp_ref, zn_ref, w_ref, b_ref, o_ref, *, tm, halo, lp, ls):
    i = pl.program_id(0)
    u = z_ref[...].astype(F32)
    prev = zp_ref[...].astype(F32)[halo - 1:halo, :]
    nxt = zn_ref[...].astype(F32)[0:1, :]
    row = lax.broadcasted_iota(jnp.int32, u.shape, 0)
    lseq = jnp.where((i * tm) // GROUP == 0, lp, ls)
    pos = (row + i * tm) & (lseq - 1)
    up = jnp.where(row == 0, prev, pltpu.roll(u, 1, 0))
    up = jnp.where(pos == 0, 0.0, up)
    un = jnp.where(row == tm - 1, nxt, pltpu.roll(u, tm - 1, 0))
    un = jnp.where(pos == lseq - 1, 0.0, un)
    w = w_ref[...]
    o_ref[...] = (up * w[0:1, :] + u * w[1:2, :] + un * w[2:3, :] + b_ref[...]).astype(o_ref.dtype)


def _short_conv(z, rows, conv_w, conv_b, lp, ls, width, tm=1024, ct=512, halo=16):
    m = rows
    nct = width // ct
    nrb = m // halo
    kern = functools.partial(_sconv_kernel, tm=tm, halo=halo, lp=lp, ls=ls)
    return pl.pallas_call(
        kern,
        grid=(m // tm, 3 * nct),
        in_specs=[pl.BlockSpec((tm, ct), lambda i, j: (i, j)),
                  pl.BlockSpec((halo, ct), lambda i, j: (jnp.maximum(i * (tm // halo) - 1, 0), j)),
                  pl.BlockSpec((halo, ct), lambda i, j: (jnp.minimum((i + 1) * (tm // halo), nrb - 1), j)),
                  pl.BlockSpec((3, ct), lambda i, j: (0, j)),
                  pl.BlockSpec((1, ct), lambda i, j: (0, j))],
        out_specs=pl.BlockSpec((None, tm, ct), lambda i, j: (j // nct, i, j % nct)),
        out_shape=jax.ShapeDtypeStruct((3, m, width), F32),
        compiler_params=_cp("parallel", "parallel"),
        name="sconv",
    )(z, z, z, conv_w, conv_b.reshape(1, -1))


def _filter_features(L):
    t = np.linspace(0.0, 1.0, L, dtype=np.float32)
    w = (np.float32(2.0 * math.pi / L) * np.arange(L, dtype=np.float32)).astype(np.float32)
    f = np.linspace(1e-4, HY_BANDS - 1, HY_BANDS, dtype=np.float32)
    wf = (w[:, None] * f).astype(np.float32)
    feats = np.concatenate([t[:, None], np.cos(wf), -np.sin(wf)], -1).astype(np.float32)
    idx = np.concatenate([np.arange(L), [0], np.arange(L - 1, 0, -1)])
    tab = np.zeros((2 * L, 32), np.float32)
    tab[:, :feats.shape[1]] = feats[idx]
    tab[:, 24] = t[idx]
    tab[:L, 25] = 1.0
    tab[L + 1:, 26] = -1.0
    return tab, feats.shape[1]


def _filter_kernel(tab_ref, w1_ref, b1_ref, fr_ref, w2_ref, b2_ref, w3_ref, dec_ref, o_ref):
    tab = tab_ref[...]
    fr = fr_ref[...]
    hdn = jnp.sin(fr[0:1, :] * (_dot_hi(tab, w1_ref[...]) + b1_ref[...]))
    hdn = jnp.sin(fr[1:2, :] * (_dot_hi(hdn, w2_ref[...]) + b2_ref[...]))
    t = tab[:, 24:25]
    sign = tab[:, 25:26] + tab[:, 26:27]
    for o in range(2):
        o_ref[o] = sign * (_dot_hi(hdn, w3_ref[o]) * jnp.exp(-t * jnp.abs(dec_ref[o])))


def _hyena_filter_taps(L, hy_w1, hy_b1, hy_freq, hy_w2, hy_b2, hy_w3, hy_decay, width, rb=256):
    tab_np, nfeat = _filter_features(L)
    fh = hy_w1.shape[1]
    w1p = jnp.zeros((32, fh), F32).at[:nfeat].set(hy_w1)
    n = 2 * L
    nhalf = L // rb
    w3d = jnp.transpose(hy_w3.reshape(fh, 2, 2, width), (1, 2, 0, 3))
    decd = hy_decay.reshape(2, 2, 1, width)
    return pl.pallas_call(
        _filter_kernel,
        grid=(n // rb,),
        in_specs=[pl.BlockSpec((rb, 32), lambda i: (i, 0)),
                  pl.BlockSpec((32, fh), lambda i: (0, 0)),
                  pl.BlockSpec((1, fh), lambda i: (0, 0)),
                  pl.BlockSpec((2, fh), lambda i: (0, 0)),
                  pl.BlockSpec((fh, fh), lambda i: (0, 0)),
                  pl.BlockSpec((1, fh), lambda i: (0, 0)),
                  pl.BlockSpec((2, None, fh, width), lambda i: (0, i // nhalf, 0, 0)),
                  pl.BlockSpec((2, None, 1, width), lambda i: (0, i // nhalf, 0, 0))],
        out_specs=pl.BlockSpec((2, rb, width), lambda i: (0, i, 0)),
        out_shape=jax.ShapeDtypeStruct((2, n, width), F32),
        compiler_params=_cp("parallel"),
        name="hyfilter",
    )(jnp.asarray(tab_np), w1p, hy_b1.reshape(1, fh), hy_freq, hy_w2, hy_b2.reshape(1, fh), w3d, decd)


def _dft_a(na, ka):
    k1 = np.arange(na // 2)[:, None].astype(np.float64)
    a = np.arange(ka)[None, :].astype(np.float64)
    th = 2.0 * np.pi * a * (k1 + 0.5) / na
    return np.concatenate([np.cos(th), -np.sin(th)], 0)


def _idft_a(na, n, rows):
    k1 = np.arange(na // 2)[None, :].astype(np.float64)
    a = np.arange(rows)[:, None].astype(np.float64)
    th = 2.0 * np.pi * a * (k1 + 0.5) / na
    return (2.0 / n) * np.concatenate([np.cos(th), -np.sin(th)], 1)


def _dft_b(na, nb, g):
    n = na * nb
    half = na // 2
    k1 = np.arange(half).astype(np.float64)
    b = np.arange(nb).astype(np.float64)
    k2 = np.arange(nb).astype(np.float64)
    phi = 2.0 * np.pi * (b[None, None, :] * k2[None, :, None] / nb
                         + b[None, None, :] * (k1[:, None, None] + 0.5) / n)
    c, s = np.cos(phi), np.sin(phi)
    ng = half // g
    r = g * nb
    fwd = np.zeros((ng, 2 * r, 2 * r))
    for q in range(g):
        rows = slice(q * nb, (q + 1) * nb)
        rows_i = slice(r + q * nb, r + (q + 1) * nb)
        cols = slice(q, r, g)
        cols_i = slice(r + q, 2 * r, g)
        cq, sq = c[q::g], s[q::g]
        fwd[:, rows, cols] = cq
        fwd[:, rows, cols_i] = sq
        fwd[:, rows_i, cols] = -sq
        fwd[:, rows_i, cols_i] = cq
    inv = np.transpose(fwd, (0, 2, 1))
    return fwd, inv


def _lmm_kernel(f_ref, x_ref, o_ref):
    o_ref[...] = _dot(f_ref[...], x_ref[...].astype(BF16)).astype(o_ref.dtype)


def _lmm(f, x, x_index, nbatch, out_dtype, tn=2048):
    mo, k = f.shape
    n = x.shape[-1]
    lead = len(x.shape) - 2
    return pl.pallas_call(
        _lmm_kernel,
        grid=(nbatch, n // tn),
        in_specs=[pl.BlockSpec((mo, k), lambda b, j: (0, 0)),
                  pl.BlockSpec((None,) * lead + (k, tn), lambda b, j: x_index(b) + (0, j))],
        out_specs=pl.BlockSpec((None, mo, tn), lambda b, j: (b, 0, j)),
        out_shape=jax.ShapeDtypeStruct((nbatch, mo, n), out_dtype),
        compiler_params=_cp("parallel", "parallel"),
        name="dft_a",
    )(f, x)


def _stage_b_rows(r_ref, r0, half, nb, kg, ct):
    return [(pl.ds(ri * half + r0, kg), slice(b * ct, (b + 1) * ct)) for ri in range(2) for b in range(nb)]


def _row_permutation(nb, al):
    p = np.zeros((nb * al, nb * al), np.float32)
    for b in range(nb):
        for a in range(al):
            p[b * al + a, a * nb + b] = 1.0
    return p


def _to_residue_major(src_ref, perm_ref, dst_ref, nb, al):
    blk = nb * al
    for i in range(src_ref.shape[0] // blk):
        z = _dot(perm_ref[...], src_ref[i * blk:(i + 1) * blk, :].astype(BF16))
        for b in range(nb):
            dst_ref[b, i * al:(i + 1) * al, :] = z[b * al:(b + 1) * al, :]


def _short_conv_residue_major(u_ref, w_ref, cb_ref, nb):
    rows = u_ref.shape[1]
    row = lax.broadcasted_iota(jnp.int32, u_ref.shape[1:], 0)
    w = w_ref[...]
    cb = cb_ref[...]
    first = u_ref[0]
    prev = jnp.where(row == 0, 0.0, pltpu.roll(u_ref[nb - 1], 1, 0))
    for b in range(nb):
        cur = first if b == 0 else u_ref[b]
        nxt = u_ref[b + 1] if b + 1 < nb else jnp.where(row == rows - 1, 0.0, pltpu.roll(first, rows - 1, 0))
        u_ref[b] = prev * w[0:1, :] + cur * w[1:2, :] + nxt * w[2:3, :] + cb
        prev = cur


def _lconv_kernel(zv_ref, zx_ref, cwv_ref, cbv_ref, cwx_ref, cbx_ref, perm_ref, permt_ref, f1_ref, g_ref,
                  tb_ref, tbi_ref, kf_ref, bias_ref, o_ref, v_ref, x_ref, r_ref, *, nb, half, kg, ct, conv_v):
    r = nb * kg
    al = perm_ref.shape[0] // nb
    _to_residue_major(zv_ref, perm_ref, v_ref, nb, al)
    _to_residue_major(zx_ref, perm_ref, x_ref, nb, al)
    if conv_v:
        _short_conv_residue_major(v_ref, cwv_ref, cbv_ref, nb)
    _short_conv_residue_major(x_ref, cwx_ref, cbx_ref, nb)
    for b in range(nb):
        r_ref[:, b * ct:(b + 1) * ct] = _dot(f1_ref[...], v_ref[b].astype(BF16))

    for g in range(half // kg):
        tiles = _stage_b_rows(r_ref, g * kg, half, nb, kg, ct)
        a = jnp.concatenate([r_ref[rs, cs] for rs, cs in tiles], axis=0).astype(BF16)
        x = _dot(tb_ref[g], a)
        kf = kf_ref[g * 2 * r:(g + 1) * 2 * r, :]
        xr, xi = x[:r], x[r:]
        kr, ki = kf[:r], kf[r:]
        y = jnp.concatenate([xr * kr - xi * ki, xr * ki + xi * kr], axis=0).astype(BF16)
        bh = _dot(tbi_ref[g], y)
        for t, (rs, cs) in enumerate(tiles):
            r_ref[rs, cs] = bh[t * kg:(t + 1) * kg, :]
    bias = bias_ref[...]
    for b in range(nb):
        y = _dot(g_ref[...], r_ref[:, b * ct:(b + 1) * ct].astype(BF16))
        v_ref[b] = x_ref[b] * (y + v_ref[b] * bias)
    blk = nb * al
    for i in range(o_ref.shape[0] // blk):
        t = jnp.concatenate([v_ref[b, i * al:(i + 1) * al, :] for b in range(nb)], axis=0).astype(BF16)
        o_ref[i * blk:(i + 1) * blk, :] = _dot(permt_ref[...], t).astype(o_ref.dtype)


def _lconv(zv, v_row0, v_col0, zx, x_row0, x_col0, conv_w, conv_b, conv_v, perm, permt, f1, g_inv, tb, tbi,
           kf, order, bias, nbatch, L, ct=256):
    c = kf.shape[-1]
    n2, _ = f1.shape
    ng, r2, _ = tb.shape
    nb = L // (n2 // 2)
    kg = r2 // (2 * nb)
    ncb = c // ct
    once = dict(pipeline_mode=pl.Buffered(1))
    kern = functools.partial(_lconv_kernel, nb=nb, half=n2 // 2, kg=kg, ct=ct, conv_v=conv_v)
    vb0, xb0 = v_row0 // L, x_row0 // L
    return pl.pallas_call(
        kern,
        grid=(ncb, nbatch),
        in_specs=[pl.BlockSpec((L, ct), lambda j, b: (vb0 + b, v_col0 * ncb + j)),
                  pl.BlockSpec((L, ct), lambda j, b: (xb0 + b, x_col0 * ncb + j)),
                  pl.BlockSpec((None, 3, ct), lambda j, b: (0, 0, j)),
                  pl.BlockSpec((None, 1, ct), lambda j, b: (0, 0, j)),
                  pl.BlockSpec((None, 3, ct), lambda j, b: (1, 0, j)),
                  pl.BlockSpec((None, 1, ct), lambda j, b: (1, 0, j)),
                  pl.BlockSpec(perm.shape, lambda j, b: (0, 0), **once),
                  pl.BlockSpec(permt.shape, lambda j, b: (0, 0), **once),
                  pl.BlockSpec(f1.shape, lambda j, b: (0, 0), **once),
                  pl.BlockSpec(g_inv.shape, lambda j, b: (0, 0), **once),
                  pl.BlockSpec(tb.shape, lambda j, b: (0, 0, 0), **once),
                  pl.BlockSpec(tbi.shape, lambda j, b: (0, 0, 0), **once),
                  pl.BlockSpec((None, ng * r2, ct), lambda j, b: (order, 0, j), **once),
                  pl.BlockSpec((1, ct), lambda j, b: (0, j))],
        out_specs=pl.BlockSpec((L, ct), lambda j, b: (b, j)),
        out_shape=jax.ShapeDtypeStruct((nbatch * L, c), BF16),
        scratch_shapes=[pltpu.VMEM((nb, L // nb, ct), F32), pltpu.VMEM((nb, L // nb, ct), F32),
                        pltpu.VMEM((n2, nb * ct), F32)],
        compiler_params=_cp("parallel", "arbitrary"),
        name="lconv",
    )(zv, zx, conv_w, conv_b, conv_w, conv_b, perm, permt, f1, g_inv, tb, tbi, kf, bias.reshape(1, c))


def _lconv_filter_kernel(x_ref, perm_ref, f1_ref, tb_ref, o_ref, t_ref, r_ref, *, nb, half, kg, ct):
    r2 = 2 * nb * kg
    _to_residue_major(x_ref, perm_ref, t_ref, nb, perm_ref.shape[0] // nb)
    for b in range(nb):
        r_ref[:, b * ct:(b + 1) * ct] = _dot(f1_ref[...], t_ref[b].astype(BF16))

    for g in range(half // kg):
        tiles = _stage_b_rows(r_ref, g * kg, half, nb, kg, ct)
        a = jnp.concatenate([r_ref[rs, cs] for rs, cs in tiles], axis=0).astype(BF16)
        o_ref[g * r2:(g + 1) * r2, :] = _dot(tb_ref[g], a)


def _lconv_filter(taps, perm, f1, tb, ct=256):
    norder, n, c = taps.shape
    n2, na = f1.shape
    nb = n // na
    ng, r2, _ = tb.shape
    kg = r2 // (2 * nb)
    kern = functools.partial(_lconv_filter_kernel, nb=nb, half=n2 // 2, kg=kg, ct=ct)
    return pl.pallas_call(
        kern,
        grid=(norder, c // ct),
        in_specs=[pl.BlockSpec((None, n, ct), lambda o, j: (o, 0, j)),
                  pl.BlockSpec(perm.shape, lambda o, j: (0, 0)),
                  pl.BlockSpec(f1.shape, lambda o, j: (0, 0)),
                  pl.BlockSpec(tb.shape, lambda o, j: (0, 0, 0))],
        out_specs=pl.BlockSpec((None, ng * r2, ct), lambda o, j: (o, 0, j)),
        out_shape=jax.ShapeDtypeStruct((norder, ng * r2, c), F32),
        scratch_shapes=[pltpu.VMEM((nb, na, ct), F32), pltpu.VMEM((n2, nb * ct), F32)],
        compiler_params=_cp("parallel", "parallel"),
        name="lconv_filter",
    )(taps, perm, f1, tb)


def _pconv_kernel(fd_ref, gd_ref, v_ref, kf_ref, xg_ref, bias_ref, o_ref, *, half):
    v = v_ref[...]
    x = _dot(fd_ref[...], v.astype(BF16))
    kf = kf_ref[...]
    xr, xi = x[:half], x[half:]
    kr, ki = kf[:half], kf[half:]
    y = jnp.concatenate([xr * kr - xi * ki, xr * ki + xi * kr], axis=0).astype(BF16)
    yt = _dot(gd_ref[...], y)
    o_ref[...] = (xg_ref[...].astype(F32) * (yt + v.astype(F32) * bias_ref[...])).astype(o_ref.dtype)


def _pconv(fd, gd, vsrc, v_lead, kf, order, xsrc, x_lead, bias, nseq, L, out_dtype, ct=512):
    c = kf.shape[-1]
    n2 = fd.shape[0]
    return pl.pallas_call(
        functools.partial(_pconv_kernel, half=n2 // 2),
        grid=(nseq, c // ct),
        in_specs=[pl.BlockSpec((n2, L), lambda s, j: (0, 0)),
                  pl.BlockSpec((L, n2), lambda s, j: (0, 0)),
                  pl.BlockSpec((None,) * len(v_lead) + (L, ct), lambda s, j: v_lead + (s, j)),
                  pl.BlockSpec((None, n2, ct), lambda s, j: (order, 0, j)),
                  pl.BlockSpec((None,) * len(x_lead) + (L, ct), lambda s, j: x_lead + (s, j)),
                  pl.BlockSpec((1, ct), lambda s, j: (0, j))],
        out_specs=pl.BlockSpec((L, ct), lambda s, j: (s, j)),
        out_shape=jax.ShapeDtypeStruct((nseq * L, c), out_dtype),
        compiler_params=_cp("parallel", "parallel"),
        name="pconv",
    )(fd, gd, vsrc, kf, xsrc, bias.reshape(1, c))


def _gate_kernel(lr_ref, wa_ref, ba_ref, o_ref):
    logits = _dot_hi(lr_ref[...], wa_ref[...]) + ba_ref[...]
    o_ref[...] = jax.nn.log_sigmoid(logits) * (1.0 / GLA_TAU)


def _gla_gates(lr, col, wa_cat, ba_cat, tm=1024):
    m = lr.shape[0]
    k, n = wa_cat.shape
    return pl.pallas_call(
        _gate_kernel,
        grid=(m // tm,),
        in_specs=[pl.BlockSpec((tm, k), lambda i: (i, col)),
                  pl.BlockSpec((k, n), lambda i: (0, 0)),
                  pl.BlockSpec((1, n), lambda i: (0, 0))],
        out_specs=pl.BlockSpec((tm, n), lambda i: (i, 0)),
        out_shape=jax.ShapeDtypeStruct((m, n), F32),
        compiler_params=_cp("parallel"),
        name="gla_gates",
    )(lr, wa_cat, ba_cat)


def _split3(x):
    hi = x.astype(BF16)
    r1 = x - hi.astype(F32)
    mid = r1.astype(BF16)
    lo = (r1 - mid.astype(F32)).astype(BF16)
    return hi, mid, lo


def _gla_dir(d, r0, q_ref, k_ref, v_ref, la_ref, o_ref, st_ref, tri, causal, ref_row, last_row,
             heads, dk, dv, scale):
    ch = GLA_CHUNK
    rows = pl.ds(pl.multiple_of(r0, ch), ch)
    la = la_ref[rows, :]
    hi, mid, lo = _split3(la)
    b = _dot(tri, hi) + _dot(tri, mid) + _dot(tri, lo)
    bref = b[ref_row:ref_row + 1, :]
    blast = b[last_row:last_row + 1, :]
    q = q_ref[rows, :].astype(F32) * scale
    k = k_ref[rows, :].astype(F32)
    qt = (q * jnp.exp(b - bref)).astype(BF16)
    kt = (k * jnp.exp(bref - b)).astype(BF16)
    qin = (q * jnp.exp(b)).astype(BF16)
    kst = (k * jnp.exp(blast - b)).astype(BF16)
    dec = jnp.exp(blast)
    for h in range(heads):
        ks = slice(h * dk, (h + 1) * dk)
        vs = slice(h * dv, (h + 1) * dv)
        att = _dot_nt(qt[:, ks], kt[:, ks])
        att = jnp.where(causal, att, 0.0).astype(BF16)
        vh = v_ref[rows, vs]
        st = st_ref[d, h]
        o_ref[rows, vs] = (_dot(att, vh) + _dot_nt(qin[:, ks], st.astype(BF16))).astype(o_ref.dtype)
        st_ref[d, h] = st * dec[:, ks] + _dot_tn(vh, kst[:, ks])


def _gla_kernel(*refs, nch, heads, dk, dv, has_s0, scale, nsq):
    seq_in = [refs[8 * q:8 * (q + 1)] for q in range(nsq)]
    rest = refs[8 * nsq:]
    if has_s0:
        s0, of, ob, st = rest
    else:
        of, ob, st = rest
        s0 = None
    ch = GLA_CHUNK

    @pl.when(pl.program_id(1) == 0)
    def _():
        if has_s0:
            st[...] = s0[...]
        else:
            st[...] = jnp.zeros_like(st)

    r_i = lax.broadcasted_iota(jnp.int32, (ch, ch), 0)
    c_i = lax.broadcasted_iota(jnp.int32, (ch, ch), 1)
    lower = r_i >= c_i
    upper = r_i <= c_i
    tri_l = jnp.where(lower, 1.0, 0.0).astype(BF16)
    tri_u = jnp.where(upper, 1.0, 0.0).astype(BF16)

    def body(c, carry):
        for q, (qf, kf, vf, laf, qb, kb, vb, lab) in enumerate(seq_in):
            _gla_dir(0, c * ch, qf, kf, vf, laf, of.at[q], st.at[q], tri_l, lower, ch // 2 - 1, ch - 1,
                     heads, dk, dv, scale)
            _gla_dir(1, (nch - 1 - c) * ch, qb, kb, vb, lab, ob.at[q], st.at[q], tri_u, upper, ch // 2, 0,
                     heads, dk, dv, scale)
        return carry

    lax.fori_loop(0, nch, body, 0)


def _gla(zmain, qcol, kcol, vcol, la, row0, nseq, L, tb, heads, dk, dv, s0=None, s0_layer=0, nsq=2):
    nblk = L // tb
    rb0 = row0 // tb
    hk, hv = heads * dk, heads * dv
    in_specs, args = [], []
    for q in range(nsq):
        def fwd(s, j, q=q):
            return rb0 + (s * nsq + q) * nblk + j

        def bwd(s, j, q=q):
            return rb0 + (s * nsq + q) * nblk + (nblk - 1 - j)

        for rowf, lcol in ((fwd, 0), (bwd, 1)):
            in_specs += [pl.BlockSpec((tb, hk), lambda s, j, rowf=rowf: (rowf(s, j), qcol)),
                         pl.BlockSpec((tb, hk), lambda s, j, rowf=rowf: (rowf(s, j), kcol)),
                         pl.BlockSpec((tb, hv), lambda s, j, rowf=rowf: (rowf(s, j), vcol)),
                         pl.BlockSpec((tb, hk), lambda s, j, rowf=rowf, lcol=lcol: (rowf(s, j), lcol))]
            args += [zmain, zmain, zmain, la]
    st_spec = pl.BlockSpec((nsq, 2, heads, dv, dk), lambda s, j: (s, 0, 0, 0, 0))
    if s0 is not None:
        in_specs.append(pl.BlockSpec((nsq, None, 2, heads, dv, dk), lambda s, j: (s, s0_layer, 0, 0, 0, 0)))
        args.append(s0)
    kern = functools.partial(_gla_kernel, nch=tb // GLA_CHUNK, heads=heads, dk=dk, dv=dv,
                             has_s0=s0 is not None, scale=dk ** -0.5, nsq=nsq)
    o_shape = jax.ShapeDtypeStruct((nseq // nsq, nsq, L, hv), BF16)
    o_f, o_b, st = pl.pallas_call(
        kern,
        grid=(nseq // nsq, nblk),
        in_specs=in_specs,
        out_specs=[pl.BlockSpec((None, nsq, tb, hv), lambda s, j: (s, 0, j, 0)),
                   pl.BlockSpec((None, nsq, tb, hv), lambda s, j: (s, 0, nblk - 1 - j, 0)),
                   st_spec],
        out_shape=[o_shape, o_shape, jax.ShapeDtypeStruct((nseq, 2, heads, dv, dk), F32)],
        compiler_params=_cp("parallel", "arbitrary"),
        name="gla",
    )(*args)
    return o_f.reshape(nseq * L, hv), o_b.reshape(nseq * L, hv), st


def _rope_tables(L, dh):
    rows = L // GRID_W
    r = np.repeat(np.arange(rows, dtype=np.float32), GRID_W)
    col = np.tile(np.arange(GRID_W, dtype=np.float32), rows)
    nf = dh // 4
    inv = (np.float32(ROPE_THETA) ** (-np.arange(nf, dtype=np.float32) / nf)).astype(np.float32)
    ang_r = (r[:, None] * inv).astype(np.float32)
    ang_c = (col[:, None] * inv).astype(np.float32)
    cos = np.concatenate([np.cos(ang_r), np.cos(ang_r), np.cos(ang_c), np.cos(ang_c)], -1)
    sin = np.concatenate([-np.sin(ang_r), np.sin(ang_r), -np.sin(ang_c), np.sin(ang_c)], -1)
    cos = np.concatenate([cos, cos], -1).astype(np.float32)
    sin = np.concatenate([sin, sin], -1).astype(np.float32)
    cos_t = np.stack([np.ones_like(cos), cos])
    sin_t = np.stack([np.zeros_like(sin), sin])
    return cos_t, sin_t


def _attn_kernel(*refs, has_ctx, lam_init, dh, bf16_exp, tq):
    if has_ctx:
        (q_ref, k_ref, v_ref, cq_ref, sq_ref, ck_ref, sk_ref, kc_ref, vc_ref, lam_ref, g_ref, o_ref,
         kt_ref, ve_ref, s0_ref, s1_ref) = refs
    else:
        (q_ref, k_ref, v_ref, cq_ref, sq_ref, ck_ref, sk_ref, lam_ref, g_ref, o_ref,
         kt_ref, ve_ref, s0_ref, s1_ref) = refs
    hw = 2 * dh
    nf = dh // 4
    lk = k_ref.shape[0]
    nsub = q_ref.shape[0] // tq
    lane = lax.broadcasted_iota(jnp.int32, (tq, hw), 1)
    first = lane < dh
    scale = dh ** -0.5 * math.log2(math.e)

    def rope(x, cos, sin):
        ln = lax.broadcasted_iota(jnp.int32, x.shape, 1)
        sw = jnp.where((ln & (2 * nf - 1)) < nf, pltpu.roll(x, hw - nf, 1), pltpu.roll(x, nf, 1))
        return x * cos + sw * sin

    @pl.when(pl.program_id(2) == 0)
    def _():
        kt_ref[...] = rope(k_ref[...], ck_ref[...], sk_ref[...]).T.astype(BF16)
        ve_ref[:, :hw] = v_ref[...].astype(BF16)
        ve_ref[:, hw:] = jnp.ones((lk, hw), BF16)

    if has_ctx:
        kct = kc_ref[...].T.astype(BF16)
        vcv = vc_ref[...].astype(BF16)
        vce = jnp.concatenate([vcv, jnp.ones_like(vcv)], axis=1)

    def scores(i, j, s_ref):
        rows = pl.ds(i * tq if isinstance(i, int) else pl.multiple_of(i * tq, tq), tq)
        q = (rope(q_ref[rows, :], cq_ref[rows, :], sq_ref[rows, :]) * scale).astype(BF16)
        qj = jnp.where(first, q, jnp.zeros_like(q)) if j == 0 else jnp.where(first, jnp.zeros_like(q), q)
        s_ref[:, :lk] = _dot(qj, kt_ref[...])
        if has_ctx:
            s_ref[:, lk:] = _dot(qj, kct)

    def prob(t):
        if bf16_exp:
            return jnp.exp2(t.astype(BF16))
        return jnp.exp2(t).astype(BF16)

    def softmax_pv(s_ref):
        s = s_ref[...]
        p = prob(s - jnp.max(s, axis=-1, keepdims=True))
        acc = _dot(p[:, :lk], ve_ref[...])
        if has_ctx:
            acc = acc + _dot(p[:, lk:], vce)
        return acc[:, :hw] / acc[:, hw:]

    lp = lam_ref[...]
    lam = (jnp.exp(jnp.sum(lp[0:1] * lp[1:2], axis=-1, keepdims=True))
           - jnp.exp(jnp.sum(lp[2:3] * lp[3:4], axis=-1, keepdims=True)) + lam_init)
    gain = g_ref[...] * (1.0 - lam_init)

    scores(0, 0, s0_ref)

    def body(i, carry):
        scores(i, 1, s1_ref)
        sm0 = softmax_pv(s0_ref)
        scores(jnp.minimum(i + 1, nsub - 1), 0, s0_ref)
        o = sm0 - lam * softmax_pv(s1_ref)
        ms = jnp.mean(o * o, axis=-1, keepdims=True)
        o_ref[pl.ds(pl.multiple_of(i * tq, tq), tq), :] = (o * lax.rsqrt(ms + RMS_EPS) * gain).astype(o_ref.dtype)
        return carry

    lax.fori_loop(0, nsub, body, 0)


def _diff_attention(zd, cos_t, sin_t, rope_kind, row0, nseq, L, tb, heads, dh, lam_p, norm_g, lam_init,
                    ctx=None, tq=256):
    hw = 2 * dh
    nqb = L // tb
    qb0 = row0 // tb
    kb0 = row0 // L
    past = 0 if ctx is None else ctx[0].shape[2]
    once = dict(pipeline_mode=pl.Buffered(1))
    qtab = pl.BlockSpec((None, tb, hw), lambda s, h, i: (rope_kind, i, 0))
    ktab = pl.BlockSpec((None, L, hw), lambda s, h, i: (rope_kind, 0, 0), **once)
    in_specs = [pl.BlockSpec((tb, hw), lambda s, h, i: (qb0 + s * nqb + i, h)),
                pl.BlockSpec((L, hw), lambda s, h, i: (kb0 + s, heads + h)),
                pl.BlockSpec((L, hw), lambda s, h, i: (kb0 + s, 2 * heads + h)),
                qtab, qtab, ktab, ktab]
    args = [zd, zd, zd, cos_t, sin_t, cos_t, sin_t]
    if ctx is not None:
        ck, cv, layer = ctx
        in_specs += [pl.BlockSpec((None, None, past, hw), lambda s, h, i: (s, layer, 0, h)),
                     pl.BlockSpec((None, None, past, hw), lambda s, h, i: (s, layer, 0, h))]
        args += [ck, cv]
    in_specs += [pl.BlockSpec(lam_p.shape, lambda s, h, i: (0, 0)),
                 pl.BlockSpec((1, hw), lambda s, h, i: (0, 0))]
    args += [lam_p, norm_g.reshape(1, hw)]
    kern = functools.partial(_attn_kernel, has_ctx=ctx is not None, lam_init=lam_init, dh=dh,
                             bf16_exp=L > 1024, tq=tq)
    return pl.pallas_call(
        kern,
        grid=(nseq, heads, nqb),
        in_specs=in_specs,
        out_specs=pl.BlockSpec((tb, hw), lambda s, h, i: (s * nqb + i, h)),
        out_shape=jax.ShapeDtypeStruct((nseq * L, heads * hw), BF16),
        scratch_shapes=[pltpu.VMEM((hw, L), BF16), pltpu.VMEM((L, 2 * hw), BF16),
                        pltpu.VMEM((tq, L + past), F32), pltpu.VMEM((tq, L + past), F32)],
        compiler_params=_cp("parallel", "parallel", "arbitrary"),
        name="diff_attn",
    )(*args)


def _mix_kernel(yaa_ref, yab_ref, ofa_ref, ofb_ref, oba_ref, obb_ref, gr_ref, ng_ref, yca_ref, ycb_ref,
                g_ref, w_ref, o_ref, acc_ref, *, rows_a, tm, heads, dv):
    k = pl.program_id(1)

    def contrib(y):
        return g_ref[...].astype(F32) * _dot(y, w_ref[...])

    @pl.when(k == 0)
    def _():
        acc_ref[...] = contrib(_pick(yaa_ref, yab_ref, rows_a, tm))

    @pl.when(k == 1)
    def _():
        o = _pick(ofa_ref, ofb_ref, rows_a, tm).astype(F32) + _pick(oba_ref, obb_ref, rows_a, tm).astype(F32)
        gate = _silu(gr_ref[...].astype(F32))
        parts = []
        for h in range(heads):
            oh = o[:, h * dv:(h + 1) * dv]
            ms = jnp.mean(oh * oh, axis=-1, keepdims=True)
            parts.append((oh * lax.rsqrt(ms + RMS_EPS) * ng_ref[...] * gate[:, h * dv:(h + 1) * dv]).astype(BF16))
        acc_ref[...] += contrib(jnp.concatenate(parts, axis=1))

    @pl.when(k == 2)
    def _():
        o_ref[...] = (acc_ref[...] + contrib(_pick(yca_ref, ycb_ref, rows_a, tm))).astype(o_ref.dtype)


def _mix(ya_a, ya_b, of_a, of_b, ob_a, ob_b, zmain, grcol, norm_g, heads, dv, yc_a, yc_b, gates, wbr, tm=512):
    ra, w = ya_a.shape
    m = gates.shape[0]
    d = wbr.shape[2]
    two = functools.partial(_two_source_specs, ra, tm, w, ndim_grid=2)
    return pl.pallas_call(
        functools.partial(_mix_kernel, rows_a=ra, tm=tm, heads=heads, dv=dv),
        grid=(m // tm, 3),
        in_specs=[*two(), *two(), *two(),
                  pl.BlockSpec((tm, w), lambda i, k: (i, grcol)),
                  pl.BlockSpec((1, dv), lambda i, k: (0, 0)),
                  *two(),
                  pl.BlockSpec((tm, d), lambda i, k: (i, k)),
                  pl.BlockSpec((None, w, d), lambda i, k: (k, 0, 0))],
        out_specs=pl.BlockSpec((tm, d), lambda i, k: (i, 0)),
        out_shape=jax.ShapeDtypeStruct((m, d), BF16),
        scratch_shapes=[pltpu.VMEM((tm, d), F32)],
        compiler_params=_cp("arbitrary", "arbitrary"),
        name="mix",
    )(ya_a, ya_b, of_a, of_b, ob_a, ob_b, zmain, norm_g.reshape(1, dv), yc_a, yc_b, gates, wbr)


def _pad_cols(w, n):
    return jnp.pad(w, ((0, 0), (0, n - w.shape[1])))


def kernel(x_prompt, x_sample, cache_k, cache_v, state_gla, c, c_ctx, w_mod, b_mod, ln_g, ln_b, ffn_w1, ffn_w3, ffn_w2, w_in, hy_conv_w, hy_conv_b, hy_w1, hy_b1, hy_freq, hy_w2, hy_b2, hy_w3, hy_decay, hy_bias, gla_wa, gla_ba, gla_norm_g, diff_lam, diff_norm_g, w_branch_a, w_branch_b, w_branch_c, w_out):
    batch, seq, d = x_prompt.shape
    dec_batch, dec_seq, _ = x_sample.shape
    depth = w_mod.shape[0]
    ffn_dim = ffn_w1.shape[3]
    hy_w = hy_bias.shape[2]
    heads_g, dk_g = 4, gla_wa.shape[3] // 4
    dv_g = gla_norm_g.shape[1]
    rank = gla_wa.shape[2]
    dh = diff_lam.shape[2]
    heads_d = cache_k.shape[3]
    dw = heads_d * 2 * dh
    gw = heads_g * dv_g
    gk = heads_g * dk_g
    assert batch * seq == GROUP and dec_seq == GROUP
    mp = batch * seq
    m = mp + dec_batch * dec_seq
    ngroups = 1 + dec_batch
    alpha = (2 * depth) ** 0.25
    lam_inits = [0.8 - 0.6 * math.exp(-0.3 * l) for l in range(depth)]

    c_main = 3 * hy_w + 2 * gk + 2 * gw
    c_lr = c_main
    c_d = c_lr + 2 * rank
    c_g = c_d + 3 * dw

    cond = jnp.concatenate([c_ctx[None], c, jnp.zeros((16 - ngroups, d), F32)], axis=0)
    mod = _modulation(cond, w_mod, b_mod)[:, :ngroups].reshape(depth, ngroups, N_MOD, d)

    na = 2 * dec_seq // FFT_NB
    n_s = 2 * dec_seq
    f1_half = jnp.asarray(_dft_a(na, na // 2), BF16)
    f1_full = jnp.asarray(_dft_a(na, na), BF16)
    g_s = jnp.asarray(_idft_a(na, n_s, na // 2), BF16)
    tb_np, tbi_np = _dft_b(na, FFT_NB, FFT_K1G)
    tb, tbi = jnp.asarray(tb_np, BF16), jnp.asarray(tbi_np, BF16)
    perm_np = _row_permutation(FFT_NB, FFT_K1G)
    perm, permt = jnp.asarray(perm_np, BF16), jnp.asarray(perm_np.T, BF16)
    fp_half = jnp.asarray(_dft_a(2 * seq, seq), BF16)
    fp_full = jnp.asarray(_dft_a(2 * seq, 2 * seq), BF16)
    g_p = jnp.asarray(_idft_a(2 * seq, 2 * seq, seq), BF16)
    cos_t, sin_t = _rope_tables(dec_seq, dh)
    cos_t, sin_t = jnp.asarray(cos_t), jnp.asarray(sin_t)
    half_rows = (na // 2) * FFT_NB
    lanes_s = FFT_NB * hy_w

    x, h = _premod(x_prompt.reshape(mp, d), x_sample.reshape(dec_batch * dec_seq, d), mod, 0)

    fp = ((ffn_dim + 511) // 512) * 512
    ck = cache_k.reshape(dec_batch, depth, cache_k.shape[2], dw)
    cv = cache_v.reshape(dec_batch, depth, cache_v.shape[2], dw)
    state_t = jnp.swapaxes(state_gla, -1, -2)
    new_k, new_v, new_s = [], [], []

    for l in range(depth):
        w2 = _cast_pad_rows(ffn_w2, l, 0, fp)
        hid = _ffn1(h, ffn_w1, ffn_w3, l, 0)
        x, h = _mm_ln(hid, w2, x, mod, l, 2, 0.5, alpha, l, 3, ln_g[l, 0], ln_b[l, 0])

        wi = w_in[l]
        zmain = _proj(h, wi[:, :c_main].astype(BF16), BF16)
        zlr = _proj(h, _pad_cols(wi[:, c_lr:c_d].astype(BF16), LANES), F32)
        zd = _proj(h, wi[:, c_d:c_g].astype(BF16), F32)
        gates = _proj(h, wi[:, c_g:].astype(BF16), BF16, act="sigmoid")

        u3 = _short_conv(zmain, mp, hy_conv_w[l], hy_conv_b[l], seq, dec_seq, hy_w)
        fargs = (hy_w1[l], hy_b1[l], hy_freq[l], hy_w2[l], hy_b2[l], hy_w3[l], hy_decay[l], hy_w)
        taps_s = _hyena_filter_taps(dec_seq, *fargs)
        taps_p = _hyena_filter_taps(seq, *fargs)
        kf_s = _lconv_filter(taps_s, perm, f1_full, tb)
        kf_p = _lmm(fp_full, taps_p, lambda b: (b,), 2, F32, tn=hy_w)

        cw3 = jnp.swapaxes(hy_conv_w[l].reshape(3, 3, hy_w), 0, 1)
        cb3 = hy_conv_b[l].reshape(3, 1, hy_w)
        z1s = _lconv(zmain, mp, 0, zmain, mp, 1, cw3[0:2], cb3[0:2], True, perm, permt, f1_half, g_s,
                     tb, tbi, kf_s, 0, hy_bias[l, 0], dec_batch, dec_seq)
        ya_s = _lconv(z1s, 0, 0, zmain, mp, 2, cw3[1:3], cb3[1:3], False, perm, permt, f1_half, g_s,
                      tb, tbi, kf_s, 1, hy_bias[l, 1], dec_batch, dec_seq)
        z1 = _pconv(fp_half, g_p, u3, (0,), kf_p, 0, u3, (1,), hy_bias[l, 0], batch, seq, F32)
        ya_p = _pconv(fp_half, g_p, z1, (), kf_p, 1, u3, (2,), hy_bias[l, 1], batch, seq, BF16)

        wa_cat = jnp.zeros((LANES, 2 * gk), F32)
        wa_cat = wa_cat.at[:rank, :gk].set(gla_wa[l, 0]).at[rank:2 * rank, gk:].set(gla_wa[l, 1])
        ba_cat = jnp.concatenate([gla_ba[l, 0], gla_ba[l, 1]]).reshape(1, 2 * gk)
        la = _gla_gates(zlr, 0, wa_cat, ba_cat)
        qcol, kcol, vcol, grcol = (3 * hy_w) // gk, (3 * hy_w) // gk + 1, (3 * hy_w + 2 * gk) // gw, \
            (3 * hy_w + 2 * gk) // gw + 1
        of_p, ob_p, st_p = _gla(zmain, qcol, kcol, vcol, la, 0, batch, seq, seq, heads_g, dk_g, dv_g)
        of_s, ob_s, _ = _gla(zmain, qcol, kcol, vcol, la, mp, dec_batch, dec_seq, 512, heads_g, dk_g, dv_g,
                             s0=state_t, s0_layer=l)
        new_s.append(jnp.swapaxes(st_p, -1, -2))

        yc_p = _diff_attention(zd, cos_t, sin_t, 0, 0, batch, seq, seq, heads_d, dh, diff_lam[l],
                               diff_norm_g[l], lam_inits[l])
        yc_s = _diff_attention(zd, cos_t, sin_t, 1, mp, dec_batch, dec_seq, 2048, heads_d, dh, diff_lam[l],
                               diff_norm_g[l], lam_inits[l], ctx=(ck, cv, l), tq=512)
        new_k.append(zd[:mp, dw:2 * dw].reshape(batch, seq, heads_d, 2, dh))
        new_v.append(zd[:mp, 2 * dw:3 * dw].reshape(batch, seq, heads_d, 2 * dh))

        wbr = jnp.stack([w_branch_a[l], w_branch_b[l], w_branch_c[l]]).astype(BF16)
        y = _mix(ya_p, ya_s, of_p, of_s, ob_p, ob_s, zmain, grcol, gla_norm_g[l], heads_g, dv_g, yc_p, yc_s,
                 gates, wbr)
        x, h = _mm_ln(y, w_out[l].astype(BF16), x, mod, l, 5, 1.0, alpha, l, 6, ln_g[l, 1], ln_b[l, 1],
                      tm=512)

        w2 = _cast_pad_rows(ffn_w2, l, 1, fp)
        hid = _ffn1(h, ffn_w1, ffn_w3, l, 1)
        if l + 1 < depth:
            x, h = _mm_ln(hid, w2, x, mod, l, 8, 0.5, alpha, l + 1, 0, ln_g[l, 2], ln_b[l, 2])
        else:
            xp, _ = _mm_ln(hid, w2, x, mod, l, 8, 0.5, alpha, l, 0, ln_g[l, 2], ln_b[l, 2], rows=mp)
            xs, _ = _mm_ln(hid, w2, x, mod, l, 8, 0.5, alpha, l, 0, ln_g[l, 2], ln_b[l, 2], row0=mp,
                           rows=m - mp)

    y_prompt = xp.reshape(batch, seq, d)
    y_sample = xs.reshape(dec_batch, dec_seq, d)
    return (y_prompt, y_sample, jnp.stack(new_k, axis=1), jnp.stack(new_v, axis=1),
            jnp.stack(new_s, axis=1))
```

```python
import functools
import math

import numpy as np
import jax
import jax.numpy as jnp
from jax import lax
from jax.experimental import pallas as pl
from jax.experimental.pallas import tpu as pltpu

F32 = jnp.float32
BF16 = jnp.bfloat16

GRID_W = 64
N_MOD = 9
HY_BANDS = 8
GLA_TAU = 16.0
GLA_CHUNK = 64
ROPE_THETA = 10000.0
LN_EPS = 1e-5
RMS_EPS = 1e-6

LANES = 128
SUBLANES_F32 = 8
VMEM_BYTES_V7X = 64 * 1024 * 1024
VMEM_LIMIT = VMEM_BYTES_V7X - 8 * 1024 * 1024

GROUP = 4096
FFT_NB = 16
FFT_K1G = 8


def _cp(*sem):
    return pltpu.CompilerParams(dimension_semantics=sem, vmem_limit_bytes=VMEM_LIMIT)


def _dot(a, b):
    return jnp.dot(a, b, preferred_element_type=F32)


def _dot_nt(a, b):
    return lax.dot_general(a, b, (((1,), (1,)), ((), ())), preferred_element_type=F32)


def _dot_tn(a, b):
    return lax.dot_general(a, b, (((0,), (0,)), ((), ())), preferred_element_type=F32)


def _dot_hi(a, b):
    return jnp.dot(a, b, preferred_element_type=F32, precision=lax.Precision.HIGHEST)


def _silu(x):
    return x * jax.nn.sigmoid(x)


def _mod_kernel(c_ref, w_ref, b_ref, o_ref):
    c = c_ref[...]
    o_ref[...] = _dot(_silu(c).astype(BF16), w_ref[...].astype(BF16)) + b_ref[...]


def _modulation(cond, w_mod, b_mod):
    depth, d, n = w_mod.shape
    r = cond.shape[0]
    tn = 1024
    return pl.pallas_call(
        _mod_kernel,
        grid=(depth, n // tn),
        in_specs=[pl.BlockSpec((r, d), lambda l, j: (0, 0)),
                  pl.BlockSpec((None, d, tn), lambda l, j: (l, 0, j)),
                  pl.BlockSpec((None, 1, tn), lambda l, j: (l, 0, j))],
        out_specs=pl.BlockSpec((None, r, tn), lambda l, j: (l, 0, j)),
        out_shape=jax.ShapeDtypeStruct((depth, r, n), F32),
        compiler_params=_cp("parallel", "parallel"),
        name="mod",
    )(cond, w_mod, b_mod.reshape(depth, 1, n))


def _two_source_specs(rows_a, tm, width, col=0, ndim_grid=1):
    na = rows_a // tm
    if ndim_grid == 1:
        return (pl.BlockSpec((tm, width), lambda i: (jnp.minimum(i, na - 1), col)),
                pl.BlockSpec((tm, width), lambda i: (jnp.maximum(i - na, 0), col)))
    return (pl.BlockSpec((tm, width), lambda i, k: (jnp.minimum(i, na - 1), col)),
            pl.BlockSpec((tm, width), lambda i, k: (jnp.maximum(i - na, 0), col)))


def _pick(a_ref, b_ref, rows_a, tm):
    return jnp.where(pl.program_id(0) < rows_a // tm, a_ref[...], b_ref[...])


def _premod_kernel(xa_ref, xb_ref, mod_ref, x_ref, o_ref, *, rows_a, tm):
    x = _pick(xa_ref, xb_ref, rows_a, tm)
    x_ref[...] = x
    o_ref[...] = (x * (1.0 + mod_ref[1:2, :]) + mod_ref[0:1, :]).astype(o_ref.dtype)


def _premod(xa, xb, mod, layer, tm=512):
    ra, d = xa.shape
    m = ra + xb.shape[0]
    return pl.pallas_call(
        functools.partial(_premod_kernel, rows_a=ra, tm=tm),
        grid=(m // tm,),
        in_specs=[*_two_source_specs(ra, tm, d),
                  pl.BlockSpec((None, None, N_MOD, d), lambda i: (layer, (i * tm) // GROUP, 0, 0))],
        out_specs=[pl.BlockSpec((tm, d), lambda i: (i, 0)), pl.BlockSpec((tm, d), lambda i: (i, 0))],
        out_shape=[jax.ShapeDtypeStruct((m, d), F32), jax.ShapeDtypeStruct((m, d), BF16)],
        compiler_params=_cp("arbitrary"),
        name="premod",
    )(xa, xb, mod)


def _ffn1_kernel(h_ref, w1_ref, w3_ref, o_ref, w1b_ref, w3b_ref, *, tf, f_valid):
    @pl.when(pl.program_id(1) == 0)
    def _():
        col = pl.program_id(0) * tf + lax.broadcasted_iota(jnp.int32, w1_ref.shape, 1)
        keep = col < f_valid
        w1b_ref[...] = jnp.where(keep, w1_ref[...], 0.0).astype(BF16)
        w3b_ref[...] = jnp.where(keep, w3_ref[...], 0.0).astype(BF16)

    h = h_ref[...]
    a = _dot(h, w1b_ref[...])
    b = _dot(h, w3b_ref[...])
    o_ref[...] = (_silu(a) * b).astype(o_ref.dtype)


def _ffn1(h, w1, w3, layer, sub, tm=1024, tf=512):
    m, d = h.shape
    f = w1.shape[-1]
    nf = pl.cdiv(f, tf)
    wspec = pl.BlockSpec((None, None, d, tf), lambda j, i: (layer, sub, 0, j))
    return pl.pallas_call(
        functools.partial(_ffn1_kernel, tf=tf, f_valid=f),
        grid=(nf, m // tm),
        in_specs=[pl.BlockSpec((tm, d), lambda j, i: (i, 0)), wspec, wspec],
        out_specs=pl.BlockSpec((tm, tf), lambda j, i: (i, j)),
        out_shape=jax.ShapeDtypeStruct((m, nf * tf), BF16),
        scratch_shapes=[pltpu.VMEM((d, tf), BF16), pltpu.VMEM((d, tf), BF16)],
        compiler_params=_cp("parallel", "arbitrary"),
        name="ffn1",
    )(h, w1, w3)


def _cast_rows_kernel(w_ref, o_ref, *, rows_valid, tr):
    row = pl.program_id(0) * tr + lax.broadcasted_iota(jnp.int32, w_ref.shape, 0)
    o_ref[...] = jnp.where(row < rows_valid, w_ref[...], 0.0).astype(o_ref.dtype)


def _cast_pad_rows(w, layer, sub, rows_out, tr=512):
    f, d = w.shape[-2:]
    return pl.pallas_call(
        functools.partial(_cast_rows_kernel, rows_valid=f, tr=tr),
        grid=(rows_out // tr,),
        in_specs=[pl.BlockSpec((None, None, tr, d), lambda i: (layer, sub, i, 0))],
        out_specs=pl.BlockSpec((tr, d), lambda i: (i, 0)),
        out_shape=jax.ShapeDtypeStruct((rows_out, d), BF16),
        compiler_params=_cp("parallel"),
        name="cast_w2",
    )(w)


def _mm_ln_kernel(a_ref, w_ref, x_ref, mod_ref, nmod_ref, g_ref, b_ref, xo_ref, ho_ref, *,
                  gate_row, coef, alpha, nshift_row):
    gate = coef * mod_ref[gate_row:gate_row + 1, :]
    xr = alpha * x_ref[...] + gate * _dot(a_ref[...], w_ref[...])
    mu = jnp.mean(xr, axis=-1, keepdims=True)
    xc = xr - mu
    var = jnp.mean(xc * xc, axis=-1, keepdims=True)
    xn = xc * lax.rsqrt(var + LN_EPS) * g_ref[...] + b_ref[...]
    xo_ref[...] = xn
    ho_ref[...] = (xn * (1.0 + nmod_ref[nshift_row + 1:nshift_row + 2, :])
                   + nmod_ref[nshift_row:nshift_row + 1, :]).astype(ho_ref.dtype)


def _mm_ln(a, w, x, mod, layer, gate_row, coef, alpha, nlayer, nshift_row, ln_g, ln_b, tm=256,
           row0=0, rows=None):
    kdim = a.shape[1]
    m = a.shape[0] if rows is None else rows
    d = w.shape[1]
    rb0 = row0 // tm
    kern = functools.partial(_mm_ln_kernel, gate_row=gate_row, coef=coef, alpha=alpha, nshift_row=nshift_row)
    return pl.pallas_call(
        kern,
        grid=(m // tm,),
        in_specs=[pl.BlockSpec((tm, kdim), lambda i: (rb0 + i, 0)),
                  pl.BlockSpec((kdim, d), lambda i: (0, 0), pipeline_mode=pl.Buffered(1)),
                  pl.BlockSpec((tm, d), lambda i: (rb0 + i, 0)),
                  pl.BlockSpec((None, None, N_MOD, d), lambda i: (layer, ((rb0 + i) * tm) // GROUP, 0, 0)),
                  pl.BlockSpec((None, None, N_MOD, d), lambda i: (nlayer, ((rb0 + i) * tm) // GROUP, 0, 0)),
                  pl.BlockSpec((1, d), lambda i: (0, 0)),
                  pl.BlockSpec((1, d), lambda i: (0, 0))],
        out_specs=[pl.BlockSpec((tm, d), lambda i: (i, 0)),
                   pl.BlockSpec((tm, d), lambda i: (i, 0))],
        out_shape=[jax.ShapeDtypeStruct((m, d), F32), jax.ShapeDtypeStruct((m, d), BF16)],
        compiler_params=_cp("parallel"),
        name="mm_ln",
    )(a, w, x, mod, mod, ln_g.reshape(1, d), ln_b.reshape(1, d))


def _proj_kernel(h_ref, w_ref, o_ref, *, act):
    r = _dot(h_ref[...], w_ref[...])
    if act == "sigmoid":
        r = jax.nn.sigmoid(r)
    o_ref[...] = r.astype(o_ref.dtype)


def _proj(h, w, out_dtype, act=None, tm=1024, tn=1024):
    m, d = h.shape
    n = w.shape[1]
    tn = min(tn, n)
    return pl.pallas_call(
        functools.partial(_proj_kernel, act=act),
        grid=(m // tm, n // tn),
        in_specs=[pl.BlockSpec((tm, d), lambda i, j: (i, 0)),
                  pl.BlockSpec((d, tn), lambda i, j: (0, j))],
        out_specs=pl.BlockSpec((tm, tn), lambda i, j: (i, j)),
        out_shape=jax.ShapeDtypeStruct((m, n), out_dtype),
        compiler_params=_cp("parallel", "arbitrary"),
        name="proj",
    )(h, w)


def _sconv_kernel(z_ref, zp_ref, zn_ref, w_ref, b_ref, o_ref, *, tm, halo, lp, ls):
    i = pl.program_id(0)
    u = z_ref[...].astype(F32)
    prev = zp_ref[...].astype(F32)[halo - 1:halo, :]
    nxt = zn_ref[...].astype(F32)[0:1, :]
    row = lax.broadcasted_iota(jnp.int32, u.shape, 0)
    lseq = jnp.where((i * tm) // GROUP == 0, lp, ls)
    pos = (row + i * tm) & (lseq - 1)
    up = jnp.where(row == 0, prev, pltpu.roll(u, 1, 0))
    up = jnp.where(pos == 0, 0.0, up)
    un = jnp.where(row == tm - 1, nxt, pltpu.roll(u, tm - 1, 0))
    un = jnp.where(pos == lseq - 1, 0.0, un)
    w = w_ref[...]
    o_ref[...] = (up * w[0:1, :] + u * w[1:2, :] + un * w[2:3, :] + b_ref[...]).astype(o_ref.dtype)


def _short_conv(z, rows, conv_w, conv_b, lp, ls, width, tm=1024, ct=512, halo=16):
    m = rows
    nct = width // ct
    nrb = m // halo
    kern = functools.partial(_sconv_kernel, tm=tm, halo=halo, lp=lp, ls=ls)
    return pl.pallas_call(
        kern,
        grid=(m // tm, 3 * nct),
        in_specs=[pl.BlockSpec((tm, ct), lambda i, j: (i, j)),
                  pl.BlockSpec((halo, ct), lambda i, j: (jnp.maximum(i * (tm // halo) - 1, 0), j)),
                  pl.BlockSpec((halo, ct), lambda i, j: (jnp.minimum((i + 1) * (tm // halo), nrb - 1), j)),
                  pl.BlockSpec((3, ct), lambda i, j: (0, j)),
                  pl.BlockSpec((1, ct), lambda i, j: (0, j))],
        out_specs=pl.BlockSpec((None, tm, ct), lambda i, j: (j // nct, i, j % nct)),
        out_shape=jax.ShapeDtypeStruct((3, m, width), F32),
        compiler_params=_cp("parallel", "parallel"),
        name="sconv",
    )(z, z, z, conv_w, conv_b.reshape(1, -1))


def _filter_features(L):
    t = np.linspace(0.0, 1.0, L, dtype=np.float32)
    w = (np.float32(2.0 * math.pi / L) * np.arange(L, dtype=np.float32)).astype(np.float32)
    f = np.linspace(1e-4, HY_BANDS - 1, HY_BANDS, dtype=np.float32)
    wf = (w[:, None] * f).astype(np.float32)
    feats = np.concatenate([t[:, None], np.cos(wf), -np.sin(wf)], -1).astype(np.float32)
    idx = np.concatenate([np.arange(L), [0], np.arange(L - 1, 0, -1)])
    tab = np.zeros((2 * L, 32), np.float32)
    tab[:, :feats.shape[1]] = feats[idx]
    tab[:, 24] = t[idx]
    tab[:L, 25] = 1.0
    tab[L + 1:, 26] = -1.0
    return tab, feats.shape[1]


def _filter_kernel(tab_ref, w1_ref, b1_ref, fr_ref, w2_ref, b2_ref, w3_ref, dec_ref, o_ref):
    tab = tab_ref[...]
    fr = fr_ref[...]
    hdn = jnp.sin(fr[0:1, :] * (_dot_hi(tab, w1_ref[...]) + b1_ref[...]))
    hdn = jnp.sin(fr[1:2, :] * (_dot_hi(hdn, w2_ref[...]) + b2_ref[...]))
    t = tab[:, 24:25]
    sign = tab[:, 25:26] + tab[:, 26:27]
    for o in range(2):
        o_ref[o] = sign * (_dot_hi(hdn, w3_ref[o]) * jnp.exp(-t * jnp.abs(dec_ref[o])))


def _hyena_filter_taps(L, hy_w1, hy_b1, hy_freq, hy_w2, hy_b2, hy_w3, hy_decay, width, rb=256):
    tab_np, nfeat = _filter_features(L)
    fh = hy_w1.shape[1]
    w1p = jnp.zeros((32, fh), F32).at[:nfeat].set(hy_w1)
    n = 2 * L
    nhalf = L // rb
    w3d = jnp.transpose(hy_w3.reshape(fh, 2, 2, width), (1, 2, 0, 3))
    decd = hy_decay.reshape(2, 2, 1, width)
    return pl.pallas_call(
        _filter_kernel,
        grid=(n // rb,),
        in_specs=[pl.BlockSpec((rb, 32), lambda i: (i, 0)),
                  pl.BlockSpec((32, fh), lambda i: (0, 0)),
                  pl.BlockSpec((1, fh), lambda i: (0, 0)),
                  pl.BlockSpec((2, fh), lambda i: (0, 0)),
                  pl.BlockSpec((fh, fh), lambda i: (0, 0)),
                  pl.BlockSpec((1, fh), lambda i: (0, 0)),
                  pl.BlockSpec((2, None, fh, width), lambda i: (0, i // nhalf, 0, 0)),
                  pl.BlockSpec((2, None, 1, width), lambda i: (0, i // nhalf, 0, 0))],
        out_specs=pl.BlockSpec((2, rb, width), lambda i: (0, i, 0)),
        out_shape=jax.ShapeDtypeStruct((2, n, width), F32),
        compiler_params=_cp("parallel"),
        name="hyfilter",
    )(jnp.asarray(tab_np), w1p, hy_b1.reshape(1, fh), hy_freq, hy_w2, hy_b2.reshape(1, fh), w3d, decd)


def _dft_a(na, ka):
    k1 = np.arange(na // 2)[:, None].astype(np.float64)
    a = np.arange(ka)[None, :].astype(np.float64)
    th = 2.0 * np.pi * a * (k1 + 0.5) / na
    return np.concatenate([np.cos(th), -np.sin(th)], 0)


def _idft_a(na, n, rows):
    k1 = np.arange(na // 2)[None, :].astype(np.float64)
    a = np.arange(rows)[:, None].astype(np.float64)
    th = 2.0 * np.pi * a * (k1 + 0.5) / na
    return (2.0 / n) * np.concatenate([np.cos(th), -np.sin(th)], 1)


def _dft_b(na, nb, g):
    n = na * nb
    half = na // 2
    k1 = np.arange(half).astype(np.float64)
    b = np.arange(nb).astype(np.float64)
    k2 = np.arange(nb).astype(np.float64)
    phi = 2.0 * np.pi * (b[None, None, :] * k2[None, :, None] / nb
                         + b[None, None, :] * (k1[:, None, None] + 0.5) / n)
    c, s = np.cos(phi), np.sin(phi)
    ng = half // g
    r = g * nb
    fwd = np.zeros((ng, 2 * r, 2 * r))
    for q in range(g):
        rows = slice(q * nb, (q + 1) * nb)
        rows_i = slice(r + q * nb, r + (q + 1) * nb)
        cols = slice(q, r, g)
        cols_i = slice(r + q, 2 * r, g)
        cq, sq = c[q::g], s[q::g]
        fwd[:, rows, cols] = cq
        fwd[:, rows, cols_i] = sq
        fwd[:, rows_i, cols] = -sq
        fwd[:, rows_i, cols_i] = cq
    inv = np.transpose(fwd, (0, 2, 1))
    return fwd, inv


def _lmm_kernel(f_ref, x_ref, o_ref):
    o_ref[...] = _dot(f_ref[...], x_ref[...].astype(BF16)).astype(o_ref.dtype)


def _lmm(f, x, x_index, nbatch, out_dtype, tn=2048):
    mo, k = f.shape
    n = x.shape[-1]
    lead = len(x.shape) - 2
    return pl.pallas_call(
        _lmm_kernel,
        grid=(nbatch, n // tn),
        in_specs=[pl.BlockSpec((mo, k), lambda b, j: (0, 0)),
                  pl.BlockSpec((None,) * lead + (k, tn), lambda b, j: x_index(b) + (0, j))],
        out_specs=pl.BlockSpec((None, mo, tn), lambda b, j: (b, 0, j)),
        out_shape=jax.ShapeDtypeStruct((nbatch, mo, n), out_dtype),
        compiler_params=_cp("parallel", "parallel"),
        name="dft_a",
    )(f, x)


def _stage_b_rows(r_ref, r0, half, nb, kg, ct):
    return [(pl.ds(ri * half + r0, kg), slice(b * ct, (b + 1) * ct)) for ri in range(2) for b in range(nb)]


def _row_permutation(nb, al):
    p = np.zeros((nb * al, nb * al), np.float32)
    for b in range(nb):
        for a in range(al):
            p[b * al + a, a * nb + b] = 1.0
    return p


def _to_residue_major(src_ref, perm_ref, dst_ref, nb, al):
    blk = nb * al
    for i in range(src_ref.shape[0] // blk):
        z = _dot(perm_ref[...], src_ref[i * blk:(i + 1) * blk, :].astype(BF16))
        for b in range(nb):
            dst_ref[b, i * al:(i + 1) * al, :] = z[b * al:(b + 1) * al, :]


def _short_conv_residue_major(u_ref, w_ref, cb_ref, nb):
    rows = u_ref.shape[1]
    row = lax.broadcasted_iota(jnp.int32, u_ref.shape[1:], 0)
    w = w_ref[...]
    cb = cb_ref[...]
    first = u_ref[0]
    prev = jnp.where(row == 0, 0.0, pltpu.roll(u_ref[nb - 1], 1, 0))
    for b in range(nb):
        cur = first if b == 0 else u_ref[b]
        nxt = u_ref[b + 1] if b + 1 < nb else jnp.where(row == rows - 1, 0.0, pltpu.roll(first, rows - 1, 0))
        u_ref[b] = prev * w[0:1, :] + cur * w[1:2, :] + nxt * w[2:3, :] + cb
        prev = cur


def _lconv_kernel(zv_ref, zx_ref, cwv_ref, cbv_ref, cwx_ref, cbx_ref, perm_ref, permt_ref, f1_ref, g_ref,
                  tb_ref, tbi_ref, kf_ref, bias_ref, o_ref, v_ref, x_ref, r_ref, *, nb, half, kg, ct, conv_v):
    r = nb * kg
    al = perm_ref.shape[0] // nb
    _to_residue_major(zv_ref, perm_ref, v_ref, nb, al)
    _to_residue_major(zx_ref, perm_ref, x_ref, nb, al)
    if conv_v:
        _short_conv_residue_major(v_ref, cwv_ref, cbv_ref, nb)
    _short_conv_residue_major(x_ref, cwx_ref, cbx_ref, nb)
    for b in range(nb):
        r_ref[:, b * ct:(b + 1) * ct] = _dot(f1_ref[...], v_ref[b].astype(BF16))

    for g in range(half // kg):
        tiles = _stage_b_rows(r_ref, g * kg, half, nb, kg, ct)
        a = jnp.concatenate([r_ref[rs, cs] for rs, cs in tiles], axis=0).astype(BF16)
        x = _dot(tb_ref[g], a)
        kf = kf_ref[g * 2 * r:(g + 1) * 2 * r, :]
        xr, xi = x[:r], x[r:]
        kr, ki = kf[:r], kf[r:]
        y = jnp.concatenate([xr * kr - xi * ki, xr * ki + xi * kr], axis=0).astype(BF16)
        bh = _dot(tbi_ref[g], y)
        for t, (rs, cs) in enumerate(tiles):
            r_ref[rs, cs] = bh[t * kg:(t + 1) * kg, :]
    bias = bias_ref[...]
    for b in range(nb):
        y = _dot(g_ref[...], r_ref[:, b * ct:(b + 1) * ct].astype(BF16))
        v_ref[b] = x_ref[b] * (y + v_ref[b] * bias)
    blk = nb * al
    for i in range(o_ref.shape[0] // blk):
        t = jnp.concatenate([v_ref[b, i * al:(i + 1) * al, :] for b in range(nb)], axis=0).astype(BF16)
        o_ref[i * blk:(i + 1) * blk, :] = _dot(permt_ref[...], t).astype(o_ref.dtype)


def _lconv(zv, v_row0, v_col0, zx, x_row0, x_col0, conv_w, conv_b, conv_v, perm, permt, f1, g_inv, tb, tbi,
           kf, order, bias, nbatch, L, ct=256):
    c = kf.shape[-1]
    n2, _ = f1.shape
    ng, r2, _ = tb.shape
    nb = L // (n2 // 2)
    kg = r2 // (2 * nb)
    ncb = c // ct
    once = dict(pipeline_mode=pl.Buffered(1))
    kern = functools.partial(_lconv_kernel, nb=nb, half=n2 // 2, kg=kg, ct=ct, conv_v=conv_v)
    vb0, xb0 = v_row0 // L, x_row0 // L
    return pl.pallas_call(
        kern,
        grid=(ncb, nbatch),
        in_specs=[pl.BlockSpec((L, ct), lambda j, b: (vb0 + b, v_col0 * ncb + j)),
                  pl.BlockSpec((L, ct), lambda j, b: (xb0 + b, x_col0 * ncb + j)),
                  pl.BlockSpec((None, 3, ct), lambda j, b: (0, 0, j)),
                  pl.BlockSpec((None, 1, ct), lambda j, b: (0, 0, j)),
                  pl.BlockSpec((None, 3, ct), lambda j, b: (1, 0, j)),
                  pl.BlockSpec((None, 1, ct), lambda j, b: (1, 0, j)),
                  pl.BlockSpec(perm.shape, lambda j, b: (0, 0), **once),
                  pl.BlockSpec(permt.shape, lambda j, b: (0, 0), **once),
                  pl.BlockSpec(f1.shape, lambda j, b: (0, 0), **once),
                  pl.BlockSpec(g_inv.shape, lambda j, b: (0, 0), **once),
                  pl.BlockSpec(tb.shape, lambda j, b: (0, 0, 0), **once),
                  pl.BlockSpec(tbi.shape, lambda j, b: (0, 0, 0), **once),
                  pl.BlockSpec((None, ng * r2, ct), lambda j, b: (order, 0, j), **once),
                  pl.BlockSpec((1, ct), lambda j, b: (0, j))],
        out_specs=pl.BlockSpec((L, ct), lambda j, b: (b, j)),
        out_shape=jax.ShapeDtypeStruct((nbatch * L, c), BF16),
        scratch_shapes=[pltpu.VMEM((nb, L // nb, ct), F32), pltpu.VMEM((nb, L // nb, ct), F32),
                        pltpu.VMEM((n2, nb * ct), F32)],
        compiler_params=_cp("parallel", "arbitrary"),
        name="lconv",
    )(zv, zx, conv_w, conv_b, conv_w, conv_b, perm, permt, f1, g_inv, tb, tbi, kf, bias.reshape(1, c))


def _lconv_filter_kernel(x_ref, perm_ref, f1_ref, tb_ref, o_ref, t_ref, r_ref, *, nb, half, kg, ct):
    r2 = 2 * nb * kg
    _to_residue_major(x_ref, perm_ref, t_ref, nb, perm_ref.shape[0] // nb)
    for b in range(nb):
        r_ref[:, b * ct:(b + 1) * ct] = _dot(f1_ref[...], t_ref[b].astype(BF16))

    for g in range(half // kg):
        tiles = _stage_b_rows(r_ref, g * kg, half, nb, kg, ct)
        a = jnp.concatenate([r_ref[rs, cs] for rs, cs in tiles], axis=0).astype(BF16)
        o_ref[g * r2:(g + 1) * r2, :] = _dot(tb_ref[g], a)


def _lconv_filter(taps, perm, f1, tb, ct=256):
    norder, n, c = taps.shape
    n2, na = f1.shape
    nb = n // na
    ng, r2, _ = tb.shape
    kg = r2 // (2 * nb)
    kern = functools.partial(_lconv_filter_kernel, nb=nb, half=n2 // 2, kg=kg, ct=ct)
    return pl.pallas_call(
        kern,
        grid=(norder, c // ct),
        in_specs=[pl.BlockSpec((None, n, ct), lambda o, j: (o, 0, j)),
                  pl.BlockSpec(perm.shape, lambda o, j: (0, 0)),
                  pl.BlockSpec(f1.shape, lambda o, j: (0, 0)),
                  pl.BlockSpec(tb.shape, lambda o, j: (0, 0, 0))],
        out_specs=pl.BlockSpec((None, ng * r2, ct), lambda o, j: (o, 0, j)),
        out_shape=jax.ShapeDtypeStruct((norder, ng * r2, c), F32),
        scratch_shapes=[pltpu.VMEM((nb, na, ct), F32), pltpu.VMEM((n2, nb * ct), F32)],
        compiler_params=_cp("parallel", "parallel"),
        name="lconv_filter",
    )(taps, perm, f1, tb)


def _pconv_kernel(fd_ref, gd_ref, v_ref, kf_ref, xg_ref, bias_ref, o_ref, *, half):
    v = v_ref[...]
    x = _dot(fd_ref[...], v.astype(BF16))
    kf = kf_ref[...]
    xr, xi = x[:half], x[half:]
    kr, ki = kf[:half], kf[half:]
    y = jnp.concatenate([xr * kr - xi * ki, xr * ki + xi * kr], axis=0).astype(BF16)
    yt = _dot(gd_ref[...], y)
    o_ref[...] = (xg_ref[...].astype(F32) * (yt + v.astype(F32) * bias_ref[...])).astype(o_ref.dtype)


def _pconv(fd, gd, vsrc, v_lead, kf, order, xsrc, x_lead, bias, nseq, L, out_dtype, ct=512):
    c = kf.shape[-1]
    n2 = fd.shape[0]
    return pl.pallas_call(
        functools.partial(_pconv_kernel, half=n2 // 2),
        grid=(nseq, c // ct),
        in_specs=[pl.BlockSpec((n2, L), lambda s, j: (0, 0)),
                  pl.BlockSpec((L, n2), lambda s, j: (0, 0)),
                  pl.BlockSpec((None,) * len(v_lead) + (L, ct), lambda s, j: v_lead + (s, j)),
                  pl.BlockSpec((None, n2, ct), lambda s, j: (order, 0, j)),
                  pl.BlockSpec((None,) * len(x_lead) + (L, ct), lambda s, j: x_lead + (s, j)),
                  pl.BlockSpec((1, ct), lambda s, j: (0, j))],
        out_specs=pl.BlockSpec((L, ct), lambda s, j: (s, j)),
        out_shape=jax.ShapeDtypeStruct((nseq * L, c), out_dtype),
        compiler_params=_cp("parallel", "parallel"),
        name="pconv",
    )(fd, gd, vsrc, kf, xsrc, bias.reshape(1, c))


def _gate_kernel(lr_ref, wa_ref, ba_ref, o_ref):
    logits = _dot_hi(lr_ref[...], wa_ref[...]) + ba_ref[...]
    o_ref[...] = jax.nn.log_sigmoid(logits) * (1.0 / GLA_TAU)


def _gla_gates(lr, col, wa_cat, ba_cat, tm=1024):
    m = lr.shape[0]
    k, n = wa_cat.shape
    return pl.pallas_call(
        _gate_kernel,
        grid=(m // tm,),
        in_specs=[pl.BlockSpec((tm, k), lambda i: (i, col)),
                  pl.BlockSpec((k, n), lambda i: (0, 0)),
                  pl.BlockSpec((1, n), lambda i: (0, 0))],
        out_specs=pl.BlockSpec((tm, n), lambda i: (i, 0)),
        out_shape=jax.ShapeDtypeStruct((m, n), F32),
        compiler_params=_cp("parallel"),
        name="gla_gates",
    )(lr, wa_cat, ba_cat)


def _split3(x):
    hi = x.astype(BF16)
    r1 = x - hi.astype(F32)
    mid = r1.astype(BF16)
    lo = (r1 - mid.astype(F32)).astype(BF16)
    return hi, mid, lo


def _gla_dir(d, r0, q_ref, k_ref, v_ref, la_ref, o_ref, st_ref, tri, causal, ref_row, last_row,
             heads, dk, dv, scale):
    ch = GLA_CHUNK
    rows = pl.ds(pl.multiple_of(r0, ch), ch)
    la = la_ref[rows, :]
    hi, mid, lo = _split3(la)
    b = _dot(tri, hi) + _dot(tri, mid) + _dot(tri, lo)
    bref = b[ref_row:ref_row + 1, :]
    blast = b[last_row:last_row + 1, :]
    q = q_ref[rows, :].astype(F32) * scale
    k = k_ref[rows, :].astype(F32)
    qt = (q * jnp.exp(b - bref)).astype(BF16)
    kt = (k * jnp.exp(bref - b)).astype(BF16)
    qin = (q * jnp.exp(b)).astype(BF16)
    kst = (k * jnp.exp(blast - b)).astype(BF16)
    dec = jnp.exp(blast)
    for h in range(heads):
        ks = slice(h * dk, (h + 1) * dk)
        vs = slice(h * dv, (h + 1) * dv)
        att = _dot_nt(qt[:, ks], kt[:, ks])
        att = jnp.where(causal, att, 0.0).astype(BF16)
        vh = v_ref[rows, vs]
        st = st_ref[d, h]
        o_ref[rows, vs] = (_dot(att, vh) + _dot_nt(qin[:, ks], st.astype(BF16))).astype(o_ref.dtype)
        st_ref[d, h] = st * dec[:, ks] + _dot_tn(vh, kst[:, ks])


def _gla_kernel(*refs, nch, heads, dk, dv, has_s0, scale, nsq):
    seq_in = [refs[8 * q:8 * (q + 1)] for q in range(nsq)]
    rest = refs[8 * nsq:]
    if has_s0:
        s0, of, ob, st = rest
    else:
        of, ob, st = rest
        s0 = None
    ch = GLA_CHUNK

    @pl.when(pl.program_id(1) == 0)
    def _():
        if has_s0:
            st[...] = s0[...]
        else:
            st[...] = jnp.zeros_like(st)

    r_i = lax.broadcasted_iota(jnp.int32, (ch, ch), 0)
    c_i = lax.broadcasted_iota(jnp.int32, (ch, ch), 1)
    lower = r_i >= c_i
    upper = r_i <= c_i
    tri_l = jnp.where(lower, 1.0, 0.0).astype(BF16)
    tri_u = jnp.where(upper, 1.0, 0.0).astype(BF16)

    def body(c, carry):
        for q, (qf, kf, vf, laf, qb, kb, vb, lab) in enumerate(seq_in):
            _gla_dir(0, c * ch, qf, kf, vf, laf, of.at[q], st.at[q], tri_l, lower, ch // 2 - 1, ch - 1,
                     heads, dk, dv, scale)
            _gla_dir(1, (nch - 1 - c) * ch, qb, kb, vb, lab, ob.at[q], st.at[q], tri_u, upper, ch // 2, 0,
                     heads, dk, dv, scale)
        return carry

    lax.fori_loop(0, nch, body, 0)


def _gla(zmain, qcol, kcol, vcol, la, row0, nseq, L, tb, heads, dk, dv, s0=None, s0_layer=0, nsq=2):
    nblk = L // tb
    rb0 = row0 // tb
    hk, hv = heads * dk, heads * dv
    in_specs, args = [], []
    for q in range(nsq):
        def fwd(s, j, q=q):
            return rb0 + (s * nsq + q) * nblk + j

        def bwd(s, j, q=q):
            return rb0 + (s * nsq + q) * nblk + (nblk - 1 - j)

        for rowf, lcol in ((fwd, 0), (bwd, 1)):
            in_specs += [pl.BlockSpec((tb, hk), lambda s, j, rowf=rowf: (rowf(s, j), qcol)),
                         pl.BlockSpec((tb, hk), lambda s, j, rowf=rowf: (rowf(s, j), kcol)),
                         pl.BlockSpec((tb, hv), lambda s, j, rowf=rowf: (rowf(s, j), vcol)),
                         pl.BlockSpec((tb, hk), lambda s, j, rowf=rowf, lcol=lcol: (rowf(s, j), lcol))]
            args += [zmain, zmain, zmain, la]
    st_spec = pl.BlockSpec((nsq, 2, heads, dv, dk), lambda s, j: (s, 0, 0, 0, 0))
    if s0 is not None:
        in_specs.append(pl.BlockSpec((nsq, None, 2, heads, dv, dk), lambda s, j: (s, s0_layer, 0, 0, 0, 0)))
        args.append(s0)
    kern = functools.partial(_gla_kernel, nch=tb // GLA_CHUNK, heads=heads, dk=dk, dv=dv,
                             has_s0=s0 is not None, scale=dk ** -0.5, nsq=nsq)
    o_shape = jax.ShapeDtypeStruct((nseq // nsq, nsq, L, hv), BF16)
    o_f, o_b, st = pl.pallas_call(
        kern,
        grid=(nseq // nsq, nblk),
        in_specs=in_specs,
        out_specs=[pl.BlockSpec((None, nsq, tb, hv), lambda s, j: (s, 0, j, 0)),
                   pl.BlockSpec((None, nsq, tb, hv), lambda s, j: (s, 0, nblk - 1 - j, 0)),
                   st_spec],
        out_shape=[o_shape, o_shape, jax.ShapeDtypeStruct((nseq, 2, heads, dv, dk), F32)],
        compiler_params=_cp("parallel", "arbitrary"),
        name="gla",
    )(*args)
    return o_f.reshape(nseq * L, hv), o_b.reshape(nseq * L, hv), st


def _gla_post_kernel(ofa_ref, ofb_ref, oba_ref, obb_ref, gr_ref, g_ref, o_ref, *, heads, dv, rows_a, tm):
    o = (_pick(ofa_ref, ofb_ref, rows_a, tm).astype(F32) + _pick(oba_ref, obb_ref, rows_a, tm).astype(F32))
    gate = _silu(gr_ref[...].astype(F32))
    g = g_ref[...]
    for h in range(heads):
        sl = slice(h * dv, (h + 1) * dv)
        oh = o[:, sl]
        ms = jnp.mean(oh * oh, axis=-1, keepdims=True)
        o_ref[:, sl] = (oh * lax.rsqrt(ms + RMS_EPS) * g * gate[:, sl]).astype(o_ref.dtype)


def _gla_post(of_a, of_b, ob_a, ob_b, zmain, grcol, norm_g, heads, dv, tm=512):
    ra, hv = of_a.shape
    m = ra + of_b.shape[0]
    return pl.pallas_call(
        functools.partial(_gla_post_kernel, heads=heads, dv=dv, rows_a=ra, tm=tm),
        grid=(m // tm,),
        in_specs=[*_two_source_specs(ra, tm, hv), *_two_source_specs(ra, tm, hv),
                  pl.BlockSpec((tm, hv), lambda i: (i, grcol)),
                  pl.BlockSpec((1, dv), lambda i: (0, 0))],
        out_specs=pl.BlockSpec((tm, hv), lambda i: (i, 0)),
        out_shape=jax.ShapeDtypeStruct((m, hv), BF16),
        compiler_params=_cp("arbitrary"),
        name="gla_post",
    )(of_a, of_b, ob_a, ob_b, zmain, norm_g.reshape(1, dv))


def _rope_tables(L, dh):
    rows = L // GRID_W
    r = np.repeat(np.arange(rows, dtype=np.float32), GRID_W)
    col = np.tile(np.arange(GRID_W, dtype=np.float32), rows)
    nf = dh // 4
    inv = (np.float32(ROPE_THETA) ** (-np.arange(nf, dtype=np.float32) / nf)).astype(np.float32)
    ang_r = (r[:, None] * inv).astype(np.float32)
    ang_c = (col[:, None] * inv).astype(np.float32)
    cos = np.concatenate([np.cos(ang_r), np.cos(ang_r), np.cos(ang_c), np.cos(ang_c)], -1)
    sin = np.concatenate([-np.sin(ang_r), np.sin(ang_r), -np.sin(ang_c), np.sin(ang_c)], -1)
    cos = np.concatenate([cos, cos], -1).astype(np.float32)
    sin = np.concatenate([sin, sin], -1).astype(np.float32)
    cos_t = np.stack([np.ones_like(cos), cos])
    sin_t = np.stack([np.zeros_like(sin), sin])
    return cos_t, sin_t


def _attn_kernel(*refs, has_ctx, lam_init, dh, bf16_exp, tq):
    if has_ctx:
        (q_ref, k_ref, v_ref, cos_ref, sin_ref, kc_ref, vc_ref, lam_ref, g_ref, o_ref,
         kt_ref, ve_ref, s0_ref, s1_ref) = refs
    else:
        q_ref, k_ref, v_ref, cos_ref, sin_ref, lam_ref, g_ref, o_ref, kt_ref, ve_ref, s0_ref, s1_ref = refs
    hw = 2 * dh
    nf = dh // 4
    lk = k_ref.shape[0]
    nsub = q_ref.shape[0] // tq
    lane = lax.broadcasted_iota(jnp.int32, (tq, hw), 1)
    first = lane < dh
    scale = dh ** -0.5 * math.log2(math.e)

    def rope(x, cos, sin):
        ln = lax.broadcasted_iota(jnp.int32, x.shape, 1)
        sw = jnp.where((ln & (2 * nf - 1)) < nf, pltpu.roll(x, hw - nf, 1), pltpu.roll(x, nf, 1))
        return x * cos + sw * sin

    kt_ref[...] = rope(k_ref[...], cos_ref[...], sin_ref[...]).T.astype(BF16)
    ve_ref[:, :hw] = v_ref[...].astype(BF16)
    ve_ref[:, hw:] = jnp.ones((lk, hw), BF16)

    if has_ctx:
        kct = kc_ref[...].T.astype(BF16)
        vcv = vc_ref[...].astype(BF16)
        vce = jnp.concatenate([vcv, jnp.ones_like(vcv)], axis=1)

    def scores(i, j, s_ref):
        rows = pl.ds(i * tq if isinstance(i, int) else pl.multiple_of(i * tq, tq), tq)
        q = (rope(q_ref[rows, :], cos_ref[rows, :], sin_ref[rows, :]) * scale).astype(BF16)
        qj = jnp.where(first, q, jnp.zeros_like(q)) if j == 0 else jnp.where(first, jnp.zeros_like(q), q)
        s_ref[:, :lk] = _dot(qj, kt_ref[...])
        if has_ctx:
            s_ref[:, lk:] = _dot(qj, kct)

    def prob(t):
        if bf16_exp:
            return jnp.exp2(t.astype(BF16))
        return jnp.exp2(t).astype(BF16)

    def softmax_pv(s_ref):
        s = s_ref[...]
        p = prob(s - jnp.max(s, axis=-1, keepdims=True))
        acc = _dot(p[:, :lk], ve_ref[...])
        if has_ctx:
            acc = acc + _dot(p[:, lk:], vce)
        return acc[:, :hw] / acc[:, hw:]

    lp = lam_ref[...]
    lam = (jnp.exp(jnp.sum(lp[0:1] * lp[1:2], axis=-1, keepdims=True))
           - jnp.exp(jnp.sum(lp[2:3] * lp[3:4], axis=-1, keepdims=True)) + lam_init)
    gain = g_ref[...] * (1.0 - lam_init)

    scores(0, 0, s0_ref)

    def body(i, carry):
        scores(i, 1, s1_ref)
        sm0 = softmax_pv(s0_ref)
        scores(jnp.minimum(i + 1, nsub - 1), 0, s0_ref)
        o = sm0 - lam * softmax_pv(s1_ref)
        ms = jnp.mean(o * o, axis=-1, keepdims=True)
        o_ref[pl.ds(pl.multiple_of(i * tq, tq), tq), :] = (o * lax.rsqrt(ms + RMS_EPS) * gain).astype(o_ref.dtype)
        return carry

    lax.fori_loop(0, nsub, body, 0)


def _diff_attention(zd, cos_t, sin_t, rope_kind, row0, nseq, L, heads, dh, lam_p, norm_g, lam_init,
                    ctx=None, tq=256):
    hw = 2 * dh
    sb0 = row0 // L
    past = 0 if ctx is None else ctx[0].shape[2]
    tab = pl.BlockSpec((None, L, hw), lambda s, h: (rope_kind, 0, 0), pipeline_mode=pl.Buffered(1))
    in_specs = [pl.BlockSpec((L, hw), lambda s, h: (sb0 + s, h)),
                pl.BlockSpec((L, hw), lambda s, h: (sb0 + s, heads + h)),
                pl.BlockSpec((L, hw), lambda s, h: (sb0 + s, 2 * heads + h)),
                tab, tab]
    args = [zd, zd, zd, cos_t, sin_t]
    if ctx is not None:
        ck, cv, layer = ctx
        in_specs += [pl.BlockSpec((None, None, past, hw), lambda s, h: (s, layer, 0, h)),
                     pl.BlockSpec((None, None, past, hw), lambda s, h: (s, layer, 0, h))]
        args += [ck, cv]
    in_specs += [pl.BlockSpec(lam_p.shape, lambda s, h: (0, 0)),
                 pl.BlockSpec((1, hw), lambda s, h: (0, 0))]
    args += [lam_p, norm_g.reshape(1, hw)]
    kern = functools.partial(_attn_kernel, has_ctx=ctx is not None, lam_init=lam_init, dh=dh,
                             bf16_exp=L > 1024, tq=min(tq, L))
    return pl.pallas_call(
        kern,
        grid=(nseq, heads),
        in_specs=in_specs,
        out_specs=pl.BlockSpec((L, hw), lambda s, h: (s, h)),
        out_shape=jax.ShapeDtypeStruct((nseq * L, heads * hw), BF16),
        scratch_shapes=[pltpu.VMEM((hw, L), BF16), pltpu.VMEM((L, 2 * hw), BF16),
                        pltpu.VMEM((min(tq, L), L + past), F32), pltpu.VMEM((min(tq, L), L + past), F32)],
        compiler_params=_cp("parallel", "parallel"),
        name="diff_attn",
    )(*args)


def _mix_kernel(yaa_ref, yab_ref, yb_ref, yca_ref, ycb_ref, g_ref, w_ref, o_ref, acc_ref, *, rows_a, tm):
    k = pl.program_id(1)

    def contrib(y):
        return g_ref[...].astype(F32) * _dot(y, w_ref[...])

    @pl.when(k == 0)
    def _():
        acc_ref[...] = contrib(_pick(yaa_ref, yab_ref, rows_a, tm))

    @pl.when(k == 1)
    def _():
        acc_ref[...] += contrib(yb_ref[...])

    @pl.when(k == 2)
    def _():
        o_ref[...] = (acc_ref[...] + contrib(_pick(yca_ref, ycb_ref, rows_a, tm))).astype(o_ref.dtype)


def _mix(ya_a, ya_b, yb, yc_a, yc_b, gates, wbr, tm=512):
    ra, w = ya_a.shape
    m = yb.shape[0]
    d = wbr.shape[2]
    return pl.pallas_call(
        functools.partial(_mix_kernel, rows_a=ra, tm=tm),
        grid=(m // tm, 3),
        in_specs=[*_two_source_specs(ra, tm, w, ndim_grid=2),
                  pl.BlockSpec((tm, w), lambda i, k: (i, 0)),
                  *_two_source_specs(ra, tm, w, ndim_grid=2),
                  pl.BlockSpec((tm, d), lambda i, k: (i, k)),
                  pl.BlockSpec((None, w, d), lambda i, k: (k, 0, 0))],
        out_specs=pl.BlockSpec((tm, d), lambda i, k: (i, 0)),
        out_shape=jax.ShapeDtypeStruct((m, d), BF16),
        scratch_shapes=[pltpu.VMEM((tm, d), F32)],
        compiler_params=_cp("arbitrary", "arbitrary"),
        name="mix",
    )(ya_a, ya_b, yb, yc_a, yc_b, gates, wbr)


def _pad_cols(w, n):
    return jnp.pad(w, ((0, 0), (0, n - w.shape[1])))


def kernel(x_prompt, x_sample, cache_k, cache_v, state_gla, c, c_ctx, w_mod, b_mod, ln_g, ln_b, ffn_w1, ffn_w3, ffn_w2, w_in, hy_conv_w, hy_conv_b, hy_w1, hy_b1, hy_freq, hy_w2, hy_b2, hy_w3, hy_decay, hy_bias, gla_wa, gla_ba, gla_norm_g, diff_lam, diff_norm_g, w_branch_a, w_branch_b, w_branch_c, w_out):
    batch, seq, d = x_prompt.shape
    dec_batch, dec_seq, _ = x_sample.shape
    depth = w_mod.shape[0]
    ffn_dim = ffn_w1.shape[3]
    hy_w = hy_bias.shape[2]
    heads_g, dk_g = 4, gla_wa.shape[3] // 4
    dv_g = gla_norm_g.shape[1]
    rank = gla_wa.shape[2]
    dh = diff_lam.shape[2]
    heads_d = cache_k.shape[3]
    dw = heads_d * 2 * dh
    gw = heads_g * dv_g
    gk = heads_g * dk_g
    assert batch * seq == GROUP and dec_seq == GROUP
    mp = batch * seq
    m = mp + dec_batch * dec_seq
    ngroups = 1 + dec_batch
    alpha = (2 * depth) ** 0.25
    lam_inits = [0.8 - 0.6 * math.exp(-0.3 * l) for l in range(depth)]

    c_main = 3 * hy_w + 2 * gk + 2 * gw
    c_lr = c_main
    c_d = c_lr + 2 * rank
    c_g = c_d + 3 * dw

    cond = jnp.concatenate([c_ctx[None], c, jnp.zeros((16 - ngroups, d), F32)], axis=0)
    mod = _modulation(cond, w_mod, b_mod)[:, :ngroups].reshape(depth, ngroups, N_MOD, d)

    na = 2 * dec_seq // FFT_NB
    n_s = 2 * dec_seq
    f1_half = jnp.asarray(_dft_a(na, na // 2), BF16)
    f1_full = jnp.asarray(_dft_a(na, na), BF16)
    g_s = jnp.asarray(_idft_a(na, n_s, na // 2), BF16)
    tb_np, tbi_np = _dft_b(na, FFT_NB, FFT_K1G)
    tb, tbi = jnp.asarray(tb_np, BF16), jnp.asarray(tbi_np, BF16)
    perm_np = _row_permutation(FFT_NB, SUBLANES_F32)
    perm, permt = jnp.asarray(perm_np, BF16), jnp.asarray(perm_np.T, BF16)
    fp_half = jnp.asarray(_dft_a(2 * seq, seq), BF16)
    fp_full = jnp.asarray(_dft_a(2 * seq, 2 * seq), BF16)
    g_p = jnp.asarray(_idft_a(2 * seq, 2 * seq, seq), BF16)
    cos_t, sin_t = _rope_tables(dec_seq, dh)
    cos_t, sin_t = jnp.asarray(cos_t), jnp.asarray(sin_t)

    x, h = _premod(x_prompt.reshape(mp, d), x_sample.reshape(dec_batch * dec_seq, d), mod, 0)

    fp = ((ffn_dim + 511) // 512) * 512
    ck = cache_k.reshape(dec_batch, depth, cache_k.shape[2], dw)
    cv = cache_v.reshape(dec_batch, depth, cache_v.shape[2], dw)
    state_t = jnp.swapaxes(state_gla, -1, -2)
    new_k, new_v, new_s = [], [], []

    for l in range(depth):
        w2 = _cast_pad_rows(ffn_w2, l, 0, fp)
        hid = _ffn1(h, ffn_w1, ffn_w3, l, 0)
        x, h = _mm_ln(hid, w2, x, mod, l, 2, 0.5, alpha, l, 3, ln_g[l, 0], ln_b[l, 0])

        wi = w_in[l]
        zmain = _proj(h, wi[:, :c_main].astype(BF16), BF16)
        zlr = _proj(h, _pad_cols(wi[:, c_lr:c_d].astype(BF16), LANES), F32)
        zd = _proj(h, wi[:, c_d:c_g].astype(BF16), F32)
        gates = _proj(h, wi[:, c_g:].astype(BF16), BF16, act="sigmoid")

        u3 = _short_conv(zmain, mp, hy_conv_w[l], hy_conv_b[l], seq, dec_seq, hy_w)
        fargs = (hy_w1[l], hy_b1[l], hy_freq[l], hy_w2[l], hy_b2[l], hy_w3[l], hy_decay[l], hy_w)
        taps_s = _hyena_filter_taps(dec_seq, *fargs)
        taps_p = _hyena_filter_taps(seq, *fargs)
        kf_s = _lconv_filter(taps_s, perm, f1_full, tb)
        kf_p = _lmm(fp_full, taps_p, lambda b: (b,), 2, F32, tn=hy_w)

        cw3 = jnp.swapaxes(hy_conv_w[l].reshape(3, 3, hy_w), 0, 1)
        cb3 = hy_conv_b[l].reshape(3, 1, hy_w)
        z1s = _lconv(zmain, mp, 0, zmain, mp, 1, cw3[0:2], cb3[0:2], True, perm, permt, f1_half, g_s,
                     tb, tbi, kf_s, 0, hy_bias[l, 0], dec_batch, dec_seq)
        ya_s = _lconv(z1s, 0, 0, zmain, mp, 2, cw3[1:3], cb3[1:3], False, perm, permt, f1_half, g_s,
                      tb, tbi, kf_s, 1, hy_bias[l, 1], dec_batch, dec_seq)
        z1 = _pconv(fp_half, g_p, u3, (0,), kf_p, 0, u3, (1,), hy_bias[l, 0], batch, seq, F32)
        ya_p = _pconv(fp_half, g_p, z1, (), kf_p, 1, u3, (2,), hy_bias[l, 1], batch, seq, BF16)

        wa_cat = jnp.zeros((LANES, 2 * gk), F32)
        wa_cat = wa_cat.at[:rank, :gk].set(gla_wa[l, 0]).at[rank:2 * rank, gk:].set(gla_wa[l, 1])
        ba_cat = jnp.concatenate([gla_ba[l, 0], gla_ba[l, 1]]).reshape(1, 2 * gk)
        la = _gla_gates(zlr, 0, wa_cat, ba_cat)
        qcol, kcol, vcol, grcol = (3 * hy_w) // gk, (3 * hy_w) // gk + 1, (3 * hy_w + 2 * gk) // gw, \
            (3 * hy_w + 2 * gk) // gw + 1
        of_p, ob_p, st_p = _gla(zmain, qcol, kcol, vcol, la, 0, batch, seq, seq, heads_g, dk_g, dv_g)
        of_s, ob_s, _ = _gla(zmain, qcol, kcol, vcol, la, mp, dec_batch, dec_seq, 512, heads_g, dk_g, dv_g,
                             s0=state_t, s0_layer=l)
        yb = _gla_post(of_p, of_s, ob_p, ob_s, zmain, grcol, gla_norm_g[l], heads_g, dv_g)
        new_s.append(jnp.swapaxes(st_p, -1, -2))

        yc_p = _diff_attention(zd, cos_t, sin_t, 0, 0, batch, seq, heads_d, dh, diff_lam[l],
                               diff_norm_g[l], lam_inits[l])
        yc_s = _diff_attention(zd, cos_t, sin_t, 1, mp, dec_batch, dec_seq, heads_d, dh, diff_lam[l],
                               diff_norm_g[l], lam_inits[l], ctx=(ck, cv, l), tq=512)
        new_k.append(zd[:mp, dw:2 * dw].reshape(batch, seq, heads_d, 2, dh))
        new_v.append(zd[:mp, 2 * dw:3 * dw].reshape(batch, seq, heads_d, 2 * dh))

        wbr = jnp.stack([w_branch_a[l], w_branch_b[l], w_branch_c[l]]).astype(BF16)
        y = _mix(ya_p, ya_s, yb, yc_p, yc_s, gates, wbr)
        x, h = _mm_ln(y, w_out[l].astype(BF16), x, mod, l, 5, 1.0, alpha, l, 6, ln_g[l, 1], ln_b[l, 1],
                      tm=512)

        w2 = _cast_pad_rows(ffn_w2, l, 1, fp)
        hid = _ffn1(h, ffn_w1, ffn_w3, l, 1)
        if l + 1 < depth:
            x, h = _mm_ln(hid, w2, x, mod, l, 8, 0.5, alpha, l + 1, 0, ln_g[l, 2], ln_b[l, 2])
        else:
            xp, _ = _mm_ln(hid, w2, x, mod, l, 8, 0.5, alpha, l, 0, ln_g[l, 2], ln_b[l, 2], rows=mp)
            xs, _ = _mm_ln(hid, w2, x, mod, l, 8, 0.5, alpha, l, 0, ln_g[l, 2], ln_b[l, 2], row0=mp,
                           rows=m - mp)

    y_prompt = xp.reshape(batch, seq, d)
    y_sample = xs.reshape(dec_batch, dec_seq, d)
    return (y_prompt, y_sample, jnp.stack(new_k, axis=1), jnp.stack(new_v, axis=1),
            jnp.stack(new_s, axis=1))
```

```python
import functools
import math

import numpy as np
import jax
import jax.numpy as jnp
from jax import lax
from jax.experimental import pallas as pl
from jax.experimental.pallas import tpu as pltpu

F32 = jnp.float32
BF16 = jnp.bfloat16

GRID_W = 64
N_MOD = 9
HY_BANDS = 8
GLA_TAU = 16.0
GLA_CHUNK = 64
ROPE_THETA = 10000.0
LN_EPS = 1e-5
RMS_EPS = 1e-6

LANES = 128
SUBLANES_F32 = 8
VMEM_BYTES_V7X = 64 * 1024 * 1024
VMEM_LIMIT = VMEM_BYTES_V7X - 8 * 1024 * 1024

GROUP = 4096
FFT_NB = 16
FFT_K1G = 8


def _cp(*sem):
    return pltpu.CompilerParams(dimension_semantics=sem, vmem_limit_bytes=VMEM_LIMIT)


def _dot(a, b):
    return jnp.dot(a, b, preferred_element_type=F32)


def _dot_nt(a, b):
    return lax.dot_general(a, b, (((1,), (1,)), ((), ())), preferred_element_type=F32)


def _dot_tn(a, b):
    return lax.dot_general(a, b, (((0,), (0,)), ((), ())), preferred_element_type=F32)


def _dot_hi(a, b):
    return jnp.dot(a, b, preferred_element_type=F32, precision=lax.Precision.HIGHEST)


def _silu(x):
    return x * jax.nn.sigmoid(x)


def _mod_kernel(c_ref, w_ref, b_ref, o_ref):
    c = c_ref[...]
    o_ref[...] = _dot(_silu(c).astype(BF16), w_ref[...].astype(BF16)) + b_ref[...]


def _modulation(cond, w_mod, b_mod):
    depth, d, n = w_mod.shape
    r = cond.shape[0]
    tn = 1024
    return pl.pallas_call(
        _mod_kernel,
        grid=(depth, n // tn),
        in_specs=[pl.BlockSpec((r, d), lambda l, j: (0, 0)),
                  pl.BlockSpec((None, d, tn), lambda l, j: (l, 0, j)),
                  pl.BlockSpec((None, 1, tn), lambda l, j: (l, 0, j))],
        out_specs=pl.BlockSpec((None, r, tn), lambda l, j: (l, 0, j)),
        out_shape=jax.ShapeDtypeStruct((depth, r, n), F32),
        compiler_params=_cp("parallel", "parallel"),
        name="mod",
    )(cond, w_mod, b_mod.reshape(depth, 1, n))


def _two_source_specs(rows_a, tm, width, col=0, ndim_grid=1):
    na = rows_a // tm
    if ndim_grid == 1:
        return (pl.BlockSpec((tm, width), lambda i: (jnp.minimum(i, na - 1), col)),
                pl.BlockSpec((tm, width), lambda i: (jnp.maximum(i - na, 0), col)))
    return (pl.BlockSpec((tm, width), lambda i, k: (jnp.minimum(i, na - 1), col)),
            pl.BlockSpec((tm, width), lambda i, k: (jnp.maximum(i - na, 0), col)))


def _pick(a_ref, b_ref, rows_a, tm):
    return jnp.where(pl.program_id(0) < rows_a // tm, a_ref[...], b_ref[...])


def _premod_kernel(xa_ref, xb_ref, mod_ref, x_ref, o_ref, *, rows_a, tm):
    x = _pick(xa_ref, xb_ref, rows_a, tm)
    x_ref[...] = x
    o_ref[...] = (x * (1.0 + mod_ref[1:2, :]) + mod_ref[0:1, :]).astype(o_ref.dtype)


def _premod(xa, xb, mod, layer, tm=512):
    ra, d = xa.shape
    m = ra + xb.shape[0]
    return pl.pallas_call(
        functools.partial(_premod_kernel, rows_a=ra, tm=tm),
        grid=(m // tm,),
        in_specs=[*_two_source_specs(ra, tm, d),
                  pl.BlockSpec((None, None, N_MOD, d), lambda i: (layer, (i * tm) // GROUP, 0, 0))],
        out_specs=[pl.BlockSpec((tm, d), lambda i: (i, 0)), pl.BlockSpec((tm, d), lambda i: (i, 0))],
        out_shape=[jax.ShapeDtypeStruct((m, d), F32), jax.ShapeDtypeStruct((m, d), BF16)],
        compiler_params=_cp("arbitrary"),
        name="premod",
    )(xa, xb, mod)


def _ffn1_kernel(h_ref, w1_ref, w3_ref, o_ref, w1b_ref, w3b_ref, *, tf, f_valid):
    @pl.when(pl.program_id(1) == 0)
    def _():
        col = pl.program_id(0) * tf + lax.broadcasted_iota(jnp.int32, w1_ref.shape, 1)
        keep = col < f_valid
        w1b_ref[...] = jnp.where(keep, w1_ref[...], 0.0).astype(BF16)
        w3b_ref[...] = jnp.where(keep, w3_ref[...], 0.0).astype(BF16)

    h = h_ref[...]
    a = _dot(h, w1b_ref[...])
    b = _dot(h, w3b_ref[...])
    o_ref[...] = (_silu(a) * b).astype(o_ref.dtype)


def _ffn1(h, w1, w3, layer, sub, tm=1024, tf=512):
    m, d = h.shape
    f = w1.shape[-1]
    nf = pl.cdiv(f, tf)
    wspec = pl.BlockSpec((None, None, d, tf), lambda j, i: (layer, sub, 0, j))
    return pl.pallas_call(
        functools.partial(_ffn1_kernel, tf=tf, f_valid=f),
        grid=(nf, m // tm),
        in_specs=[pl.BlockSpec((tm, d), lambda j, i: (i, 0)), wspec, wspec],
        out_specs=pl.BlockSpec((tm, tf), lambda j, i: (i, j)),
        out_shape=jax.ShapeDtypeStruct((m, nf * tf), BF16),
        scratch_shapes=[pltpu.VMEM((d, tf), BF16), pltpu.VMEM((d, tf), BF16)],
        compiler_params=_cp("parallel", "arbitrary"),
        name="ffn1",
    )(h, w1, w3)


def _cast_rows_kernel(w_ref, o_ref, *, rows_valid, tr):
    row = pl.program_id(0) * tr + lax.broadcasted_iota(jnp.int32, w_ref.shape, 0)
    o_ref[...] = jnp.where(row < rows_valid, w_ref[...], 0.0).astype(o_ref.dtype)


def _cast_pad_rows(w, layer, sub, rows_out, tr=512):
    f, d = w.shape[-2:]
    return pl.pallas_call(
        functools.partial(_cast_rows_kernel, rows_valid=f, tr=tr),
        grid=(rows_out // tr,),
        in_specs=[pl.BlockSpec((None, None, tr, d), lambda i: (layer, sub, i, 0))],
        out_specs=pl.BlockSpec((tr, d), lambda i: (i, 0)),
        out_shape=jax.ShapeDtypeStruct((rows_out, d), BF16),
        compiler_params=_cp("parallel"),
        name="cast_w2",
    )(w)


def _mm_ln_kernel(a_ref, w_ref, x_ref, mod_ref, nmod_ref, g_ref, b_ref, xo_ref, ho_ref, *,
                  gate_row, coef, alpha, nshift_row):
    gate = coef * mod_ref[gate_row:gate_row + 1, :]
    xr = alpha * x_ref[...] + gate * _dot(a_ref[...], w_ref[...])
    mu = jnp.mean(xr, axis=-1, keepdims=True)
    xc = xr - mu
    var = jnp.mean(xc * xc, axis=-1, keepdims=True)
    xn = xc * lax.rsqrt(var + LN_EPS) * g_ref[...] + b_ref[...]
    xo_ref[...] = xn
    ho_ref[...] = (xn * (1.0 + nmod_ref[nshift_row + 1:nshift_row + 2, :])
                   + nmod_ref[nshift_row:nshift_row + 1, :]).astype(ho_ref.dtype)


def _mm_ln(a, w, x, mod, layer, gate_row, coef, alpha, nlayer, nshift_row, ln_g, ln_b, tm=256,
           row0=0, rows=None):
    kdim = a.shape[1]
    m = a.shape[0] if rows is None else rows
    d = w.shape[1]
    rb0 = row0 // tm
    kern = functools.partial(_mm_ln_kernel, gate_row=gate_row, coef=coef, alpha=alpha, nshift_row=nshift_row)
    return pl.pallas_call(
        kern,
        grid=(m // tm,),
        in_specs=[pl.BlockSpec((tm, kdim), lambda i: (rb0 + i, 0)),
                  pl.BlockSpec((kdim, d), lambda i: (0, 0), pipeline_mode=pl.Buffered(1)),
                  pl.BlockSpec((tm, d), lambda i: (rb0 + i, 0)),
                  pl.BlockSpec((None, None, N_MOD, d), lambda i: (layer, ((rb0 + i) * tm) // GROUP, 0, 0)),
                  pl.BlockSpec((None, None, N_MOD, d), lambda i: (nlayer, ((rb0 + i) * tm) // GROUP, 0, 0)),
                  pl.BlockSpec((1, d), lambda i: (0, 0)),
                  pl.BlockSpec((1, d), lambda i: (0, 0))],
        out_specs=[pl.BlockSpec((tm, d), lambda i: (i, 0)),
                   pl.BlockSpec((tm, d), lambda i: (i, 0))],
        out_shape=[jax.ShapeDtypeStruct((m, d), F32), jax.ShapeDtypeStruct((m, d), BF16)],
        compiler_params=_cp("parallel"),
        name="mm_ln",
    )(a, w, x, mod, mod, ln_g.reshape(1, d), ln_b.reshape(1, d))


def _proj_kernel(h_ref, w_ref, o_ref, *, act):
    r = _dot(h_ref[...], w_ref[...])
    if act == "sigmoid":
        r = jax.nn.sigmoid(r)
    o_ref[...] = r.astype(o_ref.dtype)


def _proj(h, w, out_dtype, act=None, tm=1024, tn=1024):
    m, d = h.shape
    n = w.shape[1]
    tn = min(tn, n)
    return pl.pallas_call(
        functools.partial(_proj_kernel, act=act),
        grid=(n // tn, m // tm),
        in_specs=[pl.BlockSpec((tm, d), lambda j, i: (i, 0)),
                  pl.BlockSpec((d, tn), lambda j, i: (0, j))],
        out_specs=pl.BlockSpec((tm, tn), lambda j, i: (i, j)),
        out_shape=jax.ShapeDtypeStruct((m, n), out_dtype),
        compiler_params=_cp("parallel", "arbitrary"),
        name="proj",
    )(h, w)


def _sconv_kernel(z_ref, zp_ref, zn_ref, w_ref, b_ref, o_ref, *, tm, halo, lp, ls):
    i = pl.program_id(0)
    u = z_ref[...].astype(F32)
    prev = zp_ref[...].astype(F32)[halo - 1:halo, :]
    nxt = zn_ref[...].astype(F32)[0:1, :]
    row = lax.broadcasted_iota(jnp.int32, u.shape, 0)
    lseq = jnp.where((i * tm) // GROUP == 0, lp, ls)
    pos = (row + i * tm) & (lseq - 1)
    up = jnp.where(row == 0, prev, pltpu.roll(u, 1, 0))
    up = jnp.where(pos == 0, 0.0, up)
    un = jnp.where(row == tm - 1, nxt, pltpu.roll(u, tm - 1, 0))
    un = jnp.where(pos == lseq - 1, 0.0, un)
    w = w_ref[...]
    o_ref[...] = (up * w[0:1, :] + u * w[1:2, :] + un * w[2:3, :] + b_ref[...]).astype(o_ref.dtype)


def _short_conv(z, rows, conv_w, conv_b, lp, ls, width, tm=1024, ct=512, halo=16):
    m = rows
    nct = width // ct
    nrb = m // halo
    kern = functools.partial(_sconv_kernel, tm=tm, halo=halo, lp=lp, ls=ls)
    return pl.pallas_call(
        kern,
        grid=(m // tm, 3 * nct),
        in_specs=[pl.BlockSpec((tm, ct), lambda i, j: (i, j)),
                  pl.BlockSpec((halo, ct), lambda i, j: (jnp.maximum(i * (tm // halo) - 1, 0), j)),
                  pl.BlockSpec((halo, ct), lambda i, j: (jnp.minimum((i + 1) * (tm // halo), nrb - 1), j)),
                  pl.BlockSpec((3, ct), lambda i, j: (0, j)),
                  pl.BlockSpec((1, ct), lambda i, j: (0, j))],
        out_specs=pl.BlockSpec((None, tm, ct), lambda i, j: (j // nct, i, j % nct)),
        out_shape=jax.ShapeDtypeStruct((3, m, width), F32),
        compiler_params=_cp("parallel", "parallel"),
        name="sconv",
    )(z, z, z, conv_w, conv_b.reshape(1, -1))


def _filter_features(L):
    t = np.linspace(0.0, 1.0, L, dtype=np.float32)
    w = (np.float32(2.0 * math.pi / L) * np.arange(L, dtype=np.float32)).astype(np.float32)
    f = np.linspace(1e-4, HY_BANDS - 1, HY_BANDS, dtype=np.float32)
    wf = (w[:, None] * f).astype(np.float32)
    feats = np.concatenate([t[:, None], np.cos(wf), -np.sin(wf)], -1).astype(np.float32)
    idx = np.concatenate([np.arange(L), [0], np.arange(L - 1, 0, -1)])
    tab = np.zeros((2 * L, 32), np.float32)
    tab[:, :feats.shape[1]] = feats[idx]
    tab[:, 24] = t[idx]
    tab[:L, 25] = 1.0
    tab[L + 1:, 26] = -1.0
    return tab, feats.shape[1]


def _filter_kernel(tab_ref, w1_ref, b1_ref, fr_ref, w2_ref, b2_ref, w3_ref, dec_ref, o_ref):
    tab = tab_ref[...]
    fr = fr_ref[...]
    hdn = jnp.sin(fr[0:1, :] * (_dot_hi(tab, w1_ref[...]) + b1_ref[...]))
    hdn = jnp.sin(fr[1:2, :] * (_dot_hi(hdn, w2_ref[...]) + b2_ref[...]))
    t = tab[:, 24:25]
    sign = tab[:, 25:26] + tab[:, 26:27]
    for o in range(2):
        o_ref[o] = sign * (_dot_hi(hdn, w3_ref[o]) * jnp.exp(-t * jnp.abs(dec_ref[o])))


def _hyena_filter_taps(L, hy_w1, hy_b1, hy_freq, hy_w2, hy_b2, hy_w3, hy_decay, width, rb=256):
    tab_np, nfeat = _filter_features(L)
    fh = hy_w1.shape[1]
    w1p = jnp.zeros((32, fh), F32).at[:nfeat].set(hy_w1)
    n = 2 * L
    nhalf = L // rb
    w3d = jnp.transpose(hy_w3.reshape(fh, 2, 2, width), (1, 2, 0, 3))
    decd = hy_decay.reshape(2, 2, 1, width)
    return pl.pallas_call(
        _filter_kernel,
        grid=(n // rb,),
        in_specs=[pl.BlockSpec((rb, 32), lambda i: (i, 0)),
                  pl.BlockSpec((32, fh), lambda i: (0, 0)),
                  pl.BlockSpec((1, fh), lambda i: (0, 0)),
                  pl.BlockSpec((2, fh), lambda i: (0, 0)),
                  pl.BlockSpec((fh, fh), lambda i: (0, 0)),
                  pl.BlockSpec((1, fh), lambda i: (0, 0)),
                  pl.BlockSpec((2, None, fh, width), lambda i: (0, i // nhalf, 0, 0)),
                  pl.BlockSpec((2, None, 1, width), lambda i: (0, i // nhalf, 0, 0))],
        out_specs=pl.BlockSpec((2, rb, width), lambda i: (0, i, 0)),
        out_shape=jax.ShapeDtypeStruct((2, n, width), F32),
        compiler_params=_cp("parallel"),
        name="hyfilter",
    )(jnp.asarray(tab_np), w1p, hy_b1.reshape(1, fh), hy_freq, hy_w2, hy_b2.reshape(1, fh), w3d, decd)


def _dft_a(na, ka):
    k1 = np.arange(na // 2)[:, None].astype(np.float64)
    a = np.arange(ka)[None, :].astype(np.float64)
    th = 2.0 * np.pi * a * (k1 + 0.5) / na
    return np.concatenate([np.cos(th), -np.sin(th)], 0)


def _idft_a(na, n, rows):
    k1 = np.arange(na // 2)[None, :].astype(np.float64)
    a = np.arange(rows)[:, None].astype(np.float64)
    th = 2.0 * np.pi * a * (k1 + 0.5) / na
    return (2.0 / n) * np.concatenate([np.cos(th), -np.sin(th)], 1)


def _dft_b(na, nb, g):
    n = na * nb
    half = na // 2
    k1 = np.arange(half).astype(np.float64)
    b = np.arange(nb).astype(np.float64)
    k2 = np.arange(nb).astype(np.float64)
    phi = 2.0 * np.pi * (b[None, None, :] * k2[None, :, None] / nb
                         + b[None, None, :] * (k1[:, None, None] + 0.5) / n)
    c, s = np.cos(phi), np.sin(phi)
    ng = half // g
    r = g * nb
    fwd = np.zeros((ng, 2 * r, 2 * r))
    for q in range(g):
        rows = slice(q * nb, (q + 1) * nb)
        rows_i = slice(r + q * nb, r + (q + 1) * nb)
        cols = slice(q, r, g)
        cols_i = slice(r + q, 2 * r, g)
        cq, sq = c[q::g], s[q::g]
        fwd[:, rows, cols] = cq
        fwd[:, rows, cols_i] = sq
        fwd[:, rows_i, cols] = -sq
        fwd[:, rows_i, cols_i] = cq
    inv = np.transpose(fwd, (0, 2, 1))
    return fwd, inv


def _lmm_kernel(f_ref, x_ref, o_ref):
    o_ref[...] = _dot(f_ref[...], x_ref[...].astype(BF16)).astype(o_ref.dtype)


def _lmm(f, x, x_index, nbatch, out_dtype, tn=2048):
    mo, k = f.shape
    n = x.shape[-1]
    lead = len(x.shape) - 2
    return pl.pallas_call(
        _lmm_kernel,
        grid=(nbatch, n // tn),
        in_specs=[pl.BlockSpec((mo, k), lambda b, j: (0, 0)),
                  pl.BlockSpec((None,) * lead + (k, tn), lambda b, j: x_index(b) + (0, j))],
        out_specs=pl.BlockSpec((None, mo, tn), lambda b, j: (b, 0, j)),
        out_shape=jax.ShapeDtypeStruct((nbatch, mo, n), out_dtype),
        compiler_params=_cp("parallel", "parallel"),
        name="dft_a",
    )(f, x)


def _stage_b_rows(r_ref, r0, half, nb, kg, ct):
    return [(pl.ds(ri * half + r0, kg), slice(b * ct, (b + 1) * ct)) for ri in range(2) for b in range(nb)]


def _row_permutation(nb, al):
    p = np.zeros((nb * al, nb * al), np.float32)
    for b in range(nb):
        for a in range(al):
            p[b * al + a, a * nb + b] = 1.0
    return p


def _to_residue_major(src_ref, perm_ref, dst_ref, nb, al):
    blk = nb * al
    for i in range(src_ref.shape[0] // blk):
        z = _dot(perm_ref[...], src_ref[i * blk:(i + 1) * blk, :].astype(BF16))
        for b in range(nb):
            dst_ref[b, i * al:(i + 1) * al, :] = z[b * al:(b + 1) * al, :]


def _short_conv_residue_major(u_ref, w_ref, cb_ref, nb):
    rows = u_ref.shape[1]
    row = lax.broadcasted_iota(jnp.int32, u_ref.shape[1:], 0)
    w = w_ref[...]
    cb = cb_ref[...]
    first = u_ref[0]
    prev = jnp.where(row == 0, 0.0, pltpu.roll(u_ref[nb - 1], 1, 0))
    for b in range(nb):
        cur = first if b == 0 else u_ref[b]
        nxt = u_ref[b + 1] if b + 1 < nb else jnp.where(row == rows - 1, 0.0, pltpu.roll(first, rows - 1, 0))
        u_ref[b] = prev * w[0:1, :] + cur * w[1:2, :] + nxt * w[2:3, :] + cb
        prev = cur


def _lconv_kernel(zv_ref, zx_ref, cwv_ref, cbv_ref, cwx_ref, cbx_ref, perm_ref, permt_ref, f1_ref, g_ref,
                  tb_ref, tbi_ref, kf_ref, bias_ref, o_ref, v_ref, x_ref, r_ref, *, nb, half, kg, ct, conv_v):
    r = nb * kg
    al = perm_ref.shape[0] // nb
    _to_residue_major(zv_ref, perm_ref, v_ref, nb, al)
    _to_residue_major(zx_ref, perm_ref, x_ref, nb, al)
    if conv_v:
        _short_conv_residue_major(v_ref, cwv_ref, cbv_ref, nb)
    _short_conv_residue_major(x_ref, cwx_ref, cbx_ref, nb)
    for b in range(nb):
        r_ref[:, b * ct:(b + 1) * ct] = _dot(f1_ref[...], v_ref[b].astype(BF16))

    for g in range(half // kg):
        tiles = _stage_b_rows(r_ref, g * kg, half, nb, kg, ct)
        a = jnp.concatenate([r_ref[rs, cs] for rs, cs in tiles], axis=0).astype(BF16)
        x = _dot(tb_ref[g], a)
        kf = kf_ref[g * 2 * r:(g + 1) * 2 * r, :]
        xr, xi = x[:r], x[r:]
        kr, ki = kf[:r], kf[r:]
        y = jnp.concatenate([xr * kr - xi * ki, xr * ki + xi * kr], axis=0).astype(BF16)
        bh = _dot(tbi_ref[g], y)
        for t, (rs, cs) in enumerate(tiles):
            r_ref[rs, cs] = bh[t * kg:(t + 1) * kg, :]
    bias = bias_ref[...]
    for b in range(nb):
        y = _dot(g_ref[...], r_ref[:, b * ct:(b + 1) * ct].astype(BF16))
        v_ref[b] = x_ref[b] * (y + v_ref[b] * bias)
    blk = nb * al
    for i in range(o_ref.shape[0] // blk):
        t = jnp.concatenate([v_ref[b, i * al:(i + 1) * al, :] for b in range(nb)], axis=0).astype(BF16)
        o_ref[i * blk:(i + 1) * blk, :] = _dot(permt_ref[...], t).astype(o_ref.dtype)


def _lconv(zv, v_row0, v_col0, zx, x_row0, x_col0, conv_w, conv_b, conv_v, perm, permt, f1, g_inv, tb, tbi,
           kf, order, bias, nbatch, L, ct=256):
    c = kf.shape[-1]
    n2, _ = f1.shape
    ng, r2, _ = tb.shape
    nb = L // (n2 // 2)
    kg = r2 // (2 * nb)
    ncb = c // ct
    once = dict(pipeline_mode=pl.Buffered(1))
    kern = functools.partial(_lconv_kernel, nb=nb, half=n2 // 2, kg=kg, ct=ct, conv_v=conv_v)
    vb0, xb0 = v_row0 // L, x_row0 // L
    return pl.pallas_call(
        kern,
        grid=(ncb, nbatch),
        in_specs=[pl.BlockSpec((L, ct), lambda j, b: (vb0 + b, v_col0 * ncb + j)),
                  pl.BlockSpec((L, ct), lambda j, b: (xb0 + b, x_col0 * ncb + j)),
                  pl.BlockSpec((None, 3, ct), lambda j, b: (0, 0, j)),
                  pl.BlockSpec((None, 1, ct), lambda j, b: (0, 0, j)),
                  pl.BlockSpec((None, 3, ct), lambda j, b: (1, 0, j)),
                  pl.BlockSpec((None, 1, ct), lambda j, b: (1, 0, j)),
                  pl.BlockSpec(perm.shape, lambda j, b: (0, 0), **once),
                  pl.BlockSpec(permt.shape, lambda j, b: (0, 0), **once),
                  pl.BlockSpec(f1.shape, lambda j, b: (0, 0), **once),
                  pl.BlockSpec(g_inv.shape, lambda j, b: (0, 0), **once),
                  pl.BlockSpec(tb.shape, lambda j, b: (0, 0, 0), **once),
                  pl.BlockSpec(tbi.shape, lambda j, b: (0, 0, 0), **once),
                  pl.BlockSpec((None, ng * r2, ct), lambda j, b: (order, 0, j), **once),
                  pl.BlockSpec((1, ct), lambda j, b: (0, j))],
        out_specs=pl.BlockSpec((L, ct), lambda j, b: (b, j)),
        out_shape=jax.ShapeDtypeStruct((nbatch * L, c), BF16),
        scratch_shapes=[pltpu.VMEM((nb, L // nb, ct), F32), pltpu.VMEM((nb, L // nb, ct), F32),
                        pltpu.VMEM((n2, nb * ct), F32)],
        compiler_params=_cp("parallel", "arbitrary"),
        name="lconv",
    )(zv, zx, conv_w, conv_b, conv_w, conv_b, perm, permt, f1, g_inv, tb, tbi, kf, bias.reshape(1, c))


def _lconv_filter_kernel(x_ref, perm_ref, f1_ref, tb_ref, o_ref, t_ref, r_ref, *, nb, half, kg, ct):
    r2 = 2 * nb * kg
    _to_residue_major(x_ref, perm_ref, t_ref, nb, perm_ref.shape[0] // nb)
    for b in range(nb):
        r_ref[:, b * ct:(b + 1) * ct] = _dot(f1_ref[...], t_ref[b].astype(BF16))

    for g in range(half // kg):
        tiles = _stage_b_rows(r_ref, g * kg, half, nb, kg, ct)
        a = jnp.concatenate([r_ref[rs, cs] for rs, cs in tiles], axis=0).astype(BF16)
        o_ref[g * r2:(g + 1) * r2, :] = _dot(tb_ref[g], a)


def _lconv_filter(taps, perm, f1, tb, ct=256):
    norder, n, c = taps.shape
    n2, na = f1.shape
    nb = n // na
    ng, r2, _ = tb.shape
    kg = r2 // (2 * nb)
    kern = functools.partial(_lconv_filter_kernel, nb=nb, half=n2 // 2, kg=kg, ct=ct)
    return pl.pallas_call(
        kern,
        grid=(norder, c // ct),
        in_specs=[pl.BlockSpec((None, n, ct), lambda o, j: (o, 0, j)),
                  pl.BlockSpec(perm.shape, lambda o, j: (0, 0)),
                  pl.BlockSpec(f1.shape, lambda o, j: (0, 0)),
                  pl.BlockSpec(tb.shape, lambda o, j: (0, 0, 0))],
        out_specs=pl.BlockSpec((None, ng * r2, ct), lambda o, j: (o, 0, j)),
        out_shape=jax.ShapeDtypeStruct((norder, ng * r2, c), F32),
        scratch_shapes=[pltpu.VMEM((nb, na, ct), F32), pltpu.VMEM((n2, nb * ct), F32)],
        compiler_params=_cp("parallel", "parallel"),
        name="lconv_filter",
    )(taps, perm, f1, tb)


def _pconv_kernel(fd_ref, gd_ref, v_ref, kf_ref, xg_ref, bias_ref, o_ref, *, half):
    v = v_ref[...]
    x = _dot(fd_ref[...], v.astype(BF16))
    kf = kf_ref[...]
    xr, xi = x[:half], x[half:]
    kr, ki = kf[:half], kf[half:]
    y = jnp.concatenate([xr * kr - xi * ki, xr * ki + xi * kr], axis=0).astype(BF16)
    yt = _dot(gd_ref[...], y)
    o_ref[...] = (xg_ref[...].astype(F32) * (yt + v.astype(F32) * bias_ref[...])).astype(o_ref.dtype)


def _pconv(fd, gd, vsrc, v_lead, kf, order, xsrc, x_lead, bias, nseq, L, out_dtype, ct=512):
    c = kf.shape[-1]
    n2 = fd.shape[0]
    return pl.pallas_call(
        functools.partial(_pconv_kernel, half=n2 // 2),
        grid=(nseq, c // ct),
        in_specs=[pl.BlockSpec((n2, L), lambda s, j: (0, 0)),
                  pl.BlockSpec((L, n2), lambda s, j: (0, 0)),
                  pl.BlockSpec((None,) * len(v_lead) + (L, ct), lambda s, j: v_lead + (s, j)),
                  pl.BlockSpec((None, n2, ct), lambda s, j: (order, 0, j)),
                  pl.BlockSpec((None,) * len(x_lead) + (L, ct), lambda s, j: x_lead + (s, j)),
                  pl.BlockSpec((1, ct), lambda s, j: (0, j))],
        out_specs=pl.BlockSpec((L, ct), lambda s, j: (s, j)),
        out_shape=jax.ShapeDtypeStruct((nseq * L, c), out_dtype),
        compiler_params=_cp("parallel", "parallel"),
        name="pconv",
    )(fd, gd, vsrc, kf, xsrc, bias.reshape(1, c))


def _gate_kernel(lr_ref, wa_ref, ba_ref, o_ref):
    logits = _dot_hi(lr_ref[...], wa_ref[...]) + ba_ref[...]
    o_ref[...] = jax.nn.log_sigmoid(logits) * (1.0 / GLA_TAU)


def _gla_gates(lr, col, wa_cat, ba_cat, tm=1024):
    m = lr.shape[0]
    k, n = wa_cat.shape
    return pl.pallas_call(
        _gate_kernel,
        grid=(m // tm,),
        in_specs=[pl.BlockSpec((tm, k), lambda i: (i, col)),
                  pl.BlockSpec((k, n), lambda i: (0, 0)),
                  pl.BlockSpec((1, n), lambda i: (0, 0))],
        out_specs=pl.BlockSpec((tm, n), lambda i: (i, 0)),
        out_shape=jax.ShapeDtypeStruct((m, n), F32),
        compiler_params=_cp("parallel"),
        name="gla_gates",
    )(lr, wa_cat, ba_cat)


def _split3(x):
    hi = x.astype(BF16)
    r1 = x - hi.astype(F32)
    mid = r1.astype(BF16)
    lo = (r1 - mid.astype(F32)).astype(BF16)
    return hi, mid, lo


def _gla_dir(d, r0, q_ref, k_ref, v_ref, la_ref, o_ref, st_ref, tri, causal, ref_row, last_row,
             heads, dk, dv, scale):
    ch = GLA_CHUNK
    rows = pl.ds(pl.multiple_of(r0, ch), ch)
    la = la_ref[rows, :]
    hi, mid, lo = _split3(la)
    b = _dot(tri, hi) + _dot(tri, mid) + _dot(tri, lo)
    bref = b[ref_row:ref_row + 1, :]
    blast = b[last_row:last_row + 1, :]
    q = q_ref[rows, :].astype(F32) * scale
    k = k_ref[rows, :].astype(F32)
    qt = (q * jnp.exp(b - bref)).astype(BF16)
    kt = (k * jnp.exp(bref - b)).astype(BF16)
    qin = (q * jnp.exp(b)).astype(BF16)
    kst = (k * jnp.exp(blast - b)).astype(BF16)
    dec = jnp.exp(blast)
    for h in range(heads):
        ks = slice(h * dk, (h + 1) * dk)
        vs = slice(h * dv, (h + 1) * dv)
        att = _dot_nt(qt[:, ks], kt[:, ks])
        att = jnp.where(causal, att, 0.0).astype(BF16)
        vh = v_ref[rows, vs]
        st = st_ref[d, h]
        o_ref[rows, vs] = (_dot(att, vh) + _dot_nt(qin[:, ks], st.astype(BF16))).astype(o_ref.dtype)
        st_ref[d, h] = st * dec[:, ks] + _dot_tn(vh, kst[:, ks])


def _gla_kernel(*refs, nch, heads, dk, dv, has_s0, scale, nsq):
    seq_in = [refs[8 * q:8 * (q + 1)] for q in range(nsq)]
    rest = refs[8 * nsq:]
    if has_s0:
        s0, of, ob, st = rest
    else:
        of, ob, st = rest
        s0 = None
    ch = GLA_CHUNK

    @pl.when(pl.program_id(1) == 0)
    def _():
        if has_s0:
            st[...] = s0[...]
        else:
            st[...] = jnp.zeros_like(st)

    r_i = lax.broadcasted_iota(jnp.int32, (ch, ch), 0)
    c_i = lax.broadcasted_iota(jnp.int32, (ch, ch), 1)
    lower = r_i >= c_i
    upper = r_i <= c_i
    tri_l = jnp.where(lower, 1.0, 0.0).astype(BF16)
    tri_u = jnp.where(upper, 1.0, 0.0).astype(BF16)

    def body(c, carry):
        for q, (qf, kf, vf, laf, qb, kb, vb, lab) in enumerate(seq_in):
            _gla_dir(0, c * ch, qf, kf, vf, laf, of.at[q], st.at[q], tri_l, lower, ch // 2 - 1, ch - 1,
                     heads, dk, dv, scale)
            _gla_dir(1, (nch - 1 - c) * ch, qb, kb, vb, lab, ob.at[q], st.at[q], tri_u, upper, ch // 2, 0,
                     heads, dk, dv, scale)
        return carry

    lax.fori_loop(0, nch, body, 0)


def _gla(zmain, qcol, kcol, vcol, la, row0, nseq, L, tb, heads, dk, dv, s0=None, s0_layer=0, nsq=2):
    nblk = L // tb
    rb0 = row0 // tb
    hk, hv = heads * dk, heads * dv
    in_specs, args = [], []
    for q in range(nsq):
        def fwd(s, j, q=q):
            return rb0 + (s * nsq + q) * nblk + j

        def bwd(s, j, q=q):
            return rb0 + (s * nsq + q) * nblk + (nblk - 1 - j)

        for rowf, lcol in ((fwd, 0), (bwd, 1)):
            in_specs += [pl.BlockSpec((tb, hk), lambda s, j, rowf=rowf: (rowf(s, j), qcol)),
                         pl.BlockSpec((tb, hk), lambda s, j, rowf=rowf: (rowf(s, j), kcol)),
                         pl.BlockSpec((tb, hv), lambda s, j, rowf=rowf: (rowf(s, j), vcol)),
                         pl.BlockSpec((tb, hk), lambda s, j, rowf=rowf, lcol=lcol: (rowf(s, j), lcol))]
            args += [zmain, zmain, zmain, la]
    st_spec = pl.BlockSpec((nsq, 2, heads, dv, dk), lambda s, j: (s, 0, 0, 0, 0))
    if s0 is not None:
        in_specs.append(pl.BlockSpec((nsq, None, 2, heads, dv, dk), lambda s, j: (s, s0_layer, 0, 0, 0, 0)))
        args.append(s0)
    kern = functools.partial(_gla_kernel, nch=tb // GLA_CHUNK, heads=heads, dk=dk, dv=dv,
                             has_s0=s0 is not None, scale=dk ** -0.5, nsq=nsq)
    o_shape = jax.ShapeDtypeStruct((nseq // nsq, nsq, L, hv), BF16)
    o_f, o_b, st = pl.pallas_call(
        kern,
        grid=(nseq // nsq, nblk),
        in_specs=in_specs,
        out_specs=[pl.BlockSpec((None, nsq, tb, hv), lambda s, j: (s, 0, j, 0)),
                   pl.BlockSpec((None, nsq, tb, hv), lambda s, j: (s, 0, nblk - 1 - j, 0)),
                   st_spec],
        out_shape=[o_shape, o_shape, jax.ShapeDtypeStruct((nseq, 2, heads, dv, dk), F32)],
        compiler_params=_cp("parallel", "arbitrary"),
        name="gla",
    )(*args)
    return o_f.reshape(nseq * L, hv), o_b.reshape(nseq * L, hv), st


def _gla_post_kernel(ofa_ref, ofb_ref, oba_ref, obb_ref, gr_ref, g_ref, o_ref, *, heads, dv, rows_a, tm):
    o = (_pick(ofa_ref, ofb_ref, rows_a, tm).astype(F32) + _pick(oba_ref, obb_ref, rows_a, tm).astype(F32))
    gate = _silu(gr_ref[...].astype(F32))
    g = g_ref[...]
    for h in range(heads):
        sl = slice(h * dv, (h + 1) * dv)
        oh = o[:, sl]
        ms = jnp.mean(oh * oh, axis=-1, keepdims=True)
        o_ref[:, sl] = (oh * lax.rsqrt(ms + RMS_EPS) * g * gate[:, sl]).astype(o_ref.dtype)


def _gla_post(of_a, of_b, ob_a, ob_b, zmain, grcol, norm_g, heads, dv, tm=512):
    ra, hv = of_a.shape
    m = ra + of_b.shape[0]
    return pl.pallas_call(
        functools.partial(_gla_post_kernel, heads=heads, dv=dv, rows_a=ra, tm=tm),
        grid=(m // tm,),
        in_specs=[*_two_source_specs(ra, tm, hv), *_two_source_specs(ra, tm, hv),
                  pl.BlockSpec((tm, hv), lambda i: (i, grcol)),
                  pl.BlockSpec((1, dv), lambda i: (0, 0))],
        out_specs=pl.BlockSpec((tm, hv), lambda i: (i, 0)),
        out_shape=jax.ShapeDtypeStruct((m, hv), BF16),
        compiler_params=_cp("arbitrary"),
        name="gla_post",
    )(of_a, of_b, ob_a, ob_b, zmain, norm_g.reshape(1, dv))


def _rope_tables(L, dh):
    rows = L // GRID_W
    r = np.repeat(np.arange(rows, dtype=np.float32), GRID_W)
    col = np.tile(np.arange(GRID_W, dtype=np.float32), rows)
    nf = dh // 4
    inv = (np.float32(ROPE_THETA) ** (-np.arange(nf, dtype=np.float32) / nf)).astype(np.float32)
    ang_r = (r[:, None] * inv).astype(np.float32)
    ang_c = (col[:, None] * inv).astype(np.float32)
    cos = np.concatenate([np.cos(ang_r), np.cos(ang_r), np.cos(ang_c), np.cos(ang_c)], -1)
    sin = np.concatenate([-np.sin(ang_r), np.sin(ang_r), -np.sin(ang_c), np.sin(ang_c)], -1)
    cos = np.concatenate([cos, cos], -1).astype(np.float32)
    sin = np.concatenate([sin, sin], -1).astype(np.float32)
    cos_t = np.stack([np.ones_like(cos), cos])
    sin_t = np.stack([np.zeros_like(sin), sin])
    return cos_t, sin_t


def _attn_kernel(*refs, has_ctx, lam_init, dh, bf16_exp, tq):
    if has_ctx:
        (q_ref, k_ref, v_ref, cos_ref, sin_ref, kc_ref, vc_ref, lam_ref, g_ref, o_ref,
         kt_ref, ve_ref, s0_ref, s1_ref) = refs
    else:
        q_ref, k_ref, v_ref, cos_ref, sin_ref, lam_ref, g_ref, o_ref, kt_ref, ve_ref, s0_ref, s1_ref = refs
    hw = 2 * dh
    nf = dh // 4
    lk = k_ref.shape[0]
    nsub = q_ref.shape[0] // tq
    lane = lax.broadcasted_iota(jnp.int32, (tq, hw), 1)
    first = lane < dh
    scale = dh ** -0.5 * math.log2(math.e)

    def rope(x, cos, sin):
        ln = lax.broadcasted_iota(jnp.int32, x.shape, 1)
        sw = jnp.where((ln & (2 * nf - 1)) < nf, pltpu.roll(x, hw - nf, 1), pltpu.roll(x, nf, 1))
        return x * cos + sw * sin

    kt_ref[...] = rope(k_ref[...], cos_ref[...], sin_ref[...]).T.astype(BF16)
    ve_ref[:, :hw] = v_ref[...].astype(BF16)
    ve_ref[:, hw:] = jnp.ones((lk, hw), BF16)

    if has_ctx:
        kct = kc_ref[...].T.astype(BF16)
        vcv = vc_ref[...].astype(BF16)
        vce = jnp.concatenate([vcv, jnp.ones_like(vcv)], axis=1)

    def scores(i, j, s_ref):
        rows = pl.ds(i * tq if isinstance(i, int) else pl.multiple_of(i * tq, tq), tq)
        q = (rope(q_ref[rows, :], cos_ref[rows, :], sin_ref[rows, :]) * scale).astype(BF16)
        qj = jnp.where(first, q, jnp.zeros_like(q)) if j == 0 else jnp.where(first, jnp.zeros_like(q), q)
        s_ref[:, :lk] = _dot(qj, kt_ref[...])
        if has_ctx:
            s_ref[:, lk:] = _dot(qj, kct)

    def prob(t):
        if bf16_exp:
            return jnp.exp2(t.astype(BF16))
        return jnp.exp2(t).astype(BF16)

    def softmax_pv(s_ref):
        s = s_ref[...]
        p = prob(s - jnp.max(s, axis=-1, keepdims=True))
        acc = _dot(p[:, :lk], ve_ref[...])
        if has_ctx:
            acc = acc + _dot(p[:, lk:], vce)
        return acc[:, :hw] / acc[:, hw:]

    lp = lam_ref[...]
    lam = (jnp.exp(jnp.sum(lp[0:1] * lp[1:2], axis=-1, keepdims=True))
           - jnp.exp(jnp.sum(lp[2:3] * lp[3:4], axis=-1, keepdims=True)) + lam_init)
    gain = g_ref[...] * (1.0 - lam_init)

    scores(0, 0, s0_ref)

    def body(i, carry):
        scores(i, 1, s1_ref)
        sm0 = softmax_pv(s0_ref)
        scores(jnp.minimum(i + 1, nsub - 1), 0, s0_ref)
        o = sm0 - lam * softmax_pv(s1_ref)
        ms = jnp.mean(o * o, axis=-1, keepdims=True)
        o_ref[pl.ds(pl.multiple_of(i * tq, tq), tq), :] = (o * lax.rsqrt(ms + RMS_EPS) * gain).astype(o_ref.dtype)
        return carry

    lax.fori_loop(0, nsub, body, 0)


def _diff_attention(zd, cos_t, sin_t, rope_kind, row0, nseq, L, heads, dh, lam_p, norm_g, lam_init,
                    ctx=None, tq=256):
    hw = 2 * dh
    sb0 = row0 // L
    past = 0 if ctx is None else ctx[0].shape[2]
    tab = pl.BlockSpec((None, L, hw), lambda s, h: (rope_kind, 0, 0), pipeline_mode=pl.Buffered(1))
    in_specs = [pl.BlockSpec((L, hw), lambda s, h: (sb0 + s, h)),
                pl.BlockSpec((L, hw), lambda s, h: (sb0 + s, heads + h)),
                pl.BlockSpec((L, hw), lambda s, h: (sb0 + s, 2 * heads + h)),
                tab, tab]
    args = [zd, zd, zd, cos_t, sin_t]
    if ctx is not None:
        ck, cv, layer = ctx
        in_specs += [pl.BlockSpec((None, None, past, hw), lambda s, h: (s, layer, 0, h)),
                     pl.BlockSpec((None, None, past, hw), lambda s, h: (s, layer, 0, h))]
        args += [ck, cv]
    in_specs += [pl.BlockSpec(lam_p.shape, lambda s, h: (0, 0)),
                 pl.BlockSpec((1, hw), lambda s, h: (0, 0))]
    args += [lam_p, norm_g.reshape(1, hw)]
    kern = functools.partial(_attn_kernel, has_ctx=ctx is not None, lam_init=lam_init, dh=dh,
                             bf16_exp=L > 1024, tq=min(tq, L))
    return pl.pallas_call(
        kern,
        grid=(nseq, heads),
        in_specs=in_specs,
        out_specs=pl.BlockSpec((L, hw), lambda s, h: (s, h)),
        out_shape=jax.ShapeDtypeStruct((nseq * L, heads * hw), BF16),
        scratch_shapes=[pltpu.VMEM((hw, L), BF16), pltpu.VMEM((L, 2 * hw), BF16),
                        pltpu.VMEM((min(tq, L), L + past), F32), pltpu.VMEM((min(tq, L), L + past), F32)],
        compiler_params=_cp("parallel", "parallel"),
        name="diff_attn",
    )(*args)


def _mix_kernel(yaa_ref, yab_ref, yb_ref, yca_ref, ycb_ref, g_ref, w_ref, o_ref, acc_ref, *, rows_a, tm):
    k = pl.program_id(1)

    def contrib(y):
        return g_ref[...].astype(F32) * _dot(y, w_ref[...])

    @pl.when(k == 0)
    def _():
        acc_ref[...] = contrib(_pick(yaa_ref, yab_ref, rows_a, tm))

    @pl.when(k == 1)
    def _():
        acc_ref[...] += contrib(yb_ref[...])

    @pl.when(k == 2)
    def _():
        o_ref[...] = (acc_ref[...] + contrib(_pick(yca_ref, ycb_ref, rows_a, tm))).astype(o_ref.dtype)


def _mix(ya_a, ya_b, yb, yc_a, yc_b, gates, wbr, tm=512):
    ra, w = ya_a.shape
    m = yb.shape[0]
    d = wbr.shape[2]
    return pl.pallas_call(
        functools.partial(_mix_kernel, rows_a=ra, tm=tm),
        grid=(m // tm, 3),
        in_specs=[*_two_source_specs(ra, tm, w, ndim_grid=2),
                  pl.BlockSpec((tm, w), lambda i, k: (i, 0)),
                  *_two_source_specs(ra, tm, w, ndim_grid=2),
                  pl.BlockSpec((tm, d), lambda i, k: (i, k)),
                  pl.BlockSpec((None, w, d), lambda i, k: (k, 0, 0))],
        out_specs=pl.BlockSpec((tm, d), lambda i, k: (i, 0)),
        out_shape=jax.ShapeDtypeStruct((m, d), BF16),
        scratch_shapes=[pltpu.VMEM((tm, d), F32)],
        compiler_params=_cp("arbitrary", "arbitrary"),
        name="mix",
    )(ya_a, ya_b, yb, yc_a, yc_b, gates, wbr)


def _pad_cols(w, n):
    return jnp.pad(w, ((0, 0), (0, n - w.shape[1])))


def kernel(x_prompt, x_sample, cache_k, cache_v, state_gla, c, c_ctx, w_mod, b_mod, ln_g, ln_b, ffn_w1, ffn_w3, ffn_w2, w_in, hy_conv_w, hy_conv_b, hy_w1, hy_b1, hy_freq, hy_w2, hy_b2, hy_w3, hy_decay, hy_bias, gla_wa, gla_ba, gla_norm_g, diff_lam, diff_norm_g, w_branch_a, w_branch_b, w_branch_c, w_out):
    batch, seq, d = x_prompt.shape
    dec_batch, dec_seq, _ = x_sample.shape
    depth = w_mod.shape[0]
    ffn_dim = ffn_w1.shape[3]
    hy_w = hy_bias.shape[2]
    heads_g, dk_g = 4, gla_wa.shape[3] // 4
    dv_g = gla_norm_g.shape[1]
    rank = gla_wa.shape[2]
    dh = diff_lam.shape[2]
    heads_d = cache_k.shape[3]
    dw = heads_d * 2 * dh
    gw = heads_g * dv_g
    gk = heads_g * dk_g
    assert batch * seq == GROUP and dec_seq == GROUP
    mp = batch * seq
    m = mp + dec_batch * dec_seq
    ngroups = 1 + dec_batch
    alpha = (2 * depth) ** 0.25
    lam_inits = [0.8 - 0.6 * math.exp(-0.3 * l) for l in range(depth)]

    c_main = 3 * hy_w + 2 * gk + 2 * gw
    c_lr = c_main
    c_d = c_lr + 2 * rank
    c_g = c_d + 3 * dw

    cond = jnp.concatenate([c_ctx[None], c, jnp.zeros((16 - ngroups, d), F32)], axis=0)
    mod = _modulation(cond, w_mod, b_mod)[:, :ngroups].reshape(depth, ngroups, N_MOD, d)

    na = 2 * dec_seq // FFT_NB
    n_s = 2 * dec_seq
    f1_half = jnp.asarray(_dft_a(na, na // 2), BF16)
    f1_full = jnp.asarray(_dft_a(na, na), BF16)
    g_s = jnp.asarray(_idft_a(na, n_s, na // 2), BF16)
    tb_np, tbi_np = _dft_b(na, FFT_NB, FFT_K1G)
    tb, tbi = jnp.asarray(tb_np, BF16), jnp.asarray(tbi_np, BF16)
    perm_np = _row_permutation(FFT_NB, SUBLANES_F32)
    perm, permt = jnp.asarray(perm_np, BF16), jnp.asarray(perm_np.T, BF16)
    fp_half = jnp.asarray(_dft_a(2 * seq, seq), BF16)
    fp_full = jnp.asarray(_dft_a(2 * seq, 2 * seq), BF16)
    g_p = jnp.asarray(_idft_a(2 * seq, 2 * seq, seq), BF16)
    cos_t, sin_t = _rope_tables(dec_seq, dh)
    cos_t, sin_t = jnp.asarray(cos_t), jnp.asarray(sin_t)

    x, h = _premod(x_prompt.reshape(mp, d), x_sample.reshape(dec_batch * dec_seq, d), mod, 0)

    fp = ((ffn_dim + 511) // 512) * 512
    ck = cache_k.reshape(dec_batch, depth, cache_k.shape[2], dw)
    cv = cache_v.reshape(dec_batch, depth, cache_v.shape[2], dw)
    state_t = jnp.swapaxes(state_gla, -1, -2)
    new_k, new_v, new_s = [], [], []

    for l in range(depth):
        w2 = _cast_pad_rows(ffn_w2, l, 0, fp)
        hid = _ffn1(h, ffn_w1, ffn_w3, l, 0)
        x, h = _mm_ln(hid, w2, x, mod, l, 2, 0.5, alpha, l, 3, ln_g[l, 0], ln_b[l, 0])

        wi = w_in[l]
        zmain = _proj(h, wi[:, :c_main].astype(BF16), BF16)
        zlr = _proj(h, _pad_cols(wi[:, c_lr:c_d].astype(BF16), LANES), F32)
        zd = _proj(h, wi[:, c_d:c_g].astype(BF16), F32)
        gates = _proj(h, wi[:, c_g:].astype(BF16), BF16, act="sigmoid")

        u3 = _short_conv(zmain, mp, hy_conv_w[l], hy_conv_b[l], seq, dec_seq, hy_w)
        fargs = (hy_w1[l], hy_b1[l], hy_freq[l], hy_w2[l], hy_b2[l], hy_w3[l], hy_decay[l], hy_w)
        taps_s = _hyena_filter_taps(dec_seq, *fargs)
        taps_p = _hyena_filter_taps(seq, *fargs)
        kf_s = _lconv_filter(taps_s, perm, f1_full, tb)
        kf_p = _lmm(fp_full, taps_p, lambda b: (b,), 2, F32, tn=hy_w)

        cw3 = jnp.swapaxes(hy_conv_w[l].reshape(3, 3, hy_w), 0, 1)
        cb3 = hy_conv_b[l].reshape(3, 1, hy_w)
        z1s = _lconv(zmain, mp, 0, zmain, mp, 1, cw3[0:2], cb3[0:2], True, perm, permt, f1_half, g_s,
                     tb, tbi, kf_s, 0, hy_bias[l, 0], dec_batch, dec_seq)
        ya_s = _lconv(z1s, 0, 0, zmain, mp, 2, cw3[1:3], cb3[1:3], False, perm, permt, f1_half, g_s,
                      tb, tbi, kf_s, 1, hy_bias[l, 1], dec_batch, dec_seq)
        z1 = _pconv(fp_half, g_p, u3, (0,), kf_p, 0, u3, (1,), hy_bias[l, 0], batch, seq, F32)
        ya_p = _pconv(fp_half, g_p, z1, (), kf_p, 1, u3, (2,), hy_bias[l, 1], batch, seq, BF16)

        wa_cat = jnp.zeros((LANES, 2 * gk), F32)
        wa_cat = wa_cat.at[:rank, :gk].set(gla_wa[l, 0]).at[rank:2 * rank, gk:].set(gla_wa[l, 1])
        ba_cat = jnp.concatenate([gla_ba[l, 0], gla_ba[l, 1]]).reshape(1, 2 * gk)
        la = _gla_gates(zlr, 0, wa_cat, ba_cat)
        qcol, kcol, vcol, grcol = (3 * hy_w) // gk, (3 * hy_w) // gk + 1, (3 * hy_w + 2 * gk) // gw, \
            (3 * hy_w + 2 * gk) // gw + 1
        of_p, ob_p, st_p = _gla(zmain, qcol, kcol, vcol, la, 0, batch, seq, seq, heads_g, dk_g, dv_g)
        of_s, ob_s, _ = _gla(zmain, qcol, kcol, vcol, la, mp, dec_batch, dec_seq, 512, heads_g, dk_g, dv_g,
                             s0=state_t, s0_layer=l)
        yb = _gla_post(of_p, of_s, ob_p, ob_s, zmain, grcol, gla_norm_g[l], heads_g, dv_g)
        new_s.append(jnp.swapaxes(st_p, -1, -2))

        yc_p = _diff_attention(zd, cos_t, sin_t, 0, 0, batch, seq, heads_d, dh, diff_lam[l],
                               diff_norm_g[l], lam_inits[l])
        yc_s = _diff_attention(zd, cos_t, sin_t, 1, mp, dec_batch, dec_seq, heads_d, dh, diff_lam[l],
                               diff_norm_g[l], lam_inits[l], ctx=(ck, cv, l), tq=512)
        new_k.append(zd[:mp, dw:2 * dw].reshape(batch, seq, heads_d, 2, dh))
        new_v.append(zd[:mp, 2 * dw:3 * dw].reshape(batch, seq, heads_d, 2 * dh))

        wbr = jnp.stack([w_branch_a[l], w_branch_b[l], w_branch_c[l]]).astype(BF16)
        y = _mix(ya_p, ya_s, yb, yc_p, yc_s, gates, wbr)
        x, h = _mm_ln(y, w_out[l].astype(BF16), x, mod, l, 5, 1.0, alpha, l, 6, ln_g[l, 1], ln_b[l, 1],
                      tm=512)

        w2 = _cast_pad_rows(ffn_w2, l, 1, fp)
        hid = _ffn1(h, ffn_w1, ffn_w3, l, 1)
        if l + 1 < depth:
            x, h = _mm_ln(hid, w2, x, mod, l, 8, 0.5, alpha, l + 1, 0, ln_g[l, 2], ln_b[l, 2])
        else:
            xp, _ = _mm_ln(hid, w2, x, mod, l, 8, 0.5, alpha, l, 0, ln_g[l, 2], ln_b[l, 2], rows=mp)
            xs, _ = _mm_ln(hid, w2, x, mod, l, 8, 0.5, alpha, l, 0, ln_g[l, 2], ln_b[l, 2], row0=mp,
                           rows=m - mp)

    y_prompt = xp.reshape(batch, seq, d)
    y_sample = xs.reshape(dec_batch, dec_seq, d)
    return (y_prompt, y_sample, jnp.stack(new_k, axis=1), jnp.stack(new_v, axis=1),
            jnp.stack(new_s, axis=1))
```

```python
import functools
import math

import numpy as np
import jax
import jax.numpy as jnp
from jax import lax
from jax.experimental import pallas as pl
from jax.experimental.pallas import tpu as pltpu

F32 = jnp.float32
BF16 = jnp.bfloat16

GRID_W = 64
N_MOD = 9
HY_BANDS = 8
GLA_TAU = 16.0
GLA_CHUNK = 64
ROPE_THETA = 10000.0
LN_EPS = 1e-5
RMS_EPS = 1e-6

LANES = 128
SUBLANES_F32 = 8
VMEM_BYTES_V7X = 64 * 1024 * 1024
VMEM_LIMIT = VMEM_BYTES_V7X - 8 * 1024 * 1024

GROUP = 4096
FFT_NB = 16
FFT_K1G = 8


def _cp(*sem):
    return pltpu.CompilerParams(dimension_semantics=sem, vmem_limit_bytes=VMEM_LIMIT)


def _dot(a, b):
    return jnp.dot(a, b, preferred_element_type=F32)


def _dot_nt(a, b):
    return lax.dot_general(a, b, (((1,), (1,)), ((), ())), preferred_element_type=F32)


def _dot_tn(a, b):
    return lax.dot_general(a, b, (((0,), (0,)), ((), ())), preferred_element_type=F32)


def _dot_hi(a, b):
    return jnp.dot(a, b, preferred_element_type=F32, precision=lax.Precision.HIGHEST)


def _silu(x):
    return x * jax.nn.sigmoid(x)


def _mod_kernel(c_ref, w_ref, b_ref, o_ref):
    c = c_ref[...]
    o_ref[...] = _dot(_silu(c).astype(BF16), w_ref[...].astype(BF16)) + b_ref[...]


def _modulation(cond, w_mod, b_mod):
    depth, d, n = w_mod.shape
    r = cond.shape[0]
    tn = 1024
    return pl.pallas_call(
        _mod_kernel,
        grid=(depth, n // tn),
        in_specs=[pl.BlockSpec((r, d), lambda l, j: (0, 0)),
                  pl.BlockSpec((None, d, tn), lambda l, j: (l, 0, j)),
                  pl.BlockSpec((None, 1, tn), lambda l, j: (l, 0, j))],
        out_specs=pl.BlockSpec((None, r, tn), lambda l, j: (l, 0, j)),
        out_shape=jax.ShapeDtypeStruct((depth, r, n), F32),
        compiler_params=_cp("parallel", "parallel"),
        name="mod",
    )(cond, w_mod, b_mod.reshape(depth, 1, n))


def _two_source_specs(rows_a, tm, width, col=0, ndim_grid=1):
    na = rows_a // tm
    if ndim_grid == 1:
        return (pl.BlockSpec((tm, width), lambda i: (jnp.minimum(i, na - 1), col)),
                pl.BlockSpec((tm, width), lambda i: (jnp.maximum(i - na, 0), col)))
    return (pl.BlockSpec((tm, width), lambda i, k: (jnp.minimum(i, na - 1), col)),
            pl.BlockSpec((tm, width), lambda i, k: (jnp.maximum(i - na, 0), col)))


def _pick(a_ref, b_ref, rows_a, tm):
    return jnp.where(pl.program_id(0) < rows_a // tm, a_ref[...], b_ref[...])


def _premod_kernel(xa_ref, xb_ref, mod_ref, x_ref, o_ref, *, rows_a, tm):
    x = _pick(xa_ref, xb_ref, rows_a, tm)
    x_ref[...] = x
    o_ref[...] = (x * (1.0 + mod_ref[1:2, :]) + mod_ref[0:1, :]).astype(o_ref.dtype)


def _premod(xa, xb, mod, layer, tm=512):
    ra, d = xa.shape
    m = ra + xb.shape[0]
    return pl.pallas_call(
        functools.partial(_premod_kernel, rows_a=ra, tm=tm),
        grid=(m // tm,),
        in_specs=[*_two_source_specs(ra, tm, d),
                  pl.BlockSpec((None, None, N_MOD, d), lambda i: (layer, (i * tm) // GROUP, 0, 0))],
        out_specs=[pl.BlockSpec((tm, d), lambda i: (i, 0)), pl.BlockSpec((tm, d), lambda i: (i, 0))],
        out_shape=[jax.ShapeDtypeStruct((m, d), F32), jax.ShapeDtypeStruct((m, d), BF16)],
        compiler_params=_cp("arbitrary"),
        name="premod",
    )(xa, xb, mod)


def _ffn1_kernel(h_ref, w1_ref, w3_ref, o_ref, w1b_ref, w3b_ref, *, tf, f_valid):
    @pl.when(pl.program_id(1) == 0)
    def _():
        col = pl.program_id(0) * tf + lax.broadcasted_iota(jnp.int32, w1_ref.shape, 1)
        keep = col < f_valid
        w1b_ref[...] = jnp.where(keep, w1_ref[...], 0.0).astype(BF16)
        w3b_ref[...] = jnp.where(keep, w3_ref[...], 0.0).astype(BF16)

    h = h_ref[...]
    a = _dot(h, w1b_ref[...])
    b = _dot(h, w3b_ref[...])
    o_ref[...] = (_silu(a) * b).astype(o_ref.dtype)


def _ffn1(h, w1, w3, layer, sub, tm=1024, tf=512):
    m, d = h.shape
    f = w1.shape[-1]
    nf = pl.cdiv(f, tf)
    wspec = pl.BlockSpec((None, None, d, tf), lambda j, i: (layer, sub, 0, j))
    return pl.pallas_call(
        functools.partial(_ffn1_kernel, tf=tf, f_valid=f),
        grid=(nf, m // tm),
        in_specs=[pl.BlockSpec((tm, d), lambda j, i: (i, 0)), wspec, wspec],
        out_specs=pl.BlockSpec((tm, tf), lambda j, i: (i, j)),
        out_shape=jax.ShapeDtypeStruct((m, nf * tf), BF16),
        scratch_shapes=[pltpu.VMEM((d, tf), BF16), pltpu.VMEM((d, tf), BF16)],
        compiler_params=_cp("parallel", "arbitrary"),
        name="ffn1",
    )(h, w1, w3)


def _cast_rows_kernel(w_ref, o_ref, *, rows_valid, tr):
    row = pl.program_id(0) * tr + lax.broadcasted_iota(jnp.int32, w_ref.shape, 0)
    o_ref[...] = jnp.where(row < rows_valid, w_ref[...], 0.0).astype(o_ref.dtype)


def _cast_pad_rows(w, layer, sub, rows_out, tr=512):
    f, d = w.shape[-2:]
    return pl.pallas_call(
        functools.partial(_cast_rows_kernel, rows_valid=f, tr=tr),
        grid=(rows_out // tr,),
        in_specs=[pl.BlockSpec((None, None, tr, d), lambda i: (layer, sub, i, 0))],
        out_specs=pl.BlockSpec((tr, d), lambda i: (i, 0)),
        out_shape=jax.ShapeDtypeStruct((rows_out, d), BF16),
        compiler_params=_cp("parallel"),
        name="cast_w2",
    )(w)


def _mm_ln_kernel(a_ref, w_ref, x_ref, mod_ref, nmod_ref, g_ref, b_ref, xo_ref, ho_ref, *,
                  gate_row, coef, alpha, nshift_row):
    gate = coef * mod_ref[gate_row:gate_row + 1, :]
    xr = alpha * x_ref[...] + gate * _dot(a_ref[...], w_ref[...])
    mu = jnp.mean(xr, axis=-1, keepdims=True)
    xc = xr - mu
    var = jnp.mean(xc * xc, axis=-1, keepdims=True)
    xn = xc * lax.rsqrt(var + LN_EPS) * g_ref[...] + b_ref[...]
    xo_ref[...] = xn
    ho_ref[...] = (xn * (1.0 + nmod_ref[nshift_row + 1:nshift_row + 2, :])
                   + nmod_ref[nshift_row:nshift_row + 1, :]).astype(ho_ref.dtype)


def _mm_ln(a, w, x, mod, layer, gate_row, coef, alpha, nlayer, nshift_row, ln_g, ln_b, tm=256,
           row0=0, rows=None):
    kdim = a.shape[1]
    m = a.shape[0] if rows is None else rows
    d = w.shape[1]
    rb0 = row0 // tm
    kern = functools.partial(_mm_ln_kernel, gate_row=gate_row, coef=coef, alpha=alpha, nshift_row=nshift_row)
    return pl.pallas_call(
        kern,
        grid=(m // tm,),
        in_specs=[pl.BlockSpec((tm, kdim), lambda i: (rb0 + i, 0)),
                  pl.BlockSpec((kdim, d), lambda i: (0, 0), pipeline_mode=pl.Buffered(1)),
                  pl.BlockSpec((tm, d), lambda i: (rb0 + i, 0)),
                  pl.BlockSpec((None, None, N_MOD, d), lambda i: (layer, ((rb0 + i) * tm) // GROUP, 0, 0)),
                  pl.BlockSpec((None, None, N_MOD, d), lambda i: (nlayer, ((rb0 + i) * tm) // GROUP, 0, 0)),
                  pl.BlockSpec((1, d), lambda i: (0, 0)),
                  pl.BlockSpec((1, d), lambda i: (0, 0))],
        out_specs=[pl.BlockSpec((tm, d), lambda i: (i, 0)),
                   pl.BlockSpec((tm, d), lambda i: (i, 0))],
        out_shape=[jax.ShapeDtypeStruct((m, d), F32), jax.ShapeDtypeStruct((m, d), BF16)],
        compiler_params=_cp("parallel"),
        name="mm_ln",
    )(a, w, x, mod, mod, ln_g.reshape(1, d), ln_b.reshape(1, d))


def _proj_kernel(h_ref, w_ref, o_ref, *, act):
    r = _dot(h_ref[...], w_ref[...])
    if act == "sigmoid":
        r = jax.nn.sigmoid(r)
    o_ref[...] = r.astype(o_ref.dtype)


def _proj(h, w, out_dtype, act=None, tm=1024, tn=1024):
    m, d = h.shape
    n = w.shape[1]
    tn = min(tn, n)
    return pl.pallas_call(
        functools.partial(_proj_kernel, act=act),
        grid=(m // tm, n // tn),
        in_specs=[pl.BlockSpec((tm, d), lambda i, j: (i, 0)),
                  pl.BlockSpec((d, tn), lambda i, j: (0, j))],
        out_specs=pl.BlockSpec((tm, tn), lambda i, j: (i, j)),
        out_shape=jax.ShapeDtypeStruct((m, n), out_dtype),
        compiler_params=_cp("parallel", "arbitrary"),
        name="proj",
    )(h, w)


def _sconv_kernel(z_ref, zp_ref, zn_ref, w_ref, b_ref, o_ref, *, tm, halo, lp, ls):
    i = pl.program_id(0)
    u = z_ref[...].astype(F32)
    prev = zp_ref[...].astype(F32)[halo - 1:halo, :]
    nxt = zn_ref[...].astype(F32)[0:1, :]
    row = lax.broadcasted_iota(jnp.int32, u.shape, 0)
    lseq = jnp.where((i * tm) // GROUP == 0, lp, ls)
    pos = (row + i * tm) & (lseq - 1)
    up = jnp.where(row == 0, prev, pltpu.roll(u, 1, 0))
    up = jnp.where(pos == 0, 0.0, up)
    un = jnp.where(row == tm - 1, nxt, pltpu.roll(u, tm - 1, 0))
    un = jnp.where(pos == lseq - 1, 0.0, un)
    w = w_ref[...]
    o_ref[...] = (up * w[0:1, :] + u * w[1:2, :] + un * w[2:3, :] + b_ref[...]).astype(o_ref.dtype)


def _short_conv(z, rows, conv_w, conv_b, lp, ls, width, tm=1024, ct=512, halo=16):
    m = rows
    nct = width // ct
    nrb = m // halo
    kern = functools.partial(_sconv_kernel, tm=tm, halo=halo, lp=lp, ls=ls)
    return pl.pallas_call(
        kern,
        grid=(m // tm, 3 * nct),
        in_specs=[pl.BlockSpec((tm, ct), lambda i, j: (i, j)),
                  pl.BlockSpec((halo, ct), lambda i, j: (jnp.maximum(i * (tm // halo) - 1, 0), j)),
                  pl.BlockSpec((halo, ct), lambda i, j: (jnp.minimum((i + 1) * (tm // halo), nrb - 1), j)),
                  pl.BlockSpec((3, ct), lambda i, j: (0, j)),
                  pl.BlockSpec((1, ct), lambda i, j: (0, j))],
        out_specs=pl.BlockSpec((None, tm, ct), lambda i, j: (j // nct, i, j % nct)),
        out_shape=jax.ShapeDtypeStruct((3, m, width), F32),
        compiler_params=_cp("parallel", "parallel"),
        name="sconv",
    )(z, z, z, conv_w, conv_b.reshape(1, -1))


def _filter_features(L):
    t = np.linspace(0.0, 1.0, L, dtype=np.float32)
    w = (np.float32(2.0 * math.pi / L) * np.arange(L, dtype=np.float32)).astype(np.float32)
    f = np.linspace(1e-4, HY_BANDS - 1, HY_BANDS, dtype=np.float32)
    wf = (w[:, None] * f).astype(np.float32)
    feats = np.concatenate([t[:, None], np.cos(wf), -np.sin(wf)], -1).astype(np.float32)
    idx = np.concatenate([np.arange(L), [0], np.arange(L - 1, 0, -1)])
    tab = np.zeros((2 * L, 32), np.float32)
    tab[:, :feats.shape[1]] = feats[idx]
    tab[:, 24] = t[idx]
    tab[:L, 25] = 1.0
    tab[L + 1:, 26] = -1.0
    return tab, feats.shape[1]


def _filter_kernel(tab_ref, w1_ref, b1_ref, fr_ref, w2_ref, b2_ref, w3_ref, dec_ref, o_ref):
    tab = tab_ref[...]
    fr = fr_ref[...]
    hdn = jnp.sin(fr[0:1, :] * (_dot_hi(tab, w1_ref[...]) + b1_ref[...]))
    hdn = jnp.sin(fr[1:2, :] * (_dot_hi(hdn, w2_ref[...]) + b2_ref[...]))
    t = tab[:, 24:25]
    sign = tab[:, 25:26] + tab[:, 26:27]
    for o in range(2):
        o_ref[o] = sign * (_dot_hi(hdn, w3_ref[o]) * jnp.exp(-t * jnp.abs(dec_ref[o])))


def _hyena_filter_taps(L, hy_w1, hy_b1, hy_freq, hy_w2, hy_b2, hy_w3, hy_decay, width, rb=256):
    tab_np, nfeat = _filter_features(L)
    fh = hy_w1.shape[1]
    w1p = jnp.zeros((32, fh), F32).at[:nfeat].set(hy_w1)
    n = 2 * L
    nhalf = L // rb
    w3d = jnp.transpose(hy_w3.reshape(fh, 2, 2, width), (1, 2, 0, 3))
    decd = hy_decay.reshape(2, 2, 1, width)
    return pl.pallas_call(
        _filter_kernel,
        grid=(n // rb,),
        in_specs=[pl.BlockSpec((rb, 32), lambda i: (i, 0)),
                  pl.BlockSpec((32, fh), lambda i: (0, 0)),
                  pl.BlockSpec((1, fh), lambda i: (0, 0)),
                  pl.BlockSpec((2, fh), lambda i: (0, 0)),
                  pl.BlockSpec((fh, fh), lambda i: (0, 0)),
                  pl.BlockSpec((1, fh), lambda i: (0, 0)),
                  pl.BlockSpec((2, None, fh, width), lambda i: (0, i // nhalf, 0, 0)),
                  pl.BlockSpec((2, None, 1, width), lambda i: (0, i // nhalf, 0, 0))],
        out_specs=pl.BlockSpec((2, rb, width), lambda i: (0, i, 0)),
        out_shape=jax.ShapeDtypeStruct((2, n, width), F32),
        compiler_params=_cp("parallel"),
        name="hyfilter",
    )(jnp.asarray(tab_np), w1p, hy_b1.reshape(1, fh), hy_freq, hy_w2, hy_b2.reshape(1, fh), w3d, decd)


def _dft_a(na, ka):
    k1 = np.arange(na // 2)[:, None].astype(np.float64)
    a = np.arange(ka)[None, :].astype(np.float64)
    th = 2.0 * np.pi * a * (k1 + 0.5) / na
    return np.concatenate([np.cos(th), -np.sin(th)], 0)


def _idft_a(na, n, rows):
    k1 = np.arange(na // 2)[None, :].astype(np.float64)
    a = np.arange(rows)[:, None].astype(np.float64)
    th = 2.0 * np.pi * a * (k1 + 0.5) / na
    return (2.0 / n) * np.concatenate([np.cos(th), -np.sin(th)], 1)


def _dft_b(na, nb, g):
    n = na * nb
    half = na // 2
    k1 = np.arange(half).astype(np.float64)
    b = np.arange(nb).astype(np.float64)
    k2 = np.arange(nb).astype(np.float64)
    phi = 2.0 * np.pi * (b[None, None, :] * k2[None, :, None] / nb
                         + b[None, None, :] * (k1[:, None, None] + 0.5) / n)
    c, s = np.cos(phi), np.sin(phi)
    ng = half // g
    r = g * nb
    fwd = np.zeros((ng, 2 * r, 2 * r))
    for q in range(g):
        rows = slice(q * nb, (q + 1) * nb)
        rows_i = slice(r + q * nb, r + (q + 1) * nb)
        cols = slice(q, r, g)
        cols_i = slice(r + q, 2 * r, g)
        cq, sq = c[q::g], s[q::g]
        fwd[:, rows, cols] = cq
        fwd[:, rows, cols_i] = sq
        fwd[:, rows_i, cols] = -sq
        fwd[:, rows_i, cols_i] = cq
    inv = np.transpose(fwd, (0, 2, 1))
    return fwd, inv


def _lmm_kernel(f_ref, x_ref, o_ref):
    o_ref[...] = _dot(f_ref[...], x_ref[...].astype(BF16)).astype(o_ref.dtype)


def _lmm(f, x, x_index, nbatch, out_dtype, tn=2048):
    mo, k = f.shape
    n = x.shape[-1]
    lead = len(x.shape) - 2
    return pl.pallas_call(
        _lmm_kernel,
        grid=(nbatch, n // tn),
        in_specs=[pl.BlockSpec((mo, k), lambda b, j: (0, 0)),
                  pl.BlockSpec((None,) * lead + (k, tn), lambda b, j: x_index(b) + (0, j))],
        out_specs=pl.BlockSpec((None, mo, tn), lambda b, j: (b, 0, j)),
        out_shape=jax.ShapeDtypeStruct((nbatch, mo, n), out_dtype),
        compiler_params=_cp("parallel", "parallel"),
        name="dft_a",
    )(f, x)


def _stage_b_rows(r_ref, r0, half, nb, kg, ct):
    return [(pl.ds(ri * half + r0, kg), slice(b * ct, (b + 1) * ct)) for ri in range(2) for b in range(nb)]


def _row_permutation(nb, al):
    p = np.zeros((nb * al, nb * al), np.float32)
    for b in range(nb):
        for a in range(al):
            p[b * al + a, a * nb + b] = 1.0
    return p


def _to_residue_major(src_ref, perm_ref, dst_ref, nb, al):
    blk = nb * al
    for i in range(src_ref.shape[0] // blk):
        z = _dot(perm_ref[...], src_ref[i * blk:(i + 1) * blk, :].astype(BF16))
        for b in range(nb):
            dst_ref[b, i * al:(i + 1) * al, :] = z[b * al:(b + 1) * al, :]


def _short_conv_residue_major(u_ref, w_ref, cb_ref, nb):
    rows = u_ref.shape[1]
    row = lax.broadcasted_iota(jnp.int32, u_ref.shape[1:], 0)
    w = w_ref[...]
    cb = cb_ref[...]
    first = u_ref[0]
    prev = jnp.where(row == 0, 0.0, pltpu.roll(u_ref[nb - 1], 1, 0))
    for b in range(nb):
        cur = first if b == 0 else u_ref[b]
        nxt = u_ref[b + 1] if b + 1 < nb else jnp.where(row == rows - 1, 0.0, pltpu.roll(first, rows - 1, 0))
        u_ref[b] = prev * w[0:1, :] + cur * w[1:2, :] + nxt * w[2:3, :] + cb
        prev = cur


def _lconv_kernel(zv_ref, zx_ref, cwv_ref, cbv_ref, cwx_ref, cbx_ref, perm_ref, permt_ref, f1_ref, g_ref,
                  tb_ref, tbi_ref, kf_ref, bias_ref, o_ref, v_ref, x_ref, r_ref, *, nb, half, kg, ct, conv_v):
    r = nb * kg
    al = perm_ref.shape[0] // nb
    _to_residue_major(zv_ref, perm_ref, v_ref, nb, al)
    _to_residue_major(zx_ref, perm_ref, x_ref, nb, al)
    if conv_v:
        _short_conv_residue_major(v_ref, cwv_ref, cbv_ref, nb)
    _short_conv_residue_major(x_ref, cwx_ref, cbx_ref, nb)
    for b in range(nb):
        r_ref[:, b * ct:(b + 1) * ct] = _dot(f1_ref[...], v_ref[b].astype(BF16))

    for g in range(half // kg):
        tiles = _stage_b_rows(r_ref, g * kg, half, nb, kg, ct)
        a = jnp.concatenate([r_ref[rs, cs] for rs, cs in tiles], axis=0).astype(BF16)
        x = _dot(tb_ref[g], a)
        kf = kf_ref[g * 2 * r:(g + 1) * 2 * r, :]
        xr, xi = x[:r], x[r:]
        kr, ki = kf[:r], kf[r:]
        y = jnp.concatenate([xr * kr - xi * ki, xr * ki + xi * kr], axis=0).astype(BF16)
        bh = _dot(tbi_ref[g], y)
        for t, (rs, cs) in enumerate(tiles):
            r_ref[rs, cs] = bh[t * kg:(t + 1) * kg, :]
    bias = bias_ref[...]
    for b in range(nb):
        y = _dot(g_ref[...], r_ref[:, b * ct:(b + 1) * ct].astype(BF16))
        v_ref[b] = x_ref[b] * (y + v_ref[b] * bias)
    blk = nb * al
    for i in range(o_ref.shape[0] // blk):
        t = jnp.concatenate([v_ref[b, i * al:(i + 1) * al, :] for b in range(nb)], axis=0).astype(BF16)
        o_ref[i * blk:(i + 1) * blk, :] = _dot(permt_ref[...], t).astype(o_ref.dtype)


def _lconv(zv, v_row0, v_col0, zx, x_row0, x_col0, conv_w, conv_b, conv_v, perm, permt, f1, g_inv, tb, tbi,
           kf, order, bias, nbatch, L, ct=256):
    c = kf.shape[-1]
    n2, _ = f1.shape
    ng, r2, _ = tb.shape
    nb = L // (n2 // 2)
    kg = r2 // (2 * nb)
    ncb = c // ct
    once = dict(pipeline_mode=pl.Buffered(1))
    kern = functools.partial(_lconv_kernel, nb=nb, half=n2 // 2, kg=kg, ct=ct, conv_v=conv_v)
    vb0, xb0 = v_row0 // L, x_row0 // L
    return pl.pallas_call(
        kern,
        grid=(ncb, nbatch),
        in_specs=[pl.BlockSpec((L, ct), lambda j, b: (vb0 + b, v_col0 * ncb + j)),
                  pl.BlockSpec((L, ct), lambda j, b: (xb0 + b, x_col0 * ncb + j)),
                  pl.BlockSpec((None, 3, ct), lambda j, b: (0, 0, j)),
                  pl.BlockSpec((None, 1, ct), lambda j, b: (0, 0, j)),
                  pl.BlockSpec((None, 3, ct), lambda j, b: (1, 0, j)),
                  pl.BlockSpec((None, 1, ct), lambda j, b: (1, 0, j)),
                  pl.BlockSpec(perm.shape, lambda j, b: (0, 0), **once),
                  pl.BlockSpec(permt.shape, lambda j, b: (0, 0), **once),
                  pl.BlockSpec(f1.shape, lambda j, b: (0, 0), **once),
                  pl.BlockSpec(g_inv.shape, lambda j, b: (0, 0), **once),
                  pl.BlockSpec(tb.shape, lambda j, b: (0, 0, 0), **once),
                  pl.BlockSpec(tbi.shape, lambda j, b: (0, 0, 0), **once),
                  pl.BlockSpec((None, ng * r2, ct), lambda j, b: (order, 0, j), **once),
                  pl.BlockSpec((1, ct), lambda j, b: (0, j))],
        out_specs=pl.BlockSpec((L, ct), lambda j, b: (b, j)),
        out_shape=jax.ShapeDtypeStruct((nbatch * L, c), BF16),
        scratch_shapes=[pltpu.VMEM((nb, L // nb, ct), F32), pltpu.VMEM((nb, L // nb, ct), F32),
                        pltpu.VMEM((n2, nb * ct), F32)],
        compiler_params=_cp("parallel", "arbitrary"),
        name="lconv",
    )(zv, zx, conv_w, conv_b, conv_w, conv_b, perm, permt, f1, g_inv, tb, tbi, kf, bias.reshape(1, c))


def _lconv_filter_kernel(x_ref, perm_ref, f1_ref, tb_ref, o_ref, t_ref, r_ref, *, nb, half, kg, ct):
    r2 = 2 * nb * kg
    _to_residue_major(x_ref, perm_ref, t_ref, nb, perm_ref.shape[0] // nb)
    for b in range(nb):
        r_ref[:, b * ct:(b + 1) * ct] = _dot(f1_ref[...], t_ref[b].astype(BF16))

    for g in range(half // kg):
        tiles = _stage_b_rows(r_ref, g * kg, half, nb, kg, ct)
        a = jnp.concatenate([r_ref[rs, cs] for rs, cs in tiles], axis=0).astype(BF16)
        o_ref[g * r2:(g + 1) * r2, :] = _dot(tb_ref[g], a)


def _lconv_filter(taps, perm, f1, tb, ct=256):
    norder, n, c = taps.shape
    n2, na = f1.shape
    nb = n // na
    ng, r2, _ = tb.shape
    kg = r2 // (2 * nb)
    kern = functools.partial(_lconv_filter_kernel, nb=nb, half=n2 // 2, kg=kg, ct=ct)
    return pl.pallas_call(
        kern,
        grid=(norder, c // ct),
        in_specs=[pl.BlockSpec((None, n, ct), lambda o, j: (o, 0, j)),
                  pl.BlockSpec(perm.shape, lambda o, j: (0, 0)),
                  pl.BlockSpec(f1.shape, lambda o, j: (0, 0)),
                  pl.BlockSpec(tb.shape, lambda o, j: (0, 0, 0))],
        out_specs=pl.BlockSpec((None, ng * r2, ct), lambda o, j: (o, 0, j)),
        out_shape=jax.ShapeDtypeStruct((norder, ng * r2, c), F32),
        scratch_shapes=[pltpu.VMEM((nb, na, ct), F32), pltpu.VMEM((n2, nb * ct), F32)],
        compiler_params=_cp("parallel", "parallel"),
        name="lconv_filter",
    )(taps, perm, f1, tb)


def _pconv_kernel(fd_ref, gd_ref, v_ref, kf_ref, xg_ref, bias_ref, o_ref, *, half):
    v = v_ref[...]
    x = _dot(fd_ref[...], v.astype(BF16))
    kf = kf_ref[...]
    xr, xi = x[:half], x[half:]
    kr, ki = kf[:half], kf[half:]
    y = jnp.concatenate([xr * kr - xi * ki, xr * ki + xi * kr], axis=0).astype(BF16)
    yt = _dot(gd_ref[...], y)
    o_ref[...] = (xg_ref[...].astype(F32) * (yt + v.astype(F32) * bias_ref[...])).astype(o_ref.dtype)


def _pconv(fd, gd, vsrc, v_lead, kf, order, xsrc, x_lead, bias, nseq, L, out_dtype, ct=512):
    c = kf.shape[-1]
    n2 = fd.shape[0]
    return pl.pallas_call(
        functools.partial(_pconv_kernel, half=n2 // 2),
        grid=(nseq, c // ct),
        in_specs=[pl.BlockSpec((n2, L), lambda s, j: (0, 0)),
                  pl.BlockSpec((L, n2), lambda s, j: (0, 0)),
                  pl.BlockSpec((None,) * len(v_lead) + (L, ct), lambda s, j: v_lead + (s, j)),
                  pl.BlockSpec((None, n2, ct), lambda s, j: (order, 0, j)),
                  pl.BlockSpec((None,) * len(x_lead) + (L, ct), lambda s, j: x_lead + (s, j)),
                  pl.BlockSpec((1, ct), lambda s, j: (0, j))],
        out_specs=pl.BlockSpec((L, ct), lambda s, j: (s, j)),
        out_shape=jax.ShapeDtypeStruct((nseq * L, c), out_dtype),
        compiler_params=_cp("parallel", "parallel"),
        name="pconv",
    )(fd, gd, vsrc, kf, xsrc, bias.reshape(1, c))


def _gate_kernel(lr_ref, wa_ref, ba_ref, o_ref):
    logits = _dot_hi(lr_ref[...], wa_ref[...]) + ba_ref[...]
    o_ref[...] = jax.nn.log_sigmoid(logits) * (1.0 / GLA_TAU)


def _gla_gates(lr, col, wa_cat, ba_cat, tm=1024):
    m = lr.shape[0]
    k, n = wa_cat.shape
    return pl.pallas_call(
        _gate_kernel,
        grid=(m // tm,),
        in_specs=[pl.BlockSpec((tm, k), lambda i: (i, col)),
                  pl.BlockSpec((k, n), lambda i: (0, 0)),
                  pl.BlockSpec((1, n), lambda i: (0, 0))],
        out_specs=pl.BlockSpec((tm, n), lambda i: (i, 0)),
        out_shape=jax.ShapeDtypeStruct((m, n), F32),
        compiler_params=_cp("parallel"),
        name="gla_gates",
    )(lr, wa_cat, ba_cat)


def _split3(x):
    hi = x.astype(BF16)
    r1 = x - hi.astype(F32)
    mid = r1.astype(BF16)
    lo = (r1 - mid.astype(F32)).astype(BF16)
    return hi, mid, lo


def _gla_prepare(c, q_ref, k_ref, la_ref, sc_ref, dec_ref, tri, ref_row, last_row, scale):
    ch = GLA_CHUNK
    rows = slice(c * ch, (c + 1) * ch)
    hi, mid, lo = _split3(la_ref[rows, :])
    b = _dot(tri, hi) + _dot(tri, mid) + _dot(tri, lo)
    bref = b[ref_row:ref_row + 1, :]
    blast = b[last_row:last_row + 1, :]
    q = q_ref[rows, :].astype(F32) * scale
    k = k_ref[rows, :].astype(F32)
    sc_ref[0, rows, :] = (q * jnp.exp(b - bref)).astype(BF16)
    sc_ref[1, rows, :] = (k * jnp.exp(bref - b)).astype(BF16)
    sc_ref[2, rows, :] = (q * jnp.exp(b)).astype(BF16)
    sc_ref[3, rows, :] = (k * jnp.exp(blast - b)).astype(BF16)
    dec_ref[c * SUBLANES_F32:(c + 1) * SUBLANES_F32, :] = jnp.broadcast_to(jnp.exp(blast),
                                                                           (SUBLANES_F32, b.shape[1]))


def _gla_step(d, c, sc_ref, dec_ref, v_ref, o_ref, st_ref, causal, heads, dk, dv):
    ch = GLA_CHUNK
    rows = pl.ds(pl.multiple_of(c * ch, ch), ch)
    qt, kt, qin, kst = sc_ref[0, rows, :], sc_ref[1, rows, :], sc_ref[2, rows, :], sc_ref[3, rows, :]
    dec = dec_ref[pl.ds(pl.multiple_of(c * SUBLANES_F32, SUBLANES_F32), SUBLANES_F32), :][0:1, :]
    for h in range(heads):
        ks = slice(h * dk, (h + 1) * dk)
        vs = slice(h * dv, (h + 1) * dv)
        att = _dot_nt(qt[:, ks], kt[:, ks])
        att = jnp.where(causal, att, 0.0).astype(BF16)
        vh = v_ref[rows, vs]
        st = st_ref[d, h]
        o_ref[rows, vs] = (_dot(att, vh) + _dot_nt(qin[:, ks], st.astype(BF16))).astype(o_ref.dtype)
        st_ref[d, h] = st * dec[:, ks] + _dot_tn(vh, kst[:, ks])


def _gla_kernel(*refs, nch, heads, dk, dv, has_s0, scale, nsq):
    seq_in = [refs[8 * q:8 * (q + 1)] for q in range(nsq)]
    rest = refs[8 * nsq:]
    if has_s0:
        s0, of, ob, st, sc, dec = rest
    else:
        of, ob, st, sc, dec = rest
        s0 = None
    ch = GLA_CHUNK

    @pl.when(pl.program_id(1) == 0)
    def _():
        if has_s0:
            st[...] = s0[...]
        else:
            st[...] = jnp.zeros_like(st)

    r_i = lax.broadcasted_iota(jnp.int32, (ch, ch), 0)
    c_i = lax.broadcasted_iota(jnp.int32, (ch, ch), 1)
    lower = r_i >= c_i
    upper = r_i <= c_i
    tri_l = jnp.where(lower, 1.0, 0.0).astype(BF16)
    tri_u = jnp.where(upper, 1.0, 0.0).astype(BF16)

    for q, (qf, kf, vf, laf, qb, kb, vb, lab) in enumerate(seq_in):
        for c in range(nch):
            _gla_prepare(c, qf, kf, laf, sc.at[q, 0], dec.at[q, 0], tri_l, ch // 2 - 1, ch - 1, scale)
            _gla_prepare(c, qb, kb, lab, sc.at[q, 1], dec.at[q, 1], tri_u, ch // 2, 0, scale)

    def body(c, carry):
        for q, (qf, kf, vf, laf, qb, kb, vb, lab) in enumerate(seq_in):
            _gla_step(0, c, sc.at[q, 0], dec.at[q, 0], vf, of.at[q], st.at[q], lower, heads, dk, dv)
            _gla_step(1, nch - 1 - c, sc.at[q, 1], dec.at[q, 1], vb, ob.at[q], st.at[q], upper, heads, dk, dv)
        return carry

    lax.fori_loop(0, nch, body, 0)


def _gla(zmain, qcol, kcol, vcol, la, row0, nseq, L, tb, heads, dk, dv, s0=None, s0_layer=0, nsq=2):
    nblk = L // tb
    rb0 = row0 // tb
    hk, hv = heads * dk, heads * dv
    in_specs, args = [], []
    for q in range(nsq):
        def fwd(s, j, q=q):
            return rb0 + (s * nsq + q) * nblk + j

        def bwd(s, j, q=q):
            return rb0 + (s * nsq + q) * nblk + (nblk - 1 - j)

        for rowf, lcol in ((fwd, 0), (bwd, 1)):
            in_specs += [pl.BlockSpec((tb, hk), lambda s, j, rowf=rowf: (rowf(s, j), qcol)),
                         pl.BlockSpec((tb, hk), lambda s, j, rowf=rowf: (rowf(s, j), kcol)),
                         pl.BlockSpec((tb, hv), lambda s, j, rowf=rowf: (rowf(s, j), vcol)),
                         pl.BlockSpec((tb, hk), lambda s, j, rowf=rowf, lcol=lcol: (rowf(s, j), lcol))]
            args += [zmain, zmain, zmain, la]
    st_spec = pl.BlockSpec((nsq, 2, heads, dv, dk), lambda s, j: (s, 0, 0, 0, 0))
    if s0 is not None:
        in_specs.append(pl.BlockSpec((nsq, None, 2, heads, dv, dk), lambda s, j: (s, s0_layer, 0, 0, 0, 0)))
        args.append(s0)
    kern = functools.partial(_gla_kernel, nch=tb // GLA_CHUNK, heads=heads, dk=dk, dv=dv,
                             has_s0=s0 is not None, scale=dk ** -0.5, nsq=nsq)
    o_shape = jax.ShapeDtypeStruct((nseq // nsq, nsq, L, hv), BF16)
    o_f, o_b, st = pl.pallas_call(
        kern,
        grid=(nseq // nsq, nblk),
        in_specs=in_specs,
        out_specs=[pl.BlockSpec((None, nsq, tb, hv), lambda s, j: (s, 0, j, 0)),
                   pl.BlockSpec((None, nsq, tb, hv), lambda s, j: (s, 0, nblk - 1 - j, 0)),
                   st_spec],
        out_shape=[o_shape, o_shape, jax.ShapeDtypeStruct((nseq, 2, heads, dv, dk), F32)],
        scratch_shapes=[pltpu.VMEM((nsq, 2, 4, tb, hk), BF16),
                        pltpu.VMEM((nsq, 2, (tb // GLA_CHUNK) * SUBLANES_F32, hk), F32)],
        compiler_params=_cp("parallel", "arbitrary"),
        name="gla",
    )(*args)
    return o_f.reshape(nseq * L, hv), o_b.reshape(nseq * L, hv), st


def _gla_post_kernel(ofa_ref, ofb_ref, oba_ref, obb_ref, gr_ref, g_ref, o_ref, *, heads, dv, rows_a, tm):
    o = (_pick(ofa_ref, ofb_ref, rows_a, tm).astype(F32) + _pick(oba_ref, obb_ref, rows_a, tm).astype(F32))
    gate = _silu(gr_ref[...].astype(F32))
    g = g_ref[...]
    for h in range(heads):
        sl = slice(h * dv, (h + 1) * dv)
        oh = o[:, sl]
        ms = jnp.mean(oh * oh, axis=-1, keepdims=True)
        o_ref[:, sl] = (oh * lax.rsqrt(ms + RMS_EPS) * g * gate[:, sl]).astype(o_ref.dtype)


def _gla_post(of_a, of_b, ob_a, ob_b, zmain, grcol, norm_g, heads, dv, tm=512):
    ra, hv = of_a.shape
    m = ra + of_b.shape[0]
    return pl.pallas_call(
        functools.partial(_gla_post_kernel, heads=heads, dv=dv, rows_a=ra, tm=tm),
        grid=(m // tm,),
        in_specs=[*_two_source_specs(ra, tm, hv), *_two_source_specs(ra, tm, hv),
                  pl.BlockSpec((tm, hv), lambda i: (i, grcol)),
                  pl.BlockSpec((1, dv), lambda i: (0, 0))],
        out_specs=pl.BlockSpec((tm, hv), lambda i: (i, 0)),
        out_shape=jax.ShapeDtypeStruct((m, hv), BF16),
        compiler_params=_cp("arbitrary"),
        name="gla_post",
    )(of_a, of_b, ob_a, ob_b, zmain, norm_g.reshape(1, dv))


def _rope_tables(L, dh):
    rows = L // GRID_W
    r = np.repeat(np.arange(rows, dtype=np.float32), GRID_W)
    col = np.tile(np.arange(GRID_W, dtype=np.float32), rows)
    nf = dh // 4
    inv = (np.float32(ROPE_THETA) ** (-np.arange(nf, dtype=np.float32) / nf)).astype(np.float32)
    ang_r = (r[:, None] * inv).astype(np.float32)
    ang_c = (col[:, None] * inv).astype(np.float32)
    cos = np.concatenate([np.cos(ang_r), np.cos(ang_r), np.cos(ang_c), np.cos(ang_c)], -1)
    sin = np.concatenate([-np.sin(ang_r), np.sin(ang_r), -np.sin(ang_c), np.sin(ang_c)], -1)
    cos = np.concatenate([cos, cos], -1).astype(np.float32)
    sin = np.concatenate([sin, sin], -1).astype(np.float32)
    cos_t = np.stack([np.ones_like(cos), cos])
    sin_t = np.stack([np.zeros_like(sin), sin])
    return cos_t, sin_t


def _attn_kernel(*refs, has_ctx, lam_init, dh, bf16_exp, tq):
    if has_ctx:
        (q_ref, k_ref, v_ref, cos_ref, sin_ref, kc_ref, vc_ref, lam_ref, g_ref, o_ref,
         kt_ref, ve_ref, s0_ref, s1_ref) = refs
    else:
        q_ref, k_ref, v_ref, cos_ref, sin_ref, lam_ref, g_ref, o_ref, kt_ref, ve_ref, s0_ref, s1_ref = refs
    hw = 2 * dh
    nf = dh // 4
    lk = k_ref.shape[0]
    nsub = q_ref.shape[0] // tq
    lane = lax.broadcasted_iota(jnp.int32, (tq, hw), 1)
    first = lane < dh
    scale = dh ** -0.5 * math.log2(math.e)

    def rope(x, cos, sin):
        ln = lax.broadcasted_iota(jnp.int32, x.shape, 1)
        sw = jnp.where((ln & (2 * nf - 1)) < nf, pltpu.roll(x, hw - nf, 1), pltpu.roll(x, nf, 1))
        return x * cos + sw * sin

    kt_ref[...] = rope(k_ref[...], cos_ref[...], sin_ref[...]).T.astype(BF16)
    ve_ref[:, :hw] = v_ref[...].astype(BF16)
    ve_ref[:, hw:] = jnp.ones((lk, hw), BF16)

    if has_ctx:
        kct = kc_ref[...].T.astype(BF16)
        vcv = vc_ref[...].astype(BF16)
        vce = jnp.concatenate([vcv, jnp.ones_like(vcv)], axis=1)

    def scores(i, j, s_ref):
        rows = pl.ds(i * tq if isinstance(i, int) else pl.multiple_of(i * tq, tq), tq)
        q = (rope(q_ref[rows, :], cos_ref[rows, :], sin_ref[rows, :]) * scale).astype(BF16)
        qj = jnp.where(first, q, jnp.zeros_like(q)) if j == 0 else jnp.where(first, jnp.zeros_like(q), q)
        s_ref[:, :lk] = _dot(qj, kt_ref[...])
        if has_ctx:
            s_ref[:, lk:] = _dot(qj, kct)

    def prob(t):
        if bf16_exp:
            return jnp.exp2(t.astype(BF16))
        return jnp.exp2(t).astype(BF16)

    def softmax_pv(s_ref):
        s = s_ref[...]
        p = prob(s - jnp.max(s, axis=-1, keepdims=True))
        acc = _dot(p[:, :lk], ve_ref[...])
        if has_ctx:
            acc = acc + _dot(p[:, lk:], vce)
        return acc[:, :hw] / acc[:, hw:]

    lp = lam_ref[...]
    lam = (jnp.exp(jnp.sum(lp[0:1] * lp[1:2], axis=-1, keepdims=True))
           - jnp.exp(jnp.sum(lp[2:3] * lp[3:4], axis=-1, keepdims=True)) + lam_init)
    gain = g_ref[...] * (1.0 - lam_init)

    scores(0, 0, s0_ref)

    def body(i, carry):
        scores(i, 1, s1_ref)
        sm0 = softmax_pv(s0_ref)
        scores(jnp.minimum(i + 1, nsub - 1), 0, s0_ref)
        o = sm0 - lam * softmax_pv(s1_ref)
        ms = jnp.mean(o * o, axis=-1, keepdims=True)
        o_ref[pl.ds(pl.multiple_of(i * tq, tq), tq), :] = (o * lax.rsqrt(ms + RMS_EPS) * gain).astype(o_ref.dtype)
        return carry

    lax.fori_loop(0, nsub, body, 0)


def _diff_attention(zd, cos_t, sin_t, rope_kind, row0, nseq, L, heads, dh, lam_p, norm_g, lam_init,
                    ctx=None, tq=256):
    hw = 2 * dh
    sb0 = row0 // L
    past = 0 if ctx is None else ctx[0].shape[2]
    tab = pl.BlockSpec((None, L, hw), lambda s, h: (rope_kind, 0, 0), pipeline_mode=pl.Buffered(1))
    in_specs = [pl.BlockSpec((L, hw), lambda s, h: (sb0 + s, h)),
                pl.BlockSpec((L, hw), lambda s, h: (sb0 + s, heads + h)),
                pl.BlockSpec((L, hw), lambda s, h: (sb0 + s, 2 * heads + h)),
                tab, tab]
    args = [zd, zd, zd, cos_t, sin_t]
    if ctx is not None:
        ck, cv, layer = ctx
        in_specs += [pl.BlockSpec((None, None, past, hw), lambda s, h: (s, layer, 0, h)),
                     pl.BlockSpec((None, None, past, hw), lambda s, h: (s, layer, 0, h))]
        args += [ck, cv]
    in_specs += [pl.BlockSpec(lam_p.shape, lambda s, h: (0, 0)),
                 pl.BlockSpec((1, hw), lambda s, h: (0, 0))]
    args += [lam_p, norm_g.reshape(1, hw)]
    kern = functools.partial(_attn_kernel, has_ctx=ctx is not None, lam_init=lam_init, dh=dh,
                             bf16_exp=L > 1024, tq=min(tq, L))
    return pl.pallas_call(
        kern,
        grid=(nseq, heads),
        in_specs=in_specs,
        out_specs=pl.BlockSpec((L, hw), lambda s, h: (s, h)),
        out_shape=jax.ShapeDtypeStruct((nseq * L, heads * hw), BF16),
        scratch_shapes=[pltpu.VMEM((hw, L), BF16), pltpu.VMEM((L, 2 * hw), BF16),
                        pltpu.VMEM((min(tq, L), L + past), F32), pltpu.VMEM((min(tq, L), L + past), F32)],
        compiler_params=_cp("parallel", "parallel"),
        name="diff_attn",
    )(*args)


def _mix_kernel(yaa_ref, yab_ref, yb_ref, yca_ref, ycb_ref, g_ref, w_ref, o_ref, acc_ref, *, rows_a, tm):
    k = pl.program_id(1)

    def contrib(y):
        return g_ref[...].astype(F32) * _dot(y, w_ref[...])

    @pl.when(k == 0)
    def _():
        acc_ref[...] = contrib(_pick(yaa_ref, yab_ref, rows_a, tm))

    @pl.when(k == 1)
    def _():
        acc_ref[...] += contrib(yb_ref[...])

    @pl.when(k == 2)
    def _():
        o_ref[...] = (acc_ref[...] + contrib(_pick(yca_ref, ycb_ref, rows_a, tm))).astype(o_ref.dtype)


def _mix(ya_a, ya_b, yb, yc_a, yc_b, gates, wbr, tm=512):
    ra, w = ya_a.shape
    m = yb.shape[0]
    d = wbr.shape[2]
    return pl.pallas_call(
        functools.partial(_mix_kernel, rows_a=ra, tm=tm),
        grid=(m // tm, 3),
        in_specs=[*_two_source_specs(ra, tm, w, ndim_grid=2),
                  pl.BlockSpec((tm, w), lambda i, k: (i, 0)),
                  *_two_source_specs(ra, tm, w, ndim_grid=2),
                  pl.BlockSpec((tm, d), lambda i, k: (i, k)),
                  pl.BlockSpec((None, w, d), lambda i, k: (k, 0, 0))],
        out_specs=pl.BlockSpec((tm, d), lambda i, k: (i, 0)),
        out_shape=jax.ShapeDtypeStruct((m, d), BF16),
        scratch_shapes=[pltpu.VMEM((tm, d), F32)],
        compiler_params=_cp("arbitrary", "arbitrary"),
        name="mix",
    )(ya_a, ya_b, yb, yc_a, yc_b, gates, wbr)


def _pad_cols(w, n):
    return jnp.pad(w, ((0, 0), (0, n - w.shape[1])))


def kernel(x_prompt, x_sample, cache_k, cache_v, state_gla, c, c_ctx, w_mod, b_mod, ln_g, ln_b, ffn_w1, ffn_w3, ffn_w2, w_in, hy_conv_w, hy_conv_b, hy_w1, hy_b1, hy_freq, hy_w2, hy_b2, hy_w3, hy_decay, hy_bias, gla_wa, gla_ba, gla_norm_g, diff_lam, diff_norm_g, w_branch_a, w_branch_b, w_branch_c, w_out):
    batch, seq, d = x_prompt.shape
    dec_batch, dec_seq, _ = x_sample.shape
    depth = w_mod.shape[0]
    ffn_dim = ffn_w1.shape[3]
    hy_w = hy_bias.shape[2]
    heads_g, dk_g = 4, gla_wa.shape[3] // 4
    dv_g = gla_norm_g.shape[1]
    rank = gla_wa.shape[2]
    dh = diff_lam.shape[2]
    heads_d = cache_k.shape[3]
    dw = heads_d * 2 * dh
    gw = heads_g * dv_g
    gk = heads_g * dk_g
    assert batch * seq == GROUP and dec_seq == GROUP
    mp = batch * seq
    m = mp + dec_batch * dec_seq
    ngroups = 1 + dec_batch
    alpha = (2 * depth) ** 0.25
    lam_inits = [0.8 - 0.6 * math.exp(-0.3 * l) for l in range(depth)]

    c_main = 3 * hy_w + 2 * gk + 2 * gw
    c_lr = c_main
    c_d = c_lr + 2 * rank
    c_g = c_d + 3 * dw

    cond = jnp.concatenate([c_ctx[None], c, jnp.zeros((16 - ngroups, d), F32)], axis=0)
    mod = _modulation(cond, w_mod, b_mod)[:, :ngroups].reshape(depth, ngroups, N_MOD, d)

    na = 2 * dec_seq // FFT_NB
    n_s = 2 * dec_seq
    f1_half = jnp.asarray(_dft_a(na, na // 2), BF16)
    f1_full = jnp.asarray(_dft_a(na, na), BF16)
    g_s = jnp.asarray(_idft_a(na, n_s, na // 2), BF16)
    tb_np, tbi_np = _dft_b(na, FFT_NB, FFT_K1G)
    tb, tbi = jnp.asarray(tb_np, BF16), jnp.asarray(tbi_np, BF16)
    perm_np = _row_permutation(FFT_NB, SUBLANES_F32)
    perm, permt = jnp.asarray(perm_np, BF16), jnp.asarray(perm_np.T, BF16)
    fp_half = jnp.asarray(_dft_a(2 * seq, seq), BF16)
    fp_full = jnp.asarray(_dft_a(2 * seq, 2 * seq), BF16)
    g_p = jnp.asarray(_idft_a(2 * seq, 2 * seq, seq), BF16)
    cos_t, sin_t = _rope_tables(dec_seq, dh)
    cos_t, sin_t = jnp.asarray(cos_t), jnp.asarray(sin_t)

    x, h = _premod(x_prompt.reshape(mp, d), x_sample.reshape(dec_batch * dec_seq, d), mod, 0)

    fp = ((ffn_dim + 511) // 512) * 512
    ck = cache_k.reshape(dec_batch, depth, cache_k.shape[2], dw)
    cv = cache_v.reshape(dec_batch, depth, cache_v.shape[2], dw)
    state_t = jnp.swapaxes(state_gla, -1, -2)
    new_k, new_v, new_s = [], [], []

    for l in range(depth):
        w2 = _cast_pad_rows(ffn_w2, l, 0, fp)
        hid = _ffn1(h, ffn_w1, ffn_w3, l, 0)
        x, h = _mm_ln(hid, w2, x, mod, l, 2, 0.5, alpha, l, 3, ln_g[l, 0], ln_b[l, 0])

        wi = w_in[l]
        zmain = _proj(h, wi[:, :c_main].astype(BF16), BF16)
        zlr = _proj(h, _pad_cols(wi[:, c_lr:c_d].astype(BF16), LANES), F32)
        zd = _proj(h, wi[:, c_d:c_g].astype(BF16), F32)
        gates = _proj(h, wi[:, c_g:].astype(BF16), BF16, act="sigmoid")

        u3 = _short_conv(zmain, mp, hy_conv_w[l], hy_conv_b[l], seq, dec_seq, hy_w)
        fargs = (hy_w1[l], hy_b1[l], hy_freq[l], hy_w2[l], hy_b2[l], hy_w3[l], hy_decay[l], hy_w)
        taps_s = _hyena_filter_taps(dec_seq, *fargs)
        taps_p = _hyena_filter_taps(seq, *fargs)
        kf_s = _lconv_filter(taps_s, perm, f1_full, tb)
        kf_p = _lmm(fp_full, taps_p, lambda b: (b,), 2, F32, tn=hy_w)

        cw3 = jnp.swapaxes(hy_conv_w[l].reshape(3, 3, hy_w), 0, 1)
        cb3 = hy_conv_b[l].reshape(3, 1, hy_w)
        z1s = _lconv(zmain, mp, 0, zmain, mp, 1, cw3[0:2], cb3[0:2], True, perm, permt, f1_half, g_s,
                     tb, tbi, kf_s, 0, hy_bias[l, 0], dec_batch, dec_seq)
        ya_s = _lconv(z1s, 0, 0, zmain, mp, 2, cw3[1:3], cb3[1:3], False, perm, permt, f1_half, g_s,
                      tb, tbi, kf_s, 1, hy_bias[l, 1], dec_batch, dec_seq)
        z1 = _pconv(fp_half, g_p, u3, (0,), kf_p, 0, u3, (1,), hy_bias[l, 0], batch, seq, F32)
        ya_p = _pconv(fp_half, g_p, z1, (), kf_p, 1, u3, (2,), hy_bias[l, 1], batch, seq, BF16)

        wa_cat = jnp.zeros((LANES, 2 * gk), F32)
        wa_cat = wa_cat.at[:rank, :gk].set(gla_wa[l, 0]).at[rank:2 * rank, gk:].set(gla_wa[l, 1])
        ba_cat = jnp.concatenate([gla_ba[l, 0], gla_ba[l, 1]]).reshape(1, 2 * gk)
        la = _gla_gates(zlr, 0, wa_cat, ba_cat)
        qcol, kcol, vcol, grcol = (3 * hy_w) // gk, (3 * hy_w) // gk + 1, (3 * hy_w + 2 * gk) // gw, \
            (3 * hy_w + 2 * gk) // gw + 1
        of_p, ob_p, st_p = _gla(zmain, qcol, kcol, vcol, la, 0, batch, seq, seq, heads_g, dk_g, dv_g)
        of_s, ob_s, _ = _gla(zmain, qcol, kcol, vcol, la, mp, dec_batch, dec_seq, 512, heads_g, dk_g, dv_g,
                             s0=state_t, s0_layer=l)
        yb = _gla_post(of_p, of_s, ob_p, ob_s, zmain, grcol, gla_norm_g[l], heads_g, dv_g)
        new_s.append(jnp.swapaxes(st_p, -1, -2))

        yc_p = _diff_attention(zd, cos_t, sin_t, 0, 0, batch, seq, heads_d, dh, diff_lam[l],
                               diff_norm_g[l], lam_inits[l])
        yc_s = _diff_attention(zd, cos_t, sin_t, 1, mp, dec_batch, dec_seq, heads_d, dh, diff_lam[l],
                               diff_norm_g[l], lam_inits[l], ctx=(ck, cv, l), tq=512)
        new_k.append(zd[:mp, dw:2 * dw].reshape(batch, seq, heads_d, 2, dh))
        new_v.append(zd[:mp, 2 * dw:3 * dw].reshape(batch, seq, heads_d, 2 * dh))

        wbr = jnp.stack([w_branch_a[l], w_branch_b[l], w_branch_c[l]]).astype(BF16)
        y = _mix(ya_p, ya_s, yb, yc_p, yc_s, gates, wbr)
        x, h = _mm_ln(y, w_out[l].astype(BF16), x, mod, l, 5, 1.0, alpha, l, 6, ln_g[l, 1], ln_b[l, 1],
                      tm=512)

        w2 = _cast_pad_rows(ffn_w2, l, 1, fp)
        hid = _ffn1(h, ffn_w1, ffn_w3, l, 1)
        if l + 1 < depth:
            x, h = _mm_ln(hid, w2, x, mod, l, 8, 0.5, alpha, l + 1, 0, ln_g[l, 2], ln_b[l, 2])
        else:
            xp, _ = _mm_ln(hid, w2, x, mod, l, 8, 0.5, alpha, l, 0, ln_g[l, 2], ln_b[l, 2], rows=mp)
            xs, _ = _mm_ln(hid, w2, x, mod, l, 8, 0.5, alpha, l, 0, ln_g[l, 2], ln_b[l, 2], row0=mp,
                           rows=m - mp)

    y_prompt = xp.reshape(batch, seq, d)
    y_sample = xs.reshape(dec_batch, dec_seq, d)
    return (y_prompt, y_sample, jnp.stack(new_k, axis=1), jnp.stack(new_v, axis=1),
            jnp.stack(new_s, axis=1))
```

```python
import functools
import math

import numpy as np
import jax
import jax.numpy as jnp
from jax import lax
from jax.experimental import pallas as pl
from jax.experimental.pallas import tpu as pltpu

F32 = jnp.float32
BF16 = jnp.bfloat16

GRID_W = 64
N_MOD = 9
HY_BANDS = 8
GLA_TAU = 16.0
GLA_CHUNK = 64
ROPE_THETA = 10000.0
LN_EPS = 1e-5
RMS_EPS = 1e-6

LANES = 128
SUBLANES_F32 = 8
VMEM_BYTES_V7X = 64 * 1024 * 1024
VMEM_LIMIT = VMEM_BYTES_V7X - 8 * 1024 * 1024

GROUP = 4096
FFT_NB = 16
FFT_K1G = 8


def _cp(*sem):
    return pltpu.CompilerParams(dimension_semantics=sem, vmem_limit_bytes=VMEM_LIMIT)


def _dot(a, b):
    return jnp.dot(a, b, preferred_element_type=F32)


def _dot_nt(a, b):
    return lax.dot_general(a, b, (((1,), (1,)), ((), ())), preferred_element_type=F32)


def _dot_tn(a, b):
    return lax.dot_general(a, b, (((0,), (0,)), ((), ())), preferred_element_type=F32)


def _dot_hi(a, b):
    return jnp.dot(a, b, preferred_element_type=F32, precision=lax.Precision.HIGHEST)


def _silu(x):
    return x * jax.nn.sigmoid(x)


def _mod_kernel(c_ref, w_ref, b_ref, o_ref):
    c = c_ref[...]
    o_ref[...] = _dot(_silu(c).astype(BF16), w_ref[...].astype(BF16)) + b_ref[...]


def _modulation(cond, w_mod, b_mod):
    depth, d, n = w_mod.shape
    r = cond.shape[0]
    tn = 1024
    return pl.pallas_call(
        _mod_kernel,
        grid=(depth, n // tn),
        in_specs=[pl.BlockSpec((r, d), lambda l, j: (0, 0)),
                  pl.BlockSpec((None, d, tn), lambda l, j: (l, 0, j)),
                  pl.BlockSpec((None, 1, tn), lambda l, j: (l, 0, j))],
        out_specs=pl.BlockSpec((None, r, tn), lambda l, j: (l, 0, j)),
        out_shape=jax.ShapeDtypeStruct((depth, r, n), F32),
        compiler_params=_cp("parallel", "parallel"),
        name="mod",
    )(cond, w_mod, b_mod.reshape(depth, 1, n))


def _two_source_specs(rows_a, tm, width, col=0, ndim_grid=1):
    na = rows_a // tm
    if ndim_grid == 1:
        return (pl.BlockSpec((tm, width), lambda i: (jnp.minimum(i, na - 1), col)),
                pl.BlockSpec((tm, width), lambda i: (jnp.maximum(i - na, 0), col)))
    return (pl.BlockSpec((tm, width), lambda i, k: (jnp.minimum(i, na - 1), col)),
            pl.BlockSpec((tm, width), lambda i, k: (jnp.maximum(i - na, 0), col)))


def _pick(a_ref, b_ref, rows_a, tm):
    return jnp.where(pl.program_id(0) < rows_a // tm, a_ref[...], b_ref[...])


def _premod_kernel(xa_ref, xb_ref, mod_ref, x_ref, o_ref, *, rows_a, tm):
    x = _pick(xa_ref, xb_ref, rows_a, tm)
    x_ref[...] = x
    o_ref[...] = (x * (1.0 + mod_ref[1:2, :]) + mod_ref[0:1, :]).astype(o_ref.dtype)


def _premod(xa, xb, mod, layer, tm=512):
    ra, d = xa.shape
    m = ra + xb.shape[0]
    return pl.pallas_call(
        functools.partial(_premod_kernel, rows_a=ra, tm=tm),
        grid=(m // tm,),
        in_specs=[*_two_source_specs(ra, tm, d),
                  pl.BlockSpec((None, None, N_MOD, d), lambda i: (layer, (i * tm) // GROUP, 0, 0))],
        out_specs=[pl.BlockSpec((tm, d), lambda i: (i, 0)), pl.BlockSpec((tm, d), lambda i: (i, 0))],
        out_shape=[jax.ShapeDtypeStruct((m, d), F32), jax.ShapeDtypeStruct((m, d), BF16)],
        compiler_params=_cp("arbitrary"),
        name="premod",
    )(xa, xb, mod)


def _ffn1_kernel(h_ref, w1_ref, w3_ref, o_ref, w1b_ref, w3b_ref, *, tf, f_valid):
    @pl.when(pl.program_id(1) == 0)
    def _():
        col = pl.program_id(0) * tf + lax.broadcasted_iota(jnp.int32, w1_ref.shape, 1)
        keep = col < f_valid
        w1b_ref[...] = jnp.where(keep, w1_ref[...], 0.0).astype(BF16)
        w3b_ref[...] = jnp.where(keep, w3_ref[...], 0.0).astype(BF16)

    h = h_ref[...]
    a = _dot(h, w1b_ref[...])
    b = _dot(h, w3b_ref[...])
    o_ref[...] = (_silu(a) * b).astype(o_ref.dtype)


def _ffn1(h, w1, w3, layer, sub, tm=1024, tf=512):
    m, d = h.shape
    f = w1.shape[-1]
    nf = pl.cdiv(f, tf)
    wspec = pl.BlockSpec((None, None, d, tf), lambda j, i: (layer, sub, 0, j))
    return pl.pallas_call(
        functools.partial(_ffn1_kernel, tf=tf, f_valid=f),
        grid=(nf, m // tm),
        in_specs=[pl.BlockSpec((tm, d), lambda j, i: (i, 0)), wspec, wspec],
        out_specs=pl.BlockSpec((tm, tf), lambda j, i: (i, j)),
        out_shape=jax.ShapeDtypeStruct((m, nf * tf), BF16),
        scratch_shapes=[pltpu.VMEM((d, tf), BF16), pltpu.VMEM((d, tf), BF16)],
        compiler_params=_cp("parallel", "arbitrary"),
        name="ffn1",
    )(h, w1, w3)


def _cast_rows_kernel(w_ref, o_ref, *, rows_valid, tr):
    row = pl.program_id(0) * tr + lax.broadcasted_iota(jnp.int32, w_ref.shape, 0)
    o_ref[...] = jnp.where(row < rows_valid, w_ref[...], 0.0).astype(o_ref.dtype)


def _cast_pad_rows(w, layer, sub, rows_out, tr=512):
    f, d = w.shape[-2:]
    return pl.pallas_call(
        functools.partial(_cast_rows_kernel, rows_valid=f, tr=tr),
        grid=(rows_out // tr,),
        in_specs=[pl.BlockSpec((None, None, tr, d), lambda i: (layer, sub, i, 0))],
        out_specs=pl.BlockSpec((tr, d), lambda i: (i, 0)),
        out_shape=jax.ShapeDtypeStruct((rows_out, d), BF16),
        compiler_params=_cp("parallel"),
        name="cast_w2",
    )(w)


def _mm_ln_kernel(a_ref, w_ref, x_ref, mod_ref, nmod_ref, g_ref, b_ref, xo_ref, ho_ref, *,
                  gate_row, coef, alpha, nshift_row):
    gate = coef * mod_ref[gate_row:gate_row + 1, :]
    xr = alpha * x_ref[...] + gate * _dot(a_ref[...], w_ref[...])
    mu = jnp.mean(xr, axis=-1, keepdims=True)
    xc = xr - mu
    var = jnp.mean(xc * xc, axis=-1, keepdims=True)
    xn = xc * lax.rsqrt(var + LN_EPS) * g_ref[...] + b_ref[...]
    xo_ref[...] = xn
    ho_ref[...] = (xn * (1.0 + nmod_ref[nshift_row + 1:nshift_row + 2, :])
                   + nmod_ref[nshift_row:nshift_row + 1, :]).astype(ho_ref.dtype)


def _mm_ln(a, w, x, mod, layer, gate_row, coef, alpha, nlayer, nshift_row, ln_g, ln_b, tm=256,
           row0=0, rows=None):
    kdim = a.shape[1]
    m = a.shape[0] if rows is None else rows
    d = w.shape[1]
    rb0 = row0 // tm
    kern = functools.partial(_mm_ln_kernel, gate_row=gate_row, coef=coef, alpha=alpha, nshift_row=nshift_row)
    return pl.pallas_call(
        kern,
        grid=(m // tm,),
        in_specs=[pl.BlockSpec((tm, kdim), lambda i: (rb0 + i, 0)),
                  pl.BlockSpec((kdim, d), lambda i: (0, 0), pipeline_mode=pl.Buffered(1)),
                  pl.BlockSpec((tm, d), lambda i: (rb0 + i, 0)),
                  pl.BlockSpec((None, None, N_MOD, d), lambda i: (layer, ((rb0 + i) * tm) // GROUP, 0, 0)),
                  pl.BlockSpec((None, None, N_MOD, d), lambda i: (nlayer, ((rb0 + i) * tm) // GROUP, 0, 0)),
                  pl.BlockSpec((1, d), lambda i: (0, 0)),
                  pl.BlockSpec((1, d), lambda i: (0, 0))],
        out_specs=[pl.BlockSpec((tm, d), lambda i: (i, 0)),
                   pl.BlockSpec((tm, d), lambda i: (i, 0))],
        out_shape=[jax.ShapeDtypeStruct((m, d), F32), jax.ShapeDtypeStruct((m, d), BF16)],
        compiler_params=_cp("parallel"),
        name="mm_ln",
    )(a, w, x, mod, mod, ln_g.reshape(1, d), ln_b.reshape(1, d))


def _proj_kernel(h_ref, w_ref, o_ref, *, act):
    r = _dot(h_ref[...], w_ref[...])
    if act == "sigmoid":
        r = jax.nn.sigmoid(r)
    o_ref[...] = r.astype(o_ref.dtype)


def _proj(h, w, out_dtype, act=None, tm=1024, tn=1024):
    m, d = h.shape
    n = w.shape[1]
    tn = min(tn, n)
    return pl.pallas_call(
        functools.partial(_proj_kernel, act=act),
        grid=(m // tm, n // tn),
        in_specs=[pl.BlockSpec((tm, d), lambda i, j: (i, 0)),
                  pl.BlockSpec((d, tn), lambda i, j: (0, j))],
        out_specs=pl.BlockSpec((tm, tn), lambda i, j: (i, j)),
        out_shape=jax.ShapeDtypeStruct((m, n), out_dtype),
        compiler_params=_cp("parallel", "arbitrary"),
        name="proj",
    )(h, w)


def _sconv_kernel(z_ref, zp_ref, zn_ref, w_ref, b_ref, o_ref, *, tm, halo, lp, ls):
    i = pl.program_id(0)
    u = z_ref[...].astype(F32)
    prev = zp_ref[...].astype(F32)[halo - 1:halo, :]
    nxt = zn_ref[...].astype(F32)[0:1, :]
    row = lax.broadcasted_iota(jnp.int32, u.shape, 0)
    lseq = jnp.where((i * tm) // GROUP == 0, lp, ls)
    pos = (row + i * tm) & (lseq - 1)
    up = jnp.where(row == 0, prev, pltpu.roll(u, 1, 0))
    up = jnp.where(pos == 0, 0.0, up)
    un = jnp.where(row == tm - 1, nxt, pltpu.roll(u, tm - 1, 0))
    un = jnp.where(pos == lseq - 1, 0.0, un)
    w = w_ref[...]
    o_ref[...] = (up * w[0:1, :] + u * w[1:2, :] + un * w[2:3, :] + b_ref[...]).astype(o_ref.dtype)


def _short_conv(z, rows, conv_w, conv_b, lp, ls, width, tm=1024, ct=512, halo=16):
    m = rows
    nct = width // ct
    nrb = m // halo
    kern = functools.partial(_sconv_kernel, tm=tm, halo=halo, lp=lp, ls=ls)
    return pl.pallas_call(
        kern,
        grid=(m // tm, 3 * nct),
        in_specs=[pl.BlockSpec((tm, ct), lambda i, j: (i, j)),
                  pl.BlockSpec((halo, ct), lambda i, j: (jnp.maximum(i * (tm // halo) - 1, 0), j)),
                  pl.BlockSpec((halo, ct), lambda i, j: (jnp.minimum((i + 1) * (tm // halo), nrb - 1), j)),
                  pl.BlockSpec((3, ct), lambda i, j: (0, j)),
                  pl.BlockSpec((1, ct), lambda i, j: (0, j))],
        out_specs=pl.BlockSpec((None, tm, ct), lambda i, j: (j // nct, i, j % nct)),
        out_shape=jax.ShapeDtypeStruct((3, m, width), F32),
        compiler_params=_cp("parallel", "parallel"),
        name="sconv",
    )(z, z, z, conv_w, conv_b.reshape(1, -1))


def _filter_features(L):
    t = np.linspace(0.0, 1.0, L, dtype=np.float32)
    w = (np.float32(2.0 * math.pi / L) * np.arange(L, dtype=np.float32)).astype(np.float32)
    f = np.linspace(1e-4, HY_BANDS - 1, HY_BANDS, dtype=np.float32)
    wf = (w[:, None] * f).astype(np.float32)
    feats = np.concatenate([t[:, None], np.cos(wf), -np.sin(wf)], -1).astype(np.float32)
    idx = np.concatenate([np.arange(L), [0], np.arange(L - 1, 0, -1)])
    tab = np.zeros((2 * L, 32), np.float32)
    tab[:, :feats.shape[1]] = feats[idx]
    tab[:, 24] = t[idx]
    tab[:L, 25] = 1.0
    tab[L + 1:, 26] = -1.0
    return tab, feats.shape[1]


def _filter_kernel(tab_ref, w1_ref, b1_ref, fr_ref, w2_ref, b2_ref, w3_ref, dec_ref, o_ref):
    tab = tab_ref[...]
    fr = fr_ref[...]
    hdn = jnp.sin(fr[0:1, :] * (_dot_hi(tab, w1_ref[...]) + b1_ref[...]))
    hdn = jnp.sin(fr[1:2, :] * (_dot_hi(hdn, w2_ref[...]) + b2_ref[...]))
    t = tab[:, 24:25]
    sign = tab[:, 25:26] + tab[:, 26:27]
    for o in range(2):
        o_ref[o] = sign * (_dot_hi(hdn, w3_ref[o]) * jnp.exp(-t * jnp.abs(dec_ref[o])))


def _hyena_filter_taps(L, hy_w1, hy_b1, hy_freq, hy_w2, hy_b2, hy_w3, hy_decay, width, rb=256):
    tab_np, nfeat = _filter_features(L)
    fh = hy_w1.shape[1]
    w1p = jnp.zeros((32, fh), F32).at[:nfeat].set(hy_w1)
    n = 2 * L
    nhalf = L // rb
    w3d = jnp.transpose(hy_w3.reshape(fh, 2, 2, width), (1, 2, 0, 3))
    decd = hy_decay.reshape(2, 2, 1, width)
    return pl.pallas_call(
        _filter_kernel,
        grid=(n // rb,),
        in_specs=[pl.BlockSpec((rb, 32), lambda i: (i, 0)),
                  pl.BlockSpec((32, fh), lambda i: (0, 0)),
                  pl.BlockSpec((1, fh), lambda i: (0, 0)),
                  pl.BlockSpec((2, fh), lambda i: (0, 0)),
                  pl.BlockSpec((fh, fh), lambda i: (0, 0)),
                  pl.BlockSpec((1, fh), lambda i: (0, 0)),
                  pl.BlockSpec((2, None, fh, width), lambda i: (0, i // nhalf, 0, 0)),
                  pl.BlockSpec((2, None, 1, width), lambda i: (0, i // nhalf, 0, 0))],
        out_specs=pl.BlockSpec((2, rb, width), lambda i: (0, i, 0)),
        out_shape=jax.ShapeDtypeStruct((2, n, width), F32),
        compiler_params=_cp("parallel"),
        name="hyfilter",
    )(jnp.asarray(tab_np), w1p, hy_b1.reshape(1, fh), hy_freq, hy_w2, hy_b2.reshape(1, fh), w3d, decd)


def _dft_a(na, ka):
    k1 = np.arange(na // 2)[:, None].astype(np.float64)
    a = np.arange(ka)[None, :].astype(np.float64)
    th = 2.0 * np.pi * a * (k1 + 0.5) / na
    return np.concatenate([np.cos(th), -np.sin(th)], 0)


def _idft_a(na, n, rows):
    k1 = np.arange(na // 2)[None, :].astype(np.float64)
    a = np.arange(rows)[:, None].astype(np.float64)
    th = 2.0 * np.pi * a * (k1 + 0.5) / na
    return (2.0 / n) * np.concatenate([np.cos(th), -np.sin(th)], 1)


def _dft_b(na, nb, g):
    n = na * nb
    half = na // 2
    k1 = np.arange(half).astype(np.float64)
    b = np.arange(nb).astype(np.float64)
    k2 = np.arange(nb).astype(np.float64)
    phi = 2.0 * np.pi * (b[None, None, :] * k2[None, :, None] / nb
                         + b[None, None, :] * (k1[:, None, None] + 0.5) / n)
    c, s = np.cos(phi), np.sin(phi)
    ng = half // g
    r = g * nb
    fwd = np.zeros((ng, 2 * r, 2 * r))
    for q in range(g):
        rows = slice(q * nb, (q + 1) * nb)
        rows_i = slice(r + q * nb, r + (q + 1) * nb)
        cols = slice(q, r, g)
        cols_i = slice(r + q, 2 * r, g)
        cq, sq = c[q::g], s[q::g]
        fwd[:, rows, cols] = cq
        fwd[:, rows, cols_i] = sq
        fwd[:, rows_i, cols] = -sq
        fwd[:, rows_i, cols_i] = cq
    inv = np.transpose(fwd, (0, 2, 1))
    return fwd, inv


def _lmm_kernel(f_ref, x_ref, o_ref):
    o_ref[...] = _dot(f_ref[...], x_ref[...].astype(BF16)).astype(o_ref.dtype)


def _lmm(f, x, x_index, nbatch, out_dtype, tn=2048):
    mo, k = f.shape
    n = x.shape[-1]
    lead = len(x.shape) - 2
    return pl.pallas_call(
        _lmm_kernel,
        grid=(nbatch, n // tn),
        in_specs=[pl.BlockSpec((mo, k), lambda b, j: (0, 0)),
                  pl.BlockSpec((None,) * lead + (k, tn), lambda b, j: x_index(b) + (0, j))],
        out_specs=pl.BlockSpec((None, mo, tn), lambda b, j: (b, 0, j)),
        out_shape=jax.ShapeDtypeStruct((nbatch, mo, n), out_dtype),
        compiler_params=_cp("parallel", "parallel"),
        name="dft_a",
    )(f, x)


def _stage_b_rows(r_ref, r0, half, nb, kg, ct):
    return [(pl.ds(ri * half + r0, kg), slice(b * ct, (b + 1) * ct)) for ri in range(2) for b in range(nb)]


def _row_permutation(nb, al):
    p = np.zeros((nb * al, nb * al), np.float32)
    for b in range(nb):
        for a in range(al):
            p[b * al + a, a * nb + b] = 1.0
    return p


def _to_residue_major(src_ref, perm_ref, dst_ref, nb, al):
    blk = nb * al
    for i in range(src_ref.shape[0] // blk):
        z = _dot(perm_ref[...], src_ref[i * blk:(i + 1) * blk, :].astype(BF16))
        for b in range(nb):
            dst_ref[b, i * al:(i + 1) * al, :] = z[b * al:(b + 1) * al, :]


def _short_conv_residue_major(u_ref, w_ref, cb_ref, nb):
    rows = u_ref.shape[1]
    row = lax.broadcasted_iota(jnp.int32, u_ref.shape[1:], 0)
    w = w_ref[...]
    cb = cb_ref[...]
    first = u_ref[0]
    prev = jnp.where(row == 0, 0.0, pltpu.roll(u_ref[nb - 1], 1, 0))
    for b in range(nb):
        cur = first if b == 0 else u_ref[b]
        nxt = u_ref[b + 1] if b + 1 < nb else jnp.where(row == rows - 1, 0.0, pltpu.roll(first, rows - 1, 0))
        u_ref[b] = prev * w[0:1, :] + cur * w[1:2, :] + nxt * w[2:3, :] + cb
        prev = cur


def _lconv_kernel(zv_ref, zx_ref, cwv_ref, cbv_ref, cwx_ref, cbx_ref, perm_ref, permt_ref, f1_ref, g_ref,
                  tb_ref, tbi_ref, kf_ref, bias_ref, o_ref, v_ref, x_ref, r_ref, *, nb, half, kg, ct, conv_v):
    r = nb * kg
    al = perm_ref.shape[0] // nb
    _to_residue_major(zv_ref, perm_ref, v_ref, nb, al)
    _to_residue_major(zx_ref, perm_ref, x_ref, nb, al)
    if conv_v:
        _short_conv_residue_major(v_ref, cwv_ref, cbv_ref, nb)
    _short_conv_residue_major(x_ref, cwx_ref, cbx_ref, nb)
    for b in range(nb):
        r_ref[:, b * ct:(b + 1) * ct] = _dot(f1_ref[...], v_ref[b].astype(BF16))

    for g in range(half // kg):
        tiles = _stage_b_rows(r_ref, g * kg, half, nb, kg, ct)
        a = jnp.concatenate([r_ref[rs, cs] for rs, cs in tiles], axis=0).astype(BF16)
        x = _dot(tb_ref[g], a)
        kf = kf_ref[g * 2 * r:(g + 1) * 2 * r, :]
        xr, xi = x[:r], x[r:]
        kr, ki = kf[:r], kf[r:]
        y = jnp.concatenate([xr * kr - xi * ki, xr * ki + xi * kr], axis=0).astype(BF16)
        bh = _dot(tbi_ref[g], y)
        for t, (rs, cs) in enumerate(tiles):
            r_ref[rs, cs] = bh[t * kg:(t + 1) * kg, :]
    bias = bias_ref[...]
    for b in range(nb):
        y = _dot(g_ref[...], r_ref[:, b * ct:(b + 1) * ct].astype(BF16))
        v_ref[b] = x_ref[b] * (y + v_ref[b] * bias)
    blk = nb * al
    for i in range(o_ref.shape[0] // blk):
        t = jnp.concatenate([v_ref[b, i * al:(i + 1) * al, :] for b in range(nb)], axis=0).astype(BF16)
        o_ref[i * blk:(i + 1) * blk, :] = _dot(permt_ref[...], t).astype(o_ref.dtype)


def _lconv(zv, v_row0, v_col0, zx, x_row0, x_col0, conv_w, conv_b, conv_v, perm, permt, f1, g_inv, tb, tbi,
           kf, order, bias, nbatch, L, ct=256):
    c = kf.shape[-1]
    n2, _ = f1.shape
    ng, r2, _ = tb.shape
    nb = L // (n2 // 2)
    kg = r2 // (2 * nb)
    ncb = c // ct
    once = dict(pipeline_mode=pl.Buffered(1))
    kern = functools.partial(_lconv_kernel, nb=nb, half=n2 // 2, kg=kg, ct=ct, conv_v=conv_v)
    vb0, xb0 = v_row0 // L, x_row0 // L
    return pl.pallas_call(
        kern,
        grid=(ncb, nbatch),
        in_specs=[pl.BlockSpec((L, ct), lambda j, b: (vb0 + b, v_col0 * ncb + j)),
                  pl.BlockSpec((L, ct), lambda j, b: (xb0 + b, x_col0 * ncb + j)),
                  pl.BlockSpec((None, 3, ct), lambda j, b: (0, 0, j)),
                  pl.BlockSpec((None, 1, ct), lambda j, b: (0, 0, j)),
                  pl.BlockSpec((None, 3, ct), lambda j, b: (1, 0, j)),
                  pl.BlockSpec((None, 1, ct), lambda j, b: (1, 0, j)),
                  pl.BlockSpec(perm.shape, lambda j, b: (0, 0), **once),
                  pl.BlockSpec(permt.shape, lambda j, b: (0, 0), **once),
                  pl.BlockSpec(f1.shape, lambda j, b: (0, 0), **once),
                  pl.BlockSpec(g_inv.shape, lambda j, b: (0, 0), **once),
                  pl.BlockSpec(tb.shape, lambda j, b: (0, 0, 0), **once),
                  pl.BlockSpec(tbi.shape, lambda j, b: (0, 0, 0), **once),
                  pl.BlockSpec((None, ng * r2, ct), lambda j, b: (order, 0, j), **once),
                  pl.BlockSpec((1, ct), lambda j, b: (0, j))],
        out_specs=pl.BlockSpec((L, ct), lambda j, b: (b, j)),
        out_shape=jax.ShapeDtypeStruct((nbatch * L, c), BF16),
        scratch_shapes=[pltpu.VMEM((nb, L // nb, ct), F32), pltpu.VMEM((nb, L // nb, ct), F32),
                        pltpu.VMEM((n2, nb * ct), F32)],
        compiler_params=_cp("parallel", "arbitrary"),
        name="lconv",
    )(zv, zx, conv_w, conv_b, conv_w, conv_b, perm, permt, f1, g_inv, tb, tbi, kf, bias.reshape(1, c))


def _lconv_filter_kernel(x_ref, perm_ref, f1_ref, tb_ref, o_ref, t_ref, r_ref, *, nb, half, kg, ct):
    r2 = 2 * nb * kg
    _to_residue_major(x_ref, perm_ref, t_ref, nb, perm_ref.shape[0] // nb)
    for b in range(nb):
        r_ref[:, b * ct:(b + 1) * ct] = _dot(f1_ref[...], t_ref[b].astype(BF16))

    for g in range(half // kg):
        tiles = _stage_b_rows(r_ref, g * kg, half, nb, kg, ct)
        a = jnp.concatenate([r_ref[rs, cs] for rs, cs in tiles], axis=0).astype(BF16)
        o_ref[g * r2:(g + 1) * r2, :] = _dot(tb_ref[g], a)


def _lconv_filter(taps, perm, f1, tb, ct=256):
    norder, n, c = taps.shape
    n2, na = f1.shape
    nb = n // na
    ng, r2, _ = tb.shape
    kg = r2 // (2 * nb)
    kern = functools.partial(_lconv_filter_kernel, nb=nb, half=n2 // 2, kg=kg, ct=ct)
    return pl.pallas_call(
        kern,
        grid=(norder, c // ct),
        in_specs=[pl.BlockSpec((None, n, ct), lambda o, j: (o, 0, j)),
                  pl.BlockSpec(perm.shape, lambda o, j: (0, 0)),
                  pl.BlockSpec(f1.shape, lambda o, j: (0, 0)),
                  pl.BlockSpec(tb.shape, lambda o, j: (0, 0, 0))],
        out_specs=pl.BlockSpec((None, ng * r2, ct), lambda o, j: (o, 0, j)),
        out_shape=jax.ShapeDtypeStruct((norder, ng * r2, c), F32),
        scratch_shapes=[pltpu.VMEM((nb, na, ct), F32), pltpu.VMEM((n2, nb * ct), F32)],
        compiler_params=_cp("parallel", "parallel"),
        name="lconv_filter",
    )(taps, perm, f1, tb)


def _pconv_kernel(fd_ref, gd_ref, v_ref, kf_ref, xg_ref, bias_ref, o_ref, *, half):
    v = v_ref[...]
    x = _dot(fd_ref[...], v.astype(BF16))
    kf = kf_ref[...]
    xr, xi = x[:half], x[half:]
    kr, ki = kf[:half], kf[half:]
    y = jnp.concatenate([xr * kr - xi * ki, xr * ki + xi * kr], axis=0).astype(BF16)
    yt = _dot(gd_ref[...], y)
    o_ref[...] = (xg_ref[...].astype(F32) * (yt + v.astype(F32) * bias_ref[...])).astype(o_ref.dtype)


def _pconv(fd, gd, vsrc, v_lead, kf, order, xsrc, x_lead, bias, nseq, L, out_dtype, ct=512):
    c = kf.shape[-1]
    n2 = fd.shape[0]
    return pl.pallas_call(
        functools.partial(_pconv_kernel, half=n2 // 2),
        grid=(nseq, c // ct),
        in_specs=[pl.BlockSpec((n2, L), lambda s, j: (0, 0)),
                  pl.BlockSpec((L, n2), lambda s, j: (0, 0)),
                  pl.BlockSpec((None,) * len(v_lead) + (L, ct), lambda s, j: v_lead + (s, j)),
                  pl.BlockSpec((None, n2, ct), lambda s, j: (order, 0, j)),
                  pl.BlockSpec((None,) * len(x_lead) + (L, ct), lambda s, j: x_lead + (s, j)),
                  pl.BlockSpec((1, ct), lambda s, j: (0, j))],
        out_specs=pl.BlockSpec((L, ct), lambda s, j: (s, j)),
        out_shape=jax.ShapeDtypeStruct((nseq * L, c), out_dtype),
        compiler_params=_cp("parallel", "parallel"),
        name="pconv",
    )(fd, gd, vsrc, kf, xsrc, bias.reshape(1, c))


def _gate_kernel(lr_ref, wa_ref, ba_ref, o_ref):
    lr, wa = lr_ref[...], wa_ref[...]
    lh, wh = lr.astype(BF16), wa.astype(BF16)
    ll, wl = (lr - lh.astype(F32)).astype(BF16), (wa - wh.astype(F32)).astype(BF16)
    logits = _dot(lh, wh) + _dot(lh, wl) + _dot(ll, wh) + ba_ref[...]
    o_ref[...] = jax.nn.log_sigmoid(logits) * (1.0 / GLA_TAU)


def _gla_gates(lr, col, wa_cat, ba_cat, tm=1024):
    m = lr.shape[0]
    k, n = wa_cat.shape
    return pl.pallas_call(
        _gate_kernel,
        grid=(m // tm,),
        in_specs=[pl.BlockSpec((tm, k), lambda i: (i, col)),
                  pl.BlockSpec((k, n), lambda i: (0, 0)),
                  pl.BlockSpec((1, n), lambda i: (0, 0))],
        out_specs=pl.BlockSpec((tm, n), lambda i: (i, 0)),
        out_shape=jax.ShapeDtypeStruct((m, n), F32),
        compiler_params=_cp("parallel"),
        name="gla_gates",
    )(lr, wa_cat, ba_cat)


def _split3(x):
    hi = x.astype(BF16)
    r1 = x - hi.astype(F32)
    mid = r1.astype(BF16)
    lo = (r1 - mid.astype(F32)).astype(BF16)
    return hi, mid, lo


def _gla_prepare(c, q_ref, k_ref, la_ref, sc_ref, dec_ref, tri, ref_row, last_row, scale):
    ch = GLA_CHUNK
    rows = slice(c * ch, (c + 1) * ch)
    hi, mid, lo = _split3(la_ref[rows, :])
    b = _dot(tri, hi) + _dot(tri, mid) + _dot(tri, lo)
    bref = b[ref_row:ref_row + 1, :]
    blast = b[last_row:last_row + 1, :]
    q = q_ref[rows, :].astype(F32) * scale
    k = k_ref[rows, :].astype(F32)
    sc_ref[0, rows, :] = (q * jnp.exp(b - bref)).astype(BF16)
    sc_ref[1, rows, :] = (k * jnp.exp(bref - b)).astype(BF16)
    sc_ref[2, rows, :] = (q * jnp.exp(b)).astype(BF16)
    sc_ref[3, rows, :] = (k * jnp.exp(blast - b)).astype(BF16)
    dec_ref[c * SUBLANES_F32:(c + 1) * SUBLANES_F32, :] = jnp.broadcast_to(jnp.exp(blast),
                                                                           (SUBLANES_F32, b.shape[1]))


def _gla_step(d, c, sc_ref, dec_ref, v_ref, o_ref, st_ref, causal, heads, dk, dv):
    ch = GLA_CHUNK
    rows = pl.ds(pl.multiple_of(c * ch, ch), ch)
    qt, kt, qin, kst = sc_ref[0, rows, :], sc_ref[1, rows, :], sc_ref[2, rows, :], sc_ref[3, rows, :]
    dec = dec_ref[pl.ds(pl.multiple_of(c * SUBLANES_F32, SUBLANES_F32), SUBLANES_F32), :][0:1, :]
    for h in range(heads):
        ks = slice(h * dk, (h + 1) * dk)
        vs = slice(h * dv, (h + 1) * dv)
        att = _dot_nt(qt[:, ks], kt[:, ks])
        att = jnp.where(causal, att, 0.0).astype(BF16)
        vh = v_ref[rows, vs]
        st = st_ref[d, h]
        o_ref[rows, vs] = (_dot(att, vh) + _dot_nt(qin[:, ks], st.astype(BF16))).astype(o_ref.dtype)
        st_ref[d, h] = st * dec[:, ks] + _dot_tn(vh, kst[:, ks])


def _gla_kernel(*refs, nch, heads, dk, dv, has_s0, scale, nsq):
    seq_in = [refs[8 * q:8 * (q + 1)] for q in range(nsq)]
    rest = refs[8 * nsq:]
    if has_s0:
        s0, of, ob, st, sc, dec = rest
    else:
        of, ob, st, sc, dec = rest
        s0 = None
    ch = GLA_CHUNK

    @pl.when(pl.program_id(1) == 0)
    def _():
        if has_s0:
            st[...] = s0[...]
        else:
            st[...] = jnp.zeros_like(st)

    r_i = lax.broadcasted_iota(jnp.int32, (ch, ch), 0)
    c_i = lax.broadcasted_iota(jnp.int32, (ch, ch), 1)
    lower = r_i >= c_i
    upper = r_i <= c_i
    tri_l = jnp.where(lower, 1.0, 0.0).astype(BF16)
    tri_u = jnp.where(upper, 1.0, 0.0).astype(BF16)

    for q, (qf, kf, vf, laf, qb, kb, vb, lab) in enumerate(seq_in):
        for c in range(nch):
            _gla_prepare(c, qf, kf, laf, sc.at[q, 0], dec.at[q, 0], tri_l, ch // 2 - 1, ch - 1, scale)
            _gla_prepare(c, qb, kb, lab, sc.at[q, 1], dec.at[q, 1], tri_u, ch // 2, 0, scale)

    def body(c, carry):
        for q, (qf, kf, vf, laf, qb, kb, vb, lab) in enumerate(seq_in):
            _gla_step(0, c, sc.at[q, 0], dec.at[q, 0], vf, of.at[q], st.at[q], lower, heads, dk, dv)
            _gla_step(1, nch - 1 - c, sc.at[q, 1], dec.at[q, 1], vb, ob.at[q], st.at[q], upper, heads, dk, dv)
        return carry

    lax.fori_loop(0, nch, body, 0)


def _gla(zmain, qcol, kcol, vcol, la, row0, nseq, L, tb, heads, dk, dv, s0=None, s0_layer=0, nsq=2):
    nblk = L // tb
    rb0 = row0 // tb
    hk, hv = heads * dk, heads * dv
    in_specs, args = [], []
    for q in range(nsq):
        def fwd(s, j, q=q):
            return rb0 + (s * nsq + q) * nblk + j

        def bwd(s, j, q=q):
            return rb0 + (s * nsq + q) * nblk + (nblk - 1 - j)

        for rowf, lcol in ((fwd, 0), (bwd, 1)):
            in_specs += [pl.BlockSpec((tb, hk), lambda s, j, rowf=rowf: (rowf(s, j), qcol)),
                         pl.BlockSpec((tb, hk), lambda s, j, rowf=rowf: (rowf(s, j), kcol)),
                         pl.BlockSpec((tb, hv), lambda s, j, rowf=rowf: (rowf(s, j), vcol)),
                         pl.BlockSpec((tb, hk), lambda s, j, rowf=rowf, lcol=lcol: (rowf(s, j), lcol))]
            args += [zmain, zmain, zmain, la]
    st_spec = pl.BlockSpec((nsq, 2, heads, dv, dk), lambda s, j: (s, 0, 0, 0, 0))
    if s0 is not None:
        in_specs.append(pl.BlockSpec((nsq, None, 2, heads, dv, dk), lambda s, j: (s, s0_layer, 0, 0, 0, 0)))
        args.append(s0)
    kern = functools.partial(_gla_kernel, nch=tb // GLA_CHUNK, heads=heads, dk=dk, dv=dv,
                             has_s0=s0 is not None, scale=dk ** -0.5, nsq=nsq)
    o_shape = jax.ShapeDtypeStruct((nseq // nsq, nsq, L, hv), BF16)
    o_f, o_b, st = pl.pallas_call(
        kern,
        grid=(nseq // nsq, nblk),
        in_specs=in_specs,
        out_specs=[pl.BlockSpec((None, nsq, tb, hv), lambda s, j: (s, 0, j, 0)),
                   pl.BlockSpec((None, nsq, tb, hv), lambda s, j: (s, 0, nblk - 1 - j, 0)),
                   st_spec],
        out_shape=[o_shape, o_shape, jax.ShapeDtypeStruct((nseq, 2, heads, dv, dk), F32)],
        scratch_shapes=[pltpu.VMEM((nsq, 2, 4, tb, hk), BF16),
                        pltpu.VMEM((nsq, 2, (tb // GLA_CHUNK) * SUBLANES_F32, hk), F32)],
        compiler_params=_cp("parallel", "arbitrary"),
        name="gla",
    )(*args)
    return o_f.reshape(nseq * L, hv), o_b.reshape(nseq * L, hv), st


def _gla_post_kernel(ofa_ref, ofb_ref, oba_ref, obb_ref, gr_ref, g_ref, o_ref, *, heads, dv, rows_a, tm):
    o = (_pick(ofa_ref, ofb_ref, rows_a, tm).astype(F32) + _pick(oba_ref, obb_ref, rows_a, tm).astype(F32))
    gate = _silu(gr_ref[...].astype(F32))
    g = g_ref[...]
    for h in range(heads):
        sl = slice(h * dv, (h + 1) * dv)
        oh = o[:, sl]
        ms = jnp.mean(oh * oh, axis=-1, keepdims=True)
        o_ref[:, sl] = (oh * lax.rsqrt(ms + RMS_EPS) * g * gate[:, sl]).astype(o_ref.dtype)


def _gla_post(of_a, of_b, ob_a, ob_b, zmain, grcol, norm_g, heads, dv, tm=512):
    ra, hv = of_a.shape
    m = ra + of_b.shape[0]
    return pl.pallas_call(
        functools.partial(_gla_post_kernel, heads=heads, dv=dv, rows_a=ra, tm=tm),
        grid=(m // tm,),
        in_specs=[*_two_source_specs(ra, tm, hv), *_two_source_specs(ra, tm, hv),
                  pl.BlockSpec((tm, hv), lambda i: (i, grcol)),
                  pl.BlockSpec((1, dv), lambda i: (0, 0))],
        out_specs=pl.BlockSpec((tm, hv), lambda i: (i, 0)),
        out_shape=jax.ShapeDtypeStruct((m, hv), BF16),
        compiler_params=_cp("arbitrary"),
        name="gla_post",
    )(of_a, of_b, ob_a, ob_b, zmain, norm_g.reshape(1, dv))


def _rope_tables(L, dh):
    rows = L // GRID_W
    r = np.repeat(np.arange(rows, dtype=np.float32), GRID_W)
    col = np.tile(np.arange(GRID_W, dtype=np.float32), rows)
    nf = dh // 4
    inv = (np.float32(ROPE_THETA) ** (-np.arange(nf, dtype=np.float32) / nf)).astype(np.float32)
    ang_r = (r[:, None] * inv).astype(np.float32)
    ang_c = (col[:, None] * inv).astype(np.float32)
    cos = np.concatenate([np.cos(ang_r), np.cos(ang_r), np.cos(ang_c), np.cos(ang_c)], -1)
    sin = np.concatenate([-np.sin(ang_r), np.sin(ang_r), -np.sin(ang_c), np.sin(ang_c)], -1)
    cos = np.concatenate([cos, cos], -1).astype(np.float32)
    sin = np.concatenate([sin, sin], -1).astype(np.float32)
    cos_t = np.stack([np.ones_like(cos), cos])
    sin_t = np.stack([np.zeros_like(sin), sin])
    return cos_t, sin_t


def _attn_kernel(*refs, has_ctx, lam_init, dh, bf16_exp, tq):
    if has_ctx:
        (q_ref, k_ref, v_ref, cos_ref, sin_ref, kc_ref, vc_ref, lam_ref, g_ref, o_ref,
         kt_ref, ve_ref, s0_ref, s1_ref) = refs
    else:
        q_ref, k_ref, v_ref, cos_ref, sin_ref, lam_ref, g_ref, o_ref, kt_ref, ve_ref, s0_ref, s1_ref = refs
    hw = 2 * dh
    nf = dh // 4
    lk = k_ref.shape[0]
    nsub = q_ref.shape[0] // tq
    lane = lax.broadcasted_iota(jnp.int32, (tq, hw), 1)
    first = lane < dh
    scale = dh ** -0.5 * math.log2(math.e)

    def rope(x, cos, sin):
        ln = lax.broadcasted_iota(jnp.int32, x.shape, 1)
        sw = jnp.where((ln & (2 * nf - 1)) < nf, pltpu.roll(x, hw - nf, 1), pltpu.roll(x, nf, 1))
        return x * cos + sw * sin

    kt_ref[...] = rope(k_ref[...], cos_ref[...], sin_ref[...]).T.astype(BF16)
    ve_ref[:, :hw] = v_ref[...].astype(BF16)
    ve_ref[:, hw:] = jnp.ones((lk, hw), BF16)

    if has_ctx:
        kct = kc_ref[...].T.astype(BF16)
        vcv = vc_ref[...].astype(BF16)
        vce = jnp.concatenate([vcv, jnp.ones_like(vcv)], axis=1)

    def scores(i, j, s_ref):
        rows = pl.ds(i * tq if isinstance(i, int) else pl.multiple_of(i * tq, tq), tq)
        q = (rope(q_ref[rows, :], cos_ref[rows, :], sin_ref[rows, :]) * scale).astype(BF16)
        qj = jnp.where(first, q, jnp.zeros_like(q)) if j == 0 else jnp.where(first, jnp.zeros_like(q), q)
        s_ref[:, :lk] = _dot(qj, kt_ref[...])
        if has_ctx:
            s_ref[:, lk:] = _dot(qj, kct)

    def prob(t):
        if bf16_exp:
            return jnp.exp2(t.astype(BF16))
        return jnp.exp2(t).astype(BF16)

    def softmax_pv(s_ref):
        s = s_ref[...]
        p = prob(s - jnp.max(s, axis=-1, keepdims=True))
        acc = _dot(p[:, :lk], ve_ref[...])
        if has_ctx:
            acc = acc + _dot(p[:, lk:], vce)
        return acc[:, :hw] / acc[:, hw:]

    lp = lam_ref[...]
    lam = (jnp.exp(jnp.sum(lp[0:1] * lp[1:2], axis=-1, keepdims=True))
           - jnp.exp(jnp.sum(lp[2:3] * lp[3:4], axis=-1, keepdims=True)) + lam_init)
    gain = g_ref[...] * (1.0 - lam_init)

    scores(0, 0, s0_ref)

    def body(i, carry):
        scores(i, 1, s1_ref)
        sm0 = softmax_pv(s0_ref)
        scores(jnp.minimum(i + 1, nsub - 1), 0, s0_ref)
        o = sm0 - lam * softmax_pv(s1_ref)
        ms = jnp.mean(o * o, axis=-1, keepdims=True)
        o_ref[pl.ds(pl.multiple_of(i * tq, tq), tq), :] = (o * lax.rsqrt(ms + RMS_EPS) * gain).astype(o_ref.dtype)
        return carry

    lax.fori_loop(0, nsub, body, 0)


def _diff_attention(zd, cos_t, sin_t, rope_kind, row0, nseq, L, heads, dh, lam_p, norm_g, lam_init,
                    ctx=None, tq=256):
    hw = 2 * dh
    sb0 = row0 // L
    past = 0 if ctx is None else ctx[0].shape[2]
    tab = pl.BlockSpec((None, L, hw), lambda s, h: (rope_kind, 0, 0), pipeline_mode=pl.Buffered(1))
    in_specs = [pl.BlockSpec((L, hw), lambda s, h: (sb0 + s, h)),
                pl.BlockSpec((L, hw), lambda s, h: (sb0 + s, heads + h)),
                pl.BlockSpec((L, hw), lambda s, h: (sb0 + s, 2 * heads + h)),
                tab, tab]
    args = [zd, zd, zd, cos_t, sin_t]
    if ctx is not None:
        ck, cv, layer = ctx
        in_specs += [pl.BlockSpec((None, None, past, hw), lambda s, h: (s, layer, 0, h)),
                     pl.BlockSpec((None, None, past, hw), lambda s, h: (s, layer, 0, h))]
        args += [ck, cv]
    in_specs += [pl.BlockSpec(lam_p.shape, lambda s, h: (0, 0)),
                 pl.BlockSpec((1, hw), lambda s, h: (0, 0))]
    args += [lam_p, norm_g.reshape(1, hw)]
    kern = functools.partial(_attn_kernel, has_ctx=ctx is not None, lam_init=lam_init, dh=dh,
                             bf16_exp=L > 1024, tq=min(tq, L))
    return pl.pallas_call(
        kern,
        grid=(nseq, heads),
        in_specs=in_specs,
        out_specs=pl.BlockSpec((L, hw), lambda s, h: (s, h)),
        out_shape=jax.ShapeDtypeStruct((nseq * L, heads * hw), BF16),
        scratch_shapes=[pltpu.VMEM((hw, L), BF16), pltpu.VMEM((L, 2 * hw), BF16),
                        pltpu.VMEM((min(tq, L), L + past), F32), pltpu.VMEM((min(tq, L), L + past), F32)],
        compiler_params=_cp("parallel", "parallel"),
        name="diff_attn",
    )(*args)


def _mix_kernel(yaa_ref, yab_ref, yb_ref, yca_ref, ycb_ref, g_ref, w_ref, o_ref, acc_ref, *, rows_a, tm):
    k = pl.program_id(1)

    def contrib(y):
        return g_ref[...].astype(F32) * _dot(y, w_ref[...])

    @pl.when(k == 0)
    def _():
        acc_ref[...] = contrib(_pick(yaa_ref, yab_ref, rows_a, tm))

    @pl.when(k == 1)
    def _():
        acc_ref[...] += contrib(yb_ref[...])

    @pl.when(k == 2)
    def _():
        o_ref[...] = (acc_ref[...] + contrib(_pick(yca_ref, ycb_ref, rows_a, tm))).astype(o_ref.dtype)


def _mix(ya_a, ya_b, yb, yc_a, yc_b, gates, wbr, tm=512):
    ra, w = ya_a.shape
    m = yb.shape[0]
    d = wbr.shape[2]
    return pl.pallas_call(
        functools.partial(_mix_kernel, rows_a=ra, tm=tm),
        grid=(m // tm, 3),
        in_specs=[*_two_source_specs(ra, tm, w, ndim_grid=2),
                  pl.BlockSpec((tm, w), lambda i, k: (i, 0)),
                  *_two_source_specs(ra, tm, w, ndim_grid=2),
                  pl.BlockSpec((tm, d), lambda i, k: (i, k)),
                  pl.BlockSpec((None, w, d), lambda i, k: (k, 0, 0))],
        out_specs=pl.BlockSpec((tm, d), lambda i, k: (i, 0)),
        out_shape=jax.ShapeDtypeStruct((m, d), BF16),
        scratch_shapes=[pltpu.VMEM((tm, d), F32)],
        compiler_params=_cp("arbitrary", "arbitrary"),
        name="mix",
    )(ya_a, ya_b, yb, yc_a, yc_b, gates, wbr)


def _pad_cols(w, n):
    return jnp.pad(w, ((0, 0), (0, n - w.shape[1])))


def kernel(x_prompt, x_sample, cache_k, cache_v, state_gla, c, c_ctx, w_mod, b_mod, ln_g, ln_b, ffn_w1, ffn_w3, ffn_w2, w_in, hy_conv_w, hy_conv_b, hy_w1, hy_b1, hy_freq, hy_w2, hy_b2, hy_w3, hy_decay, hy_bias, gla_wa, gla_ba, gla_norm_g, diff_lam, diff_norm_g, w_branch_a, w_branch_b, w_branch_c, w_out):
    batch, seq, d = x_prompt.shape
    dec_batch, dec_seq, _ = x_sample.shape
    depth = w_mod.shape[0]
    ffn_dim = ffn_w1.shape[3]
    hy_w = hy_bias.shape[2]
    heads_g, dk_g = 4, gla_wa.shape[3] // 4
    dv_g = gla_norm_g.shape[1]
    rank = gla_wa.shape[2]
    dh = diff_lam.shape[2]
    heads_d = cache_k.shape[3]
    dw = heads_d * 2 * dh
    gw = heads_g * dv_g
    gk = heads_g * dk_g
    assert batch * seq == GROUP and dec_seq == GROUP
    mp = batch * seq
    m = mp + dec_batch * dec_seq
    ngroups = 1 + dec_batch
    alpha = (2 * depth) ** 0.25
    lam_inits = [0.8 - 0.6 * math.exp(-0.3 * l) for l in range(depth)]

    c_main = 3 * hy_w + 2 * gk + 2 * gw
    c_lr = c_main
    c_d = c_lr + 2 * rank
    c_g = c_d + 3 * dw

    cond = jnp.concatenate([c_ctx[None], c, jnp.zeros((16 - ngroups, d), F32)], axis=0)
    mod = _modulation(cond, w_mod, b_mod)[:, :ngroups].reshape(depth, ngroups, N_MOD, d)

    na = 2 * dec_seq // FFT_NB
    n_s = 2 * dec_seq
    f1_half = jnp.asarray(_dft_a(na, na // 2), BF16)
    f1_full = jnp.asarray(_dft_a(na, na), BF16)
    g_s = jnp.asarray(_idft_a(na, n_s, na // 2), BF16)
    tb_np, tbi_np = _dft_b(na, FFT_NB, FFT_K1G)
    tb, tbi = jnp.asarray(tb_np, BF16), jnp.asarray(tbi_np, BF16)
    perm_np = _row_permutation(FFT_NB, SUBLANES_F32)
    perm, permt = jnp.asarray(perm_np, BF16), jnp.asarray(perm_np.T, BF16)
    fp_half = jnp.asarray(_dft_a(2 * seq, seq), BF16)
    fp_full = jnp.asarray(_dft_a(2 * seq, 2 * seq), BF16)
    g_p = jnp.asarray(_idft_a(2 * seq, 2 * seq, seq), BF16)
    cos_t, sin_t = _rope_tables(dec_seq, dh)
    cos_t, sin_t = jnp.asarray(cos_t), jnp.asarray(sin_t)

    x, h = _premod(x_prompt.reshape(mp, d), x_sample.reshape(dec_batch * dec_seq, d), mod, 0)

    fp = ((ffn_dim + 511) // 512) * 512
    ck = cache_k.reshape(dec_batch, depth, cache_k.shape[2], dw)
    cv = cache_v.reshape(dec_batch, depth, cache_v.shape[2], dw)
    state_t = jnp.swapaxes(state_gla, -1, -2)
    new_k, new_v, new_s = [], [], []

    for l in range(depth):
        w2 = _cast_pad_rows(ffn_w2, l, 0, fp)
        hid = _ffn1(h, ffn_w1, ffn_w3, l, 0)
        x, h = _mm_ln(hid, w2, x, mod, l, 2, 0.5, alpha, l, 3, ln_g[l, 0], ln_b[l, 0])

        wi = w_in[l]
        zmain = _proj(h, wi[:, :c_main].astype(BF16), BF16)
        zlr = _proj(h, _pad_cols(wi[:, c_lr:c_d].astype(BF16), LANES), F32)
        zd = _proj(h, wi[:, c_d:c_g].astype(BF16), F32)
        gates = _proj(h, wi[:, c_g:].astype(BF16), BF16, act="sigmoid")

        u3 = _short_conv(zmain, mp, hy_conv_w[l], hy_conv_b[l], seq, dec_seq, hy_w)
        fargs = (hy_w1[l], hy_b1[l], hy_freq[l], hy_w2[l], hy_b2[l], hy_w3[l], hy_decay[l], hy_w)
        taps_s = _hyena_filter_taps(dec_seq, *fargs)
        taps_p = _hyena_filter_taps(seq, *fargs)
        kf_s = _lconv_filter(taps_s, perm, f1_full, tb)
        kf_p = _lmm(fp_full, taps_p, lambda b: (b,), 2, F32, tn=hy_w)

        cw3 = jnp.swapaxes(hy_conv_w[l].reshape(3, 3, hy_w), 0, 1)
        cb3 = hy_conv_b[l].reshape(3, 1, hy_w)
        z1s = _lconv(zmain, mp, 0, zmain, mp, 1, cw3[0:2], cb3[0:2], True, perm, permt, f1_half, g_s,
                     tb, tbi, kf_s, 0, hy_bias[l, 0], dec_batch, dec_seq)
        ya_s = _lconv(z1s, 0, 0, zmain, mp, 2, cw3[1:3], cb3[1:3], False, perm, permt, f1_half, g_s,
                      tb, tbi, kf_s, 1, hy_bias[l, 1], dec_batch, dec_seq)
        z1 = _pconv(fp_half, g_p, u3, (0,), kf_p, 0, u3, (1,), hy_bias[l, 0], batch, seq, F32)
        ya_p = _pconv(fp_half, g_p, z1, (), kf_p, 1, u3, (2,), hy_bias[l, 1], batch, seq, BF16)

        wa_cat = jnp.zeros((LANES, 2 * gk), F32)
        wa_cat = wa_cat.at[:rank, :gk].set(gla_wa[l, 0]).at[rank:2 * rank, gk:].set(gla_wa[l, 1])
        ba_cat = jnp.concatenate([gla_ba[l, 0], gla_ba[l, 1]]).reshape(1, 2 * gk)
        la = _gla_gates(zlr, 0, wa_cat, ba_cat)
        qcol, kcol, vcol, grcol = (3 * hy_w) // gk, (3 * hy_w) // gk + 1, (3 * hy_w + 2 * gk) // gw, \
            (3 * hy_w + 2 * gk) // gw + 1
        of_p, ob_p, st_p = _gla(zmain, qcol, kcol, vcol, la, 0, batch, seq, seq, heads_g, dk_g, dv_g)
        of_s, ob_s, _ = _gla(zmain, qcol, kcol, vcol, la, mp, dec_batch, dec_seq, 512, heads_g, dk_g, dv_g,
                             s0=state_t, s0_layer=l)
        yb = _gla_post(of_p, of_s, ob_p, ob_s, zmain, grcol, gla_norm_g[l], heads_g, dv_g)
        new_s.append(jnp.swapaxes(st_p, -1, -2))

        yc_p = _diff_attention(zd, cos_t, sin_t, 0, 0, batch, seq, heads_d, dh, diff_lam[l],
                               diff_norm_g[l], lam_inits[l])
        yc_s = _diff_attention(zd, cos_t, sin_t, 1, mp, dec_batch, dec_seq, heads_d, dh, diff_lam[l],
                               diff_norm_g[l], lam_inits[l], ctx=(ck, cv, l), tq=512)
        new_k.append(zd[:mp, dw:2 * dw].reshape(batch, seq, heads_d, 2, dh))
        new_v.append(zd[:mp, 2 * dw:3 * dw].reshape(batch, seq, heads_d, 2 * dh))

        wbr = jnp.stack([w_branch_a[l], w_branch_b[l], w_branch_c[l]]).astype(BF16)
        y = _mix(ya_p, ya_s, yb, yc_p, yc_s, gates, wbr)
        x, h = _mm_ln(y, w_out[l].astype(BF16), x, mod, l, 5, 1.0, alpha, l, 6, ln_g[l, 1], ln_b[l, 1],
                      tm=512)

        w2 = _cast_pad_rows(ffn_w2, l, 1, fp)
        hid = _ffn1(h, ffn_w1, ffn_w3, l, 1)
        if l + 1 < depth:
            x, h = _mm_ln(hid, w2, x, mod, l, 8, 0.5, alpha, l + 1, 0, ln_g[l, 2], ln_b[l, 2])
        else:
            xp, _ = _mm_ln(hid, w2, x, mod, l, 8, 0.5, alpha, l, 0, ln_g[l, 2], ln_b[l, 2], rows=mp)
            xs, _ = _mm_ln(hid, w2, x, mod, l, 8, 0.5, alpha, l, 0, ln_g[l, 2], ln_b[l, 2], row0=mp,
                           rows=m - mp)

    y_prompt = xp.reshape(batch, seq, d)
    y_sample = xs.reshape(dec_batch, dec_seq, d)
    return (y_prompt, y_sample, jnp.stack(new_k, axis=1), jnp.stack(new_v, axis=1),
            jnp.stack(new_s, axis=1))
```
